```python
import math
import jax, jax.numpy as jnp
from jax import lax
import numpy as np

D_MODEL = 1024
BATCH = 4
SEQ = 4096
DEPTH = 1
DEC_BATCH = 32
DEC_SEQ = 1
PAST_LEN = 8192
PAGE_SIZE = 128

HEAD_DIM = 64
DIL_GROUPS = ((128, 1), (512, 4), (2048, 16))
N_GROUPS = 3
HEADS_PER_GROUP = 4
N_DIL_HEADS = N_GROUPS * HEADS_PER_GROUP
DIL_WIDTH = N_DIL_HEADS * HEAD_DIM
DIL_OUT = HEADS_PER_GROUP * HEAD_DIM
BLOCK = 128
POOL_WINDOWS = (2, 4, 8, 16)
POOL_GROUP = 128
POOL_WIDTH = POOL_GROUP * len(POOL_WINDOWS)
POOL_STATE = max(POOL_WINDOWS) - 1
MEM_LEN = 256
MEM_HEADS = 4
MEM_HEAD_DIM = 128
MEM_WIDTH = MEM_HEADS * MEM_HEAD_DIM
N_BUCKETS = 32
MAX_DISTANCE = 2048
D_FF = 2816
N_BRANCH = 3
IN_COLS = 3 * DIL_WIDTH + POOL_WIDTH + MEM_WIDTH + N_BRANCH * D_MODEL
EPS = 1e-6
NEG_INF = -1e30

kernel_name = "hybrid_dilated_pool_memory_decoder_step"


def rms_norm(x, g):
    xf = x.astype(jnp.float32)
    xf = xf * lax.rsqrt(jnp.mean(xf * xf, axis=-1, keepdims=True) + EPS)
    return xf.astype(x.dtype) * g


def swiglu(h, wg, wu, wd):
    return (jax.nn.silu(h @ wg) * (h @ wu)) @ wd


def ffn_half(x, g, wg, wu, wd):
    return x + 0.5 * swiglu(rms_norm(x, g), wg, wu, wd)


def rel_bucket(n):
    max_exact = N_BUCKETS // 2
    nf = jnp.maximum(n, 1).astype(jnp.float32)
    large = max_exact + (jnp.log(nf / max_exact) / math.log(MAX_DISTANCE / max_exact)
                         * (N_BUCKETS - max_exact)).astype(jnp.int32)
    large = jnp.minimum(large, N_BUCKETS - 1)
    return jnp.where(n < max_exact, n, large)


def dilation_bias(rel_bias, g, window, dil):
    j = jnp.arange(window // dil + 1, dtype=jnp.int32)
    b = rel_bias[rel_bucket(j * dil)]
    return b[:, g * HEADS_PER_GROUP:(g + 1) * HEADS_PER_GROUP].astype(jnp.float32)


def split_proj(u, w_in_l):
    n, t, _ = u.shape
    p = u @ w_in_l
    o = 3 * DIL_WIDTH
    qkv = p[..., :o].reshape(n, t, 3, N_GROUPS, HEADS_PER_GROUP, HEAD_DIM)
    z = p[..., o:o + POOL_WIDTH]
    o += POOL_WIDTH
    qm = p[..., o:o + MEM_WIDTH].reshape(n, t, MEM_HEADS, MEM_HEAD_DIM)
    o += MEM_WIDTH
    gates = jax.nn.sigmoid(p[..., o:].astype(jnp.float32)).astype(u.dtype)
    return qkv, z, qm, gates.reshape(n, t, N_BRANCH, D_MODEL)


def dilated_prompt(q, k, v, bias_j, dil, window):
    n, s, h, dh = q.shape
    nk = window // dil
    L = s // dil
    nb = -(-L // BLOCK)
    Lp = nb * BLOCK

    def to_blocks(a):
        a = a.reshape(n, L, dil, h, dh).transpose(0, 2, 1, 3, 4)
        a = jnp.pad(a, ((0, 0), (0, 0), (0, Lp - L), (0, 0), (0, 0)))
        return a.reshape(n, dil, nb, BLOCK, h, dh)

    def with_prev(a):
        prev = jnp.concatenate([jnp.zeros_like(a[:, :, :1]), a[:, :, :-1]], axis=2)
        return jnp.concatenate([prev, a], axis=3)

    qb = to_blocks(q)
    kk = with_prev(to_blocks(k))
    vv = with_prev(to_blocks(v))
    scores = jnp.einsum('brnqhd,brnkhd->brnhqk', qb, kk,
                        preferred_element_type=jnp.float32) * (1.0 / math.sqrt(dh))
    qi = jnp.arange(BLOCK)[:, None]
    ki = jnp.arange(2 * BLOCK)[None, :]
    rel = qi + BLOCK - ki
    valid = (rel >= 0) & (rel <= nk)
    key_exists = (jnp.arange(nb)[:, None] > 0) | (ki >= BLOCK)
    bias = bias_j[jnp.clip(rel, 0, nk)].transpose(2, 0, 1)
    mask = valid[None, :, :] & key_exists[:, None, :]
    logits = jnp.where(mask[None, None, :, None], scores + bias, NEG_INF)
    lse = jax.nn.logsumexp(logits, axis=-1)
    pr = jnp.exp(logits - lse[..., None])
    o = jnp.einsum('brnhqk,brnkhd->brnqhd', pr, vv.astype(jnp.float32))
    o = o.reshape(n, dil, Lp, h, dh)[:, :, :L].transpose(0, 2, 1, 3, 4).reshape(n, s, h, dh)
    lse = lse.transpose(0, 1, 2, 4, 3).reshape(n, dil, Lp, h)[:, :, :L]
    lse = lse.transpose(0, 2, 1, 3).reshape(n, s, h)
    return o, lse


def dilated_sample(q, kv_buf, k_new, v_new, bias_j, dil, window):
    wb = kv_buf.shape[1]
    t = q.shape[1]
    kc = jnp.concatenate([kv_buf[:, :, 0], k_new], axis=1)
    vc = jnp.concatenate([kv_buf[:, :, 1], v_new], axis=1)
    nk = window // dil
    idx = wb + jnp.arange(t)[:, None] - dil * jnp.arange(nk + 1)[None, :]
    valid = idx >= 0
    idxc = jnp.maximum(idx, 0)
    kg = kc[:, idxc]
    vg = vc[:, idxc]
    scores = jnp.einsum('nthd,ntjhd->nhtj', q, kg,
                        preferred_element_type=jnp.float32) * (1.0 / math.sqrt(q.shape[-1]))
    logits = jnp.where(valid[None, None], scores + bias_j.T[None, :, None, :], NEG_INF)
    lse = jax.nn.logsumexp(logits, axis=-1)
    pr = jnp.exp(logits - lse[..., None])
    o = jnp.einsum('nhtj,ntjhd->nthd', pr, vg.astype(jnp.float32))
    return o, lse.transpose(0, 2, 1)


def combine_groups(outs, lses):
    o = jnp.stack(outs, axis=2)
    w = jax.nn.softmax(jnp.stack(lses, axis=2), axis=2)
    o = jnp.sum(w[..., None] * o, axis=2)
    return o.reshape(o.shape[0], o.shape[1], DIL_OUT)


def pool_mixer(z_hist, z_new, pos0, w_pool_l, pool_scale_l):
    n, lh, c = z_hist.shape
    t = z_new.shape[1]
    zf = jnp.concatenate([z_hist, z_new], axis=1).astype(jnp.float32)
    c0 = jnp.concatenate([jnp.zeros((n, 1, c), jnp.float32), jnp.cumsum(zf, axis=1)], axis=1)
    r = lh + jnp.arange(t)
    pos = pos0 + jnp.arange(t)
    hi = c0[:, r + 1]
    outs = []
    for gi, kw in enumerate(POOL_WINDOWS):
        sl = slice(gi * POOL_GROUP, (gi + 1) * POOL_GROUP)
        lo = c0[:, jnp.maximum(r + 1 - kw, 0), sl]
        cnt = jnp.minimum(kw, pos + 1).astype(jnp.float32)
        outs.append((hi[..., sl] - lo) / cnt[None, :, None] - zf[:, lh:, sl])
    pooled = jnp.stack(outs, axis=2)
    mixed = jnp.einsum('ntgc,gcd->ntgd', pooled, w_pool_l.astype(jnp.float32)).reshape(n, t, POOL_WIDTH)
    return mixed * pool_scale_l.astype(jnp.float32)


def memory_kv(mem, g_mem_l, w_mem_kv_l):
    n = mem.shape[0]
    return (rms_norm(mem, g_mem_l) @ w_mem_kv_l).reshape(n, MEM_LEN, 2, MEM_HEADS, MEM_HEAD_DIM)


def mem_attend(qm, kv):
    scores = jnp.einsum('nthd,nmhd->nhtm', qm, kv[:, :, 0],
                        preferred_element_type=jnp.float32) * (1.0 / math.sqrt(MEM_HEAD_DIM))
    pr = jax.nn.softmax(scores, axis=-1)
    o = jnp.einsum('nhtm,nmhd->nthd', pr, kv[:, :, 1].astype(jnp.float32))
    return o.reshape(o.shape[0], o.shape[1], MEM_WIDTH)


def merge(a_out, b_out, c_out, gates, w_oa_l, w_ob_l, w_oc_l, w_out_l):
    m = (gates[:, :, 0] * (a_out @ w_oa_l) + gates[:, :, 1] * (b_out @ w_ob_l)
         + gates[:, :, 2] * (c_out @ w_oc_l))
    return m @ w_out_l


def setup_inputs(seed: int = 0) -> dict:
    key = jax.random.key(seed)
    ks = iter(jax.random.split(key, 40))
    f32 = jnp.float32

    def nrm(shape, scale):
        return jax.random.normal(next(ks), shape, f32) * scale

    def gain(shape):
        return 1.0 + nrm(shape, 0.1)

    D = D_MODEL
    inp = {}
    inp["x_prompt"] = nrm((BATCH, SEQ, D), 1.0)
    inp["x_sample"] = nrm((DEC_BATCH, DEC_SEQ, D), 1.0)
    for g, (win, dil) in enumerate(DIL_GROUPS):
        inp["cache_win%d_kv" % g] = nrm((DEPTH, DEC_BATCH, min(win, PAST_LEN), 2, HEADS_PER_GROUP, HEAD_DIM), 1.0)
    inp["state_pool"] = nrm((DEPTH, DEC_BATCH, POOL_STATE, POOL_WIDTH), 1.0)
    inp["cache_mem_kv"] = nrm((DEPTH, DEC_BATCH, MEM_LEN, 2, MEM_HEADS, MEM_HEAD_DIM), 1.0)
    inp["mem_prompt"] = nrm((BATCH, MEM_LEN, D), 1.0)
    inp["rel_bias"] = nrm((N_BUCKETS, N_DIL_HEADS), 0.5)
    inp["g_ffn1"] = gain((DEPTH, D))
    inp["w1_gate"] = nrm((DEPTH, D, D_FF), D ** -0.5)
    inp["w1_up"] = nrm((DEPTH, D, D_FF), D ** -0.5)
    inp["w1_down"] = nrm((DEPTH, D_FF, D), D_FF ** -0.5)
    inp["g_mix"] = gain((DEPTH, D))
    inp["w_in"] = nrm((DEPTH, D, IN_COLS), D ** -0.5)
    inp["w_pool"] = nrm((DEPTH, len(POOL_WINDOWS), POOL_GROUP, POOL_GROUP), POOL_GROUP ** -0.5)
    inp["pool_scale"] = gain((DEPTH, POOL_WIDTH))
    inp["g_mem"] = gain((DEPTH, D))
    inp["w_mem_kv"] = nrm((DEPTH, D, 2 * MEM_WIDTH), D ** -0.5)
    inp["w_oa"] = nrm((DEPTH, DIL_OUT, D), DIL_OUT ** -0.5)
    inp["w_ob"] = nrm((DEPTH, POOL_WIDTH, D), POOL_WIDTH ** -0.5)
    inp["w_oc"] = nrm((DEPTH, MEM_WIDTH, D), MEM_WIDTH ** -0.5)
    inp["w_out"] = nrm((DEPTH, D, D), D ** -0.5)
    inp["g_ffn2"] = gain((DEPTH, D))
    inp["w2_gate"] = nrm((DEPTH, D, D_FF), D ** -0.5)
    inp["w2_up"] = nrm((DEPTH, D, D_FF), D ** -0.5)
    inp["w2_down"] = nrm((DEPTH, D_FF, D), D_FF ** -0.5)
    inp["g_final"] = gain((D,))
    return inp


def reference(x_prompt, x_sample, cache_win0_kv, cache_win1_kv, cache_win2_kv, state_pool, cache_mem_kv,
              mem_prompt, rel_bias, g_ffn1, w1_gate, w1_up, w1_down, g_mix, w_in, w_pool, pool_scale,
              g_mem, w_mem_kv, w_oa, w_ob, w_oc, w_out, g_ffn2, w2_gate, w2_up, w2_down, g_final):
    win_caches = (cache_win0_kv, cache_win1_kv, cache_win2_kv)
    biases = [dilation_bias(rel_bias, g, win, dil) for g, (win, dil) in enumerate(DIL_GROUPS)]

    def block(x, l, mix_fn):
        x = ffn_half(x, g_ffn1[l], w1_gate[l], w1_up[l], w1_down[l])
        u = rms_norm(x, g_mix[l])
        qkv, z, qm, gates = split_proj(u, w_in[l])
        a_out, b_out, c_out, states = mix_fn(qkv, z, qm)
        dt = x.dtype
        x = x + merge(a_out.astype(dt), b_out.astype(dt), c_out.astype(dt), gates,
                      w_oa[l], w_ob[l], w_oc[l], w_out[l])
        x = ffn_half(x, g_ffn2[l], w2_gate[l], w2_up[l], w2_down[l])
        return x, states

    st_p = [[] for _ in range(5)]
    st_s = [[] for _ in range(4)]
    xp = x_prompt
    xs = x_sample
    for l in range(DEPTH):
        def prompt_mix(qkv, z, qm):
            s = qkv.shape[1]
            outs, lses, wins = [], [], []
            for g, (win, dil) in enumerate(DIL_GROUPS):
                q_g, k_g, v_g = qkv[:, :, 0, g], qkv[:, :, 1, g], qkv[:, :, 2, g]
                o, lse = dilated_prompt(q_g, k_g, v_g, biases[g], dil, win)
                outs.append(o)
                lses.append(lse)
                wins.append(jnp.stack([k_g, v_g], axis=2)[:, s - min(win, s):])
            a_out = combine_groups(outs, lses)
            b_out = pool_mixer(z[:, :0], z, 0, w_pool[l], pool_scale[l])
            mkv = memory_kv(mem_prompt, g_mem[l], w_mem_kv[l])
            c_out = mem_attend(qm, mkv)
            return a_out, b_out, c_out, (wins[0], wins[1], wins[2], z[:, s - POOL_STATE:], mkv)

        def sample_mix(qkv, z, qm):
            outs, lses, rows = [], [], []
            for g, (win, dil) in enumerate(DIL_GROUPS):
                q_g, k_g, v_g = qkv[:, :, 0, g], qkv[:, :, 1, g], qkv[:, :, 2, g]
                o, lse = dilated_sample(q_g, win_caches[g][l], k_g, v_g, biases[g], dil, win)
                outs.append(o)
                lses.append(lse)
                rows.append(jnp.stack([k_g, v_g], axis=2))
            a_out = combine_groups(outs, lses)
            b_out = pool_mixer(state_pool[l], z, PAST_LEN, w_pool[l], pool_scale[l])
            c_out = mem_attend(qm, cache_mem_kv[l])
            return a_out, b_out, c_out, (rows[0], rows[1], rows[2], z)

        xp, sp = block(xp, l, prompt_mix)
        xs, ss = block(xs, l, sample_mix)
        for i in range(5):
            st_p[i].append(sp[i])
        for i in range(4):
            st_s[i].append(ss[i])

    y_prompt = rms_norm(xp, g_final)
    y_sample = rms_norm(xs, g_final)
    new_win0_kv_prompt = jnp.stack(st_p[0], axis=0)
    new_win1_kv_prompt = jnp.stack(st_p[1], axis=0)
    new_win2_kv_prompt = jnp.stack(st_p[2], axis=0)
    new_pool_prompt = jnp.stack(st_p[3], axis=0)
    new_mem_kv_prompt = jnp.stack(st_p[4], axis=0)
    new_win0_kv_sample = jnp.stack(st_s[0], axis=0)
    new_win1_kv_sample = jnp.stack(st_s[1], axis=0)
    new_win2_kv_sample = jnp.stack(st_s[2], axis=0)
    new_pool_sample = jnp.stack(st_s[3], axis=0)
    return (y_prompt, y_sample, new_win0_kv_prompt, new_win1_kv_prompt, new_win2_kv_prompt, new_pool_prompt,
            new_mem_kv_prompt, new_win0_kv_sample, new_win1_kv_sample, new_win2_kv_sample, new_pool_sample)
```

```python
import functools
import math

import numpy as np
import jax
import jax.numpy as jnp
from jax import lax
from jax.experimental import pallas as pl
from jax.experimental.pallas import tpu as pltpu

F32 = jnp.float32
BF16 = jnp.bfloat16

D_MODEL = 1024
D_FF = 2816
HEAD_DIM = 64
HEADS = 4
DIL_GROUPS = ((128, 1), (512, 4), (2048, 16))
N_GROUPS = 3
GROUP_COLS = HEADS * HEAD_DIM
QKV_COLS = 3 * GROUP_COLS
STRIDES = 128
POOL_WINDOWS = (2, 4, 8, 16)
POOL_GROUP = 128
POOL_WIDTH = 512
POOL_STATE = 15
POOL_HALO = 16
MEM_LEN = 256
MEM_HEADS = 4
MEM_HEAD_DIM = 128
MEM_WIDTH = 512
N_BUCKETS = 32
MAX_DISTANCE = 2048
N_BRANCH = 3
EPS = 1e-6
NEG_INF = -1e30
PAST_LEN = 8192

V7X_VMEM_LIMIT_BYTES = 56 * 1024 * 1024
TOKEN_TILE = 512


def _params(*sem):
    return pltpu.CompilerParams(dimension_semantics=sem,
                                vmem_limit_bytes=V7X_VMEM_LIMIT_BYTES)


def _resident(shape):
    zeros = (0,) * len(shape)
    return pl.BlockSpec(shape, lambda *_: zeros, pipeline_mode=pl.Buffered(1))


def _rms(x, g):
    return x * lax.rsqrt(jnp.mean(x * x, axis=-1, keepdims=True) + EPS) * g


def _dot(a, b):
    return jnp.dot(a, b, preferred_element_type=F32)


def _dot_t(a, b):
    return lax.dot_general(a, b, (((1,), (1,)), ((), ())), preferred_element_type=F32)


def _ffn_body(x_ref, g_ref, wg_ref, wu_ref, wd_ref, *rest, final):
    o_ref = rest[-1]
    x = x_ref[...]
    h = _rms(x, g_ref[...]).astype(BF16)
    a = _dot(h, wg_ref[...])
    b = _dot(h, wu_ref[...])
    act = (a * jax.nn.sigmoid(a) * b).astype(BF16)
    y = x + 0.5 * _dot(act, wd_ref[...])
    if final:
        y = _rms(y, rest[0][...])
    o_ref[...] = y


def _ffn(x, g, wg, wu, wd, g_final=None):
    m = x.shape[0]
    tm = min(TOKEN_TILE, m)
    final = g_final is not None
    row = pl.BlockSpec((tm, D_MODEL), lambda i: (i, 0))
    in_specs = [row, _resident((1, D_MODEL)), _resident((D_MODEL, D_FF)),
                _resident((D_MODEL, D_FF)), _resident((D_FF, D_MODEL))]
    args = [x, g, wg, wu, wd]
    if final:
        in_specs.append(_resident((1, D_MODEL)))
        args.append(g_final)
    return pl.pallas_call(
        functools.partial(_ffn_body, final=final),
        grid=(m // tm,),
        in_specs=in_specs,
        out_specs=row,
        out_shape=jax.ShapeDtypeStruct((m, D_MODEL), F32),
        compiler_params=_params("parallel"),
        name="ffn_final" if final else "ffn",
    )(*args)


Z_OFF = N_GROUPS * QKV_COLS
QM_OFF = Z_OFF + POOL_WIDTH
GATE_OFF = QM_OFF + MEM_WIDTH
IN_COLS = GATE_OFF + N_BRANCH * D_MODEL


def _inproj_body(x_ref, g_ref, w_ref, qkv0, qkv1, qkv2, kv0, kv1, kv2,
                 z_ref, qm_ref, gate_ref):
    u = _rms(x_ref[...], g_ref[...]).astype(BF16)
    tm = u.shape[0]
    for g, (qkv_ref, kv_ref) in enumerate(((qkv0, kv0), (qkv1, kv1), (qkv2, kv2))):
        p = _dot(u, w_ref[:, g * QKV_COLS:(g + 1) * QKV_COLS])
        qkv_ref[...] = p.astype(qkv_ref.dtype)
        rows = kv_ref.shape[1]
        kv_ref[0] = p[tm - rows:, GROUP_COLS:]
    z_ref[...] = _dot(u, w_ref[:, Z_OFF:QM_OFF])
    qm_ref[...] = _dot(u, w_ref[:, QM_OFF:GATE_OFF]).astype(qm_ref.dtype)
    for b in range(N_BRANCH):
        lo = GATE_OFF + b * D_MODEL
        p = _dot(u, w_ref[:, lo:lo + D_MODEL])
        gate_ref[:, b * D_MODEL:(b + 1) * D_MODEL] = jax.nn.sigmoid(p).astype(BF16)


def _inproj(x, g, w, n, s, windows, q_dtype):
    tm = min(TOKEN_TILE, s)
    tj = s // tm

    def tok(cols):
        return pl.BlockSpec((tm, cols), lambda b, j: (b * tj + j, 0))

    kv_specs, kv_shapes = [], []
    for win in windows:
        rows = min(win, tm)
        assert tm % rows == 0 and win % rows == 0
        first = (s - win) // tm
        if win >= tm:
            spec = pl.BlockSpec((1, rows, 2 * GROUP_COLS),
                                lambda b, j, first=first: (b, jnp.maximum(j - first, 0), 0))
        else:
            spec = pl.BlockSpec((1, rows, 2 * GROUP_COLS), lambda b, j: (b, 0, 0))
        kv_specs.append(spec)
        kv_shapes.append(jax.ShapeDtypeStruct((n, win, 2 * GROUP_COLS), F32))
    m = n * s
    out_shape = ([jax.ShapeDtypeStruct((m, QKV_COLS), q_dtype)] * N_GROUPS + kv_shapes + [
        jax.ShapeDtypeStruct((m, POOL_WIDTH), F32),
        jax.ShapeDtypeStruct((m, MEM_WIDTH), q_dtype),
        jax.ShapeDtypeStruct((m, N_BRANCH * D_MODEL), BF16)])
    out_specs = ([tok(QKV_COLS)] * N_GROUPS + kv_specs
                 + [tok(POOL_WIDTH), tok(MEM_WIDTH), tok(N_BRANCH * D_MODEL)])
    return pl.pallas_call(
        _inproj_body,
        grid=(n, tj),
        in_specs=[tok(D_MODEL), _resident((1, D_MODEL)), _resident((D_MODEL, IN_COLS))],
        out_specs=out_specs,
        out_shape=out_shape,
        compiler_params=_params("arbitrary", "arbitrary"),
        name="inproj",
    )(x, g, w)


def _attn_body(cur_ref, prev_ref, bias_ref, o_ref, lse_ref, kv_scr, *, rows):
    lb = pl.program_id(2)
    kv_scr[0:STRIDES, :] = prev_ref[0, :, GROUP_COLS:]
    kv_scr[STRIDES:, :] = cur_ref[0, :, GROUP_COLS:]
    col = lax.broadcasted_iota(jnp.int32, (1, 2 * STRIDES), 1)
    prev_cols = (col < STRIDES).astype(F32)

    def sub(i, carry):
        base = pl.multiple_of(i * STRIDES, STRIDES)
        q = cur_ref[0, pl.ds(base, STRIDES), 0:GROUP_COLS]
        kv = kv_scr[pl.ds(base, 2 * STRIDES), :]
        no_prev = jnp.where(jnp.logical_and(lb == 0, i == 0), NEG_INF, 0.0)
        outs, lses = [], []
        for h in range(HEADS):
            hs = slice(h * HEAD_DIM, (h + 1) * HEAD_DIM)
            vs = slice(GROUP_COLS + h * HEAD_DIM, GROUP_COLS + (h + 1) * HEAD_DIM)
            s = _dot_t(q[:, hs], kv[:, hs]) + bias_ref[h] + prev_cols * no_prev
            m = jnp.max(s, axis=-1, keepdims=True)
            p = jnp.exp(s - m)
            l = jnp.sum(p, axis=-1, keepdims=True)
            outs.append(_dot(p.astype(BF16), kv[:, vs]) / l)
            lses.append(jnp.broadcast_to(m + jnp.log(l), (STRIDES, HEAD_DIM)))
        o_ref[0, pl.ds(base, STRIDES), :] = jnp.concatenate(outs, axis=1)
        lse_ref[0, pl.ds(base, STRIDES), :] = jnp.concatenate(lses, axis=1)
        return carry

    lax.fori_loop(0, rows // STRIDES, sub, 0)


def _attn(qkv, bias, n, s, dil):
    cls_len = s // dil
    rows = min(cls_len, 1024)
    blocks = cls_len // rows
    sub_per_block = rows // STRIDES
    view = qkv.reshape(n, cls_len, dil * QKV_COLS)
    out_spec = pl.BlockSpec((1, rows, GROUP_COLS), lambda b, r, k: (b, k, r))
    out_sds = jax.ShapeDtypeStruct((n, cls_len, dil * GROUP_COLS), F32)
    o, lse = pl.pallas_call(
        functools.partial(_attn_body, rows=rows),
        grid=(n, dil, blocks),
        in_specs=[
            pl.BlockSpec((1, rows, QKV_COLS), lambda b, r, k: (b, k, r)),
            pl.BlockSpec((1, STRIDES, QKV_COLS),
                         lambda b, r, k: (b, jnp.maximum(k * sub_per_block - 1, 0), r)),
            _resident((HEADS, STRIDES, 2 * STRIDES)),
        ],
        out_specs=[out_spec, out_spec],
        out_shape=[out_sds, out_sds],
        scratch_shapes=[pltpu.VMEM((rows + STRIDES, 2 * GROUP_COLS), BF16)],
        compiler_params=_params("parallel", "parallel", "arbitrary"),
        name="attn_d%d" % dil,
    )(view, view, bias)
    return o.reshape(n * s, GROUP_COLS), lse.reshape(n * s, GROUP_COLS)


def _memkv_body(mem_ref, g_ref, w_ref, o_ref):
    o_ref[...] = _dot(_rms(mem_ref[...], g_ref[...]).astype(BF16), w_ref[...])


def _memkv(mem, g, w):
    m = mem.shape[0]
    row = pl.BlockSpec((MEM_LEN, D_MODEL), lambda i: (i, 0))
    return pl.pallas_call(
        _memkv_body,
        grid=(m // MEM_LEN,),
        in_specs=[row, _resident((1, D_MODEL)), _resident((D_MODEL, 2 * MEM_WIDTH))],
        out_specs=pl.BlockSpec((MEM_LEN, 2 * MEM_WIDTH), lambda i: (i, 0)),
        out_shape=jax.ShapeDtypeStruct((m, 2 * MEM_WIDTH), F32),
        compiler_params=_params("parallel"),
        name="memkv",
    )(mem, g, w)


def _merge_math(a, pooled, c, gate_ref, x, wpool_ref, scale_ref, woa_ref, wob_ref,
                woc_ref, wout_ref):
    mixed = [_dot(pooled[gi].astype(BF16), wpool_ref[gi]) for gi in range(len(POOL_WINDOWS))]
    b = jnp.concatenate(mixed, axis=1) * scale_ref[...]
    m = (gate_ref[:, 0:D_MODEL].astype(F32) * _dot(a.astype(BF16), woa_ref[...])
         + gate_ref[:, D_MODEL:2 * D_MODEL].astype(F32) * _dot(b.astype(BF16), wob_ref[...])
         + gate_ref[:, 2 * D_MODEL:].astype(F32) * _dot(c.astype(BF16), woc_ref[...]))
    return x + _dot(m.astype(BF16), wout_ref[...])


def _mix_body(o0, o1, o2, l0, l1, l2, z_ref, halo_ref, qm_ref, mkv_ref, gate_ref, x_ref,
              wpool_ref, scale_ref, woa_ref, wob_ref, woc_ref, wout_ref, out_ref, *, tiles):
    j = pl.program_id(1)
    tm = x_ref.shape[0]
    lses = [l0[...], l1[...], l2[...]]
    mx = jnp.maximum(jnp.maximum(lses[0], lses[1]), lses[2])
    es = [jnp.exp(l - mx) for l in lses]
    a = (es[0] * o0[...] + es[1] * o1[...] + es[2] * o2[...]) / (es[0] + es[1] + es[2])
    z = z_ref[...]
    halo = jnp.where(j == 0, 0.0, halo_ref[...])
    zc = jnp.concatenate([halo, z], axis=0)
    pos = j * tm + lax.broadcasted_iota(jnp.int32, (tm, 1), 0)
    pooled = []
    for gi, kw in enumerate(POOL_WINDOWS):
        cs = slice(gi * POOL_GROUP, (gi + 1) * POOL_GROUP)
        run = zc[:, cs]
        width = 1
        while width < kw:
            run = run[width:] + run[:-width]
            width *= 2
        first = POOL_HALO - (kw - 1)
        cnt = jnp.minimum(kw, pos + 1).astype(F32)
        pooled.append(run[first:first + tm] / cnt - z[:, cs])
    mkv = mkv_ref[...].astype(BF16)
    qm = qm_ref[...]
    cs_out = []
    for h in range(MEM_HEADS):
        hs = slice(h * MEM_HEAD_DIM, (h + 1) * MEM_HEAD_DIM)
        vs = slice(MEM_WIDTH + h * MEM_HEAD_DIM, MEM_WIDTH + (h + 1) * MEM_HEAD_DIM)
        s = _dot_t(qm[:, hs], mkv[:, hs]) * (1.0 / math.sqrt(MEM_HEAD_DIM))
        mm = jnp.max(s, axis=-1, keepdims=True)
        p = jnp.exp(s - mm)
        l = jnp.sum(p, axis=-1, keepdims=True)
        cs_out.append(_dot(p.astype(BF16), mkv[:, vs]) / l)
    c = jnp.concatenate(cs_out, axis=1)
    out_ref[...] = _merge_math(a, pooled, c, gate_ref, x_ref[...], wpool_ref, scale_ref,
                               woa_ref, wob_ref, woc_ref, wout_ref)


def _merge_weight_specs():
    return [_resident((len(POOL_WINDOWS), POOL_GROUP, POOL_GROUP)), _resident((1, POOL_WIDTH)),
            _resident((GROUP_COLS, D_MODEL)), _resident((POOL_WIDTH, D_MODEL)),
            _resident((MEM_WIDTH, D_MODEL)), _resident((D_MODEL, D_MODEL))]


def _mix_prompt(os, lses, z, qm, mkv, gates, x, weights, n, s):
    tm = TOKEN_TILE
    tj = s // tm
    halo_per_tile = tm // POOL_HALO

    def tok(cols):
        return pl.BlockSpec((tm, cols), lambda b, j: (b * tj + j, 0))

    halo = pl.BlockSpec((POOL_HALO, POOL_WIDTH),
                        lambda b, j: (jnp.maximum((b * tj + j) * halo_per_tile - 1, 0), 0))
    in_specs = ([tok(GROUP_COLS)] * 6 + [tok(POOL_WIDTH), halo, tok(MEM_WIDTH),
                pl.BlockSpec((MEM_LEN, 2 * MEM_WIDTH), lambda b, j: (b, 0)),
                tok(N_BRANCH * D_MODEL), tok(D_MODEL)] + _merge_weight_specs())
    return pl.pallas_call(
        functools.partial(_mix_body, tiles=tj),
        grid=(n, tj),
        in_specs=in_specs,
        out_specs=tok(D_MODEL),
        out_shape=jax.ShapeDtypeStruct((n * s, D_MODEL), F32),
        compiler_params=_params("parallel", "parallel"),
        name="mix_prompt",
    )(*os, *lses, z, z, qm, mkv, gates, x, *weights)


def _sample_branch_body(q_ref, kvn_ref, c0, c1, c2, bias_ref, bias0_ref, z_ref, st_ref,
                        qm_ref, cm_ref, a_ref, pooled_ref, c_ref):
    outs, lses = [], []
    for g, cache_ref in enumerate((c0, c1, c2)):
        q = q_ref[0, g]
        kn = kvn_ref[0, g, 0]
        vn = kvn_ref[0, g, 1]
        k = cache_ref[0, :, 0, 0]
        v = cache_ref[0, :, 0, 1]
        s = jnp.sum(k * q[None], axis=-1, keepdims=True) + bias_ref[g]
        sn = jnp.sum(kn * q, axis=-1, keepdims=True) + bias0_ref[g]
        m = jnp.maximum(jnp.max(s, axis=0), sn)
        p = jnp.exp(s - m[None])
        pn = jnp.exp(sn - m)
        l = jnp.sum(p, axis=0) + pn
        outs.append((jnp.sum(p * v, axis=0) + pn * vn) / l)
        lses.append(m + jnp.log(l))
    mx = jnp.maximum(jnp.maximum(lses[0], lses[1]), lses[2])
    es = [jnp.exp(lse - mx) for lse in lses]
    a_ref[0] = ((es[0] * outs[0] + es[1] * outs[1] + es[2] * outs[2])
                / (es[0] + es[1] + es[2]))

    zn = z_ref[0]
    st = st_ref[0]
    pooled = []
    for gi, kw in enumerate(POOL_WINDOWS):
        cs = slice(gi * POOL_GROUP, (gi + 1) * POOL_GROUP)
        tot = jnp.sum(st[POOL_STATE - (kw - 1):, cs], axis=0, keepdims=True) + zn[:, cs]
        pooled.append(tot / float(min(kw, PAST_LEN + 1)) - zn[:, cs])
    pooled_ref[0] = jnp.concatenate(pooled, axis=1)

    qm = qm_ref[0]
    km = cm_ref[0, :, 0]
    vm = cm_ref[0, :, 1]
    s = jnp.sum(km * qm[None], axis=-1, keepdims=True) * (1.0 / math.sqrt(MEM_HEAD_DIM))
    m = jnp.max(s, axis=0)
    p = jnp.exp(s - m[None])
    c_ref[0] = jnp.sum(p * vm, axis=0) / jnp.sum(p, axis=0)


def _sample_branches(q, kvn, caches, bias, bias0, z, state, qm, cache_mem):
    nb = z.shape[0]

    def per_request(shape):
        zeros = (0,) * len(shape)
        return pl.BlockSpec((1,) + shape, lambda b: (b,) + zeros)

    views, cache_specs = [], []
    for cache, (win, dil) in zip(caches, DIL_GROUPS):
        views.append(cache.reshape(nb, win // dil, dil, 2, HEADS, HEAD_DIM))
        cache_specs.append(pl.BlockSpec((1, STRIDES, 1, 2, HEADS, HEAD_DIM),
                                        lambda b: (b, 0, 0, 0, 0, 0)))
    in_specs = ([per_request((N_GROUPS, HEADS, HEAD_DIM)),
                 per_request((N_GROUPS, 2, HEADS, HEAD_DIM))] + cache_specs + [
        _resident((N_GROUPS, STRIDES, HEADS, 1)), _resident((N_GROUPS, HEADS, 1)),
        per_request((1, POOL_WIDTH)), per_request((POOL_STATE, POOL_WIDTH)),
        per_request((MEM_HEADS, MEM_HEAD_DIM)),
        per_request((MEM_LEN, 2, MEM_HEADS, MEM_HEAD_DIM))])
    return pl.pallas_call(
        _sample_branch_body,
        grid=(nb,),
        in_specs=in_specs,
        out_specs=[per_request((HEADS, HEAD_DIM)), per_request((1, POOL_WIDTH)),
                   per_request((MEM_HEADS, MEM_HEAD_DIM))],
        out_shape=[jax.ShapeDtypeStruct((nb, HEADS, HEAD_DIM), F32),
                   jax.ShapeDtypeStruct((nb, 1, POOL_WIDTH), F32),
                   jax.ShapeDtypeStruct((nb, MEM_HEADS, MEM_HEAD_DIM), F32)],
        compiler_params=_params("parallel"),
        name="sample_branches",
    )(q, kvn, *views, bias, bias0, z, state, qm, cache_mem)


def _sample_merge_body(a_ref, pooled_ref, c_ref, gate_ref, x_ref, wpool_ref, scale_ref,
                       woa_ref, wob_ref, woc_ref, wout_ref, out_ref):
    pooled_all = pooled_ref[...]
    pooled = [pooled_all[:, gi * POOL_GROUP:(gi + 1) * POOL_GROUP]
              for gi in range(len(POOL_WINDOWS))]
    out_ref[...] = _merge_math(a_ref[...], pooled, c_ref[...], gate_ref, x_ref[...],
                               wpool_ref, scale_ref, woa_ref, wob_ref, woc_ref, wout_ref)


def _sample_merge(a, pooled, c, gates, x, weights):
    nb = x.shape[0]
    full = lambda cols: _resident((nb, cols))
    return pl.pallas_call(
        _sample_merge_body,
        grid=(1,),
        in_specs=[full(GROUP_COLS), full(POOL_WIDTH), full(MEM_WIDTH),
                  full(N_BRANCH * D_MODEL), full(D_MODEL)] + _merge_weight_specs(),
        out_specs=pl.BlockSpec((nb, D_MODEL), lambda i: (0, 0)),
        out_shape=jax.ShapeDtypeStruct((nb, D_MODEL), F32),
        compiler_params=_params("arbitrary"),
        name="sample_merge",
    )(a, pooled, c, gates, x, *weights)


def _rel_bucket(n):
    max_exact = N_BUCKETS // 2
    nf = jnp.maximum(n, 1).astype(F32)
    large = max_exact + (jnp.log(nf / max_exact) / math.log(MAX_DISTANCE / max_exact)
                         * (N_BUCKETS - max_exact)).astype(jnp.int32)
    large = jnp.minimum(large, N_BUCKETS - 1)
    return jnp.where(n < max_exact, n, large)


def _stride_bias(rel_bias, g, dil):
    j = jnp.arange(STRIDES + 1, dtype=jnp.int32)
    return rel_bias[_rel_bucket(j * dil)][:, g * HEADS:(g + 1) * HEADS].astype(F32)


def _band_bias(bias_j):
    rel = np.arange(STRIDES)[:, None] + STRIDES - np.arange(2 * STRIDES)[None, :]
    valid = (rel >= 0) & (rel <= STRIDES)
    band = bias_j[np.clip(rel, 0, STRIDES)]
    return jnp.where(valid[None], band.transpose(2, 0, 1), NEG_INF)


def _in_weight(w_in_l):
    width = N_GROUPS * GROUP_COLS
    parts = []
    for g in range(N_GROUPS):
        for t in range(3):
            lo = t * width + g * GROUP_COLS
            blk = w_in_l[:, lo:lo + GROUP_COLS]
            parts.append(blk * (1.0 / math.sqrt(HEAD_DIM)) if t == 0 else blk)
    parts.append(w_in_l[:, 3 * width:])
    return jnp.concatenate(parts, axis=1).astype(BF16)


def kernel(x_prompt, x_sample, cache_win0_kv, cache_win1_kv, cache_win2_kv, state_pool, cache_mem_kv, mem_prompt, rel_bias, g_ffn1, w1_gate, w1_up, w1_down, g_mix, w_in, w_pool, pool_scale, g_mem, w_mem_kv, w_oa, w_ob, w_oc, w_out, g_ffn2, w2_gate, w2_up, w2_down, g_final):
    n, s, _ = x_prompt.shape
    nb = x_sample.shape[0]
    depth = g_ffn1.shape[0]
    win_caches = (cache_win0_kv, cache_win1_kv, cache_win2_kv)
    bias_js = [_stride_bias(rel_bias, g, dil) for g, (_, dil) in enumerate(DIL_GROUPS)]
    band = [_band_bias(b) for b in bias_js]
    bias_cache = jnp.stack([b[STRIDES:0:-1] for b in bias_js])[..., None]
    bias_new = jnp.stack([b[0] for b in bias_js])[..., None]
    gfin = g_final.reshape(1, D_MODEL)

    xp = x_prompt.reshape(n * s, D_MODEL)
    xs = x_sample.reshape(nb, D_MODEL)
    st_p = [[] for _ in range(5)]
    st_s = [[] for _ in range(4)]
    for l in range(depth):
        last = l == depth - 1
        bf = lambda w: w[l].astype(BF16)
        vec = lambda v: v[l].reshape(1, -1)
        w1 = (bf(w1_gate), bf(w1_up), bf(w1_down))
        w2 = (bf(w2_gate), bf(w2_up), bf(w2_down))
        win_l = _in_weight(w_in[l])
        merge_w = (bf(w_pool), vec(pool_scale), bf(w_oa), bf(w_ob), bf(w_oc), bf(w_out))

        xp = _ffn(xp, vec(g_ffn1), *w1)
        res = _inproj(xp, vec(g_mix), win_l, n, s, [min(w, s) for w, _ in DIL_GROUPS], BF16)
        qkvs, kvwins, z, qm, gates = res[0:3], res[3:6], res[6], res[7], res[8]
        os, lses = [], []
        for g, (_, dil) in enumerate(DIL_GROUPS):
            o, lse = _attn(qkvs[g], band[g], n, s, dil)
            os.append(o)
            lses.append(lse)
        mkv = _memkv(mem_prompt.reshape(n * MEM_LEN, D_MODEL), vec(g_mem), bf(w_mem_kv))
        xp = _mix_prompt(os, lses, z, qm, mkv, gates, xp, merge_w, n, s)
        xp = _ffn(xp, vec(g_ffn2), *w2, g_final=gfin if last else None)
        for g, (win, _) in enumerate(DIL_GROUPS):
            st_p[g].append(kvwins[g].reshape(n, min(win, s), 2, HEADS, HEAD_DIM))
        st_p[3].append(z.reshape(n, s, POOL_WIDTH)[:, s - POOL_STATE:])
        st_p[4].append(mkv.reshape(n, MEM_LEN, 2, MEM_HEADS, MEM_HEAD_DIM))

        xs = _ffn(xs, vec(g_ffn1), *w1)
        res = _inproj(xs, vec(g_mix), win_l, 1, nb, [nb] * N_GROUPS, F32)
        qkvs, kvns, z, qm, gates = res[0:3], res[3:6], res[6], res[7], res[8]
        q = jnp.stack([t[:, :GROUP_COLS].reshape(nb, HEADS, HEAD_DIM) for t in qkvs], axis=1)
        kvns = [t.reshape(nb, 2, HEADS, HEAD_DIM) for t in kvns]
        a, pooled, c = _sample_branches(
            q, jnp.stack(kvns, axis=1), [cw[l] for cw in win_caches], bias_cache, bias_new,
            z.reshape(nb, 1, POOL_WIDTH), state_pool[l],
            qm.reshape(nb, MEM_HEADS, MEM_HEAD_DIM), cache_mem_kv[l])
        xs = _sample_merge(a.reshape(nb, -1), pooled.reshape(nb, -1), c.reshape(nb, -1),
                           gates, xs, merge_w)
        xs = _ffn(xs, vec(g_ffn2), *w2, g_final=gfin if last else None)
        for g in range(N_GROUPS):
            st_s[g].append(kvns[g].reshape(nb, 1, 2, HEADS, HEAD_DIM))
        st_s[3].append(z.reshape(nb, 1, POOL_WIDTH))

    y_prompt = xp.reshape(n, s, D_MODEL)
    y_sample = xs.reshape(nb, 1, D_MODEL)
    stack = lambda ts: jnp.stack(ts, axis=0)
    return (y_prompt, y_sample, stack(st_p[0]), stack(st_p[1]), stack(st_p[2]), stack(st_p[3]),
            stack(st_p[4]), stack(st_s[0]), stack(st_s[1]), stack(st_s[2]), stack(st_s[3]))
```

```python
import functools
import math

import jax
import jax.numpy as jnp
from jax import lax
from jax.experimental import pallas as pl
from jax.experimental.pallas import tpu as pltpu

F32 = jnp.float32
BF16 = jnp.bfloat16

D_MODEL = 1024
D_FF = 2816
HEAD_DIM = 64
HEADS = 4
DIL_GROUPS = ((128, 1), (512, 4), (2048, 16))
N_GROUPS = 3
GROUP_COLS = HEADS * HEAD_DIM
QKV_COLS = 3 * GROUP_COLS
STRIDES = 128
POOL_WINDOWS = (2, 4, 8, 16)
POOL_GROUP = 128
POOL_WIDTH = 512
POOL_STATE = 15
POOL_HALO = 16
MEM_LEN = 256
MEM_HEADS = 4
MEM_HEAD_DIM = 128
MEM_WIDTH = 512
N_BUCKETS = 32
MAX_DISTANCE = 2048
N_BRANCH = 3
EPS = 1e-6
NEG_INF = -1e30
PAST_LEN = 8192

V7X_VMEM_LIMIT_BYTES = 56 * 1024 * 1024
TOKEN_TILE = 512


def _params(*sem):
    return pltpu.CompilerParams(dimension_semantics=sem,
                                vmem_limit_bytes=V7X_VMEM_LIMIT_BYTES)


def _resident(shape):
    zeros = (0,) * len(shape)
    return pl.BlockSpec(shape, lambda *_: zeros, pipeline_mode=pl.Buffered(1))


def _rms(x, g):
    return x * lax.rsqrt(jnp.mean(x * x, axis=-1, keepdims=True) + EPS) * g


def _dot(a, b):
    return jnp.dot(a, b, preferred_element_type=F32)


def _dot_t(a, b):
    return lax.dot_general(a, b, (((1,), (1,)), ((), ())), preferred_element_type=F32)


def _ffn_body(x_ref, g_ref, wg_ref, wu_ref, wd_ref, *rest, final):
    o_ref = rest[-1]
    x = x_ref[...]
    h = _rms(x, g_ref[...]).astype(BF16)
    a = _dot(h, wg_ref[...])
    b = _dot(h, wu_ref[...])
    act = (a * jax.nn.sigmoid(a) * b).astype(BF16)
    y = x + 0.5 * _dot(act, wd_ref[...])
    if final:
        y = _rms(y, rest[0][...])
    o_ref[...] = y


def _ffn(x, g, wg, wu, wd, g_final=None):
    m = x.shape[0]
    tm = min(TOKEN_TILE, m)
    final = g_final is not None
    row = pl.BlockSpec((tm, D_MODEL), lambda i: (i, 0))
    in_specs = [row, _resident((1, D_MODEL)), _resident((D_MODEL, D_FF)),
                _resident((D_MODEL, D_FF)), _resident((D_FF, D_MODEL))]
    args = [x, g, wg, wu, wd]
    if final:
        in_specs.append(_resident((1, D_MODEL)))
        args.append(g_final)
    return pl.pallas_call(
        functools.partial(_ffn_body, final=final),
        grid=(m // tm,),
        in_specs=in_specs,
        out_specs=row,
        out_shape=jax.ShapeDtypeStruct((m, D_MODEL), F32),
        compiler_params=_params("parallel"),
        name="ffn_final" if final else "ffn",
    )(*args)


Z_OFF = N_GROUPS * QKV_COLS
QM_OFF = Z_OFF + POOL_WIDTH
GATE_OFF = QM_OFF + MEM_WIDTH
IN_COLS = GATE_OFF + N_BRANCH * D_MODEL


def _inproj_body(x_ref, g_ref, w_ref, qkv0, qkv1, qkv2, kv0, kv1, kv2,
                 z_ref, qm_ref, gate_ref):
    u = _rms(x_ref[...], g_ref[...]).astype(BF16)
    tm = u.shape[0]
    for g, (qkv_ref, kv_ref) in enumerate(((qkv0, kv0), (qkv1, kv1), (qkv2, kv2))):
        p = _dot(u, w_ref[:, g * QKV_COLS:(g + 1) * QKV_COLS])
        qkv_ref[...] = p.astype(qkv_ref.dtype)
        rows = kv_ref.shape[1]
        kv_ref[0] = p[tm - rows:, GROUP_COLS:]
    z_ref[...] = _dot(u, w_ref[:, Z_OFF:QM_OFF])
    qm_ref[...] = _dot(u, w_ref[:, QM_OFF:GATE_OFF]).astype(qm_ref.dtype)
    for b in range(N_BRANCH):
        lo = GATE_OFF + b * D_MODEL
        p = _dot(u, w_ref[:, lo:lo + D_MODEL])
        gate_ref[:, b * D_MODEL:(b + 1) * D_MODEL] = jax.nn.sigmoid(p).astype(BF16)


def _inproj(x, g, w, n, s, windows, q_dtype):
    tm = min(TOKEN_TILE, s)
    tj = s // tm

    def tok(cols):
        return pl.BlockSpec((tm, cols), lambda b, j: (b * tj + j, 0))

    kv_specs, kv_shapes = [], []
    for win in windows:
        rows = min(win, tm)
        assert tm % rows == 0 and win % rows == 0
        first = (s - win) // tm
        if win >= tm:
            spec = pl.BlockSpec((1, rows, 2 * GROUP_COLS),
                                lambda b, j, first=first: (b, jnp.maximum(j - first, 0), 0))
        else:
            spec = pl.BlockSpec((1, rows, 2 * GROUP_COLS), lambda b, j: (b, 0, 0))
        kv_specs.append(spec)
        kv_shapes.append(jax.ShapeDtypeStruct((n, win, 2 * GROUP_COLS), F32))
    m = n * s
    out_shape = ([jax.ShapeDtypeStruct((m, QKV_COLS), q_dtype)] * N_GROUPS + kv_shapes + [
        jax.ShapeDtypeStruct((m, POOL_WIDTH), F32),
        jax.ShapeDtypeStruct((m, MEM_WIDTH), q_dtype),
        jax.ShapeDtypeStruct((m, N_BRANCH * D_MODEL), BF16)])
    out_specs = ([tok(QKV_COLS)] * N_GROUPS + kv_specs
                 + [tok(POOL_WIDTH), tok(MEM_WIDTH), tok(N_BRANCH * D_MODEL)])
    return pl.pallas_call(
        _inproj_body,
        grid=(n, tj),
        in_specs=[tok(D_MODEL), _resident((1, D_MODEL)), _resident((D_MODEL, IN_COLS))],
        out_specs=out_specs,
        out_shape=out_shape,
        compiler_params=_params("arbitrary", "arbitrary"),
        name="inproj",
    )(x, g, w)


def _attn_body(cur_ref, prev_ref, bias_ref, o_ref, lse_ref, kv_scr, band_scr, *, rows):
    lb = pl.program_id(2)
    kv_scr[0:STRIDES, :] = prev_ref[0, :, GROUP_COLS:]
    kv_scr[STRIDES:, :] = cur_ref[0, :, GROUP_COLS:]
    for h in range(HEADS):
        row = jnp.broadcast_to(bias_ref[h], (STRIDES, 2 * STRIDES))
        band_scr[h] = pltpu.roll(row, 0, 1, stride=1, stride_axis=0)
    col = lax.broadcasted_iota(jnp.int32, (1, 2 * STRIDES), 1)
    prev_cols = (col < STRIDES).astype(F32)

    def sub(i, carry):
        base = pl.multiple_of(i * STRIDES, STRIDES)
        q = cur_ref[0, pl.ds(base, STRIDES), 0:GROUP_COLS]
        kv = kv_scr[pl.ds(base, 2 * STRIDES), :]
        no_prev = jnp.where(jnp.logical_and(lb == 0, i == 0), NEG_INF, 0.0)
        outs, lses = [], []
        for h in range(HEADS):
            hs = slice(h * HEAD_DIM, (h + 1) * HEAD_DIM)
            vs = slice(GROUP_COLS + h * HEAD_DIM, GROUP_COLS + (h + 1) * HEAD_DIM)
            s = _dot_t(q[:, hs], kv[:, hs]) + band_scr[h] + prev_cols * no_prev
            m = jnp.max(s, axis=-1, keepdims=True)
            p = jnp.exp(s - m)
            l = jnp.sum(p, axis=-1, keepdims=True)
            outs.append(_dot(p.astype(BF16), kv[:, vs]) / l)
            lses.append(jnp.broadcast_to(m + jnp.log(l), (STRIDES, HEAD_DIM)))
        o_ref[0, pl.ds(base, STRIDES), :] = jnp.concatenate(outs, axis=1)
        lse_ref[0, pl.ds(base, STRIDES), :] = jnp.concatenate(lses, axis=1)
        return carry

    lax.fori_loop(0, rows // STRIDES, sub, 0)


def _attn(qkv, bias, n, s, dil):
    cls_len = s // dil
    rows = min(cls_len, 1024)
    blocks = cls_len // rows
    sub_per_block = rows // STRIDES
    view = qkv.reshape(n, cls_len, dil * QKV_COLS)
    out_spec = pl.BlockSpec((1, rows, GROUP_COLS), lambda b, r, k: (b, k, r))
    out_sds = jax.ShapeDtypeStruct((n, cls_len, dil * GROUP_COLS), F32)
    o, lse = pl.pallas_call(
        functools.partial(_attn_body, rows=rows),
        grid=(n, dil, blocks),
        in_specs=[
            pl.BlockSpec((1, rows, QKV_COLS), lambda b, r, k: (b, k, r)),
            pl.BlockSpec((1, STRIDES, QKV_COLS),
                         lambda b, r, k: (b, jnp.maximum(k * sub_per_block - 1, 0), r)),
            _resident((HEADS, 1, 2 * STRIDES)),
        ],
        out_specs=[out_spec, out_spec],
        out_shape=[out_sds, out_sds],
        scratch_shapes=[pltpu.VMEM((rows + STRIDES, 2 * GROUP_COLS), BF16),
                        pltpu.VMEM((HEADS, STRIDES, 2 * STRIDES), F32)],
        compiler_params=_params("parallel", "parallel", "arbitrary"),
        name="attn_d%d" % dil,
    )(view, view, bias)
    return o.reshape(n * s, GROUP_COLS), lse.reshape(n * s, GROUP_COLS)


def _memkv_body(mem_ref, g_ref, w_ref, o_ref):
    o_ref[...] = _dot(_rms(mem_ref[...], g_ref[...]).astype(BF16), w_ref[...])


def _memkv(mem, g, w):
    m = mem.shape[0]
    row = pl.BlockSpec((MEM_LEN, D_MODEL), lambda i: (i, 0))
    return pl.pallas_call(
        _memkv_body,
        grid=(m // MEM_LEN,),
        in_specs=[row, _resident((1, D_MODEL)), _resident((D_MODEL, 2 * MEM_WIDTH))],
        out_specs=pl.BlockSpec((MEM_LEN, 2 * MEM_WIDTH), lambda i: (i, 0)),
        out_shape=jax.ShapeDtypeStruct((m, 2 * MEM_WIDTH), F32),
        compiler_params=_params("parallel"),
        name="memkv",
    )(mem, g, w)


def _merge_math(a, pooled, c, gate_ref, x, wpool_ref, scale_ref, woa_ref, wob_ref,
                woc_ref, wout_ref):
    mixed = [_dot(pooled[gi].astype(BF16), wpool_ref[gi]) for gi in range(len(POOL_WINDOWS))]
    b = jnp.concatenate(mixed, axis=1) * scale_ref[...]
    m = (gate_ref[:, 0:D_MODEL].astype(F32) * _dot(a.astype(BF16), woa_ref[...])
         + gate_ref[:, D_MODEL:2 * D_MODEL].astype(F32) * _dot(b.astype(BF16), wob_ref[...])
         + gate_ref[:, 2 * D_MODEL:].astype(F32) * _dot(c.astype(BF16), woc_ref[...]))
    return x + _dot(m.astype(BF16), wout_ref[...])


def _mix_body(o0, o1, o2, l0, l1, l2, z_ref, halo_ref, qm_ref, mkv_ref, gate_ref, x_ref,
              wpool_ref, scale_ref, woa_ref, wob_ref, woc_ref, wout_ref, out_ref):
    j = pl.program_id(1)
    tm = x_ref.shape[0]
    lses = [l0[...], l1[...], l2[...]]
    mx = jnp.maximum(jnp.maximum(lses[0], lses[1]), lses[2])
    es = [jnp.exp(l - mx) for l in lses]
    a = (es[0] * o0[...] + es[1] * o1[...] + es[2] * o2[...]) / (es[0] + es[1] + es[2])
    z = z_ref[...]
    halo = jnp.where(j == 0, 0.0, halo_ref[...])
    zc = jnp.concatenate([halo, z], axis=0)
    pos = j * tm + lax.broadcasted_iota(jnp.int32, (tm, 1), 0)
    pooled = []
    for gi, kw in enumerate(POOL_WINDOWS):
        cs = slice(gi * POOL_GROUP, (gi + 1) * POOL_GROUP)
        run = zc[:, cs]
        width = 1
        while width < kw:
            run = run[width:] + run[:-width]
            width *= 2
        first = POOL_HALO - (kw - 1)
        cnt = jnp.minimum(kw, pos + 1).astype(F32)
        pooled.append(run[first:first + tm] / cnt - z[:, cs])
    mkv = mkv_ref[...].astype(BF16)
    qm = qm_ref[...]
    cs_out = []
    for h in range(MEM_HEADS):
        hs = slice(h * MEM_HEAD_DIM, (h + 1) * MEM_HEAD_DIM)
        vs = slice(MEM_WIDTH + h * MEM_HEAD_DIM, MEM_WIDTH + (h + 1) * MEM_HEAD_DIM)
        s = _dot_t(qm[:, hs], mkv[:, hs]) * (1.0 / math.sqrt(MEM_HEAD_DIM))
        mm = jnp.max(s, axis=-1, keepdims=True)
        p = jnp.exp(s - mm)
        l = jnp.sum(p, axis=-1, keepdims=True)
        cs_out.append(_dot(p.astype(BF16), mkv[:, vs]) / l)
    c = jnp.concatenate(cs_out, axis=1)
    out_ref[...] = _merge_math(a, pooled, c, gate_ref, x_ref[...], wpool_ref, scale_ref,
                               woa_ref, wob_ref, woc_ref, wout_ref)


def _merge_weight_specs():
    return [_resident((len(POOL_WINDOWS), POOL_GROUP, POOL_GROUP)), _resident((1, POOL_WIDTH)),
            _resident((GROUP_COLS, D_MODEL)), _resident((POOL_WIDTH, D_MODEL)),
            _resident((MEM_WIDTH, D_MODEL)), _resident((D_MODEL, D_MODEL))]


def _mix_prompt(os, lses, z, qm, mkv, gates, x, weights, n, s):
    tm = TOKEN_TILE
    tj = s // tm
    halo_per_tile = tm // POOL_HALO

    def tok(cols):
        return pl.BlockSpec((tm, cols), lambda b, j: (b * tj + j, 0))

    halo = pl.BlockSpec((POOL_HALO, POOL_WIDTH),
                        lambda b, j: (jnp.maximum((b * tj + j) * halo_per_tile - 1, 0), 0))
    in_specs = ([tok(GROUP_COLS)] * 6 + [tok(POOL_WIDTH), halo, tok(MEM_WIDTH),
                pl.BlockSpec((MEM_LEN, 2 * MEM_WIDTH), lambda b, j: (b, 0)),
                tok(N_BRANCH * D_MODEL), tok(D_MODEL)] + _merge_weight_specs())
    return pl.pallas_call(
        _mix_body,
        grid=(n, tj),
        in_specs=in_specs,
        out_specs=tok(D_MODEL),
        out_shape=jax.ShapeDtypeStruct((n * s, D_MODEL), F32),
        compiler_params=_params("parallel", "parallel"),
        name="mix_prompt",
    )(*os, *lses, z, z, qm, mkv, gates, x, *weights)


def _sample_branch_body(q_ref, kvn_ref, c0, c1, c2, b0, b1, b2, bias0_ref, z_ref, st_ref,
                        qm_ref, cm_ref, a_ref, pooled_ref, c_ref):
    outs, lses = [], []
    for g, (cache_ref, bias_ref) in enumerate(((c0, b0), (c1, b1), (c2, b2))):
        q = q_ref[0, g]
        kn = kvn_ref[0, g, 0]
        vn = kvn_ref[0, g, 1]
        s = jnp.sum(cache_ref[0, 0] * q, axis=1, keepdims=True) + bias_ref[...]
        sn = jnp.sum(kn * q, axis=1, keepdims=True) + bias0_ref[g]
        m = jnp.maximum(jnp.max(s, axis=-1, keepdims=True), sn)
        p = jnp.exp(s - m)
        pn = jnp.exp(sn - m)
        l = jnp.sum(p, axis=-1, keepdims=True) + pn
        pv = jnp.sum(cache_ref[0, 1] * p, axis=-1, keepdims=True)
        outs.append((pv + pn * vn) / l)
        lses.append(m + jnp.log(l))
    mx = jnp.maximum(jnp.maximum(lses[0], lses[1]), lses[2])
    es = [jnp.exp(lse - mx) for lse in lses]
    a_ref[0] = ((es[0] * outs[0] + es[1] * outs[1] + es[2] * outs[2])
                / (es[0] + es[1] + es[2]))

    zn = z_ref[0]
    st = st_ref[0]
    pooled = []
    for gi, kw in enumerate(POOL_WINDOWS):
        cs = slice(gi * POOL_GROUP, (gi + 1) * POOL_GROUP)
        tot = jnp.sum(st[POOL_STATE - (kw - 1):, cs], axis=0, keepdims=True) + zn[:, cs]
        pooled.append(tot / float(min(kw, PAST_LEN + 1)) - zn[:, cs])
    pooled_ref[0] = jnp.concatenate(pooled, axis=1)

    qm = qm_ref[0]
    km = cm_ref[0, :, 0]
    vm = cm_ref[0, :, 1]
    s = jnp.sum(km * qm[None], axis=-1, keepdims=True) * (1.0 / math.sqrt(MEM_HEAD_DIM))
    m = jnp.max(s, axis=0)
    p = jnp.exp(s - m[None])
    c_ref[0] = jnp.sum(p * vm, axis=0) / jnp.sum(p, axis=0)


def _sample_branches(q, kvn, caches, biases, bias0, z, state, qm, cache_mem):
    nb = z.shape[0]

    def per_request(shape):
        zeros = (0,) * len(shape)
        return pl.BlockSpec((1,) + shape, lambda b: (b,) + zeros)

    in_specs = ([per_request(q.shape[1:]), per_request(kvn.shape[1:])]
                + [per_request(c.shape[1:]) for c in caches]
                + [_resident(b.shape) for b in biases] + [
        _resident(bias0.shape),
        per_request((1, POOL_WIDTH)), per_request((POOL_STATE, POOL_WIDTH)),
        per_request((MEM_HEADS, MEM_HEAD_DIM)),
        per_request((MEM_LEN, 2, MEM_HEADS, MEM_HEAD_DIM))])
    return pl.pallas_call(
        _sample_branch_body,
        grid=(nb,),
        in_specs=in_specs,
        out_specs=[per_request((HEADS, HEAD_DIM, 1)), per_request((1, POOL_WIDTH)),
                   per_request((MEM_HEADS, MEM_HEAD_DIM))],
        out_shape=[jax.ShapeDtypeStruct((nb, HEADS, HEAD_DIM, 1), F32),
                   jax.ShapeDtypeStruct((nb, 1, POOL_WIDTH), F32),
                   jax.ShapeDtypeStruct((nb, MEM_HEADS, MEM_HEAD_DIM), F32)],
        compiler_params=_params("parallel"),
        name="sample_branches",
    )(q, kvn, *caches, *biases, bias0, z, state, qm, cache_mem)


def _sample_merge_body(a_ref, pooled_ref, c_ref, gate_ref, x_ref, wpool_ref, scale_ref,
                       woa_ref, wob_ref, woc_ref, wout_ref, out_ref):
    pooled_all = pooled_ref[...]
    pooled = [pooled_all[:, gi * POOL_GROUP:(gi + 1) * POOL_GROUP]
              for gi in range(len(POOL_WINDOWS))]
    out_ref[...] = _merge_math(a_ref[...], pooled, c_ref[...], gate_ref, x_ref[...],
                               wpool_ref, scale_ref, woa_ref, wob_ref, woc_ref, wout_ref)


def _sample_merge(a, pooled, c, gates, x, weights):
    nb = x.shape[0]
    full = lambda cols: _resident((nb, cols))
    return pl.pallas_call(
        _sample_merge_body,
        grid=(1,),
        in_specs=[full(GROUP_COLS), full(POOL_WIDTH), full(MEM_WIDTH),
                  full(N_BRANCH * D_MODEL), full(D_MODEL)] + _merge_weight_specs(),
        out_specs=pl.BlockSpec((nb, D_MODEL), lambda i: (0, 0)),
        out_shape=jax.ShapeDtypeStruct((nb, D_MODEL), F32),
        compiler_params=_params("arbitrary"),
        name="sample_merge",
    )(a, pooled, c, gates, x, *weights)


def _rel_bucket(n):
    max_exact = N_BUCKETS // 2
    nf = jnp.maximum(n, 1).astype(F32)
    large = max_exact + (jnp.log(nf / max_exact) / math.log(MAX_DISTANCE / max_exact)
                         * (N_BUCKETS - max_exact)).astype(jnp.int32)
    large = jnp.minimum(large, N_BUCKETS - 1)
    return jnp.where(n < max_exact, n, large)


def _stride_bias(rel_bias, g, dil):
    j = jnp.arange(STRIDES + 1, dtype=jnp.int32)
    return rel_bias[_rel_bucket(j * dil)][:, g * HEADS:(g + 1) * HEADS].astype(F32)


def _band_row(bias_j):
    row = jnp.concatenate([bias_j[::-1], jnp.full((STRIDES - 1, HEADS), NEG_INF, F32)], axis=0)
    return row.T.reshape(HEADS, 1, 2 * STRIDES)


def _cache_bias(bias_j, dil):
    on_grid = bias_j[STRIDES:0:-1].T
    full = jnp.full((HEADS, STRIDES, dil), NEG_INF, F32).at[:, :, 0].set(on_grid)
    return full.reshape(HEADS, 1, STRIDES * dil)


def _in_weight(w_in_l):
    width = N_GROUPS * GROUP_COLS
    parts = []
    for g in range(N_GROUPS):
        for t in range(3):
            lo = t * width + g * GROUP_COLS
            blk = w_in_l[:, lo:lo + GROUP_COLS]
            parts.append(blk * (1.0 / math.sqrt(HEAD_DIM)) if t == 0 else blk)
    parts.append(w_in_l[:, 3 * width:])
    return jnp.concatenate(parts, axis=1).astype(BF16)


def kernel(x_prompt, x_sample, cache_win0_kv, cache_win1_kv, cache_win2_kv, state_pool, cache_mem_kv, mem_prompt, rel_bias, g_ffn1, w1_gate, w1_up, w1_down, g_mix, w_in, w_pool, pool_scale, g_mem, w_mem_kv, w_oa, w_ob, w_oc, w_out, g_ffn2, w2_gate, w2_up, w2_down, g_final):
    n, s, _ = x_prompt.shape
    nb = x_sample.shape[0]
    depth = g_ffn1.shape[0]
    win_caches = (cache_win0_kv, cache_win1_kv, cache_win2_kv)
    bias_js = [_stride_bias(rel_bias, g, dil) for g, (_, dil) in enumerate(DIL_GROUPS)]
    band = [_band_row(b) for b in bias_js]
    bias_cache = [_cache_bias(b, dil) for b, (_, dil) in zip(bias_js, DIL_GROUPS)]
    bias_new = jnp.stack([b[0] for b in bias_js]).reshape(N_GROUPS, HEADS, 1, 1)
    gfin = g_final.reshape(1, D_MODEL)

    xp = x_prompt.reshape(n * s, D_MODEL)
    xs = x_sample.reshape(nb, D_MODEL)
    st_p = [[] for _ in range(5)]
    st_s = [[] for _ in range(4)]
    for l in range(depth):
        last = l == depth - 1
        bf = lambda w: w[l].astype(BF16)
        vec = lambda v: v[l].reshape(1, -1)
        w1 = (bf(w1_gate), bf(w1_up), bf(w1_down))
        w2 = (bf(w2_gate), bf(w2_up), bf(w2_down))
        win_l = _in_weight(w_in[l])
        merge_w = (bf(w_pool), vec(pool_scale), bf(w_oa), bf(w_ob), bf(w_oc), bf(w_out))

        xp = _ffn(xp, vec(g_ffn1), *w1)
        res = _inproj(xp, vec(g_mix), win_l, n, s, [min(w, s) for w, _ in DIL_GROUPS], BF16)
        qkvs, kvwins, z, qm, gates = res[0:3], res[3:6], res[6], res[7], res[8]
        os, lses = [], []
        for g, (_, dil) in enumerate(DIL_GROUPS):
            o, lse = _attn(qkvs[g], band[g], n, s, dil)
            os.append(o)
            lses.append(lse)
        mkv = _memkv(mem_prompt.reshape(n * MEM_LEN, D_MODEL), vec(g_mem), bf(w_mem_kv))
        xp = _mix_prompt(os, lses, z, qm, mkv, gates, xp, merge_w, n, s)
        xp = _ffn(xp, vec(g_ffn2), *w2, g_final=gfin if last else None)
        for g, (win, _) in enumerate(DIL_GROUPS):
            st_p[g].append(kvwins[g].reshape(n, min(win, s), 2, HEADS, HEAD_DIM))
        st_p[3].append(z.reshape(n, s, POOL_WIDTH)[:, s - POOL_STATE:])
        st_p[4].append(mkv.reshape(n, MEM_LEN, 2, MEM_HEADS, MEM_HEAD_DIM))

        xs = _ffn(xs, vec(g_ffn1), *w1)
        res = _inproj(xs, vec(g_mix), win_l, 1, nb, [nb] * N_GROUPS, F32)
        qkvs, kvns, z, qm, gates = res[0:3], res[3:6], res[6], res[7], res[8]
        q = jnp.stack([t[:, :GROUP_COLS].reshape(nb, HEADS, HEAD_DIM, 1) for t in qkvs], axis=1)
        kvns = [t.reshape(nb, 2, HEADS, HEAD_DIM) for t in kvns]
        a, pooled, c = _sample_branches(
            q, jnp.stack(kvns, axis=1)[..., None],
            [jnp.transpose(cw[l], (0, 2, 3, 4, 1)) for cw in win_caches], bias_cache, bias_new,
            z.reshape(nb, 1, POOL_WIDTH), state_pool[l],
            qm.reshape(nb, MEM_HEADS, MEM_HEAD_DIM), cache_mem_kv[l])
        xs = _sample_merge(a.reshape(nb, -1), pooled.reshape(nb, -1), c.reshape(nb, -1),
                           gates, xs, merge_w)
        xs = _ffn(xs, vec(g_ffn2), *w2, g_final=gfin if last else None)
        for g in range(N_GROUPS):
            st_s[g].append(kvns[g].reshape(nb, 1, 2, HEADS, HEAD_DIM))
        st_s[3].append(z.reshape(nb, 1, POOL_WIDTH))

    y_prompt = xp.reshape(n, s, D_MODEL)
    y_sample = xs.reshape(nb, 1, D_MODEL)
    stack = lambda ts: jnp.stack(ts, axis=0)
    return (y_prompt, y_sample, stack(st_p[0]), stack(st_p[1]), stack(st_p[2]), stack(st_p[3]),
            stack(st_p[4]), stack(st_s[0]), stack(st_s[1]), stack(st_s[2]), stack(st_s[3]))
```

```python
import functools
import math

import jax
import jax.numpy as jnp
from jax import lax
from jax.experimental import pallas as pl
from jax.experimental.pallas import tpu as pltpu

F32 = jnp.float32
BF16 = jnp.bfloat16

D_MODEL = 1024
D_FF = 2816
LANES = 128
HEAD_DIM = 64
HEADS = 4
DIL_GROUPS = ((128, 1), (512, 4), (2048, 16))
N_GROUPS = 3
GROUP_COLS = HEADS * HEAD_DIM
QKV_COLS = 3 * GROUP_COLS
STRIDES = 128
POOL_WINDOWS = (2, 4, 8, 16)
POOL_GROUP = 128
POOL_WIDTH = 512
POOL_STATE = 15
POOL_HALO = 16
MEM_LEN = 256
MEM_HEADS = 4
MEM_HEAD_DIM = 128
MEM_WIDTH = 512
N_BUCKETS = 32
MAX_DISTANCE = 2048
N_BRANCH = 3
EPS = 1e-6
NEG_INF = -1e30
LOG2E = math.log2(math.e)
LN2 = math.log(2.0)
PAST_LEN = 8192

V7X_VMEM_LIMIT_BYTES = 56 * 1024 * 1024
TOKEN_TILE = 512


def _params(*sem):
    return pltpu.CompilerParams(dimension_semantics=sem,
                                vmem_limit_bytes=V7X_VMEM_LIMIT_BYTES)


def _resident(shape):
    zeros = (0,) * len(shape)
    return pl.BlockSpec(shape, lambda *_: zeros, pipeline_mode=pl.Buffered(1))


def _rms(x, g):
    return x * lax.rsqrt(jnp.mean(x * x, axis=-1, keepdims=True) + EPS) * g


def _dot(a, b):
    return jnp.dot(a, b, preferred_element_type=F32)


def _dot_t(a, b):
    return lax.dot_general(a, b, (((1,), (1,)), ((), ())), preferred_element_type=F32)


def _ffn_body(x_ref, g_ref, wg_ref, wu_ref, wd_ref, *rest, final):
    o_ref = rest[-1]
    x = x_ref[...]
    h = _rms(x, g_ref[...]).astype(BF16)
    a = _dot(h, wg_ref[...])
    b = _dot(h, wu_ref[...])
    act = (a * jax.nn.sigmoid(a) * b).astype(BF16)
    y = x + 0.5 * _dot(act, wd_ref[...])
    if final:
        y = _rms(y, rest[0][...])
    o_ref[...] = y


def _ffn(x, g, wg, wu, wd, g_final=None):
    m = x.shape[0]
    tm = min(TOKEN_TILE, m)
    final = g_final is not None
    row = pl.BlockSpec((tm, D_MODEL), lambda i: (i, 0))
    in_specs = [row, _resident((1, D_MODEL)), _resident((D_MODEL, D_FF)),
                _resident((D_MODEL, D_FF)), _resident((D_FF, D_MODEL))]
    args = [x, g, wg, wu, wd]
    if final:
        in_specs.append(_resident((1, D_MODEL)))
        args.append(g_final)
    return pl.pallas_call(
        functools.partial(_ffn_body, final=final),
        grid=(m // tm,),
        in_specs=in_specs,
        out_specs=row,
        out_shape=jax.ShapeDtypeStruct((m, D_MODEL), F32),
        compiler_params=_params("parallel"),
        name="ffn_final" if final else "ffn",
    )(*args)


Z_OFF = N_GROUPS * QKV_COLS
QM_OFF = Z_OFF + POOL_WIDTH
GATE_OFF = QM_OFF + MEM_WIDTH
IN_COLS = GATE_OFF + N_BRANCH * D_MODEL


def _inproj_body(x_ref, g_ref, w_ref, qkv0, qkv1, qkv2, kv0, kv1, kv2,
                 z_ref, qm_ref, gate_ref, u_scr, kv_scr, *, q_scale):
    uf = _rms(x_ref[...], g_ref[...])
    u = uf.astype(BF16)
    tm = u.shape[0]
    if any(ref.shape[1] > 1 for ref in (qkv0, qkv1, qkv2)):
        for c in range(D_MODEL // LANES):
            u_scr[c] = uf[:, c * LANES:(c + 1) * LANES]
    for g, (qkv_ref, kv_ref) in enumerate(((qkv0, kv0), (qkv1, kv1), (qkv2, kv2))):
        dil, per_class = qkv_ref.shape[1], qkv_ref.shape[2]
        w_g = w_ref[:, g * QKV_COLS:(g + 1) * QKV_COLS]
        rows = kv_ref.shape[1]
        def put_qkv(r, blk):
            q = blk[:, :GROUP_COLS] if q_scale == 1.0 else blk[:, :GROUP_COLS] * q_scale
            qkv_ref[0, r, :, 0:GROUP_COLS] = q.astype(qkv_ref.dtype)
            qkv_ref[0, r, :, GROUP_COLS:] = blk[:, GROUP_COLS:].astype(qkv_ref.dtype)

        if dil == 1:
            p = _dot(u, w_g)
            put_qkv(0, p)
            kv_ref[0] = p[tm - rows:, GROUP_COLS:]
        else:
            u_cls = jnp.concatenate(
                [jnp.concatenate([u_scr[c, pl.ds(r, per_class, stride=dil), :]
                                  for c in range(D_MODEL // LANES)], axis=1)
                 for r in range(dil)], axis=0)
            p = _dot(u_cls.astype(BF16), w_g)
            assert rows == tm
            for r in range(dil):
                blk = p[r * per_class:(r + 1) * per_class]
                put_qkv(r, blk)
                for c in range(2 * GROUP_COLS // LANES):
                    lo = GROUP_COLS + c * LANES
                    kv_scr[c, pl.ds(r, per_class, stride=dil), :] = blk[:, lo:lo + LANES]
            kv_ref[0] = jnp.concatenate(
                [kv_scr[c] for c in range(2 * GROUP_COLS // LANES)], axis=1)
    z_ref[...] = _dot(u, w_ref[:, Z_OFF:QM_OFF])
    qm_ref[...] = _dot(u, w_ref[:, QM_OFF:GATE_OFF]).astype(qm_ref.dtype)
    for b in range(N_BRANCH):
        lo = GATE_OFF + b * D_MODEL
        p = _dot(u, w_ref[:, lo:lo + D_MODEL])
        gate_ref[:, b * D_MODEL:(b + 1) * D_MODEL] = jax.nn.sigmoid(p).astype(BF16)


def _inproj(x, g, w, n, s, windows, dils, q_dtype, q_scale):
    tm = min(TOKEN_TILE, s)
    tj = s // tm

    def tok(cols):
        return pl.BlockSpec((tm, cols), lambda b, j: (b * tj + j, 0))

    qkv_specs = [pl.BlockSpec((1, d, tm // d, QKV_COLS), lambda b, j: (b, 0, j, 0)) for d in dils]
    qkv_shapes = [jax.ShapeDtypeStruct((n, d, s // d, QKV_COLS), q_dtype) for d in dils]

    kv_specs, kv_shapes = [], []
    for win in windows:
        rows = min(win, tm)
        assert tm % rows == 0 and win % rows == 0
        first = (s - win) // tm
        if win >= tm:
            spec = pl.BlockSpec((1, rows, 2 * GROUP_COLS),
                                lambda b, j, first=first: (b, jnp.maximum(j - first, 0), 0))
        else:
            spec = pl.BlockSpec((1, rows, 2 * GROUP_COLS), lambda b, j: (b, 0, 0))
        kv_specs.append(spec)
        kv_shapes.append(jax.ShapeDtypeStruct((n, win, 2 * GROUP_COLS), F32))
    m = n * s
    out_shape = (qkv_shapes + kv_shapes + [
        jax.ShapeDtypeStruct((m, POOL_WIDTH), F32),
        jax.ShapeDtypeStruct((m, MEM_WIDTH), q_dtype),
        jax.ShapeDtypeStruct((m, N_BRANCH * D_MODEL), BF16)])
    out_specs = (qkv_specs + kv_specs
                 + [tok(POOL_WIDTH), tok(MEM_WIDTH), tok(N_BRANCH * D_MODEL)])
    return pl.pallas_call(
        functools.partial(_inproj_body, q_scale=q_scale),
        grid=(n, tj),
        in_specs=[tok(D_MODEL), _resident((1, D_MODEL)), _resident((D_MODEL, IN_COLS))],
        out_specs=out_specs,
        out_shape=out_shape,
        scratch_shapes=[pltpu.VMEM((D_MODEL // LANES, tm, LANES), F32),
                        pltpu.VMEM((2 * GROUP_COLS // LANES, tm, LANES), F32)],
        compiler_params=_params("arbitrary", "arbitrary"),
        name="inproj",
    )(x, g, w)


ATTN_CHUNKS = 16
HEAD_LANES = HEADS * STRIDES


def _band_body(row_ref, band_ref):
    for g in range(N_GROUPS):
        for h in range(HEADS):
            rows = jnp.broadcast_to(row_ref[g, h] * LOG2E, (STRIDES, 2 * STRIDES))
            band = pltpu.roll(rows, 0, 1, stride=1, stride_axis=0).T
            band_ref[g, :, h * STRIDES:(h + 1) * STRIDES] = band


def _bands(rows):
    return pl.pallas_call(
        _band_body,
        grid=(1,),
        in_specs=[_resident(rows.shape)],
        out_specs=pl.BlockSpec((N_GROUPS, 2 * STRIDES, HEAD_LANES), lambda i: (0, 0, 0)),
        out_shape=jax.ShapeDtypeStruct((N_GROUPS, 2 * STRIDES, HEAD_LANES), F32),
        compiler_params=_params("arbitrary"),
        name="bands",
    )(rows)


def _attn_body(qkv_ref, prev_ref, band_ref, o_ref, lse_ref, *, dil, chunks):
    first = pl.program_id(1) == 0
    lane_head = lax.broadcasted_iota(jnp.int32, (1, GROUP_COLS), 1) // HEAD_DIM
    for r in range(dil):
        k_all = jnp.concatenate([prev_ref[0, r, :, GROUP_COLS:2 * GROUP_COLS],
                                 qkv_ref[0, r, :, GROUP_COLS:2 * GROUP_COLS]], axis=0)
        v_all = jnp.concatenate([prev_ref[0, r, :, 2 * GROUP_COLS:],
                                 qkv_ref[0, r, :, 2 * GROUP_COLS:]], axis=0)
        vt_all = v_all.astype(F32).T.astype(BF16)
        for i in range(chunks):
            q = qkv_ref[0, r, i * STRIDES:(i + 1) * STRIDES, 0:GROUP_COLS]
            qm = jnp.concatenate(
                [jnp.where(lane_head == h, q, jnp.zeros_like(q)) for h in range(HEADS)], axis=0)
            st = _dot_t(k_all[i * STRIDES:(i + 2) * STRIDES], qm)
            vt2 = vt_all[:, i * STRIDES:(i + 2) * STRIDES]
            o_parts, lse_parts = [], []
            for h in range(HEADS):
                hl = slice(h * STRIDES, (h + 1) * STRIDES)
                s_h = st[:, hl] + band_ref[:, hl]
                if i == 0:
                    s_h = jnp.concatenate(
                        [jnp.where(first, NEG_INF, s_h[:STRIDES]), s_h[STRIDES:]], axis=0)
                m = jnp.max(s_h, axis=0, keepdims=True)
                e = jnp.exp2(s_h - m)
                l = jnp.sum(e, axis=0, keepdims=True)
                ot = _dot(vt2[h * HEAD_DIM:(h + 1) * HEAD_DIM], e.astype(BF16))
                o_parts.append(ot * (1.0 / l))
                lse_parts.append(jnp.broadcast_to(m * LN2 + jnp.log(l), (HEAD_DIM, STRIDES)))
            rows = pl.ds(i * STRIDES * dil + r, STRIDES, stride=dil)
            o_rows = jnp.concatenate(o_parts, axis=0).T
            lse_rows = jnp.concatenate(lse_parts, axis=0).T
            for c in range(GROUP_COLS // LANES):
                o_ref[c, rows, :] = o_rows[:, c * LANES:(c + 1) * LANES]
                lse_ref[c, rows, :] = lse_rows[:, c * LANES:(c + 1) * LANES]


def _attn(qkv, bands, g, n, s, dil):
    chunks = ATTN_CHUNKS // dil
    span = chunks * STRIDES
    steps = s // (span * dil)
    slabs = GROUP_COLS // LANES
    out_spec = pl.BlockSpec((slabs, span * dil, LANES), lambda b, k: (0, b * steps + k, 0))
    out_sds = jax.ShapeDtypeStruct((slabs, n * s, LANES), F32)
    return pl.pallas_call(
        functools.partial(_attn_body, dil=dil, chunks=chunks),
        grid=(n, steps),
        in_specs=[pl.BlockSpec((1, dil, span, QKV_COLS), lambda b, k: (b, 0, k, 0)),
                  pl.BlockSpec((1, dil, STRIDES, QKV_COLS),
                               lambda b, k: (b, 0, jnp.maximum(k * chunks - 1, 0), 0)),
                  pl.BlockSpec((None, 2 * STRIDES, HEAD_LANES), lambda b, k: (g, 0, 0))],
        out_specs=[out_spec, out_spec],
        out_shape=[out_sds, out_sds],
        compiler_params=_params("parallel", "parallel"),
        name="attn_d%d" % dil,
    )(qkv, qkv, bands)


def _memkv_body(mem_ref, g_ref, w_ref, o_ref):
    o_ref[...] = _dot(_rms(mem_ref[...], g_ref[...]).astype(BF16), w_ref[...])


def _memkv(mem, g, w):
    m = mem.shape[0]
    row = pl.BlockSpec((MEM_LEN, D_MODEL), lambda i: (i, 0))
    return pl.pallas_call(
        _memkv_body,
        grid=(m // MEM_LEN,),
        in_specs=[row, _resident((1, D_MODEL)), _resident((D_MODEL, 2 * MEM_WIDTH))],
        out_specs=pl.BlockSpec((MEM_LEN, 2 * MEM_WIDTH), lambda i: (i, 0)),
        out_shape=jax.ShapeDtypeStruct((m, 2 * MEM_WIDTH), F32),
        compiler_params=_params("parallel"),
        name="memkv",
    )(mem, g, w)


def _merge_math(a, pooled, c, gate_ref, x, wpool_ref, scale_ref, woa_ref, wob_ref,
                woc_ref, wout_ref):
    mixed = [_dot(pooled[gi].astype(BF16), wpool_ref[gi]) for gi in range(len(POOL_WINDOWS))]
    b = jnp.concatenate(mixed, axis=1) * scale_ref[...]
    m = (gate_ref[:, 0:D_MODEL].astype(F32) * _dot(a.astype(BF16), woa_ref[...])
         + gate_ref[:, D_MODEL:2 * D_MODEL].astype(F32) * _dot(b.astype(BF16), wob_ref[...])
         + gate_ref[:, 2 * D_MODEL:].astype(F32) * _dot(c.astype(BF16), woc_ref[...]))
    return x + _dot(m.astype(BF16), wout_ref[...])


def _mix_body(o0, o1, o2, l0, l1, l2, z_ref, halo_ref, qm_ref, mkv_ref, gate_ref, x_ref,
              wpool_ref, scale_ref, woa_ref, wob_ref, woc_ref, wout_ref, out_ref):
    j = pl.program_id(1)
    tm = x_ref.shape[0]
    unslab = lambda ref: jnp.concatenate([ref[c] for c in range(ref.shape[0])], axis=1)
    lses = [unslab(l0), unslab(l1), unslab(l2)]
    mx = jnp.maximum(jnp.maximum(lses[0], lses[1]), lses[2])
    es = [jnp.exp(l - mx) for l in lses]
    a = ((es[0] * unslab(o0) + es[1] * unslab(o1) + es[2] * unslab(o2))
         / (es[0] + es[1] + es[2]))
    z = z_ref[...]
    halo = jnp.where(j == 0, 0.0, halo_ref[...])
    zc = jnp.concatenate([halo, z], axis=0)
    pos = j * tm + lax.broadcasted_iota(jnp.int32, (tm, 1), 0)
    pooled = []
    for gi, kw in enumerate(POOL_WINDOWS):
        cs = slice(gi * POOL_GROUP, (gi + 1) * POOL_GROUP)
        run = zc[:, cs]
        width = 1
        while width < kw:
            run = run[width:] + run[:-width]
            width *= 2
        first = POOL_HALO - (kw - 1)
        cnt = jnp.minimum(kw, pos + 1).astype(F32)
        pooled.append(run[first:first + tm] / cnt - z[:, cs])
    mkv = mkv_ref[...].astype(BF16)
    qm = qm_ref[...]
    cs_out = []
    for h in range(MEM_HEADS):
        hs = slice(h * MEM_HEAD_DIM, (h + 1) * MEM_HEAD_DIM)
        vs = slice(MEM_WIDTH + h * MEM_HEAD_DIM, MEM_WIDTH + (h + 1) * MEM_HEAD_DIM)
        s = _dot_t(qm[:, hs], mkv[:, hs]) * (1.0 / math.sqrt(MEM_HEAD_DIM))
        mm = jnp.max(s, axis=-1, keepdims=True)
        p = jnp.exp(s - mm)
        l = jnp.sum(p, axis=-1, keepdims=True)
        cs_out.append(_dot(p.astype(BF16), mkv[:, vs]) / l)
    c = jnp.concatenate(cs_out, axis=1)
    out_ref[...] = _merge_math(a, pooled, c, gate_ref, x_ref[...], wpool_ref, scale_ref,
                               woa_ref, wob_ref, woc_ref, wout_ref)


def _merge_weight_specs():
    return [_resident((len(POOL_WINDOWS), POOL_GROUP, POOL_GROUP)), _resident((1, POOL_WIDTH)),
            _resident((GROUP_COLS, D_MODEL)), _resident((POOL_WIDTH, D_MODEL)),
            _resident((MEM_WIDTH, D_MODEL)), _resident((D_MODEL, D_MODEL))]


def _mix_prompt(os, lses, z, qm, mkv, gates, x, weights, n, s):
    tm = TOKEN_TILE
    tj = s // tm
    halo_per_tile = tm // POOL_HALO

    def tok(cols):
        return pl.BlockSpec((tm, cols), lambda b, j: (b * tj + j, 0))

    halo = pl.BlockSpec((POOL_HALO, POOL_WIDTH),
                        lambda b, j: (jnp.maximum((b * tj + j) * halo_per_tile - 1, 0), 0))
    slab = pl.BlockSpec((GROUP_COLS // LANES, tm, LANES), lambda b, j: (0, b * tj + j, 0))
    in_specs = ([slab] * 6 + [tok(POOL_WIDTH), halo, tok(MEM_WIDTH),
                pl.BlockSpec((MEM_LEN, 2 * MEM_WIDTH), lambda b, j: (b, 0)),
                tok(N_BRANCH * D_MODEL), tok(D_MODEL)] + _merge_weight_specs())
    return pl.pallas_call(
        _mix_body,
        grid=(n, tj),
        in_specs=in_specs,
        out_specs=tok(D_MODEL),
        out_shape=jax.ShapeDtypeStruct((n * s, D_MODEL), F32),
        compiler_params=_params("parallel", "parallel"),
        name="mix_prompt",
    )(*os, *lses, z, z, qm, mkv, gates, x, *weights)


def _sample_branch_body(q_ref, kvn_ref, c0, c1, c2, b0, b1, b2, bias0_ref, z_ref, st_ref,
                        qm_ref, cm_ref, a_ref, pooled_ref, c_ref):
    outs, lses = [], []
    for g, (cache_ref, bias_ref) in enumerate(((c0, b0), (c1, b1), (c2, b2))):
        q = q_ref[0, g]
        kn = kvn_ref[0, g, 0]
        vn = kvn_ref[0, g, 1]
        s = jnp.sum(cache_ref[0, 0] * q, axis=1, keepdims=True) + bias_ref[...]
        sn = jnp.sum(kn * q, axis=1, keepdims=True) + bias0_ref[g]
        m = jnp.maximum(jnp.max(s, axis=-1, keepdims=True), sn)
        p = jnp.exp(s - m)
        pn = jnp.exp(sn - m)
        l = jnp.sum(p, axis=-1, keepdims=True) + pn
        pv = jnp.sum(cache_ref[0, 1] * p, axis=-1, keepdims=True)
        outs.append((pv + pn * vn) / l)
        lses.append(m + jnp.log(l))
    mx = jnp.maximum(jnp.maximum(lses[0], lses[1]), lses[2])
    es = [jnp.exp(lse - mx) for lse in lses]
    a_ref[0] = ((es[0] * outs[0] + es[1] * outs[1] + es[2] * outs[2])
                / (es[0] + es[1] + es[2]))

    zn = z_ref[0]
    st = st_ref[0]
    pooled = []
    for gi, kw in enumerate(POOL_WINDOWS):
        cs = slice(gi * POOL_GROUP, (gi + 1) * POOL_GROUP)
        tot = jnp.sum(st[POOL_STATE - (kw - 1):, cs], axis=0, keepdims=True) + zn[:, cs]
        pooled.append(tot / float(min(kw, PAST_LEN + 1)) - zn[:, cs])
    pooled_ref[0] = jnp.concatenate(pooled, axis=1)

    qm = qm_ref[0]
    km = cm_ref[0, :, 0]
    vm = cm_ref[0, :, 1]
    s = jnp.sum(km * qm[None], axis=-1, keepdims=True) * (1.0 / math.sqrt(MEM_HEAD_DIM))
    m = jnp.max(s, axis=0)
    p = jnp.exp(s - m[None])
    c_ref[0] = jnp.sum(p * vm, axis=0) / jnp.sum(p, axis=0)


def _sample_branches(q, kvn, caches, biases, bias0, z, state, qm, cache_mem):
    nb = z.shape[0]

    def per_request(shape):
        zeros = (0,) * len(shape)
        return pl.BlockSpec((1,) + shape, lambda b: (b,) + zeros)

    in_specs = ([per_request(q.shape[1:]), per_request(kvn.shape[1:])]
                + [per_request(c.shape[1:]) for c in caches]
                + [_resident(b.shape) for b in biases] + [
        _resident(bias0.shape),
        per_request((1, POOL_WIDTH)), per_request((POOL_STATE, POOL_WIDTH)),
        per_request((MEM_HEADS, MEM_HEAD_DIM)),
        per_request((MEM_LEN, 2, MEM_HEADS, MEM_HEAD_DIM))])
    return pl.pallas_call(
        _sample_branch_body,
        grid=(nb,),
        in_specs=in_specs,
        out_specs=[per_request((HEADS, HEAD_DIM, 1)), per_request((1, POOL_WIDTH)),
                   per_request((MEM_HEADS, MEM_HEAD_DIM))],
        out_shape=[jax.ShapeDtypeStruct((nb, HEADS, HEAD_DIM, 1), F32),
                   jax.ShapeDtypeStruct((nb, 1, POOL_WIDTH), F32),
                   jax.ShapeDtypeStruct((nb, MEM_HEADS, MEM_HEAD_DIM), F32)],
        compiler_params=_params("parallel"),
        name="sample_branches",
    )(q, kvn, *caches, *biases, bias0, z, state, qm, cache_mem)


def _sample_merge_body(a_ref, pooled_ref, c_ref, gate_ref, x_ref, wpool_ref, scale_ref,
                       woa_ref, wob_ref, woc_ref, wout_ref, out_ref):
    pooled_all = pooled_ref[...]
    pooled = [pooled_all[:, gi * POOL_GROUP:(gi + 1) * POOL_GROUP]
              for gi in range(len(POOL_WINDOWS))]
    out_ref[...] = _merge_math(a_ref[...], pooled, c_ref[...], gate_ref, x_ref[...],
                               wpool_ref, scale_ref, woa_ref, wob_ref, woc_ref, wout_ref)


def _sample_merge(a, pooled, c, gates, x, weights):
    nb = x.shape[0]
    full = lambda cols: _resident((nb, cols))
    return pl.pallas_call(
        _sample_merge_body,
        grid=(1,),
        in_specs=[full(GROUP_COLS), full(POOL_WIDTH), full(MEM_WIDTH),
                  full(N_BRANCH * D_MODEL), full(D_MODEL)] + _merge_weight_specs(),
        out_specs=pl.BlockSpec((nb, D_MODEL), lambda i: (0, 0)),
        out_shape=jax.ShapeDtypeStruct((nb, D_MODEL), F32),
        compiler_params=_params("arbitrary"),
        name="sample_merge",
    )(a, pooled, c, gates, x, *weights)


def _rel_bucket(n):
    max_exact = N_BUCKETS // 2
    nf = jnp.maximum(n, 1).astype(F32)
    large = max_exact + (jnp.log(nf / max_exact) / math.log(MAX_DISTANCE / max_exact)
                         * (N_BUCKETS - max_exact)).astype(jnp.int32)
    large = jnp.minimum(large, N_BUCKETS - 1)
    return jnp.where(n < max_exact, n, large)


def _stride_bias(rel_bias, g, dil):
    j = jnp.arange(STRIDES + 1, dtype=jnp.int32)
    return rel_bias[_rel_bucket(j * dil)][:, g * HEADS:(g + 1) * HEADS].astype(F32)


def _band_row(bias_j):
    row = jnp.concatenate([bias_j[::-1], jnp.full((STRIDES - 1, HEADS), NEG_INF, F32)], axis=0)
    return row.T.reshape(HEADS, 1, 2 * STRIDES)


def _cache_bias(bias_j, dil):
    on_grid = bias_j[STRIDES:0:-1].T
    full = jnp.full((HEADS, STRIDES, dil), NEG_INF, F32).at[:, :, 0].set(on_grid)
    return full.reshape(HEADS, 1, STRIDES * dil)


def _in_weight(w_in_l):
    width = N_GROUPS * GROUP_COLS
    parts = []
    for g in range(N_GROUPS):
        for t in range(3):
            lo = t * width + g * GROUP_COLS
            blk = w_in_l[:, lo:lo + GROUP_COLS]
            parts.append(blk * (1.0 / math.sqrt(HEAD_DIM)) if t == 0 else blk)
    parts.append(w_in_l[:, 3 * width:])
    return jnp.concatenate(parts, axis=1).astype(BF16)


def kernel(x_prompt, x_sample, cache_win0_kv, cache_win1_kv, cache_win2_kv, state_pool, cache_mem_kv, mem_prompt, rel_bias, g_ffn1, w1_gate, w1_up, w1_down, g_mix, w_in, w_pool, pool_scale, g_mem, w_mem_kv, w_oa, w_ob, w_oc, w_out, g_ffn2, w2_gate, w2_up, w2_down, g_final):
    n, s, _ = x_prompt.shape
    nb = x_sample.shape[0]
    depth = g_ffn1.shape[0]
    win_caches = (cache_win0_kv, cache_win1_kv, cache_win2_kv)
    bias_js = [_stride_bias(rel_bias, g, dil) for g, (_, dil) in enumerate(DIL_GROUPS)]
    bands = _bands(jnp.stack([_band_row(b) for b in bias_js]))
    bias_cache = [_cache_bias(b, dil) for b, (_, dil) in zip(bias_js, DIL_GROUPS)]
    bias_new = jnp.stack([b[0] for b in bias_js]).reshape(N_GROUPS, HEADS, 1, 1)
    gfin = g_final.reshape(1, D_MODEL)

    xp = x_prompt.reshape(n * s, D_MODEL)
    xs = x_sample.reshape(nb, D_MODEL)
    st_p = [[] for _ in range(5)]
    st_s = [[] for _ in range(4)]
    for l in range(depth):
        last = l == depth - 1
        bf = lambda w: w[l].astype(BF16)
        vec = lambda v: v[l].reshape(1, -1)
        w1 = (bf(w1_gate), bf(w1_up), bf(w1_down))
        w2 = (bf(w2_gate), bf(w2_up), bf(w2_down))
        win_l = _in_weight(w_in[l])
        merge_w = (bf(w_pool), vec(pool_scale), bf(w_oa), bf(w_ob), bf(w_oc), bf(w_out))

        xp = _ffn(xp, vec(g_ffn1), *w1)
        res = _inproj(xp, vec(g_mix), win_l, n, s, [min(w, s) for w, _ in DIL_GROUPS],
                      [d for _, d in DIL_GROUPS], BF16, LOG2E)
        qkvs, kvwins, z, qm, gates = res[0:3], res[3:6], res[6], res[7], res[8]
        os, lses = [], []
        for g, (_, dil) in enumerate(DIL_GROUPS):
            o, lse = _attn(qkvs[g], bands, g, n, s, dil)
            os.append(o)
            lses.append(lse)
        mkv = _memkv(mem_prompt.reshape(n * MEM_LEN, D_MODEL), vec(g_mem), bf(w_mem_kv))
        xp = _mix_prompt(os, lses, z, qm, mkv, gates, xp, merge_w, n, s)
        xp = _ffn(xp, vec(g_ffn2), *w2, g_final=gfin if last else None)
        for g, (win, _) in enumerate(DIL_GROUPS):
            st_p[g].append(kvwins[g].reshape(n, min(win, s), 2, HEADS, HEAD_DIM))
        st_p[3].append(z.reshape(n, s, POOL_WIDTH)[:, s - POOL_STATE:])
        st_p[4].append(mkv.reshape(n, MEM_LEN, 2, MEM_HEADS, MEM_HEAD_DIM))

        xs = _ffn(xs, vec(g_ffn1), *w1)
        res = _inproj(xs, vec(g_mix), win_l, 1, nb, [nb] * N_GROUPS, [1] * N_GROUPS, F32,
                      1.0)
        qkvs, kvns, z, qm, gates = res[0:3], res[3:6], res[6], res[7], res[8]
        q = jnp.stack([t.reshape(nb, QKV_COLS)[:, :GROUP_COLS].reshape(nb, HEADS, HEAD_DIM, 1)
                       for t in qkvs], axis=1)
        kvns = [t.reshape(nb, 2, HEADS, HEAD_DIM) for t in kvns]
        a, pooled, c = _sample_branches(
            q, jnp.stack(kvns, axis=1)[..., None],
            [jnp.transpose(cw[l], (0, 2, 3, 4, 1)) for cw in win_caches], bias_cache, bias_new,
            z.reshape(nb, 1, POOL_WIDTH), state_pool[l],
            qm.reshape(nb, MEM_HEADS, MEM_HEAD_DIM), cache_mem_kv[l])
        xs = _sample_merge(a.reshape(nb, -1), pooled.reshape(nb, -1), c.reshape(nb, -1),
                           gates, xs, merge_w)
        xs = _ffn(xs, vec(g_ffn2), *w2, g_final=gfin if last else None)
        for g in range(N_GROUPS):
            st_s[g].append(kvns[g].reshape(nb, 1, 2, HEADS, HEAD_DIM))
        st_s[3].append(z.reshape(nb, 1, POOL_WIDTH))

    y_prompt = xp.reshape(n, s, D_MODEL)
    y_sample = xs.reshape(nb, 1, D_MODEL)
    stack = lambda ts: jnp.stack(ts, axis=0)
    return (y_prompt, y_sample, stack(st_p[0]), stack(st_p[1]), stack(st_p[2]), stack(st_p[3]),
            stack(st_p[4]), stack(st_s[0]), stack(st_s[1]), stack(st_s[2]), stack(st_s[3]))
```

```python
import functools
import math

import jax
import jax.numpy as jnp
from jax import lax
from jax.experimental import pallas as pl
from jax.experimental.pallas import tpu as pltpu

F32 = jnp.float32
BF16 = jnp.bfloat16

D_MODEL = 1024
D_FF = 2816
LANES = 128
HEAD_DIM = 64
HEADS = 4
DIL_GROUPS = ((128, 1), (512, 4), (2048, 16))
N_GROUPS = 3
GROUP_COLS = HEADS * HEAD_DIM
QKV_COLS = 3 * GROUP_COLS
STRIDES = 128
POOL_WINDOWS = (2, 4, 8, 16)
POOL_GROUP = 128
POOL_WIDTH = 512
POOL_STATE = 15
POOL_HALO = 16
MEM_LEN = 256
MEM_HEADS = 4
MEM_HEAD_DIM = 128
MEM_WIDTH = 512
N_BUCKETS = 32
MAX_DISTANCE = 2048
N_BRANCH = 3
EPS = 1e-6
NEG_INF = -1e30
LOG2E = math.log2(math.e)
LN2 = math.log(2.0)
PAST_LEN = 8192

V7X_VMEM_LIMIT_BYTES = 56 * 1024 * 1024
TOKEN_TILE = 512


def _params(*sem):
    return pltpu.CompilerParams(dimension_semantics=sem,
                                vmem_limit_bytes=V7X_VMEM_LIMIT_BYTES)


def _resident(shape):
    zeros = (0,) * len(shape)
    return pl.BlockSpec(shape, lambda *_: zeros, pipeline_mode=pl.Buffered(1))


def _rms(x, g):
    return x * lax.rsqrt(jnp.mean(x * x, axis=-1, keepdims=True) + EPS) * g


def _dot(a, b):
    return jnp.dot(a, b, preferred_element_type=F32)


def _dot_t(a, b):
    return lax.dot_general(a, b, (((1,), (1,)), ((), ())), preferred_element_type=F32)


def _ffn_body(x_ref, g_ref, wg_ref, wu_ref, wd_ref, *rest, final):
    o_ref = rest[-1]
    x = x_ref[...]
    h = _rms(x, g_ref[...]).astype(BF16)
    a = _dot(h, wg_ref[...])
    b = _dot(h, wu_ref[...])
    act = (a * jax.nn.sigmoid(a) * b).astype(BF16)
    y = x + 0.5 * _dot(act, wd_ref[...])
    if final:
        y = _rms(y, rest[0][...])
    o_ref[...] = y


def _ffn(x, g, wg, wu, wd, g_final=None):
    m = x.shape[0]
    tm = min(TOKEN_TILE, m)
    final = g_final is not None
    row = pl.BlockSpec((tm, D_MODEL), lambda i: (i, 0))
    in_specs = [row, _resident((1, D_MODEL)), _resident((D_MODEL, D_FF)),
                _resident((D_MODEL, D_FF)), _resident((D_FF, D_MODEL))]
    args = [x, g, wg, wu, wd]
    if final:
        in_specs.append(_resident((1, D_MODEL)))
        args.append(g_final)
    return pl.pallas_call(
        functools.partial(_ffn_body, final=final),
        grid=(m // tm,),
        in_specs=in_specs,
        out_specs=row,
        out_shape=jax.ShapeDtypeStruct((m, D_MODEL), F32),
        compiler_params=_params("parallel"),
        name="ffn_final" if final else "ffn",
    )(*args)


Z_OFF = N_GROUPS * QKV_COLS
QM_OFF = Z_OFF + POOL_WIDTH
GATE_OFF = QM_OFF + MEM_WIDTH
IN_COLS = GATE_OFF + N_BRANCH * D_MODEL


def _inproj_body(x_ref, g_ref, w_ref, qkv0, qkv1, qkv2, kv0, kv1, kv2,
                 z_ref, qm_ref, u_scr, kv_scr, *, q_scale):
    uf = _rms(x_ref[...], g_ref[...])
    u = uf.astype(BF16)
    tm = u.shape[0]
    if any(ref.shape[1] > 1 for ref in (qkv0, qkv1, qkv2)):
        for c in range(D_MODEL // LANES):
            u_scr[c] = uf[:, c * LANES:(c + 1) * LANES]
    for g, (qkv_ref, kv_ref) in enumerate(((qkv0, kv0), (qkv1, kv1), (qkv2, kv2))):
        dil, per_class = qkv_ref.shape[1], qkv_ref.shape[2]
        w_g = w_ref[:, g * QKV_COLS:(g + 1) * QKV_COLS]
        rows = kv_ref.shape[1]
        def put_qkv(r, blk):
            q = blk[:, :GROUP_COLS] if q_scale == 1.0 else blk[:, :GROUP_COLS] * q_scale
            qkv_ref[0, r, :, 0:GROUP_COLS] = q.astype(qkv_ref.dtype)
            qkv_ref[0, r, :, GROUP_COLS:] = blk[:, GROUP_COLS:].astype(qkv_ref.dtype)

        if dil == 1:
            p = _dot(u, w_g)
            put_qkv(0, p)
            kv_ref[0] = p[tm - rows:, GROUP_COLS:]
        else:
            u_cls = jnp.concatenate(
                [jnp.concatenate([u_scr[c, pl.ds(r, per_class, stride=dil), :]
                                  for c in range(D_MODEL // LANES)], axis=1)
                 for r in range(dil)], axis=0)
            p = _dot(u_cls.astype(BF16), w_g)
            assert rows == tm
            for r in range(dil):
                blk = p[r * per_class:(r + 1) * per_class]
                put_qkv(r, blk)
                for c in range(2 * GROUP_COLS // LANES):
                    lo = GROUP_COLS + c * LANES
                    kv_scr[c, pl.ds(r, per_class, stride=dil), :] = blk[:, lo:lo + LANES]
            kv_ref[0] = jnp.concatenate(
                [kv_scr[c] for c in range(2 * GROUP_COLS // LANES)], axis=1)
    z_ref[...] = _dot(u, w_ref[:, Z_OFF:QM_OFF])
    qm_ref[...] = _dot(u, w_ref[:, QM_OFF:GATE_OFF]).astype(qm_ref.dtype)


def _inproj(x, g, w, n, s, windows, dils, q_dtype, q_scale):
    tm = min(TOKEN_TILE, s)
    tj = s // tm

    def tok(cols):
        return pl.BlockSpec((tm, cols), lambda b, j: (b * tj + j, 0))

    qkv_specs = [pl.BlockSpec((1, d, tm // d, QKV_COLS), lambda b, j: (b, 0, j, 0)) for d in dils]
    qkv_shapes = [jax.ShapeDtypeStruct((n, d, s // d, QKV_COLS), q_dtype) for d in dils]

    kv_specs, kv_shapes = [], []
    for win in windows:
        rows = min(win, tm)
        assert tm % rows == 0 and win % rows == 0
        first = (s - win) // tm
        if win >= tm:
            spec = pl.BlockSpec((1, rows, 2 * GROUP_COLS),
                                lambda b, j, first=first: (b, jnp.maximum(j - first, 0), 0))
        else:
            spec = pl.BlockSpec((1, rows, 2 * GROUP_COLS), lambda b, j: (b, 0, 0))
        kv_specs.append(spec)
        kv_shapes.append(jax.ShapeDtypeStruct((n, win, 2 * GROUP_COLS), F32))
    m = n * s
    out_shape = (qkv_shapes + kv_shapes + [
        jax.ShapeDtypeStruct((m, POOL_WIDTH), F32),
        jax.ShapeDtypeStruct((m, MEM_WIDTH), q_dtype)])
    out_specs = qkv_specs + kv_specs + [tok(POOL_WIDTH), tok(MEM_WIDTH)]
    return pl.pallas_call(
        functools.partial(_inproj_body, q_scale=q_scale),
        grid=(n, tj),
        in_specs=[tok(D_MODEL), _resident((1, D_MODEL)), _resident((D_MODEL, GATE_OFF))],
        out_specs=out_specs,
        out_shape=out_shape,
        scratch_shapes=[pltpu.VMEM((D_MODEL // LANES, tm, LANES), F32),
                        pltpu.VMEM((2 * GROUP_COLS // LANES, tm, LANES), F32)],
        compiler_params=_params("arbitrary", "arbitrary"),
        name="inproj",
    )(x, g, w)


ATTN_CHUNKS = 16
HEAD_LANES = HEADS * STRIDES


def _band_body(row_ref, band_ref):
    for g in range(N_GROUPS):
        for h in range(HEADS):
            rows = jnp.broadcast_to(row_ref[g, h] * LOG2E, (STRIDES, 2 * STRIDES))
            band = pltpu.roll(rows, 0, 1, stride=1, stride_axis=0).T
            band_ref[g, :, h * STRIDES:(h + 1) * STRIDES] = band


def _bands(rows):
    return pl.pallas_call(
        _band_body,
        grid=(1,),
        in_specs=[_resident(rows.shape)],
        out_specs=pl.BlockSpec((N_GROUPS, 2 * STRIDES, HEAD_LANES), lambda i: (0, 0, 0)),
        out_shape=jax.ShapeDtypeStruct((N_GROUPS, 2 * STRIDES, HEAD_LANES), F32),
        compiler_params=_params("arbitrary"),
        name="bands",
    )(rows)


def _attn_body(qkv_ref, prev_ref, band_ref, o_ref, lse_ref, *, dil, chunks):
    first = pl.program_id(1) == 0
    lane_head = lax.broadcasted_iota(jnp.int32, (1, GROUP_COLS), 1) // HEAD_DIM
    for r in range(dil):
        k_all = jnp.concatenate([prev_ref[0, r, :, GROUP_COLS:2 * GROUP_COLS],
                                 qkv_ref[0, r, :, GROUP_COLS:2 * GROUP_COLS]], axis=0)
        v_all = jnp.concatenate([prev_ref[0, r, :, 2 * GROUP_COLS:],
                                 qkv_ref[0, r, :, 2 * GROUP_COLS:]], axis=0)
        vt_all = v_all.astype(F32).T.astype(BF16)
        for i in range(chunks):
            q = qkv_ref[0, r, i * STRIDES:(i + 1) * STRIDES, 0:GROUP_COLS]
            qm = jnp.concatenate(
                [jnp.where(lane_head == h, q, jnp.zeros_like(q)) for h in range(HEADS)], axis=0)
            st = _dot_t(k_all[i * STRIDES:(i + 2) * STRIDES], qm)
            vt2 = vt_all[:, i * STRIDES:(i + 2) * STRIDES]
            o_parts, lse_parts = [], []
            for h in range(HEADS):
                hl = slice(h * STRIDES, (h + 1) * STRIDES)
                s_h = st[:, hl] + band_ref[:, hl]
                if i == 0:
                    s_h = jnp.concatenate(
                        [jnp.where(first, NEG_INF, s_h[:STRIDES]), s_h[STRIDES:]], axis=0)
                m = jnp.max(s_h, axis=0, keepdims=True)
                e = jnp.exp2(s_h - m)
                l = jnp.sum(e, axis=0, keepdims=True)
                ot = _dot(vt2[h * HEAD_DIM:(h + 1) * HEAD_DIM], e.astype(BF16))
                o_parts.append(ot * (1.0 / l))
                lse_parts.append(jnp.broadcast_to(m * LN2 + jnp.log(l), (HEAD_DIM, STRIDES)))
            rows = pl.ds(i * STRIDES * dil + r, STRIDES, stride=dil)
            o_rows = jnp.concatenate(o_parts, axis=0).T
            lse_rows = jnp.concatenate(lse_parts, axis=0).T
            for c in range(GROUP_COLS // LANES):
                o_ref[c, rows, :] = o_rows[:, c * LANES:(c + 1) * LANES]
                lse_ref[c, rows, :] = lse_rows[:, c * LANES:(c + 1) * LANES]


def _attn(qkv, bands, g, n, s, dil):
    chunks = ATTN_CHUNKS // dil
    span = chunks * STRIDES
    steps = s // (span * dil)
    slabs = GROUP_COLS // LANES
    out_spec = pl.BlockSpec((slabs, span * dil, LANES), lambda b, k: (0, b * steps + k, 0))
    out_sds = jax.ShapeDtypeStruct((slabs, n * s, LANES), F32)
    return pl.pallas_call(
        functools.partial(_attn_body, dil=dil, chunks=chunks),
        grid=(n, steps),
        in_specs=[pl.BlockSpec((1, dil, span, QKV_COLS), lambda b, k: (b, 0, k, 0)),
                  pl.BlockSpec((1, dil, STRIDES, QKV_COLS),
                               lambda b, k: (b, 0, jnp.maximum(k * chunks - 1, 0), 0)),
                  pl.BlockSpec((None, 2 * STRIDES, HEAD_LANES), lambda b, k: (g, 0, 0))],
        out_specs=[out_spec, out_spec],
        out_shape=[out_sds, out_sds],
        compiler_params=_params("parallel", "parallel"),
        name="attn_d%d" % dil,
    )(qkv, qkv, bands)


def _memkv_body(mem_ref, g_ref, w_ref, o_ref):
    o_ref[...] = _dot(_rms(mem_ref[...], g_ref[...]).astype(BF16), w_ref[...])


def _memkv(mem, g, w):
    m = mem.shape[0]
    row = pl.BlockSpec((MEM_LEN, D_MODEL), lambda i: (i, 0))
    return pl.pallas_call(
        _memkv_body,
        grid=(m // MEM_LEN,),
        in_specs=[row, _resident((1, D_MODEL)), _resident((D_MODEL, 2 * MEM_WIDTH))],
        out_specs=pl.BlockSpec((MEM_LEN, 2 * MEM_WIDTH), lambda i: (i, 0)),
        out_shape=jax.ShapeDtypeStruct((m, 2 * MEM_WIDTH), F32),
        compiler_params=_params("parallel"),
        name="memkv",
    )(mem, g, w)


def _merge_math(a, pooled, c, x, gmix_ref, wgate_ref, wpool_ref, scale_ref, woa_ref, wob_ref,
                woc_ref, wout_ref):
    u = _rms(x, gmix_ref[...]).astype(BF16)
    mixed = [_dot(pooled[gi].astype(BF16), wpool_ref[gi]) for gi in range(len(POOL_WINDOWS))]
    b = jnp.concatenate(mixed, axis=1) * scale_ref[...]
    m = None
    for k, (branch, wo_ref) in enumerate(((a, woa_ref), (b, wob_ref), (c, woc_ref))):
        gate = jax.nn.sigmoid(_dot(u, wgate_ref[:, k * D_MODEL:(k + 1) * D_MODEL]))
        term = gate * _dot(branch.astype(BF16), wo_ref[...])
        m = term if m is None else m + term
    return x + _dot(m.astype(BF16), wout_ref[...])


def _mix_body(o0, o1, o2, l0, l1, l2, z_ref, halo_ref, qm_ref, mkv_ref, x_ref, *rest):
    merge_refs, out_ref = rest[:-1], rest[-1]
    j = pl.program_id(1)
    tm = x_ref.shape[0]
    unslab = lambda ref: jnp.concatenate([ref[c] for c in range(ref.shape[0])], axis=1)
    lses = [unslab(l0), unslab(l1), unslab(l2)]
    mx = jnp.maximum(jnp.maximum(lses[0], lses[1]), lses[2])
    es = [jnp.exp(l - mx) for l in lses]
    a = ((es[0] * unslab(o0) + es[1] * unslab(o1) + es[2] * unslab(o2))
         / (es[0] + es[1] + es[2]))
    z = z_ref[...]
    halo = jnp.where(j == 0, 0.0, halo_ref[...])
    zc = jnp.concatenate([halo, z], axis=0)
    pos = j * tm + lax.broadcasted_iota(jnp.int32, (tm, 1), 0)
    pooled = []
    for gi, kw in enumerate(POOL_WINDOWS):
        cs = slice(gi * POOL_GROUP, (gi + 1) * POOL_GROUP)
        run = zc[:, cs]
        width = 1
        while width < kw:
            run = run[width:] + run[:-width]
            width *= 2
        first = POOL_HALO - (kw - 1)
        cnt = jnp.minimum(kw, pos + 1).astype(F32)
        pooled.append(run[first:first + tm] / cnt - z[:, cs])
    mkv = mkv_ref[...].astype(BF16)
    qm = qm_ref[...]
    cs_out = []
    for h in range(MEM_HEADS):
        hs = slice(h * MEM_HEAD_DIM, (h + 1) * MEM_HEAD_DIM)
        vs = slice(MEM_WIDTH + h * MEM_HEAD_DIM, MEM_WIDTH + (h + 1) * MEM_HEAD_DIM)
        s = _dot_t(qm[:, hs], mkv[:, hs]) * (1.0 / math.sqrt(MEM_HEAD_DIM))
        mm = jnp.max(s, axis=-1, keepdims=True)
        p = jnp.exp(s - mm)
        l = jnp.sum(p, axis=-1, keepdims=True)
        cs_out.append(_dot(p.astype(BF16), mkv[:, vs]) / l)
    c = jnp.concatenate(cs_out, axis=1)
    out_ref[...] = _merge_math(a, pooled, c, x_ref[...], *merge_refs)


def _merge_weight_specs():
    return [_resident((1, D_MODEL)), _resident((D_MODEL, N_BRANCH * D_MODEL)),
            _resident((len(POOL_WINDOWS), POOL_GROUP, POOL_GROUP)), _resident((1, POOL_WIDTH)),
            _resident((GROUP_COLS, D_MODEL)), _resident((POOL_WIDTH, D_MODEL)),
            _resident((MEM_WIDTH, D_MODEL)), _resident((D_MODEL, D_MODEL))]


def _mix_prompt(os, lses, z, qm, mkv, x, weights, n, s):
    tm = TOKEN_TILE
    tj = s // tm
    halo_per_tile = tm // POOL_HALO

    def tok(cols):
        return pl.BlockSpec((tm, cols), lambda b, j: (b * tj + j, 0))

    halo = pl.BlockSpec((POOL_HALO, POOL_WIDTH),
                        lambda b, j: (jnp.maximum((b * tj + j) * halo_per_tile - 1, 0), 0))
    slab = pl.BlockSpec((GROUP_COLS // LANES, tm, LANES), lambda b, j: (0, b * tj + j, 0))
    in_specs = ([slab] * 6 + [tok(POOL_WIDTH), halo, tok(MEM_WIDTH),
                pl.BlockSpec((MEM_LEN, 2 * MEM_WIDTH), lambda b, j: (b, 0)),
                tok(D_MODEL)] + _merge_weight_specs())
    return pl.pallas_call(
        _mix_body,
        grid=(n, tj),
        in_specs=in_specs,
        out_specs=tok(D_MODEL),
        out_shape=jax.ShapeDtypeStruct((n * s, D_MODEL), F32),
        compiler_params=_params("parallel", "parallel"),
        name="mix_prompt",
    )(*os, *lses, z, z, qm, mkv, x, *weights)


def _sample_branch_body(q_ref, kvn_ref, c0, c1, c2, b0, b1, b2, bias0_ref, z_ref, st_ref,
                        qm_ref, cm_ref, a_ref, pooled_ref, c_ref):
    outs, lses = [], []
    for g, (cache_ref, bias_ref) in enumerate(((c0, b0), (c1, b1), (c2, b2))):
        q = q_ref[0, g]
        kn = kvn_ref[0, g, 0]
        vn = kvn_ref[0, g, 1]
        s = jnp.sum(cache_ref[0, 0] * q, axis=1, keepdims=True) + bias_ref[...]
        sn = jnp.sum(kn * q, axis=1, keepdims=True) + bias0_ref[g]
        m = jnp.maximum(jnp.max(s, axis=-1, keepdims=True), sn)
        p = jnp.exp(s - m)
        pn = jnp.exp(sn - m)
        l = jnp.sum(p, axis=-1, keepdims=True) + pn
        pv = jnp.sum(cache_ref[0, 1] * p, axis=-1, keepdims=True)
        outs.append((pv + pn * vn) / l)
        lses.append(m + jnp.log(l))
    mx = jnp.maximum(jnp.maximum(lses[0], lses[1]), lses[2])
    es = [jnp.exp(lse - mx) for lse in lses]
    a_ref[0] = ((es[0] * outs[0] + es[1] * outs[1] + es[2] * outs[2])
                / (es[0] + es[1] + es[2]))

    zn = z_ref[0]
    st = st_ref[0]
    pooled = []
    for gi, kw in enumerate(POOL_WINDOWS):
        cs = slice(gi * POOL_GROUP, (gi + 1) * POOL_GROUP)
        tot = jnp.sum(st[POOL_STATE - (kw - 1):, cs], axis=0, keepdims=True) + zn[:, cs]
        pooled.append(tot / float(min(kw, PAST_LEN + 1)) - zn[:, cs])
    pooled_ref[0] = jnp.concatenate(pooled, axis=1)

    qm = qm_ref[0]
    km = cm_ref[0, :, 0]
    vm = cm_ref[0, :, 1]
    s = jnp.sum(km * qm[None], axis=-1, keepdims=True) * (1.0 / math.sqrt(MEM_HEAD_DIM))
    m = jnp.max(s, axis=0)
    p = jnp.exp(s - m[None])
    c_ref[0] = jnp.sum(p * vm, axis=0) / jnp.sum(p, axis=0)


def _sample_branches(q, kvn, caches, biases, bias0, z, state, qm, cache_mem):
    nb = z.shape[0]

    def per_request(shape):
        zeros = (0,) * len(shape)
        return pl.BlockSpec((1,) + shape, lambda b: (b,) + zeros)

    in_specs = ([per_request(q.shape[1:]), per_request(kvn.shape[1:])]
                + [per_request(c.shape[1:]) for c in caches]
                + [_resident(b.shape) for b in biases] + [
        _resident(bias0.shape),
        per_request((1, POOL_WIDTH)), per_request((POOL_STATE, POOL_WIDTH)),
        per_request((MEM_HEADS, MEM_HEAD_DIM)),
        per_request((MEM_LEN, 2, MEM_HEADS, MEM_HEAD_DIM))])
    return pl.pallas_call(
        _sample_branch_body,
        grid=(nb,),
        in_specs=in_specs,
        out_specs=[per_request((HEADS, HEAD_DIM, 1)), per_request((1, POOL_WIDTH)),
                   per_request((MEM_HEADS, MEM_HEAD_DIM))],
        out_shape=[jax.ShapeDtypeStruct((nb, HEADS, HEAD_DIM, 1), F32),
                   jax.ShapeDtypeStruct((nb, 1, POOL_WIDTH), F32),
                   jax.ShapeDtypeStruct((nb, MEM_HEADS, MEM_HEAD_DIM), F32)],
        compiler_params=_params("parallel"),
        name="sample_branches",
    )(q, kvn, *caches, *biases, bias0, z, state, qm, cache_mem)


def _sample_merge_body(a_ref, pooled_ref, c_ref, x_ref, *rest):
    merge_refs, out_ref = rest[:-1], rest[-1]
    pooled_all = pooled_ref[...]
    pooled = [pooled_all[:, gi * POOL_GROUP:(gi + 1) * POOL_GROUP]
              for gi in range(len(POOL_WINDOWS))]
    out_ref[...] = _merge_math(a_ref[...], pooled, c_ref[...], x_ref[...], *merge_refs)


def _sample_merge(a, pooled, c, x, weights):
    nb = x.shape[0]
    full = lambda cols: _resident((nb, cols))
    return pl.pallas_call(
        _sample_merge_body,
        grid=(1,),
        in_specs=[full(GROUP_COLS), full(POOL_WIDTH), full(MEM_WIDTH), full(D_MODEL)]
        + _merge_weight_specs(),
        out_specs=pl.BlockSpec((nb, D_MODEL), lambda i: (0, 0)),
        out_shape=jax.ShapeDtypeStruct((nb, D_MODEL), F32),
        compiler_params=_params("arbitrary"),
        name="sample_merge",
    )(a, pooled, c, x, *weights)


def _rel_bucket(n):
    max_exact = N_BUCKETS // 2
    nf = jnp.maximum(n, 1).astype(F32)
    large = max_exact + (jnp.log(nf / max_exact) / math.log(MAX_DISTANCE / max_exact)
                         * (N_BUCKETS - max_exact)).astype(jnp.int32)
    large = jnp.minimum(large, N_BUCKETS - 1)
    return jnp.where(n < max_exact, n, large)


def _stride_bias(rel_bias, g, dil):
    j = jnp.arange(STRIDES + 1, dtype=jnp.int32)
    return rel_bias[_rel_bucket(j * dil)][:, g * HEADS:(g + 1) * HEADS].astype(F32)


def _band_row(bias_j):
    row = jnp.concatenate([bias_j[::-1], jnp.full((STRIDES - 1, HEADS), NEG_INF, F32)], axis=0)
    return row.T.reshape(HEADS, 1, 2 * STRIDES)


def _cache_bias(bias_j, dil):
    on_grid = bias_j[STRIDES:0:-1].T
    full = jnp.full((HEADS, STRIDES, dil), NEG_INF, F32).at[:, :, 0].set(on_grid)
    return full.reshape(HEADS, 1, STRIDES * dil)


def _in_weight(w_in_l):
    width = N_GROUPS * GROUP_COLS
    parts = []
    for g in range(N_GROUPS):
        for t in range(3):
            lo = t * width + g * GROUP_COLS
            blk = w_in_l[:, lo:lo + GROUP_COLS]
            parts.append(blk * (1.0 / math.sqrt(HEAD_DIM)) if t == 0 else blk)
    parts.append(w_in_l[:, 3 * width:GATE_OFF])
    return jnp.concatenate(parts, axis=1).astype(BF16), w_in_l[:, GATE_OFF:].astype(BF16)


def kernel(x_prompt, x_sample, cache_win0_kv, cache_win1_kv, cache_win2_kv, state_pool, cache_mem_kv, mem_prompt, rel_bias, g_ffn1, w1_gate, w1_up, w1_down, g_mix, w_in, w_pool, pool_scale, g_mem, w_mem_kv, w_oa, w_ob, w_oc, w_out, g_ffn2, w2_gate, w2_up, w2_down, g_final):
    n, s, _ = x_prompt.shape
    nb = x_sample.shape[0]
    depth = g_ffn1.shape[0]
    win_caches = (cache_win0_kv, cache_win1_kv, cache_win2_kv)
    bias_js = [_stride_bias(rel_bias, g, dil) for g, (_, dil) in enumerate(DIL_GROUPS)]
    bands = _bands(jnp.stack([_band_row(b) for b in bias_js]))
    bias_cache = [_cache_bias(b, dil) for b, (_, dil) in zip(bias_js, DIL_GROUPS)]
    bias_new = jnp.stack([b[0] for b in bias_js]).reshape(N_GROUPS, HEADS, 1, 1)
    gfin = g_final.reshape(1, D_MODEL)

    xp = x_prompt.reshape(n * s, D_MODEL)
    xs = x_sample.reshape(nb, D_MODEL)
    st_p = [[] for _ in range(5)]
    st_s = [[] for _ in range(4)]
    for l in range(depth):
        last = l == depth - 1
        bf = lambda w: w[l].astype(BF16)
        vec = lambda v: v[l].reshape(1, -1)
        w1 = (bf(w1_gate), bf(w1_up), bf(w1_down))
        w2 = (bf(w2_gate), bf(w2_up), bf(w2_down))
        win_l, wgate_l = _in_weight(w_in[l])
        merge_w = (vec(g_mix), wgate_l, bf(w_pool), vec(pool_scale), bf(w_oa), bf(w_ob),
                   bf(w_oc), bf(w_out))

        xp = _ffn(xp, vec(g_ffn1), *w1)
        res = _inproj(xp, vec(g_mix), win_l, n, s, [min(w, s) for w, _ in DIL_GROUPS],
                      [d for _, d in DIL_GROUPS], BF16, LOG2E)
        qkvs, kvwins, z, qm = res[0:3], res[3:6], res[6], res[7]
        os, lses = [], []
        for g, (_, dil) in enumerate(DIL_GROUPS):
            o, lse = _attn(qkvs[g], bands, g, n, s, dil)
            os.append(o)
            lses.append(lse)
        mkv = _memkv(mem_prompt.reshape(n * MEM_LEN, D_MODEL), vec(g_mem), bf(w_mem_kv))
        xp = _mix_prompt(os, lses, z, qm, mkv, xp, merge_w, n, s)
        xp = _ffn(xp, vec(g_ffn2), *w2, g_final=gfin if last else None)
        for g, (win, _) in enumerate(DIL_GROUPS):
            st_p[g].append(kvwins[g].reshape(n, min(win, s), 2, HEADS, HEAD_DIM))
        st_p[3].append(z.reshape(n, s, POOL_WIDTH)[:, s - POOL_STATE:])
        st_p[4].append(mkv.reshape(n, MEM_LEN, 2, MEM_HEADS, MEM_HEAD_DIM))

        xs = _ffn(xs, vec(g_ffn1), *w1)
        res = _inproj(xs, vec(g_mix), win_l, 1, nb, [nb] * N_GROUPS, [1] * N_GROUPS, F32,
                      1.0)
        qkvs, kvns, z, qm = res[0:3], res[3:6], res[6], res[7]
        q = jnp.stack([t.reshape(nb, QKV_COLS)[:, :GROUP_COLS].reshape(nb, HEADS, HEAD_DIM, 1)
                       for t in qkvs], axis=1)
        kvns = [t.reshape(nb, 2, HEADS, HEAD_DIM) for t in kvns]
        a, pooled, c = _sample_branches(
            q, jnp.stack(kvns, axis=1)[..., None],
            [jnp.transpose(cw[l], (0, 2, 3, 4, 1)) for cw in win_caches], bias_cache, bias_new,
            z.reshape(nb, 1, POOL_WIDTH), state_pool[l],
            qm.reshape(nb, MEM_HEADS, MEM_HEAD_DIM), cache_mem_kv[l])
        xs = _sample_merge(a.reshape(nb, -1), pooled.reshape(nb, -1), c.reshape(nb, -1),
                           xs, merge_w)
        xs = _ffn(xs, vec(g_ffn2), *w2, g_final=gfin if last else None)
        for g in range(N_GROUPS):
            st_s[g].append(kvns[g].reshape(nb, 1, 2, HEADS, HEAD_DIM))
        st_s[3].append(z.reshape(nb, 1, POOL_WIDTH))

    y_prompt = xp.reshape(n, s, D_MODEL)
    y_sample = xs.reshape(nb, 1, D_MODEL)
    stack = lambda ts: jnp.stack(ts, axis=0)
    return (y_prompt, y_sample, stack(st_p[0]), stack(st_p[1]), stack(st_p[2]), stack(st_p[3]),
            stack(st_p[4]), stack(st_s[0]), stack(st_s[1]), stack(st_s[2]), stack(st_s[3]))
```

```python
import functools
import math

import jax
import jax.numpy as jnp
from jax import lax
from jax.experimental import pallas as pl
from jax.experimental.pallas import tpu as pltpu

F32 = jnp.float32
BF16 = jnp.bfloat16

D_MODEL = 1024
D_FF = 2816
LANES = 128
HEAD_DIM = 64
HEADS = 4
DIL_GROUPS = ((128, 1), (512, 4), (2048, 16))
N_GROUPS = 3
GROUP_COLS = HEADS * HEAD_DIM
QKV_COLS = 3 * GROUP_COLS
STRIDES = 128
POOL_WINDOWS = (2, 4, 8, 16)
POOL_GROUP = 128
POOL_WIDTH = 512
POOL_STATE = 15
POOL_HALO = 16
MEM_LEN = 256
MEM_HEADS = 4
MEM_HEAD_DIM = 128
MEM_WIDTH = 512
N_BUCKETS = 32
MAX_DISTANCE = 2048
N_BRANCH = 3
EPS = 1e-6
NEG_INF = -1e30
LOG2E = math.log2(math.e)
LN2 = math.log(2.0)
PAST_LEN = 8192

V7X_VMEM_LIMIT_BYTES = 56 * 1024 * 1024
TOKEN_TILE = 512


def _params(*sem):
    return pltpu.CompilerParams(dimension_semantics=sem,
                                vmem_limit_bytes=V7X_VMEM_LIMIT_BYTES)


def _resident(shape):
    zeros = (0,) * len(shape)
    return pl.BlockSpec(shape, lambda *_: zeros, pipeline_mode=pl.Buffered(1))


def _rms(x, g):
    return x * lax.rsqrt(jnp.mean(x * x, axis=-1, keepdims=True) + EPS) * g


def _dot(a, b):
    return jnp.dot(a, b, preferred_element_type=F32)


def _dot_t(a, b):
    return lax.dot_general(a, b, (((1,), (1,)), ((), ())), preferred_element_type=F32)


def _ffn_body(x_ref, g_ref, wg_ref, wu_ref, wd_ref, *rest, final):
    o_ref = rest[-1]
    x = x_ref[...]
    h = _rms(x, g_ref[...]).astype(BF16)
    a = _dot(h, wg_ref[...])
    b = _dot(h, wu_ref[...])
    act = (a * jax.nn.sigmoid(a) * b).astype(BF16)
    y = x + 0.5 * _dot(act, wd_ref[...])
    if final:
        y = _rms(y, rest[0][...])
    o_ref[...] = y


def _ffn(x, g, wg, wu, wd, g_final=None):
    m = x.shape[0]
    tm = min(TOKEN_TILE, m)
    final = g_final is not None
    row = pl.BlockSpec((tm, D_MODEL), lambda i: (i, 0))
    in_specs = [row, _resident((1, D_MODEL)), _resident((D_MODEL, D_FF)),
                _resident((D_MODEL, D_FF)), _resident((D_FF, D_MODEL))]
    args = [x, g, wg, wu, wd]
    if final:
        in_specs.append(_resident((1, D_MODEL)))
        args.append(g_final)
    return pl.pallas_call(
        functools.partial(_ffn_body, final=final),
        grid=(m // tm,),
        in_specs=in_specs,
        out_specs=row,
        out_shape=jax.ShapeDtypeStruct((m, D_MODEL), F32),
        compiler_params=_params("parallel"),
        name="ffn_final" if final else "ffn",
    )(*args)


Z_OFF = N_GROUPS * QKV_COLS
QM_OFF = Z_OFF + POOL_WIDTH
GATE_OFF = QM_OFF + MEM_WIDTH
IN_COLS = GATE_OFF + N_BRANCH * D_MODEL


def _inproj_body(x_ref, g_ref, w_ref, *rest, q_scale, with_windows):
    qkv_refs = rest[0:N_GROUPS]
    kv_refs = rest[N_GROUPS:2 * N_GROUPS] if with_windows else (None,) * N_GROUPS
    z_ref, qm_ref, u_scr, kv_scr = rest[-4:]
    uf = _rms(x_ref[...], g_ref[...])
    u = uf.astype(BF16)
    tm = u.shape[0]
    if any(ref.shape[1] > 1 for ref in qkv_refs):
        for c in range(D_MODEL // LANES):
            u_scr[c] = uf[:, c * LANES:(c + 1) * LANES]
    width = N_GROUPS * GROUP_COLS
    for g, (qkv_ref, kv_ref) in enumerate(zip(qkv_refs, kv_refs)):
        dil, per_class = qkv_ref.shape[1], qkv_ref.shape[2]
        w_g = jnp.concatenate(
            [w_ref[:, t * width + g * GROUP_COLS:t * width + (g + 1) * GROUP_COLS]
             for t in range(3)], axis=1)
        rows = kv_ref.shape[2] if with_windows else 0

        def put_qkv(r, blk):
            qkv_ref[0, r, :, 0:GROUP_COLS] = (blk[:, :GROUP_COLS] * q_scale).astype(qkv_ref.dtype)
            qkv_ref[0, r, :, GROUP_COLS:] = blk[:, GROUP_COLS:].astype(qkv_ref.dtype)

        if dil == 1:
            p = _dot(u, w_g)
            put_qkv(0, p)
            if with_windows:
                kv_ref[0] = p[tm - rows:, GROUP_COLS:].T
        else:
            u_cls = jnp.concatenate(
                [jnp.concatenate([u_scr[c, pl.ds(r, per_class, stride=dil), :]
                                  for c in range(D_MODEL // LANES)], axis=1)
                 for r in range(dil)], axis=0)
            p = _dot(u_cls.astype(BF16), w_g)
            assert with_windows and rows == tm
            for r in range(dil):
                blk = p[r * per_class:(r + 1) * per_class]
                put_qkv(r, blk)
                for c in range(2 * GROUP_COLS // LANES):
                    lo = GROUP_COLS + c * LANES
                    kv_scr[c, pl.ds(r, per_class, stride=dil), :] = blk[:, lo:lo + LANES]
            kv_ref[0] = jnp.concatenate(
                [kv_scr[c] for c in range(2 * GROUP_COLS // LANES)], axis=1).T
    z_ref[...] = _dot(u, w_ref[:, Z_OFF:QM_OFF])
    qm_ref[...] = _dot(u, w_ref[:, QM_OFF:GATE_OFF]).astype(qm_ref.dtype)


def _inproj(x, g, w, n, s, windows, dils, q_dtype, q_scale):
    tm = min(TOKEN_TILE, s)
    tj = s // tm

    def tok(cols):
        return pl.BlockSpec((tm, cols), lambda b, j: (b * tj + j, 0))

    qkv_specs = [pl.BlockSpec((1, d, tm // d, QKV_COLS), lambda b, j: (b, 0, j, 0)) for d in dils]
    qkv_shapes = [jax.ShapeDtypeStruct((n, d, s // d, QKV_COLS), q_dtype) for d in dils]

    kv_specs, kv_shapes = [], []
    for win in windows or ():
        rows = min(win, tm)
        assert tm % rows == 0 and win % rows == 0
        first = (s - win) // tm
        if win >= tm:
            spec = pl.BlockSpec((1, 2 * GROUP_COLS, rows),
                                lambda b, j, first=first: (b, 0, jnp.maximum(j - first, 0)))
        else:
            spec = pl.BlockSpec((1, 2 * GROUP_COLS, rows), lambda b, j: (b, 0, 0))
        kv_specs.append(spec)
        kv_shapes.append(jax.ShapeDtypeStruct((n, 2 * GROUP_COLS, win), F32))
    m = n * s
    out_shape = (qkv_shapes + kv_shapes + [
        jax.ShapeDtypeStruct((m, POOL_WIDTH), F32),
        jax.ShapeDtypeStruct((m, MEM_WIDTH), q_dtype)])
    out_specs = qkv_specs + kv_specs + [tok(POOL_WIDTH), tok(MEM_WIDTH)]
    return pl.pallas_call(
        functools.partial(_inproj_body, q_scale=q_scale, with_windows=bool(windows)),
        grid=(n, tj),
        in_specs=[tok(D_MODEL), _resident((1, D_MODEL)),
                  pl.BlockSpec((D_MODEL, GATE_OFF), lambda b, j: (0, 0),
                               pipeline_mode=pl.Buffered(1))],
        out_specs=out_specs,
        out_shape=out_shape,
        scratch_shapes=[pltpu.VMEM((D_MODEL // LANES, tm, LANES), F32),
                        pltpu.VMEM((2 * GROUP_COLS // LANES, tm, LANES), F32)],
        compiler_params=_params("arbitrary", "arbitrary"),
        name="inproj",
    )(x, g, w)


ATTN_CHUNKS = 16
HEAD_LANES = HEADS * STRIDES


def _band_body(row_ref, band_ref):
    for g in range(N_GROUPS):
        for h in range(HEADS):
            rows = jnp.broadcast_to(row_ref[g, h] * LOG2E, (STRIDES, 2 * STRIDES))
            band = pltpu.roll(rows, 0, 1, stride=1, stride_axis=0).T
            band_ref[g, :, h * STRIDES:(h + 1) * STRIDES] = band


def _bands(rows):
    return pl.pallas_call(
        _band_body,
        grid=(1,),
        in_specs=[_resident(rows.shape)],
        out_specs=pl.BlockSpec((N_GROUPS, 2 * STRIDES, HEAD_LANES), lambda i: (0, 0, 0)),
        out_shape=jax.ShapeDtypeStruct((N_GROUPS, 2 * STRIDES, HEAD_LANES), F32),
        compiler_params=_params("arbitrary"),
        name="bands",
    )(rows)


def _attn_body(qkv_ref, prev_ref, band_ref, o_ref, lse_ref, *, dil, chunks):
    first = pl.program_id(1) == 0
    lane_head = lax.broadcasted_iota(jnp.int32, (1, GROUP_COLS), 1) // HEAD_DIM
    for r in range(dil):
        k_all = jnp.concatenate([prev_ref[0, r, :, GROUP_COLS:2 * GROUP_COLS],
                                 qkv_ref[0, r, :, GROUP_COLS:2 * GROUP_COLS]], axis=0)
        v_all = jnp.concatenate([prev_ref[0, r, :, 2 * GROUP_COLS:],
                                 qkv_ref[0, r, :, 2 * GROUP_COLS:]], axis=0)
        vt_all = v_all.astype(F32).T.astype(BF16)
        for i in range(chunks):
            q = qkv_ref[0, r, i * STRIDES:(i + 1) * STRIDES, 0:GROUP_COLS]
            qm = jnp.concatenate(
                [jnp.where(lane_head == h, q, jnp.zeros_like(q)) for h in range(HEADS)], axis=0)
            st = _dot_t(k_all[i * STRIDES:(i + 2) * STRIDES], qm)
            vt2 = vt_all[:, i * STRIDES:(i + 2) * STRIDES]
            o_parts, lse_parts = [], []
            for h in range(HEADS):
                hl = slice(h * STRIDES, (h + 1) * STRIDES)
                s_h = st[:, hl] + band_ref[:, hl]
                if i == 0:
                    s_h = jnp.concatenate(
                        [jnp.where(first, NEG_INF, s_h[:STRIDES]), s_h[STRIDES:]], axis=0)
                m = jnp.max(s_h, axis=0, keepdims=True)
                e = jnp.exp2(s_h - m)
                l = jnp.sum(e, axis=0, keepdims=True)
                ot = _dot(vt2[h * HEAD_DIM:(h + 1) * HEAD_DIM], e.astype(BF16))
                o_parts.append(ot * (1.0 / l))
                lse_parts.append(jnp.broadcast_to(m * LN2 + jnp.log(l), (HEAD_DIM, STRIDES)))
            rows = pl.ds(i * STRIDES * dil + r, STRIDES, stride=dil)
            o_rows = jnp.concatenate(o_parts, axis=0).T
            lse_rows = jnp.concatenate(lse_parts, axis=0).T
            for c in range(GROUP_COLS // LANES):
                o_ref[c, rows, :] = o_rows[:, c * LANES:(c + 1) * LANES]
                lse_ref[c, rows, :] = lse_rows[:, c * LANES:(c + 1) * LANES]


def _attn(qkv, bands, g, n, s, dil):
    chunks = ATTN_CHUNKS // dil
    span = chunks * STRIDES
    steps = s // (span * dil)
    slabs = GROUP_COLS // LANES
    out_spec = pl.BlockSpec((slabs, span * dil, LANES), lambda b, k: (0, b * steps + k, 0))
    out_sds = jax.ShapeDtypeStruct((slabs, n * s, LANES), F32)
    return pl.pallas_call(
        functools.partial(_attn_body, dil=dil, chunks=chunks),
        grid=(n, steps),
        in_specs=[pl.BlockSpec((1, dil, span, QKV_COLS), lambda b, k: (b, 0, k, 0)),
                  pl.BlockSpec((1, dil, STRIDES, QKV_COLS),
                               lambda b, k: (b, 0, jnp.maximum(k * chunks - 1, 0), 0)),
                  pl.BlockSpec((None, 2 * STRIDES, HEAD_LANES), lambda b, k: (g, 0, 0))],
        out_specs=[out_spec, out_spec],
        out_shape=[out_sds, out_sds],
        compiler_params=_params("parallel", "parallel"),
        name="attn_d%d" % dil,
    )(qkv, qkv, bands)


def _memkv_body(mem_ref, g_ref, w_ref, o_ref):
    o_ref[...] = _dot(_rms(mem_ref[...], g_ref[...]).astype(BF16), w_ref[...])


def _memkv(mem, g, w):
    m = mem.shape[0]
    row = pl.BlockSpec((MEM_LEN, D_MODEL), lambda i: (i, 0))
    return pl.pallas_call(
        _memkv_body,
        grid=(m // MEM_LEN,),
        in_specs=[row, _resident((1, D_MODEL)), _resident((D_MODEL, 2 * MEM_WIDTH))],
        out_specs=pl.BlockSpec((MEM_LEN, 2 * MEM_WIDTH), lambda i: (i, 0)),
        out_shape=jax.ShapeDtypeStruct((m, 2 * MEM_WIDTH), F32),
        compiler_params=_params("parallel"),
        name="memkv",
    )(mem, g, w)


def _merge_math(a, pooled, c, x, gmix_ref, win_ref, wpool_ref, scale_ref, woa_ref, wob_ref,
                woc_ref, wout_ref):
    u = _rms(x, gmix_ref[...]).astype(BF16)
    mixed = [_dot(pooled[gi].astype(BF16), wpool_ref[gi]) for gi in range(len(POOL_WINDOWS))]
    b = jnp.concatenate(mixed, axis=1) * scale_ref[...]
    m = None
    for k, (branch, wo_ref) in enumerate(((a, woa_ref), (b, wob_ref), (c, woc_ref))):
        lo = GATE_OFF + k * D_MODEL
        gate = jax.nn.sigmoid(_dot(u, win_ref[:, lo:lo + D_MODEL]))
        term = gate * _dot(branch.astype(BF16), wo_ref[...])
        m = term if m is None else m + term
    return x + _dot(m.astype(BF16), wout_ref[...])


def _mix_body(o0, o1, o2, l0, l1, l2, z_ref, halo_ref, qm_ref, mkv_ref, x_ref, *rest):
    merge_refs, out_ref = rest[:-1], rest[-1]
    j = pl.program_id(1)
    tm = x_ref.shape[0]
    unslab = lambda ref: jnp.concatenate([ref[c] for c in range(ref.shape[0])], axis=1)
    lses = [unslab(l0), unslab(l1), unslab(l2)]
    mx = jnp.maximum(jnp.maximum(lses[0], lses[1]), lses[2])
    es = [jnp.exp(l - mx) for l in lses]
    a = ((es[0] * unslab(o0) + es[1] * unslab(o1) + es[2] * unslab(o2))
         / (es[0] + es[1] + es[2]))
    z = z_ref[...]
    halo = jnp.where(j == 0, 0.0, halo_ref[...])
    zc = jnp.concatenate([halo, z], axis=0)
    pos = j * tm + lax.broadcasted_iota(jnp.int32, (tm, 1), 0)
    pooled = []
    for gi, kw in enumerate(POOL_WINDOWS):
        cs = slice(gi * POOL_GROUP, (gi + 1) * POOL_GROUP)
        run = zc[:, cs]
        width = 1
        while width < kw:
            run = run[width:] + run[:-width]
            width *= 2
        first = POOL_HALO - (kw - 1)
        cnt = jnp.minimum(kw, pos + 1).astype(F32)
        pooled.append(run[first:first + tm] / cnt - z[:, cs])
    mkv = mkv_ref[...].astype(BF16)
    qm = qm_ref[...]
    cs_out = []
    for h in range(MEM_HEADS):
        hs = slice(h * MEM_HEAD_DIM, (h + 1) * MEM_HEAD_DIM)
        vs = slice(MEM_WIDTH + h * MEM_HEAD_DIM, MEM_WIDTH + (h + 1) * MEM_HEAD_DIM)
        s = _dot_t(qm[:, hs], mkv[:, hs]) * (1.0 / math.sqrt(MEM_HEAD_DIM))
        mm = jnp.max(s, axis=-1, keepdims=True)
        p = jnp.exp(s - mm)
        l = jnp.sum(p, axis=-1, keepdims=True)
        cs_out.append(_dot(p.astype(BF16), mkv[:, vs]) / l)
    c = jnp.concatenate(cs_out, axis=1)
    out_ref[...] = _merge_math(a, pooled, c, x_ref[...], *merge_refs)


def _merge_weight_specs():
    return [_resident((1, D_MODEL)), _resident((D_MODEL, IN_COLS)),
            _resident((len(POOL_WINDOWS), POOL_GROUP, POOL_GROUP)), _resident((1, POOL_WIDTH)),
            _resident((GROUP_COLS, D_MODEL)), _resident((POOL_WIDTH, D_MODEL)),
            _resident((MEM_WIDTH, D_MODEL)), _resident((D_MODEL, D_MODEL))]


def _mix_prompt(os, lses, z, qm, mkv, x, weights, n, s):
    tm = TOKEN_TILE
    tj = s // tm
    halo_per_tile = tm // POOL_HALO

    def tok(cols):
        return pl.BlockSpec((tm, cols), lambda b, j: (b * tj + j, 0))

    halo = pl.BlockSpec((POOL_HALO, POOL_WIDTH),
                        lambda b, j: (jnp.maximum((b * tj + j) * halo_per_tile - 1, 0), 0))
    slab = pl.BlockSpec((GROUP_COLS // LANES, tm, LANES), lambda b, j: (0, b * tj + j, 0))
    in_specs = ([slab] * 6 + [tok(POOL_WIDTH), halo, tok(MEM_WIDTH),
                pl.BlockSpec((MEM_LEN, 2 * MEM_WIDTH), lambda b, j: (b, 0)),
                tok(D_MODEL)] + _merge_weight_specs())
    return pl.pallas_call(
        _mix_body,
        grid=(n, tj),
        in_specs=in_specs,
        out_specs=tok(D_MODEL),
        out_shape=jax.ShapeDtypeStruct((n * s, D_MODEL), F32),
        compiler_params=_params("parallel", "parallel"),
        name="mix_prompt",
    )(*os, *lses, z, z, qm, mkv, x, *weights)


def _sample_branch_body(qkv0, qkv1, qkv2, c0, c1, c2, b0, b1, b2, bias0_ref, z_ref, st_ref,
                        qm_ref, cm_ref, a_ref, pooled_ref, c_ref):
    b = pl.program_id(0)
    row = pl.ds(b, 1)
    eye = (lax.broadcasted_iota(jnp.int32, (HEAD_DIM, HEAD_DIM), 0)
           == lax.broadcasted_iota(jnp.int32, (HEAD_DIM, HEAD_DIM), 1))
    cube = (HEADS, HEAD_DIM, HEAD_DIM)

    def heads_of(vec, lo):
        return jnp.stack([vec[:, lo + h * HEAD_DIM:lo + (h + 1) * HEAD_DIM]
                          for h in range(HEADS)], axis=0)

    outs, lses = [], []
    for g, (qkv_ref, cache_ref, bias_ref) in enumerate(
            ((qkv0, c0, b0), (qkv1, c1, b1), (qkv2, c2, b2))):
        qkv = qkv_ref[row, :]
        q, kn, vn = heads_of(qkv, 0), heads_of(qkv, GROUP_COLS), heads_of(qkv, 2 * GROUP_COLS)
        q_col = jnp.sum(jnp.where(eye, jnp.broadcast_to(q, cube), 0.0), axis=2, keepdims=True)
        s = jnp.sum(cache_ref[0, 0] * q_col, axis=1, keepdims=True) + bias_ref[...]
        sn = jnp.sum(kn * q, axis=2, keepdims=True) + bias0_ref[g]
        m = jnp.maximum(jnp.max(s, axis=2, keepdims=True), sn)
        p = jnp.exp(s - m)
        pn = jnp.exp(sn - m)
        l = jnp.sum(p, axis=2, keepdims=True) + pn
        pv = jnp.sum(cache_ref[0, 1] * p, axis=2, keepdims=True)
        pv_row = jnp.sum(jnp.where(eye, jnp.broadcast_to(pv, cube), 0.0), axis=1, keepdims=True)
        outs.append((pv_row + pn * vn) / l)
        lses.append(m + jnp.log(l))
    mx = jnp.maximum(jnp.maximum(lses[0], lses[1]), lses[2])
    es = [jnp.exp(lse - mx) for lse in lses]
    a = (es[0] * outs[0] + es[1] * outs[1] + es[2] * outs[2]) / (es[0] + es[1] + es[2])
    a_ref[row, :] = jnp.concatenate([a[h] for h in range(HEADS)], axis=1)

    zn = z_ref[row, :]
    st = st_ref[:, row, :]
    pooled = []
    for gi, kw in enumerate(POOL_WINDOWS):
        cs = slice(gi * POOL_GROUP, (gi + 1) * POOL_GROUP)
        tot = jnp.sum(st[POOL_STATE - (kw - 1):, :, cs], axis=0) + zn[:, cs]
        pooled.append(tot / float(min(kw, PAST_LEN + 1)) - zn[:, cs])
    pooled_ref[row, :] = jnp.concatenate(pooled, axis=1)

    qm_row = qm_ref[row, :]
    qm = jnp.concatenate([qm_row[:, h * MEM_HEAD_DIM:(h + 1) * MEM_HEAD_DIM]
                          for h in range(MEM_HEADS)], axis=0)
    km = cm_ref[0, :, 0]
    vm = cm_ref[0, :, 1]
    s = jnp.sum(km * qm[None], axis=-1, keepdims=True) * (1.0 / math.sqrt(MEM_HEAD_DIM))
    m = jnp.max(s, axis=0)
    p = jnp.exp(s - m[None])
    c = jnp.sum(p * vm, axis=0) / jnp.sum(p, axis=0)
    c_ref[row, :] = jnp.concatenate([c[h:h + 1] for h in range(MEM_HEADS)], axis=1)


def _sample_branches(qkvs, caches, biases, bias0, z, state, qm, cache_mem):
    nb = z.shape[0]

    def per_request(shape):
        zeros = (0,) * len(shape)
        return pl.BlockSpec((1,) + shape, lambda b: (b,) + zeros)

    def whole(shape):
        zeros = (0,) * len(shape)
        return pl.BlockSpec(shape, lambda b: zeros)

    in_specs = ([whole(t.shape) for t in qkvs]
                + [per_request(c.shape[1:]) for c in caches]
                + [whole(t.shape) for t in biases]
                + [whole(bias0.shape), whole(z.shape), whole(state.shape), whole(qm.shape),
                   per_request((MEM_LEN, 2, MEM_HEADS, MEM_HEAD_DIM))])
    widths = (GROUP_COLS, POOL_WIDTH, MEM_WIDTH)
    return pl.pallas_call(
        _sample_branch_body,
        grid=(nb,),
        in_specs=in_specs,
        out_specs=[whole((nb, w)) for w in widths],
        out_shape=[jax.ShapeDtypeStruct((nb, w), F32) for w in widths],
        compiler_params=_params("arbitrary"),
        name="sample_branches",
    )(*qkvs, *caches, *biases, bias0, z, state, qm, cache_mem)


def _sample_merge_body(a_ref, pooled_ref, c_ref, x_ref, *rest):
    merge_refs, out_ref = rest[:-1], rest[-1]
    pooled_all = pooled_ref[...]
    pooled = [pooled_all[:, gi * POOL_GROUP:(gi + 1) * POOL_GROUP]
              for gi in range(len(POOL_WINDOWS))]
    out_ref[...] = _merge_math(a_ref[...], pooled, c_ref[...], x_ref[...], *merge_refs)


def _sample_merge(a, pooled, c, x, weights):
    nb = x.shape[0]
    full = lambda cols: _resident((nb, cols))
    return pl.pallas_call(
        _sample_merge_body,
        grid=(1,),
        in_specs=[full(GROUP_COLS), full(POOL_WIDTH), full(MEM_WIDTH), full(D_MODEL)]
        + _merge_weight_specs(),
        out_specs=pl.BlockSpec((nb, D_MODEL), lambda i: (0, 0)),
        out_shape=jax.ShapeDtypeStruct((nb, D_MODEL), F32),
        compiler_params=_params("arbitrary"),
        name="sample_merge",
    )(a, pooled, c, x, *weights)


def _rel_bucket(n):
    max_exact = N_BUCKETS // 2
    nf = jnp.maximum(n, 1).astype(F32)
    large = max_exact + (jnp.log(nf / max_exact) / math.log(MAX_DISTANCE / max_exact)
                         * (N_BUCKETS - max_exact)).astype(jnp.int32)
    large = jnp.minimum(large, N_BUCKETS - 1)
    return jnp.where(n < max_exact, n, large)


def _stride_bias(rel_bias, g, dil):
    j = jnp.arange(STRIDES + 1, dtype=jnp.int32)
    return rel_bias[_rel_bucket(j * dil)][:, g * HEADS:(g + 1) * HEADS].astype(F32)


def _band_row(bias_j):
    row = jnp.concatenate([bias_j[::-1], jnp.full((STRIDES - 1, HEADS), NEG_INF, F32)], axis=0)
    return row.T.reshape(HEADS, 1, 2 * STRIDES)


def _cache_bias(bias_j, dil):
    on_grid = bias_j[STRIDES:0:-1].T
    full = jnp.full((HEADS, STRIDES, dil), NEG_INF, F32).at[:, :, 0].set(on_grid)
    return full.reshape(HEADS, 1, STRIDES * dil)


def kernel(x_prompt, x_sample, cache_win0_kv, cache_win1_kv, cache_win2_kv, state_pool, cache_mem_kv, mem_prompt, rel_bias, g_ffn1, w1_gate, w1_up, w1_down, g_mix, w_in, w_pool, pool_scale, g_mem, w_mem_kv, w_oa, w_ob, w_oc, w_out, g_ffn2, w2_gate, w2_up, w2_down, g_final):
    n, s, _ = x_prompt.shape
    nb = x_sample.shape[0]
    depth = g_ffn1.shape[0]
    win_caches = (cache_win0_kv, cache_win1_kv, cache_win2_kv)
    bias_js = [_stride_bias(rel_bias, g, dil) for g, (_, dil) in enumerate(DIL_GROUPS)]
    bands = _bands(jnp.stack([_band_row(b) for b in bias_js]))
    bias_cache = [_cache_bias(b, dil) for b, (_, dil) in zip(bias_js, DIL_GROUPS)]
    bias_new = jnp.stack([b[0] for b in bias_js]).reshape(N_GROUPS, HEADS, 1, 1)
    gfin = g_final.reshape(1, D_MODEL)

    xp = x_prompt.reshape(n * s, D_MODEL)
    xs = x_sample.reshape(nb, D_MODEL)
    st_p = [[] for _ in range(5)]
    st_s = [[] for _ in range(4)]
    for l in range(depth):
        last = l == depth - 1
        bf = lambda w: w[l].astype(BF16)
        vec = lambda v: v[l].reshape(1, -1)
        w1 = (bf(w1_gate), bf(w1_up), bf(w1_down))
        w2 = (bf(w2_gate), bf(w2_up), bf(w2_down))
        win_l = bf(w_in)
        merge_w = (vec(g_mix), win_l, bf(w_pool), vec(pool_scale), bf(w_oa), bf(w_ob),
                   bf(w_oc), bf(w_out))
        q_scale = 1.0 / math.sqrt(HEAD_DIM)

        xp = _ffn(xp, vec(g_ffn1), *w1)
        res = _inproj(xp, vec(g_mix), win_l, n, s, [min(w, s) for w, _ in DIL_GROUPS],
                      [d for _, d in DIL_GROUPS], BF16, q_scale * LOG2E)
        qkvs, kvwins, z, qm = res[0:3], res[3:6], res[6], res[7]
        os, lses = [], []
        for g, (_, dil) in enumerate(DIL_GROUPS):
            o, lse = _attn(qkvs[g], bands, g, n, s, dil)
            os.append(o)
            lses.append(lse)
        mkv = _memkv(mem_prompt.reshape(n * MEM_LEN, D_MODEL), vec(g_mem), bf(w_mem_kv))
        xp = _mix_prompt(os, lses, z, qm, mkv, xp, merge_w, n, s)
        xp = _ffn(xp, vec(g_ffn2), *w2, g_final=gfin if last else None)
        for g, (win, _) in enumerate(DIL_GROUPS):
            kv_t = kvwins[g].reshape(n, 2, HEADS, HEAD_DIM, min(win, s))
            st_p[g].append(jnp.transpose(kv_t, (0, 4, 1, 2, 3)))
        st_p[3].append(z.reshape(n, s, POOL_WIDTH)[:, s - POOL_STATE:])
        st_p[4].append(mkv.reshape(n, MEM_LEN, 2, MEM_HEADS, MEM_HEAD_DIM))

        xs = _ffn(xs, vec(g_ffn1), *w1)
        res = _inproj(xs, vec(g_mix), win_l, 1, nb, None, [1] * N_GROUPS, F32, q_scale)
        qkvs, z, qm = [t.reshape(nb, QKV_COLS) for t in res[0:3]], res[3], res[4]
        kvns = [t[:, GROUP_COLS:].reshape(nb, 2, HEADS, HEAD_DIM) for t in qkvs]
        a, pooled, c = _sample_branches(
            qkvs, [jnp.transpose(cw[l], (0, 2, 3, 4, 1)) for cw in win_caches], bias_cache,
            bias_new, z, jnp.transpose(state_pool[l], (1, 0, 2)), qm, cache_mem_kv[l])
        xs = _sample_merge(a, pooled, c, xs, merge_w)
        xs = _ffn(xs, vec(g_ffn2), *w2, g_final=gfin if last else None)
        for g in range(N_GROUPS):
            st_s[g].append(kvns[g].reshape(nb, 1, 2, HEADS, HEAD_DIM))
        st_s[3].append(z.reshape(nb, 1, POOL_WIDTH))

    y_prompt = xp.reshape(n, s, D_MODEL)
    y_sample = xs.reshape(nb, 1, D_MODEL)
    stack = lambda ts: jnp.stack(ts, axis=0)
    return (y_prompt, y_sample, stack(st_p[0]), stack(st_p[1]), stack(st_p[2]), stack(st_p[3]),
            stack(st_p[4]), stack(st_s[0]), stack(st_s[1]), stack(st_s[2]), stack(st_s[3]))
```

```python
import functools
import math

import jax
import jax.numpy as jnp
from jax import lax
from jax.experimental import pallas as pl
from jax.experimental.pallas import tpu as pltpu

F32 = jnp.float32
BF16 = jnp.bfloat16

D_MODEL = 1024
D_FF = 2816
LANES = 128
HEAD_DIM = 64
HEADS = 4
DIL_GROUPS = ((128, 1), (512, 4), (2048, 16))
N_GROUPS = 3
GROUP_COLS = HEADS * HEAD_DIM
QKV_COLS = 3 * GROUP_COLS
STRIDES = 128
POOL_WINDOWS = (2, 4, 8, 16)
POOL_GROUP = 128
POOL_WIDTH = 512
POOL_STATE = 15
POOL_HALO = 16
MEM_LEN = 256
MEM_HEADS = 4
MEM_HEAD_DIM = 128
MEM_WIDTH = 512
N_BUCKETS = 32
MAX_DISTANCE = 2048
N_BRANCH = 3
EPS = 1e-6
NEG_INF = -1e30
LOG2E = math.log2(math.e)
LN2 = math.log(2.0)
PAST_LEN = 8192

V7X_VMEM_LIMIT_BYTES = 56 * 1024 * 1024
TOKEN_TILE = 512


def _params(*sem):
    return pltpu.CompilerParams(dimension_semantics=sem,
                                vmem_limit_bytes=V7X_VMEM_LIMIT_BYTES)


def _resident(shape):
    zeros = (0,) * len(shape)
    return pl.BlockSpec(shape, lambda *_: zeros, pipeline_mode=pl.Buffered(1))


def _rms(x, g):
    return x * lax.rsqrt(jnp.mean(x * x, axis=-1, keepdims=True) + EPS) * g


def _dot(a, b):
    return jnp.dot(a, b, preferred_element_type=F32)


def _dot_t(a, b):
    return lax.dot_general(a, b, (((1,), (1,)), ((), ())), preferred_element_type=F32)


def _ffn_body(x_ref, g_ref, wg_ref, wu_ref, wd_ref, *rest, final):
    o_ref = rest[-1]
    x = x_ref[...]
    h = _rms(x, g_ref[...]).astype(BF16)
    a = _dot(h, wg_ref[...])
    b = _dot(h, wu_ref[...])
    act = (a * jax.nn.sigmoid(a) * b).astype(BF16)
    y = x + 0.5 * _dot(act, wd_ref[...])
    if final:
        y = _rms(y, rest[0][...])
    o_ref[...] = y


def _ffn(x, g, wg, wu, wd, g_final=None):
    m = x.shape[0]
    tm = min(TOKEN_TILE, m)
    final = g_final is not None
    row = pl.BlockSpec((tm, D_MODEL), lambda i: (i, 0))
    in_specs = [row, _resident((1, D_MODEL)), _resident((D_MODEL, D_FF)),
                _resident((D_MODEL, D_FF)), _resident((D_FF, D_MODEL))]
    args = [x, g, wg, wu, wd]
    if final:
        in_specs.append(_resident((1, D_MODEL)))
        args.append(g_final)
    return pl.pallas_call(
        functools.partial(_ffn_body, final=final),
        grid=(m // tm,),
        in_specs=in_specs,
        out_specs=row,
        out_shape=jax.ShapeDtypeStruct((m, D_MODEL), F32),
        compiler_params=_params("parallel"),
        name="ffn_final" if final else "ffn",
    )(*args)


Z_OFF = N_GROUPS * QKV_COLS
QM_OFF = Z_OFF + POOL_WIDTH
GATE_OFF = QM_OFF + MEM_WIDTH
IN_COLS = GATE_OFF + N_BRANCH * D_MODEL


def _inproj_body(x_ref, g_ref, w_ref, *rest, q_scale, with_windows):
    qkv_refs = rest[0:N_GROUPS]
    kv_refs = rest[N_GROUPS:2 * N_GROUPS] if with_windows else (None,) * N_GROUPS
    z_ref, qm_ref, u_scr, kv_scr = rest[-4:]
    uf = _rms(x_ref[...], g_ref[...])
    u = uf.astype(BF16)
    tm = u.shape[0]
    if any(ref.shape[1] > 1 for ref in qkv_refs):
        for c in range(D_MODEL // LANES):
            u_scr[c] = uf[:, c * LANES:(c + 1) * LANES]
    width = N_GROUPS * GROUP_COLS
    for g, (qkv_ref, kv_ref) in enumerate(zip(qkv_refs, kv_refs)):
        dil, per_class = qkv_ref.shape[1], qkv_ref.shape[2]
        w_g = jnp.concatenate(
            [w_ref[:, t * width + g * GROUP_COLS:t * width + (g + 1) * GROUP_COLS]
             for t in range(3)], axis=1)
        rows = kv_ref.shape[2] if with_windows else 0

        def put_qkv(r, blk):
            qkv_ref[0, r, :, 0:GROUP_COLS] = (blk[:, :GROUP_COLS] * q_scale).astype(qkv_ref.dtype)
            qkv_ref[0, r, :, GROUP_COLS:] = blk[:, GROUP_COLS:].astype(qkv_ref.dtype)

        if dil == 1:
            p = _dot(u, w_g)
            put_qkv(0, p)
            if with_windows:
                kv_ref[0] = p[tm - rows:, GROUP_COLS:].T
        else:
            u_cls = jnp.concatenate(
                [jnp.concatenate([u_scr[c, pl.ds(r, per_class, stride=dil), :]
                                  for c in range(D_MODEL // LANES)], axis=1)
                 for r in range(dil)], axis=0)
            p = _dot(u_cls.astype(BF16), w_g)
            assert with_windows and rows == tm
            for r in range(dil):
                blk = p[r * per_class:(r + 1) * per_class]
                put_qkv(r, blk)
                for c in range(2 * GROUP_COLS // LANES):
                    lo = GROUP_COLS + c * LANES
                    kv_scr[c, pl.ds(r, per_class, stride=dil), :] = blk[:, lo:lo + LANES]
            kv_ref[0] = jnp.concatenate(
                [kv_scr[c] for c in range(2 * GROUP_COLS // LANES)], axis=1).T
    z_ref[...] = _dot(u, w_ref[:, Z_OFF:QM_OFF])
    qm_ref[...] = _dot(u, w_ref[:, QM_OFF:GATE_OFF]).astype(qm_ref.dtype)


def _inproj(x, g, w, n, s, windows, dils, q_dtype, q_scale):
    tm = min(TOKEN_TILE, s)
    tj = s // tm

    def tok(cols):
        return pl.BlockSpec((tm, cols), lambda b, j: (b * tj + j, 0))

    qkv_specs = [pl.BlockSpec((1, d, tm // d, QKV_COLS), lambda b, j: (b, 0, j, 0)) for d in dils]
    qkv_shapes = [jax.ShapeDtypeStruct((n, d, s // d, QKV_COLS), q_dtype) for d in dils]

    kv_specs, kv_shapes = [], []
    for win in windows or ():
        rows = min(win, tm)
        assert tm % rows == 0 and win % rows == 0
        first = (s - win) // tm
        if win >= tm:
            spec = pl.BlockSpec((1, 2 * GROUP_COLS, rows),
                                lambda b, j, first=first: (b, 0, jnp.maximum(j - first, 0)))
        else:
            spec = pl.BlockSpec((1, 2 * GROUP_COLS, rows), lambda b, j: (b, 0, 0))
        kv_specs.append(spec)
        kv_shapes.append(jax.ShapeDtypeStruct((n, 2 * GROUP_COLS, win), F32))
    m = n * s
    out_shape = (qkv_shapes + kv_shapes + [
        jax.ShapeDtypeStruct((m, POOL_WIDTH), F32),
        jax.ShapeDtypeStruct((m, MEM_WIDTH), q_dtype)])
    out_specs = qkv_specs + kv_specs + [tok(POOL_WIDTH), tok(MEM_WIDTH)]
    return pl.pallas_call(
        functools.partial(_inproj_body, q_scale=q_scale, with_windows=bool(windows)),
        grid=(n, tj),
        in_specs=[tok(D_MODEL), _resident((1, D_MODEL)),
                  pl.BlockSpec((D_MODEL, GATE_OFF), lambda b, j: (0, 0),
                               pipeline_mode=pl.Buffered(1))],
        out_specs=out_specs,
        out_shape=out_shape,
        scratch_shapes=[pltpu.VMEM((D_MODEL // LANES, tm, LANES), F32),
                        pltpu.VMEM((2 * GROUP_COLS // LANES, tm, LANES), F32)],
        compiler_params=_params("arbitrary", "arbitrary"),
        name="inproj",
    )(x, g, w)


ATTN_CHUNKS = 16
HEAD_LANES = HEADS * STRIDES


def _band_body(row_ref, band_ref):
    for g in range(N_GROUPS):
        for h in range(HEADS):
            rows = jnp.broadcast_to(row_ref[g, h] * LOG2E, (STRIDES, 2 * STRIDES))
            band = pltpu.roll(rows, 0, 1, stride=1, stride_axis=0).T
            band_ref[g, :, h * STRIDES:(h + 1) * STRIDES] = band


def _bands(rows):
    return pl.pallas_call(
        _band_body,
        grid=(1,),
        in_specs=[_resident(rows.shape)],
        out_specs=pl.BlockSpec((N_GROUPS, 2 * STRIDES, HEAD_LANES), lambda i: (0, 0, 0)),
        out_shape=jax.ShapeDtypeStruct((N_GROUPS, 2 * STRIDES, HEAD_LANES), F32),
        compiler_params=_params("arbitrary"),
        name="bands",
    )(rows)


def _attn_body(qkv_ref, prev_ref, band_ref, o_ref, lse_ref, *, dil, chunks):
    first = pl.program_id(1) == 0
    lane_head = lax.broadcasted_iota(jnp.int32, (1, GROUP_COLS), 1) // HEAD_DIM

    def keys_of(r):
        return jnp.concatenate([prev_ref[0, r, :, GROUP_COLS:2 * GROUP_COLS],
                                qkv_ref[0, r, :, GROUP_COLS:2 * GROUP_COLS]], axis=0)

    def scores(k_all, r, i):
        q = qkv_ref[0, r, i * STRIDES:(i + 1) * STRIDES, 0:GROUP_COLS]
        qm = jnp.concatenate(
            [jnp.where(lane_head == h, q, jnp.zeros_like(q)) for h in range(HEADS)], axis=0)
        return _dot_t(k_all[i * STRIDES:(i + 2) * STRIDES], qm)

    order = [(r, i) for r in range(dil) for i in range(chunks)]
    k_all = keys_of(0)
    st_next = scores(k_all, 0, 0)
    for idx, (r, i) in enumerate(order):
        st = st_next
        if i == 0:
            v_all = jnp.concatenate([prev_ref[0, r, :, 2 * GROUP_COLS:],
                                     qkv_ref[0, r, :, 2 * GROUP_COLS:]], axis=0)
            vt_all = v_all.astype(F32).T.astype(BF16)
        if idx + 1 < len(order):
            r2, i2 = order[idx + 1]
            if i2 == 0:
                k_all = keys_of(r2)
            st_next = scores(k_all, r2, i2)
        if True:
            vt2 = vt_all[:, i * STRIDES:(i + 2) * STRIDES]
            o_parts, lse_parts = [], []
            for h in range(HEADS):
                hl = slice(h * STRIDES, (h + 1) * STRIDES)
                s_h = st[:, hl] + band_ref[:, hl]
                if i == 0:
                    s_h = jnp.concatenate(
                        [jnp.where(first, NEG_INF, s_h[:STRIDES]), s_h[STRIDES:]], axis=0)
                m = jnp.max(s_h, axis=0, keepdims=True)
                e = jnp.exp2(s_h - m)
                l = jnp.sum(e, axis=0, keepdims=True)
                ot = _dot(vt2[h * HEAD_DIM:(h + 1) * HEAD_DIM], e.astype(BF16))
                o_parts.append(ot * (1.0 / l))
                lse_parts.append(jnp.broadcast_to(m * LN2 + jnp.log(l), (HEAD_DIM, STRIDES)))
            rows = pl.ds(i * STRIDES * dil + r, STRIDES, stride=dil)
            o_rows = jnp.concatenate(o_parts, axis=0).T
            lse_rows = jnp.concatenate(lse_parts, axis=0).T
            for c in range(GROUP_COLS // LANES):
                o_ref[c, rows, :] = o_rows[:, c * LANES:(c + 1) * LANES]
                lse_ref[c, rows, :] = lse_rows[:, c * LANES:(c + 1) * LANES]


def _attn(qkv, bands, g, n, s, dil):
    chunks = ATTN_CHUNKS // dil
    span = chunks * STRIDES
    steps = s // (span * dil)
    slabs = GROUP_COLS // LANES
    out_spec = pl.BlockSpec((slabs, span * dil, LANES), lambda b, k: (0, b * steps + k, 0))
    out_sds = jax.ShapeDtypeStruct((slabs, n * s, LANES), F32)
    return pl.pallas_call(
        functools.partial(_attn_body, dil=dil, chunks=chunks),
        grid=(n, steps),
        in_specs=[pl.BlockSpec((1, dil, span, QKV_COLS), lambda b, k: (b, 0, k, 0)),
                  pl.BlockSpec((1, dil, STRIDES, QKV_COLS),
                               lambda b, k: (b, 0, jnp.maximum(k * chunks - 1, 0), 0)),
                  pl.BlockSpec((None, 2 * STRIDES, HEAD_LANES), lambda b, k: (g, 0, 0))],
        out_specs=[out_spec, out_spec],
        out_shape=[out_sds, out_sds],
        compiler_params=_params("parallel", "parallel"),
        name="attn_d%d" % dil,
    )(qkv, qkv, bands)


def _memkv_body(mem_ref, g_ref, w_ref, o_ref):
    o_ref[...] = _dot(_rms(mem_ref[...], g_ref[...]).astype(BF16), w_ref[...])


def _memkv(mem, g, w):
    m = mem.shape[0]
    row = pl.BlockSpec((MEM_LEN, D_MODEL), lambda i: (i, 0))
    return pl.pallas_call(
        _memkv_body,
        grid=(m // MEM_LEN,),
        in_specs=[row, _resident((1, D_MODEL)), _resident((D_MODEL, 2 * MEM_WIDTH))],
        out_specs=pl.BlockSpec((MEM_LEN, 2 * MEM_WIDTH), lambda i: (i, 0)),
        out_shape=jax.ShapeDtypeStruct((m, 2 * MEM_WIDTH), F32),
        compiler_params=_params("parallel"),
        name="memkv",
    )(mem, g, w)


def _merge_math(a, pooled, c, x, gmix_ref, win_ref, wpool_ref, scale_ref, woa_ref, wob_ref,
                woc_ref, wout_ref):
    u = _rms(x, gmix_ref[...]).astype(BF16)
    mixed = [_dot(pooled[gi].astype(BF16), wpool_ref[gi]) for gi in range(len(POOL_WINDOWS))]
    b = jnp.concatenate(mixed, axis=1) * scale_ref[...]
    m = None
    for k, (branch, wo_ref) in enumerate(((a, woa_ref), (b, wob_ref), (c, woc_ref))):
        lo = GATE_OFF + k * D_MODEL
        gate = jax.nn.sigmoid(_dot(u, win_ref[:, lo:lo + D_MODEL]))
        term = gate * _dot(branch.astype(BF16), wo_ref[...])
        m = term if m is None else m + term
    return x + _dot(m.astype(BF16), wout_ref[...])


def _mix_body(o0, o1, o2, l0, l1, l2, z_ref, halo_ref, qm_ref, mkv_ref, x_ref, *rest):
    merge_refs, out_ref = rest[:-1], rest[-1]
    j = pl.program_id(1)
    tm = x_ref.shape[0]
    unslab = lambda ref: jnp.concatenate([ref[c] for c in range(ref.shape[0])], axis=1)
    lses = [unslab(l0), unslab(l1), unslab(l2)]
    mx = jnp.maximum(jnp.maximum(lses[0], lses[1]), lses[2])
    es = [jnp.exp(l - mx) for l in lses]
    a = ((es[0] * unslab(o0) + es[1] * unslab(o1) + es[2] * unslab(o2))
         / (es[0] + es[1] + es[2]))
    z = z_ref[...]
    halo = jnp.where(j == 0, 0.0, halo_ref[...])
    zc = jnp.concatenate([halo, z], axis=0)
    pos = j * tm + lax.broadcasted_iota(jnp.int32, (tm, 1), 0)
    pooled = []
    for gi, kw in enumerate(POOL_WINDOWS):
        cs = slice(gi * POOL_GROUP, (gi + 1) * POOL_GROUP)
        run = zc[:, cs]
        width = 1
        while width < kw:
            run = run[width:] + run[:-width]
            width *= 2
        first = POOL_HALO - (kw - 1)
        cnt = jnp.minimum(kw, pos + 1).astype(F32)
        pooled.append(run[first:first + tm] / cnt - z[:, cs])
    mkv = mkv_ref[...].astype(BF16)
    qm = qm_ref[...]
    cs_out = []
    for h in range(MEM_HEADS):
        hs = slice(h * MEM_HEAD_DIM, (h + 1) * MEM_HEAD_DIM)
        vs = slice(MEM_WIDTH + h * MEM_HEAD_DIM, MEM_WIDTH + (h + 1) * MEM_HEAD_DIM)
        s = _dot_t(qm[:, hs], mkv[:, hs]) * (1.0 / math.sqrt(MEM_HEAD_DIM))
        mm = jnp.max(s, axis=-1, keepdims=True)
        p = jnp.exp(s - mm)
        l = jnp.sum(p, axis=-1, keepdims=True)
        cs_out.append(_dot(p.astype(BF16), mkv[:, vs]) / l)
    c = jnp.concatenate(cs_out, axis=1)
    out_ref[...] = _merge_math(a, pooled, c, x_ref[...], *merge_refs)


def _merge_weight_specs():
    return [_resident((1, D_MODEL)), _resident((D_MODEL, IN_COLS)),
            _resident((len(POOL_WINDOWS), POOL_GROUP, POOL_GROUP)), _resident((1, POOL_WIDTH)),
            _resident((GROUP_COLS, D_MODEL)), _resident((POOL_WIDTH, D_MODEL)),
            _resident((MEM_WIDTH, D_MODEL)), _resident((D_MODEL, D_MODEL))]


def _mix_prompt(os, lses, z, qm, mkv, x, weights, n, s):
    tm = TOKEN_TILE
    tj = s // tm
    halo_per_tile = tm // POOL_HALO

    def tok(cols):
        return pl.BlockSpec((tm, cols), lambda b, j: (b * tj + j, 0))

    halo = pl.BlockSpec((POOL_HALO, POOL_WIDTH),
                        lambda b, j: (jnp.maximum((b * tj + j) * halo_per_tile - 1, 0), 0))
    slab = pl.BlockSpec((GROUP_COLS // LANES, tm, LANES), lambda b, j: (0, b * tj + j, 0))
    in_specs = ([slab] * 6 + [tok(POOL_WIDTH), halo, tok(MEM_WIDTH),
                pl.BlockSpec((MEM_LEN, 2 * MEM_WIDTH), lambda b, j: (b, 0)),
                tok(D_MODEL)] + _merge_weight_specs())
    return pl.pallas_call(
        _mix_body,
        grid=(n, tj),
        in_specs=in_specs,
        out_specs=tok(D_MODEL),
        out_shape=jax.ShapeDtypeStruct((n * s, D_MODEL), F32),
        compiler_params=_params("parallel", "parallel"),
        name="mix_prompt",
    )(*os, *lses, z, z, qm, mkv, x, *weights)


def _sample_branch_body(qkv0, qkv1, qkv2, c0, c1, c2, b0, b1, b2, bias0_ref, z_ref, st_ref,
                        qm_ref, cm_ref, a_ref, pooled_ref, c_ref):
    b = pl.program_id(0)
    row = pl.ds(b, 1)
    eye = (lax.broadcasted_iota(jnp.int32, (HEAD_DIM, HEAD_DIM), 0)
           == lax.broadcasted_iota(jnp.int32, (HEAD_DIM, HEAD_DIM), 1))
    cube = (HEADS, HEAD_DIM, HEAD_DIM)

    def heads_of(vec, lo):
        return jnp.stack([vec[:, lo + h * HEAD_DIM:lo + (h + 1) * HEAD_DIM]
                          for h in range(HEADS)], axis=0)

    outs, lses = [], []
    for g, (qkv_ref, cache_ref, bias_ref) in enumerate(
            ((qkv0, c0, b0), (qkv1, c1, b1), (qkv2, c2, b2))):
        qkv = qkv_ref[row, :]
        q, kn, vn = heads_of(qkv, 0), heads_of(qkv, GROUP_COLS), heads_of(qkv, 2 * GROUP_COLS)
        q_col = jnp.sum(jnp.where(eye, jnp.broadcast_to(q, cube), 0.0), axis=2, keepdims=True)
        s = jnp.sum(cache_ref[0, 0] * q_col, axis=1, keepdims=True) + bias_ref[...]
        sn = jnp.sum(kn * q, axis=2, keepdims=True) + bias0_ref[g]
        m = jnp.maximum(jnp.max(s, axis=2, keepdims=True), sn)
        p = jnp.exp(s - m)
        pn = jnp.exp(sn - m)
        l = jnp.sum(p, axis=2, keepdims=True) + pn
        pv = jnp.sum(cache_ref[0, 1] * p, axis=2, keepdims=True)
        pv_row = jnp.sum(jnp.where(eye, jnp.broadcast_to(pv, cube), 0.0), axis=1, keepdims=True)
        outs.append((pv_row + pn * vn) / l)
        lses.append(m + jnp.log(l))
    mx = jnp.maximum(jnp.maximum(lses[0], lses[1]), lses[2])
    es = [jnp.exp(lse - mx) for lse in lses]
    a = (es[0] * outs[0] + es[1] * outs[1] + es[2] * outs[2]) / (es[0] + es[1] + es[2])
    a_ref[row, :] = jnp.concatenate([a[h] for h in range(HEADS)], axis=1)

    zn = z_ref[row, :]
    st = st_ref[:, row, :]
    pooled = []
    for gi, kw in enumerate(POOL_WINDOWS):
        cs = slice(gi * POOL_GROUP, (gi + 1) * POOL_GROUP)
        tot = jnp.sum(st[POOL_STATE - (kw - 1):, :, cs], axis=0) + zn[:, cs]
        pooled.append(tot / float(min(kw, PAST_LEN + 1)) - zn[:, cs])
    pooled_ref[row, :] = jnp.concatenate(pooled, axis=1)

    qm_row = qm_ref[row, :]
    qm = jnp.concatenate([qm_row[:, h * MEM_HEAD_DIM:(h + 1) * MEM_HEAD_DIM]
                          for h in range(MEM_HEADS)], axis=0)
    km = cm_ref[0, :, 0]
    vm = cm_ref[0, :, 1]
    s = jnp.sum(km * qm[None], axis=-1, keepdims=True) * (1.0 / math.sqrt(MEM_HEAD_DIM))
    m = jnp.max(s, axis=0)
    p = jnp.exp(s - m[None])
    c = jnp.sum(p * vm, axis=0) / jnp.sum(p, axis=0)
    c_ref[row, :] = jnp.concatenate([c[h:h + 1] for h in range(MEM_HEADS)], axis=1)


def _sample_branches(qkvs, caches, biases, bias0, z, state, qm, cache_mem):
    nb = z.shape[0]

    def per_request(shape):
        zeros = (0,) * len(shape)
        return pl.BlockSpec((1,) + shape, lambda b: (b,) + zeros)

    def whole(shape):
        zeros = (0,) * len(shape)
        return pl.BlockSpec(shape, lambda b: zeros)

    in_specs = ([whole(t.shape) for t in qkvs]
                + [per_request(c.shape[1:]) for c in caches]
                + [whole(t.shape) for t in biases]
                + [whole(bias0.shape), whole(z.shape), whole(state.shape), whole(qm.shape),
                   per_request((MEM_LEN, 2, MEM_HEADS, MEM_HEAD_DIM))])
    widths = (GROUP_COLS, POOL_WIDTH, MEM_WIDTH)
    return pl.pallas_call(
        _sample_branch_body,
        grid=(nb,),
        in_specs=in_specs,
        out_specs=[whole((nb, w)) for w in widths],
        out_shape=[jax.ShapeDtypeStruct((nb, w), F32) for w in widths],
        compiler_params=_params("arbitrary"),
        name="sample_branches",
    )(*qkvs, *caches, *biases, bias0, z, state, qm, cache_mem)


def _sample_merge_body(a_ref, pooled_ref, c_ref, x_ref, *rest):
    merge_refs, out_ref = rest[:-1], rest[-1]
    pooled_all = pooled_ref[...]
    pooled = [pooled_all[:, gi * POOL_GROUP:(gi + 1) * POOL_GROUP]
              for gi in range(len(POOL_WINDOWS))]
    out_ref[...] = _merge_math(a_ref[...], pooled, c_ref[...], x_ref[...], *merge_refs)


def _sample_merge(a, pooled, c, x, weights):
    nb = x.shape[0]
    full = lambda cols: _resident((nb, cols))
    return pl.pallas_call(
        _sample_merge_body,
        grid=(1,),
        in_specs=[full(GROUP_COLS), full(POOL_WIDTH), full(MEM_WIDTH), full(D_MODEL)]
        + _merge_weight_specs(),
        out_specs=pl.BlockSpec((nb, D_MODEL), lambda i: (0, 0)),
        out_shape=jax.ShapeDtypeStruct((nb, D_MODEL), F32),
        compiler_params=_params("arbitrary"),
        name="sample_merge",
    )(a, pooled, c, x, *weights)


def _rel_bucket(n):
    max_exact = N_BUCKETS // 2
    nf = jnp.maximum(n, 1).astype(F32)
    large = max_exact + (jnp.log(nf / max_exact) / math.log(MAX_DISTANCE / max_exact)
                         * (N_BUCKETS - max_exact)).astype(jnp.int32)
    large = jnp.minimum(large, N_BUCKETS - 1)
    return jnp.where(n < max_exact, n, large)


def _stride_bias(rel_bias, g, dil):
    j = jnp.arange(STRIDES + 1, dtype=jnp.int32)
    return rel_bias[_rel_bucket(j * dil)][:, g * HEADS:(g + 1) * HEADS].astype(F32)


def _band_row(bias_j):
    row = jnp.concatenate([bias_j[::-1], jnp.full((STRIDES - 1, HEADS), NEG_INF, F32)], axis=0)
    return row.T.reshape(HEADS, 1, 2 * STRIDES)


def _cache_bias(bias_j, dil):
    on_grid = bias_j[STRIDES:0:-1].T
    full = jnp.full((HEADS, STRIDES, dil), NEG_INF, F32).at[:, :, 0].set(on_grid)
    return full.reshape(HEADS, 1, STRIDES * dil)


def kernel(x_prompt, x_sample, cache_win0_kv, cache_win1_kv, cache_win2_kv, state_pool, cache_mem_kv, mem_prompt, rel_bias, g_ffn1, w1_gate, w1_up, w1_down, g_mix, w_in, w_pool, pool_scale, g_mem, w_mem_kv, w_oa, w_ob, w_oc, w_out, g_ffn2, w2_gate, w2_up, w2_down, g_final):
    n, s, _ = x_prompt.shape
    nb = x_sample.shape[0]
    depth = g_ffn1.shape[0]
    win_caches = (cache_win0_kv, cache_win1_kv, cache_win2_kv)
    bias_js = [_stride_bias(rel_bias, g, dil) for g, (_, dil) in enumerate(DIL_GROUPS)]
    bands = _bands(jnp.stack([_band_row(b) for b in bias_js]))
    bias_cache = [_cache_bias(b, dil) for b, (_, dil) in zip(bias_js, DIL_GROUPS)]
    bias_new = jnp.stack([b[0] for b in bias_js]).reshape(N_GROUPS, HEADS, 1, 1)
    gfin = g_final.reshape(1, D_MODEL)

    xp = x_prompt.reshape(n * s, D_MODEL)
    xs = x_sample.reshape(nb, D_MODEL)
    st_p = [[] for _ in range(5)]
    st_s = [[] for _ in range(4)]
    for l in range(depth):
        last = l == depth - 1
        bf = lambda w: w[l].astype(BF16)
        vec = lambda v: v[l].reshape(1, -1)
        w1 = (bf(w1_gate), bf(w1_up), bf(w1_down))
        w2 = (bf(w2_gate), bf(w2_up), bf(w2_down))
        win_l = bf(w_in)
        merge_w = (vec(g_mix), win_l, bf(w_pool), vec(pool_scale), bf(w_oa), bf(w_ob),
                   bf(w_oc), bf(w_out))
        q_scale = 1.0 / math.sqrt(HEAD_DIM)

        xp = _ffn(xp, vec(g_ffn1), *w1)
        res = _inproj(xp, vec(g_mix), win_l, n, s, [min(w, s) for w, _ in DIL_GROUPS],
                      [d for _, d in DIL_GROUPS], BF16, q_scale * LOG2E)
        qkvs, kvwins, z, qm = res[0:3], res[3:6], res[6], res[7]
        os, lses = [], []
        for g, (_, dil) in enumerate(DIL_GROUPS):
            o, lse = _attn(qkvs[g], bands, g, n, s, dil)
            os.append(o)
            lses.append(lse)
        mkv = _memkv(mem_prompt.reshape(n * MEM_LEN, D_MODEL), vec(g_mem), bf(w_mem_kv))
        xp = _mix_prompt(os, lses, z, qm, mkv, xp, merge_w, n, s)
        xp = _ffn(xp, vec(g_ffn2), *w2, g_final=gfin if last else None)
        for g, (win, _) in enumerate(DIL_GROUPS):
            kv_t = kvwins[g].reshape(n, 2, HEADS, HEAD_DIM, min(win, s))
            st_p[g].append(jnp.transpose(kv_t, (0, 4, 1, 2, 3)))
        st_p[3].append(z.reshape(n, s, POOL_WIDTH)[:, s - POOL_STATE:])
        st_p[4].append(mkv.reshape(n, MEM_LEN, 2, MEM_HEADS, MEM_HEAD_DIM))

        xs = _ffn(xs, vec(g_ffn1), *w1)
        res = _inproj(xs, vec(g_mix), win_l, 1, nb, None, [1] * N_GROUPS, F32, q_scale)
        qkvs, z, qm = [t.reshape(nb, QKV_COLS) for t in res[0:3]], res[3], res[4]
        kvns = [t[:, GROUP_COLS:].reshape(nb, 2, HEADS, HEAD_DIM) for t in qkvs]
        a, pooled, c = _sample_branches(
            qkvs, [jnp.transpose(cw[l], (0, 2, 3, 4, 1)) for cw in win_caches], bias_cache,
            bias_new, z, jnp.transpose(state_pool[l], (1, 0, 2)), qm, cache_mem_kv[l])
        xs = _sample_merge(a, pooled, c, xs, merge_w)
        xs = _ffn(xs, vec(g_ffn2), *w2, g_final=gfin if last else None)
        for g in range(N_GROUPS):
            st_s[g].append(kvns[g].reshape(nb, 1, 2, HEADS, HEAD_DIM))
        st_s[3].append(z.reshape(nb, 1, POOL_WIDTH))

    y_prompt = xp.reshape(n, s, D_MODEL)
    y_sample = xs.reshape(nb, 1, D_MODEL)
    stack = lambda ts: jnp.stack(ts, axis=0)
    return (y_prompt, y_sample, stack(st_p[0]), stack(st_p[1]), stack(st_p[2]), stack(st_p[3]),
            stack(st_p[4]), stack(st_s[0]), stack(st_s[1]), stack(st_s[2]), stack(st_s[3]))
```

```python
import functools
import math

import jax
import jax.numpy as jnp
from jax import lax
from jax.experimental import pallas as pl
from jax.experimental.pallas import tpu as pltpu

F32 = jnp.float32
BF16 = jnp.bfloat16

D_MODEL = 1024
D_FF = 2816
LANES = 128
HEAD_DIM = 64
HEADS = 4
DIL_GROUPS = ((128, 1), (512, 4), (2048, 16))
N_GROUPS = 3
GROUP_COLS = HEADS * HEAD_DIM
QKV_COLS = 3 * GROUP_COLS
STRIDES = 128
POOL_WINDOWS = (2, 4, 8, 16)
POOL_GROUP = 128
POOL_WIDTH = 512
POOL_STATE = 15
POOL_HALO = 16
MEM_LEN = 256
MEM_HEADS = 4
MEM_HEAD_DIM = 128
MEM_WIDTH = 512
N_BUCKETS = 32
MAX_DISTANCE = 2048
N_BRANCH = 3
EPS = 1e-6
NEG_INF = -1e30
LOG2E = math.log2(math.e)
LN2 = math.log(2.0)
PAST_LEN = 8192

V7X_VMEM_LIMIT_BYTES = 56 * 1024 * 1024
TOKEN_TILE = 512


def _params(*sem):
    return pltpu.CompilerParams(dimension_semantics=sem,
                                vmem_limit_bytes=V7X_VMEM_LIMIT_BYTES)


def _resident(shape):
    zeros = (0,) * len(shape)
    return pl.BlockSpec(shape, lambda *_: zeros, pipeline_mode=pl.Buffered(1))


def _rms(x, g):
    return x * lax.rsqrt(jnp.mean(x * x, axis=-1, keepdims=True) + EPS) * g


def _dot(a, b):
    return jnp.dot(a, b, preferred_element_type=F32)


def _dot_t(a, b):
    return lax.dot_general(a, b, (((1,), (1,)), ((), ())), preferred_element_type=F32)


def _ffn_body(x_ref, g_ref, wg_ref, wu_ref, wd_ref, *rest, final):
    o_ref = rest[-1]
    x = x_ref[...]
    h = _rms(x, g_ref[...]).astype(BF16)
    a = _dot(h, wg_ref[...])
    b = _dot(h, wu_ref[...])
    act = (a * jax.nn.sigmoid(a) * b).astype(BF16)
    y = x + 0.5 * _dot(act, wd_ref[...])
    if final:
        y = _rms(y, rest[0][...])
    o_ref[...] = y


def _ffn(x, g, wg, wu, wd, g_final=None):
    m = x.shape[0]
    tm = min(TOKEN_TILE, m)
    final = g_final is not None
    row = pl.BlockSpec((tm, D_MODEL), lambda i: (i, 0))
    in_specs = [row, _resident((1, D_MODEL)), _resident((D_MODEL, D_FF)),
                _resident((D_MODEL, D_FF)), _resident((D_FF, D_MODEL))]
    args = [x, g, wg, wu, wd]
    if final:
        in_specs.append(_resident((1, D_MODEL)))
        args.append(g_final)
    return pl.pallas_call(
        functools.partial(_ffn_body, final=final),
        grid=(m // tm,),
        in_specs=in_specs,
        out_specs=row,
        out_shape=jax.ShapeDtypeStruct((m, D_MODEL), F32),
        compiler_params=_params("parallel"),
        name="ffn_final" if final else "ffn",
    )(*args)


Z_OFF = N_GROUPS * QKV_COLS
QM_OFF = Z_OFF + POOL_WIDTH
GATE_OFF = QM_OFF + MEM_WIDTH
IN_COLS = GATE_OFF + N_BRANCH * D_MODEL


def _inproj_body(x_ref, g_ref, w_ref, *rest, q_scale, with_windows):
    qkv_refs = rest[0:N_GROUPS]
    kv_refs = rest[N_GROUPS:2 * N_GROUPS] if with_windows else (None,) * N_GROUPS
    z_ref, qm_ref, u_scr, kv_scr = rest[-4:]
    uf = _rms(x_ref[...], g_ref[...])
    u = uf.astype(BF16)
    tm = u.shape[0]
    if any(ref.shape[1] > 1 for ref in qkv_refs):
        for c in range(D_MODEL // LANES):
            u_scr[c] = uf[:, c * LANES:(c + 1) * LANES]
    width = N_GROUPS * GROUP_COLS
    for g, (qkv_ref, kv_ref) in enumerate(zip(qkv_refs, kv_refs)):
        dil, per_class = qkv_ref.shape[1], qkv_ref.shape[2]
        w_g = jnp.concatenate(
            [w_ref[:, t * width + g * GROUP_COLS:t * width + (g + 1) * GROUP_COLS]
             for t in range(3)], axis=1)
        rows = kv_ref.shape[2] if with_windows else 0

        def put_qkv(r, blk):
            qkv_ref[0, r, :, 0:GROUP_COLS] = (blk[:, :GROUP_COLS] * q_scale).astype(qkv_ref.dtype)
            qkv_ref[0, r, :, GROUP_COLS:] = blk[:, GROUP_COLS:].astype(qkv_ref.dtype)

        if dil == 1:
            p = _dot(u, w_g)
            put_qkv(0, p)
            if with_windows:
                kv_ref[0] = p[tm - rows:, GROUP_COLS:].T
        else:
            u_cls = jnp.concatenate(
                [jnp.concatenate([u_scr[c, pl.ds(r, per_class, stride=dil), :]
                                  for c in range(D_MODEL // LANES)], axis=1)
                 for r in range(dil)], axis=0)
            p = _dot(u_cls.astype(BF16), w_g)
            assert with_windows and rows == tm
            for r in range(dil):
                blk = p[r * per_class:(r + 1) * per_class]
                put_qkv(r, blk)
                for c in range(2 * GROUP_COLS // LANES):
                    lo = GROUP_COLS + c * LANES
                    kv_scr[c, pl.ds(r, per_class, stride=dil), :] = blk[:, lo:lo + LANES]
            kv_ref[0] = jnp.concatenate(
                [kv_scr[c] for c in range(2 * GROUP_COLS // LANES)], axis=1).T
    z_ref[...] = _dot(u, w_ref[:, Z_OFF:QM_OFF])
    qm_ref[...] = _dot(u, w_ref[:, QM_OFF:GATE_OFF]).astype(qm_ref.dtype)


def _inproj(x, g, w, n, s, windows, dils, q_dtype, q_scale, rider=()):
    tm = min(TOKEN_TILE, s)
    tj = s // tm

    def tok(cols):
        return pl.BlockSpec((tm, cols), lambda b, j: (b * tj + j, 0))

    qkv_specs = [pl.BlockSpec((1, d, tm // d, QKV_COLS), lambda b, j: (b, 0, j, 0)) for d in dils]
    qkv_shapes = [jax.ShapeDtypeStruct((n, d, s // d, QKV_COLS), q_dtype) for d in dils]

    kv_specs, kv_shapes = [], []
    for win in windows or ():
        rows = min(win, tm)
        assert tm % rows == 0 and win % rows == 0
        first = (s - win) // tm
        if win >= tm:
            spec = pl.BlockSpec((1, 2 * GROUP_COLS, rows),
                                lambda b, j, first=first: (b, 0, jnp.maximum(j - first, 0)))
        else:
            spec = pl.BlockSpec((1, 2 * GROUP_COLS, rows), lambda b, j: (b, 0, 0))
        kv_specs.append(spec)
        kv_shapes.append(jax.ShapeDtypeStruct((n, 2 * GROUP_COLS, win), F32))
    m = n * s
    out_shape = (qkv_shapes + kv_shapes + [
        jax.ShapeDtypeStruct((m, POOL_WIDTH), F32),
        jax.ShapeDtypeStruct((m, MEM_WIDTH), q_dtype)])
    out_specs = qkv_specs + kv_specs + [tok(POOL_WIDTH), tok(MEM_WIDTH)]
    in_specs = [tok(D_MODEL), _resident((1, D_MODEL)),
                pl.BlockSpec((D_MODEL, GATE_OFF), lambda b, j: (0, 0),
                             pipeline_mode=pl.Buffered(1))]
    body = functools.partial(_inproj_body, q_scale=q_scale, with_windows=bool(windows))
    n_own_out = len(out_specs)
    if rider:
        assert rider[0].shape[0] == n * tj
        r_in, r_out, r_shapes = _sample_branch_specs(rider, lambda b, j: b * tj + j)
        in_specs, out_specs, out_shape = in_specs + r_in, out_specs + r_out, out_shape + r_shapes

        def body(*refs, own=body):
            ins, rest = refs[:3], refs[3:]
            r_ins, rest = rest[:N_SAMPLE_INPUTS], rest[N_SAMPLE_INPUTS:]
            outs, r_outs, scratch = rest[:n_own_out], rest[n_own_out:-2], rest[-2:]
            own(*ins, *outs, *scratch)
            _sample_branch_math(pl.program_id(0) * tj + pl.program_id(1), *r_ins, *r_outs)

    return pl.pallas_call(
        body,
        grid=(n, tj),
        in_specs=in_specs,
        out_specs=out_specs,
        out_shape=out_shape,
        scratch_shapes=[pltpu.VMEM((D_MODEL // LANES, tm, LANES), F32),
                        pltpu.VMEM((2 * GROUP_COLS // LANES, tm, LANES), F32)],
        compiler_params=_params("arbitrary", "arbitrary"),
        name="inproj",
    )(x, g, w, *rider)


ATTN_CHUNKS = 16
HEAD_LANES = HEADS * STRIDES


def _band_body(row_ref, band_ref):
    for g in range(N_GROUPS):
        for h in range(HEADS):
            rows = jnp.broadcast_to(row_ref[g, h] * LOG2E, (STRIDES, 2 * STRIDES))
            band = pltpu.roll(rows, 0, 1, stride=1, stride_axis=0).T
            band_ref[g, :, h * STRIDES:(h + 1) * STRIDES] = band


def _bands(rows):
    return pl.pallas_call(
        _band_body,
        grid=(1,),
        in_specs=[_resident(rows.shape)],
        out_specs=pl.BlockSpec((N_GROUPS, 2 * STRIDES, HEAD_LANES), lambda i: (0, 0, 0)),
        out_shape=jax.ShapeDtypeStruct((N_GROUPS, 2 * STRIDES, HEAD_LANES), F32),
        compiler_params=_params("arbitrary"),
        name="bands",
    )(rows)


def _attn_body(qkv_ref, prev_ref, band_ref, o_ref, lse_ref, *, dil, chunks):
    first = pl.program_id(1) == 0
    lane_head = lax.broadcasted_iota(jnp.int32, (1, GROUP_COLS), 1) // HEAD_DIM

    def keys_of(r):
        return jnp.concatenate([prev_ref[0, r, :, GROUP_COLS:2 * GROUP_COLS],
                                qkv_ref[0, r, :, GROUP_COLS:2 * GROUP_COLS]], axis=0)

    def scores(k_all, r, i):
        q = qkv_ref[0, r, i * STRIDES:(i + 1) * STRIDES, 0:GROUP_COLS]
        qm = jnp.concatenate(
            [jnp.where(lane_head == h, q, jnp.zeros_like(q)) for h in range(HEADS)], axis=0)
        return _dot_t(k_all[i * STRIDES:(i + 2) * STRIDES], qm)

    order = [(r, i) for r in range(dil) for i in range(chunks)]
    k_all = keys_of(0)
    st_next = scores(k_all, 0, 0)
    for idx, (r, i) in enumerate(order):
        st = st_next
        if i == 0:
            v_all = jnp.concatenate([prev_ref[0, r, :, 2 * GROUP_COLS:],
                                     qkv_ref[0, r, :, 2 * GROUP_COLS:]], axis=0)
            vt_all = v_all.astype(F32).T.astype(BF16)
        if idx + 1 < len(order):
            r2, i2 = order[idx + 1]
            if i2 == 0:
                k_all = keys_of(r2)
            st_next = scores(k_all, r2, i2)
        if True:
            vt2 = vt_all[:, i * STRIDES:(i + 2) * STRIDES]
            o_parts, lse_parts = [], []
            for h in range(HEADS):
                hl = slice(h * STRIDES, (h + 1) * STRIDES)
                s_h = st[:, hl] + band_ref[:, hl]
                if i == 0:
                    s_h = jnp.concatenate(
                        [jnp.where(first, NEG_INF, s_h[:STRIDES]), s_h[STRIDES:]], axis=0)
                m = jnp.max(s_h, axis=0, keepdims=True)
                e = jnp.exp2(s_h - m)
                l = jnp.sum(e, axis=0, keepdims=True)
                ot = _dot(vt2[h * HEAD_DIM:(h + 1) * HEAD_DIM], e.astype(BF16))
                o_parts.append(ot * (1.0 / l))
                lse_parts.append(jnp.broadcast_to(m * LN2 + jnp.log(l), (HEAD_DIM, STRIDES)))
            rows = pl.ds(i * STRIDES * dil + r, STRIDES, stride=dil)
            o_rows = jnp.concatenate(o_parts, axis=0).T
            lse_rows = jnp.concatenate(lse_parts, axis=0).T
            for c in range(GROUP_COLS // LANES):
                o_ref[c, rows, :] = o_rows[:, c * LANES:(c + 1) * LANES]
                lse_ref[c, rows, :] = lse_rows[:, c * LANES:(c + 1) * LANES]


def _attn(qkv, bands, g, n, s, dil):
    chunks = ATTN_CHUNKS // dil
    span = chunks * STRIDES
    steps = s // (span * dil)
    slabs = GROUP_COLS // LANES
    out_spec = pl.BlockSpec((slabs, span * dil, LANES), lambda b, k: (0, b * steps + k, 0))
    out_sds = jax.ShapeDtypeStruct((slabs, n * s, LANES), F32)
    return pl.pallas_call(
        functools.partial(_attn_body, dil=dil, chunks=chunks),
        grid=(n, steps),
        in_specs=[pl.BlockSpec((1, dil, span, QKV_COLS), lambda b, k: (b, 0, k, 0)),
                  pl.BlockSpec((1, dil, STRIDES, QKV_COLS),
                               lambda b, k: (b, 0, jnp.maximum(k * chunks - 1, 0), 0)),
                  pl.BlockSpec((None, 2 * STRIDES, HEAD_LANES), lambda b, k: (g, 0, 0))],
        out_specs=[out_spec, out_spec],
        out_shape=[out_sds, out_sds],
        compiler_params=_params("parallel", "parallel"),
        name="attn_d%d" % dil,
    )(qkv, qkv, bands)


def _memkv_body(mem_ref, g_ref, w_ref, o_ref):
    o_ref[...] = _dot(_rms(mem_ref[...], g_ref[...]).astype(BF16), w_ref[...])


def _memkv(mem, g, w):
    m = mem.shape[0]
    row = pl.BlockSpec((MEM_LEN, D_MODEL), lambda i: (i, 0))
    return pl.pallas_call(
        _memkv_body,
        grid=(m // MEM_LEN,),
        in_specs=[row, _resident((1, D_MODEL)), _resident((D_MODEL, 2 * MEM_WIDTH))],
        out_specs=pl.BlockSpec((MEM_LEN, 2 * MEM_WIDTH), lambda i: (i, 0)),
        out_shape=jax.ShapeDtypeStruct((m, 2 * MEM_WIDTH), F32),
        compiler_params=_params("parallel"),
        name="memkv",
    )(mem, g, w)


def _gate(u, win_ref, k):
    lo = GATE_OFF + k * D_MODEL
    return jax.nn.sigmoid(_dot(u, win_ref[:, lo:lo + D_MODEL]))


def _merge_math(a, pooled, c, x, gates, wpool_ref, scale_ref, woa_ref, wob_ref,
                woc_ref, wout_ref):
    mixed = [_dot(pooled[gi].astype(BF16), wpool_ref[gi]) for gi in range(len(POOL_WINDOWS))]
    b = jnp.concatenate(mixed, axis=1) * scale_ref[...]
    m = None
    for k, (branch, wo_ref) in enumerate(((a, woa_ref), (b, wob_ref), (c, woc_ref))):
        term = gates[k] * _dot(branch.astype(BF16), wo_ref[...])
        m = term if m is None else m + term
    return x + _dot(m.astype(BF16), wout_ref[...])


def _mix_body(o0, o1, o2, l0, l1, l2, z_ref, halo_ref, qm_ref, mkv_ref, x_ref, *rest):
    (gmix_ref, win_ref), merge_refs, out_ref = rest[:2], rest[2:-1], rest[-1]
    j = pl.program_id(1)
    tm = x_ref.shape[0]
    x = x_ref[...]
    u = _rms(x, gmix_ref[...]).astype(BF16)
    mkv = mkv_ref[...].astype(BF16)
    qm = qm_ref[...]
    head = lambda h: slice(h * MEM_HEAD_DIM, (h + 1) * MEM_HEAD_DIM)
    scores = [_dot_t(qm[:, head(h)], mkv[:, head(h)]) * (1.0 / math.sqrt(MEM_HEAD_DIM))
              for h in range(MEM_HEADS)]
    gate_cols = []

    def gate_piece():
        lo = GATE_OFF + len(gate_cols) * GROUP_COLS
        gate_cols.append(jax.nn.sigmoid(_dot(u, win_ref[:, lo:lo + GROUP_COLS])))

    probs = []
    for h in range(MEM_HEADS):
        gate_piece()
        mm = jnp.max(scores[h], axis=-1, keepdims=True)
        p = jnp.exp(scores[h] - mm)
        probs.append((p.astype(BF16), jnp.sum(p, axis=-1, keepdims=True)))
    gate_piece()
    c = jnp.concatenate(
        [_dot(p, mkv[:, MEM_WIDTH + h * MEM_HEAD_DIM:MEM_WIDTH + (h + 1) * MEM_HEAD_DIM]) / l
         for h, (p, l) in enumerate(probs)], axis=1)
    gate_piece()
    unslab = lambda ref: jnp.concatenate([ref[c] for c in range(ref.shape[0])], axis=1)
    lses = [unslab(l0), unslab(l1), unslab(l2)]
    mx = jnp.maximum(jnp.maximum(lses[0], lses[1]), lses[2])
    es = [jnp.exp(l - mx) for l in lses]
    gate_piece()
    a = ((es[0] * unslab(o0) + es[1] * unslab(o1) + es[2] * unslab(o2))
         / (es[0] + es[1] + es[2]))
    z = z_ref[...]
    halo = jnp.where(j == 0, 0.0, halo_ref[...])
    zc = jnp.concatenate([halo, z], axis=0)
    pos = j * tm + lax.broadcasted_iota(jnp.int32, (tm, 1), 0)
    pooled = []
    for gi, kw in enumerate(POOL_WINDOWS):
        gate_piece()
        cs = slice(gi * POOL_GROUP, (gi + 1) * POOL_GROUP)
        run = zc[:, cs]
        width = 1
        while width < kw:
            run = run[width:] + run[:-width]
            width *= 2
        first = POOL_HALO - (kw - 1)
        cnt = jnp.minimum(kw, pos + 1).astype(F32)
        pooled.append(run[first:first + tm] / cnt - z[:, cs])
    while len(gate_cols) < N_BRANCH * D_MODEL // GROUP_COLS:
        gate_piece()
    per = D_MODEL // GROUP_COLS
    gates = [jnp.concatenate(gate_cols[k * per:(k + 1) * per], axis=1) for k in range(N_BRANCH)]
    out_ref[...] = _merge_math(a, pooled, c, x, gates, *merge_refs)


def _merge_weight_specs():
    return [_resident((1, D_MODEL)), _resident((D_MODEL, IN_COLS)),
            _resident((len(POOL_WINDOWS), POOL_GROUP, POOL_GROUP)), _resident((1, POOL_WIDTH)),
            _resident((GROUP_COLS, D_MODEL)), _resident((POOL_WIDTH, D_MODEL)),
            _resident((MEM_WIDTH, D_MODEL)), _resident((D_MODEL, D_MODEL))]


def _mix_prompt(os, lses, z, qm, mkv, x, weights, n, s):
    tm = TOKEN_TILE
    tj = s // tm
    halo_per_tile = tm // POOL_HALO

    def tok(cols):
        return pl.BlockSpec((tm, cols), lambda b, j: (b * tj + j, 0))

    halo = pl.BlockSpec((POOL_HALO, POOL_WIDTH),
                        lambda b, j: (jnp.maximum((b * tj + j) * halo_per_tile - 1, 0), 0))
    slab = pl.BlockSpec((GROUP_COLS // LANES, tm, LANES), lambda b, j: (0, b * tj + j, 0))
    in_specs = ([slab] * 6 + [tok(POOL_WIDTH), halo, tok(MEM_WIDTH),
                pl.BlockSpec((MEM_LEN, 2 * MEM_WIDTH), lambda b, j: (b, 0)),
                tok(D_MODEL)] + _merge_weight_specs())
    return pl.pallas_call(
        _mix_body,
        grid=(n, tj),
        in_specs=in_specs,
        out_specs=tok(D_MODEL),
        out_shape=jax.ShapeDtypeStruct((n * s, D_MODEL), F32),
        compiler_params=_params("parallel", "parallel"),
        name="mix_prompt",
    )(*os, *lses, z, z, qm, mkv, x, *weights)


N_SAMPLE_INPUTS = 14


def _sample_branch_math(req, qkv0, qkv1, qkv2, c0, c1, c2, b0, b1, b2, bias0_ref, z_ref, st_ref,
                        qm_ref, cm_ref, a_ref, pooled_ref, c_ref):
    row = pl.ds(req, 1)
    eye = (lax.broadcasted_iota(jnp.int32, (HEAD_DIM, HEAD_DIM), 0)
           == lax.broadcasted_iota(jnp.int32, (HEAD_DIM, HEAD_DIM), 1))
    cube = (HEADS, HEAD_DIM, HEAD_DIM)

    def heads_of(vec, lo):
        return jnp.stack([vec[:, lo + h * HEAD_DIM:lo + (h + 1) * HEAD_DIM]
                          for h in range(HEADS)], axis=0)

    outs, lses = [], []
    for g, (qkv_ref, cache_ref, bias_ref) in enumerate(
            ((qkv0, c0, b0), (qkv1, c1, b1), (qkv2, c2, b2))):
        qkv = qkv_ref[row, :]
        q, kn, vn = heads_of(qkv, 0), heads_of(qkv, GROUP_COLS), heads_of(qkv, 2 * GROUP_COLS)
        q_col = jnp.sum(jnp.where(eye, jnp.broadcast_to(q, cube), 0.0), axis=2, keepdims=True)
        s = jnp.sum(cache_ref[0, 0] * q_col, axis=1, keepdims=True) + bias_ref[...]
        sn = jnp.sum(kn * q, axis=2, keepdims=True) + bias0_ref[g]
        m = jnp.maximum(jnp.max(s, axis=2, keepdims=True), sn)
        p = jnp.exp(s - m)
        pn = jnp.exp(sn - m)
        l = jnp.sum(p, axis=2, keepdims=True) + pn
        pv = jnp.sum(cache_ref[0, 1] * p, axis=2, keepdims=True)
        pv_row = jnp.sum(jnp.where(eye, jnp.broadcast_to(pv, cube), 0.0), axis=1, keepdims=True)
        outs.append((pv_row + pn * vn) / l)
        lses.append(m + jnp.log(l))
    mx = jnp.maximum(jnp.maximum(lses[0], lses[1]), lses[2])
    es = [jnp.exp(lse - mx) for lse in lses]
    a = (es[0] * outs[0] + es[1] * outs[1] + es[2] * outs[2]) / (es[0] + es[1] + es[2])
    a_ref[row, :] = jnp.concatenate([a[h] for h in range(HEADS)], axis=1)

    zn = z_ref[row, :]
    st = st_ref[:, row, :]
    pooled = []
    for gi, kw in enumerate(POOL_WINDOWS):
        cs = slice(gi * POOL_GROUP, (gi + 1) * POOL_GROUP)
        tot = jnp.sum(st[POOL_STATE - (kw - 1):, :, cs], axis=0) + zn[:, cs]
        pooled.append(tot / float(min(kw, PAST_LEN + 1)) - zn[:, cs])
    pooled_ref[row, :] = jnp.concatenate(pooled, axis=1)

    qm_row = qm_ref[row, :]
    qm = jnp.concatenate([qm_row[:, h * MEM_HEAD_DIM:(h + 1) * MEM_HEAD_DIM]
                          for h in range(MEM_HEADS)], axis=0)
    km = cm_ref[0, :, 0]
    vm = cm_ref[0, :, 1]
    s = jnp.sum(km * qm[None], axis=-1, keepdims=True) * (1.0 / math.sqrt(MEM_HEAD_DIM))
    m = jnp.max(s, axis=0)
    p = jnp.exp(s - m[None])
    c = jnp.sum(p * vm, axis=0) / jnp.sum(p, axis=0)
    c_ref[row, :] = jnp.concatenate([c[h:h + 1] for h in range(MEM_HEADS)], axis=1)


def _sample_branch_specs(operands, request_of):
    assert len(operands) == N_SAMPLE_INPUTS
    nb = operands[0].shape[0]

    def per_request(t):
        zeros = (0,) * (t.ndim - 1)
        return pl.BlockSpec((1,) + t.shape[1:], lambda *g: (request_of(*g),) + zeros)

    def whole(shape):
        zeros = (0,) * len(shape)
        return pl.BlockSpec(shape, lambda *g: zeros)

    streamed = (3, 4, 5, 13)
    in_specs = [per_request(t) if k in streamed else whole(t.shape)
                for k, t in enumerate(operands)]
    widths = (GROUP_COLS, POOL_WIDTH, MEM_WIDTH)
    return (in_specs, [whole((nb, w)) for w in widths],
            [jax.ShapeDtypeStruct((nb, w), F32) for w in widths])


def _sample_merge_body(a_ref, pooled_ref, c_ref, x_ref, *rest):
    (gmix_ref, win_ref), merge_refs, out_ref = rest[:2], rest[2:-1], rest[-1]
    pooled_all = pooled_ref[...]
    pooled = [pooled_all[:, gi * POOL_GROUP:(gi + 1) * POOL_GROUP]
              for gi in range(len(POOL_WINDOWS))]
    x = x_ref[...]
    u = _rms(x, gmix_ref[...]).astype(BF16)
    gates = [_gate(u, win_ref, k) for k in range(N_BRANCH)]
    out_ref[...] = _merge_math(a_ref[...], pooled, c_ref[...], x, gates, *merge_refs)


def _sample_merge(a, pooled, c, x, weights):
    nb = x.shape[0]
    full = lambda cols: _resident((nb, cols))
    return pl.pallas_call(
        _sample_merge_body,
        grid=(1,),
        in_specs=[full(GROUP_COLS), full(POOL_WIDTH), full(MEM_WIDTH), full(D_MODEL)]
        + _merge_weight_specs(),
        out_specs=pl.BlockSpec((nb, D_MODEL), lambda i: (0, 0)),
        out_shape=jax.ShapeDtypeStruct((nb, D_MODEL), F32),
        compiler_params=_params("arbitrary"),
        name="sample_merge",
    )(a, pooled, c, x, *weights)


def _rel_bucket(n):
    max_exact = N_BUCKETS // 2
    nf = jnp.maximum(n, 1).astype(F32)
    large = max_exact + (jnp.log(nf / max_exact) / math.log(MAX_DISTANCE / max_exact)
                         * (N_BUCKETS - max_exact)).astype(jnp.int32)
    large = jnp.minimum(large, N_BUCKETS - 1)
    return jnp.where(n < max_exact, n, large)


def _stride_bias(rel_bias, g, dil):
    j = jnp.arange(STRIDES + 1, dtype=jnp.int32)
    return rel_bias[_rel_bucket(j * dil)][:, g * HEADS:(g + 1) * HEADS].astype(F32)


def _band_row(bias_j):
    row = jnp.concatenate([bias_j[::-1], jnp.full((STRIDES - 1, HEADS), NEG_INF, F32)], axis=0)
    return row.T.reshape(HEADS, 1, 2 * STRIDES)


def _cache_bias(bias_j, dil):
    on_grid = bias_j[STRIDES:0:-1].T
    full = jnp.full((HEADS, STRIDES, dil), NEG_INF, F32).at[:, :, 0].set(on_grid)
    return full.reshape(HEADS, 1, STRIDES * dil)


def kernel(x_prompt, x_sample, cache_win0_kv, cache_win1_kv, cache_win2_kv, state_pool, cache_mem_kv, mem_prompt, rel_bias, g_ffn1, w1_gate, w1_up, w1_down, g_mix, w_in, w_pool, pool_scale, g_mem, w_mem_kv, w_oa, w_ob, w_oc, w_out, g_ffn2, w2_gate, w2_up, w2_down, g_final):
    n, s, _ = x_prompt.shape
    nb = x_sample.shape[0]
    depth = g_ffn1.shape[0]
    win_caches = (cache_win0_kv, cache_win1_kv, cache_win2_kv)
    bias_js = [_stride_bias(rel_bias, g, dil) for g, (_, dil) in enumerate(DIL_GROUPS)]
    bands = _bands(jnp.stack([_band_row(b) for b in bias_js]))
    bias_cache = [_cache_bias(b, dil) for b, (_, dil) in zip(bias_js, DIL_GROUPS)]
    bias_new = jnp.stack([b[0] for b in bias_js]).reshape(N_GROUPS, HEADS, 1, 1)
    gfin = g_final.reshape(1, D_MODEL)

    xp = x_prompt.reshape(n * s, D_MODEL)
    xs = x_sample.reshape(nb, D_MODEL)
    st_p = [[] for _ in range(5)]
    st_s = [[] for _ in range(4)]
    for l in range(depth):
        last = l == depth - 1
        bf = lambda w: w[l].astype(BF16)
        vec = lambda v: v[l].reshape(1, -1)
        w1 = (bf(w1_gate), bf(w1_up), bf(w1_down))
        w2 = (bf(w2_gate), bf(w2_up), bf(w2_down))
        win_l = bf(w_in)
        merge_w = (vec(g_mix), win_l, bf(w_pool), vec(pool_scale), bf(w_oa), bf(w_ob),
                   bf(w_oc), bf(w_out))
        q_scale = 1.0 / math.sqrt(HEAD_DIM)

        xs = _ffn(xs, vec(g_ffn1), *w1)
        res = _inproj(xs, vec(g_mix), win_l, 1, nb, None, [1] * N_GROUPS, F32, q_scale)
        s_qkvs, s_z, s_qm = [t.reshape(nb, QKV_COLS) for t in res[0:3]], res[3], res[4]
        rider = (s_qkvs + [jnp.transpose(cw[l], (0, 2, 3, 4, 1)) for cw in win_caches]
                 + bias_cache + [bias_new, s_z, jnp.transpose(state_pool[l], (1, 0, 2)), s_qm,
                                 cache_mem_kv[l]])

        xp = _ffn(xp, vec(g_ffn1), *w1)
        res = _inproj(xp, vec(g_mix), win_l, n, s, [min(w, s) for w, _ in DIL_GROUPS],
                      [d for _, d in DIL_GROUPS], BF16, q_scale * LOG2E, rider)
        qkvs, kvwins, z, qm = res[0:3], res[3:6], res[6], res[7]
        s_a, s_pooled, s_c = res[8:11]
        os, lses = [], []
        for g, (_, dil) in enumerate(DIL_GROUPS):
            o, lse = _attn(qkvs[g], bands, g, n, s, dil)
            os.append(o)
            lses.append(lse)
        mkv = _memkv(mem_prompt.reshape(n * MEM_LEN, D_MODEL), vec(g_mem), bf(w_mem_kv))
        xp = _mix_prompt(os, lses, z, qm, mkv, xp, merge_w, n, s)
        xp = _ffn(xp, vec(g_ffn2), *w2, g_final=gfin if last else None)
        for g, (win, _) in enumerate(DIL_GROUPS):
            kv_t = kvwins[g].reshape(n, 2, HEADS, HEAD_DIM, min(win, s))
            st_p[g].append(jnp.transpose(kv_t, (0, 4, 1, 2, 3)))
        st_p[3].append(z.reshape(n, s, POOL_WIDTH)[:, s - POOL_STATE:])
        st_p[4].append(mkv.reshape(n, MEM_LEN, 2, MEM_HEADS, MEM_HEAD_DIM))

        xs = _sample_merge(s_a, s_pooled, s_c, xs, merge_w)
        xs = _ffn(xs, vec(g_ffn2), *w2, g_final=gfin if last else None)
        for g in range(N_GROUPS):
            st_s[g].append(s_qkvs[g][:, GROUP_COLS:].reshape(nb, 1, 2, HEADS, HEAD_DIM))
        st_s[3].append(s_z.reshape(nb, 1, POOL_WIDTH))

    y_prompt = xp.reshape(n, s, D_MODEL)
    y_sample = xs.reshape(nb, 1, D_MODEL)
    stack = lambda ts: jnp.stack(ts, axis=0)
    return (y_prompt, y_sample, stack(st_p[0]), stack(st_p[1]), stack(st_p[2]), stack(st_p[3]),
            stack(st_p[4]), stack(st_s[0]), stack(st_s[1]), stack(st_s[2]), stack(st_s[3]))
```

```python
import functools
import math

import jax
import jax.numpy as jnp
from jax import lax
from jax.experimental import pallas as pl
from jax.experimental.pallas import tpu as pltpu

F32 = jnp.float32
BF16 = jnp.bfloat16

D_MODEL = 1024
D_FF = 2816
LANES = 128
HEAD_DIM = 64
HEADS = 4
DIL_GROUPS = ((128, 1), (512, 4), (2048, 16))
N_GROUPS = 3
GROUP_COLS = HEADS * HEAD_DIM
QKV_COLS = 3 * GROUP_COLS
STRIDES = 128
POOL_WINDOWS = (2, 4, 8, 16)
POOL_GROUP = 128
POOL_WIDTH = 512
POOL_STATE = 15
POOL_HALO = 16
MEM_LEN = 256
MEM_HEADS = 4
MEM_HEAD_DIM = 128
MEM_WIDTH = 512
N_BUCKETS = 32
MAX_DISTANCE = 2048
N_BRANCH = 3
EPS = 1e-6
NEG_INF = -1e30
LOG2E = math.log2(math.e)
LN2 = math.log(2.0)
PAST_LEN = 8192

V7X_VMEM_LIMIT_BYTES = 56 * 1024 * 1024
TOKEN_TILE = 512


def _params(*sem):
    return pltpu.CompilerParams(dimension_semantics=sem,
                                vmem_limit_bytes=V7X_VMEM_LIMIT_BYTES)


def _resident(shape):
    zeros = (0,) * len(shape)
    return pl.BlockSpec(shape, lambda *_: zeros, pipeline_mode=pl.Buffered(1))


def _rms(x, g):
    return x * lax.rsqrt(jnp.mean(x * x, axis=-1, keepdims=True) + EPS) * g


def _dot(a, b):
    return jnp.dot(a, b, preferred_element_type=F32)


def _dot_t(a, b):
    return lax.dot_general(a, b, (((1,), (1,)), ((), ())), preferred_element_type=F32)


BF16_SUBLANES = 16


def _ffn_body(x_ref, g_ref, wg_ref, wu_ref, wd_ref, *rest, final, n_casts):
    rest, cast_out = (rest[:-n_casts], rest[-n_casts:]) if n_casts else (rest, ())
    cast_in, o_ref = rest[len(rest) - 1 - n_casts:-1], rest[-1]
    x = x_ref[...]
    h = _rms(x, g_ref[...]).astype(BF16)
    a = _dot(h, wg_ref[...])
    b = _dot(h, wu_ref[...])
    act = (a * jax.nn.sigmoid(a) * b).astype(BF16)
    y = x + 0.5 * _dot(act, wd_ref[...])
    if final:
        y = _rms(y, rest[0][...])
    o_ref[...] = y
    for src, dst in zip(cast_in, cast_out):
        dst[...] = src[...].astype(BF16)


def _ffn(x, g, wg, wu, wd, g_final=None, casts=()):
    m = x.shape[0]
    tm = min(TOKEN_TILE, m)
    steps = m // tm
    final = g_final is not None
    row = pl.BlockSpec((tm, D_MODEL), lambda i: (i, 0))
    in_specs = [row, _resident((1, D_MODEL)), _resident((D_MODEL, D_FF)),
                _resident((D_MODEL, D_FF)), _resident((D_FF, D_MODEL))]
    args = [x, g, wg, wu, wd]
    if final:
        in_specs.append(_resident((1, D_MODEL)))
        args.append(g_final)
    cast_specs = []
    for w in casts:
        blocks = math.gcd(steps, w.shape[0] // BF16_SUBLANES)
        per = steps // blocks
        cast_specs.append(pl.BlockSpec((w.shape[0] // blocks, w.shape[1]),
                                       lambda i, per=per: (i // per, 0)))
    out = pl.pallas_call(
        functools.partial(_ffn_body, final=final, n_casts=len(casts)),
        grid=(steps,),
        in_specs=in_specs + cast_specs,
        out_specs=[row] + cast_specs,
        out_shape=[jax.ShapeDtypeStruct((m, D_MODEL), F32)]
        + [jax.ShapeDtypeStruct(w.shape, BF16) for w in casts],
        compiler_params=_params("arbitrary" if casts else "parallel"),
        name="ffn_final" if final else "ffn",
    )(*args, *casts)
    return (out[0], out[1:]) if casts else out[0]


Z_OFF = N_GROUPS * QKV_COLS
QM_OFF = Z_OFF + POOL_WIDTH
GATE_OFF = QM_OFF + MEM_WIDTH
IN_COLS = GATE_OFF + N_BRANCH * D_MODEL


def _inproj_body(x_ref, g_ref, w_ref, *rest, q_scale, with_windows):
    qkv_refs = rest[0:N_GROUPS]
    kv_refs = rest[N_GROUPS:2 * N_GROUPS] if with_windows else (None,) * N_GROUPS
    z_ref, qm_ref, u_scr, kv_scr = rest[-4:]
    uf = _rms(x_ref[...], g_ref[...])
    u = uf.astype(BF16)
    tm = u.shape[0]
    if any(ref.shape[1] > 1 for ref in qkv_refs):
        for c in range(D_MODEL // LANES):
            u_scr[c] = uf[:, c * LANES:(c + 1) * LANES]
    width = N_GROUPS * GROUP_COLS
    for g, (qkv_ref, kv_ref) in enumerate(zip(qkv_refs, kv_refs)):
        dil, per_class = qkv_ref.shape[1], qkv_ref.shape[2]
        w_g = jnp.concatenate(
            [w_ref[:, t * width + g * GROUP_COLS:t * width + (g + 1) * GROUP_COLS]
             for t in range(3)], axis=1)
        rows = kv_ref.shape[2] if with_windows else 0

        def put_qkv(r, blk):
            qkv_ref[0, r, :, 0:GROUP_COLS] = (blk[:, :GROUP_COLS] * q_scale).astype(qkv_ref.dtype)
            qkv_ref[0, r, :, GROUP_COLS:] = blk[:, GROUP_COLS:].astype(qkv_ref.dtype)

        if dil == 1:
            p = _dot(u, w_g)
            put_qkv(0, p)
            if with_windows:
                kv_ref[0] = p[tm - rows:, GROUP_COLS:].T
        else:
            u_cls = jnp.concatenate(
                [jnp.concatenate([u_scr[c, pl.ds(r, per_class, stride=dil), :]
                                  for c in range(D_MODEL // LANES)], axis=1)
                 for r in range(dil)], axis=0)
            p = _dot(u_cls.astype(BF16), w_g)
            assert with_windows and rows == tm
            for r in range(dil):
                blk = p[r * per_class:(r + 1) * per_class]
                put_qkv(r, blk)
                for c in range(2 * GROUP_COLS // LANES):
                    lo = GROUP_COLS + c * LANES
                    kv_scr[c, pl.ds(r, per_class, stride=dil), :] = blk[:, lo:lo + LANES]
            kv_ref[0] = jnp.concatenate(
                [kv_scr[c] for c in range(2 * GROUP_COLS // LANES)], axis=1).T
    z_ref[...] = _dot(u, w_ref[:, Z_OFF:QM_OFF])
    qm_ref[...] = _dot(u, w_ref[:, QM_OFF:GATE_OFF]).astype(qm_ref.dtype)


def _inproj(x, g, w, n, s, windows, dils, q_dtype, q_scale, rider=()):
    tm = min(TOKEN_TILE, s)
    tj = s // tm

    def tok(cols):
        return pl.BlockSpec((tm, cols), lambda b, j: (b * tj + j, 0))

    qkv_specs = [pl.BlockSpec((1, d, tm // d, QKV_COLS), lambda b, j: (b, 0, j, 0)) for d in dils]
    qkv_shapes = [jax.ShapeDtypeStruct((n, d, s // d, QKV_COLS), q_dtype) for d in dils]

    kv_specs, kv_shapes = [], []
    for win in windows or ():
        rows = min(win, tm)
        assert tm % rows == 0 and win % rows == 0
        first = (s - win) // tm
        if win >= tm:
            spec = pl.BlockSpec((1, 2 * GROUP_COLS, rows),
                                lambda b, j, first=first: (b, 0, jnp.maximum(j - first, 0)))
        else:
            spec = pl.BlockSpec((1, 2 * GROUP_COLS, rows), lambda b, j: (b, 0, 0))
        kv_specs.append(spec)
        kv_shapes.append(jax.ShapeDtypeStruct((n, 2 * GROUP_COLS, win), F32))
    m = n * s
    out_shape = (qkv_shapes + kv_shapes + [
        jax.ShapeDtypeStruct((m, POOL_WIDTH), F32),
        jax.ShapeDtypeStruct((m, MEM_WIDTH), q_dtype)])
    out_specs = qkv_specs + kv_specs + [tok(POOL_WIDTH), tok(MEM_WIDTH)]
    in_specs = [tok(D_MODEL), _resident((1, D_MODEL)),
                pl.BlockSpec((D_MODEL, GATE_OFF), lambda b, j: (0, 0),
                             pipeline_mode=pl.Buffered(1))]
    body = functools.partial(_inproj_body, q_scale=q_scale, with_windows=bool(windows))
    n_own_out = len(out_specs)
    if rider:
        assert rider[0].shape[0] == n * tj
        r_in, r_out, r_shapes = _sample_branch_specs(rider, lambda b, j: b * tj + j)
        in_specs, out_specs, out_shape = in_specs + r_in, out_specs + r_out, out_shape + r_shapes

        def body(*refs, own=body):
            ins, rest = refs[:3], refs[3:]
            r_ins, rest = rest[:N_SAMPLE_INPUTS], rest[N_SAMPLE_INPUTS:]
            outs, r_outs, scratch = rest[:n_own_out], rest[n_own_out:-2], rest[-2:]
            own(*ins, *outs, *scratch)
            _sample_branch_math(pl.program_id(0) * tj + pl.program_id(1), *r_ins, *r_outs)

    return pl.pallas_call(
        body,
        grid=(n, tj),
        in_specs=in_specs,
        out_specs=out_specs,
        out_shape=out_shape,
        scratch_shapes=[pltpu.VMEM((D_MODEL // LANES, tm, LANES), F32),
                        pltpu.VMEM((2 * GROUP_COLS // LANES, tm, LANES), F32)],
        compiler_params=_params("arbitrary", "arbitrary"),
        name="inproj",
    )(x, g, w, *rider)


ATTN_CHUNKS = 16
HEAD_LANES = HEADS * STRIDES


def _band_body(row_ref, band_ref):
    for g in range(N_GROUPS):
        for h in range(HEADS):
            rows = jnp.broadcast_to(row_ref[g, h] * LOG2E, (STRIDES, 2 * STRIDES))
            band = pltpu.roll(rows, 0, 1, stride=1, stride_axis=0).T
            band_ref[g, :, h * STRIDES:(h + 1) * STRIDES] = band


def _bands(rows):
    return pl.pallas_call(
        _band_body,
        grid=(1,),
        in_specs=[_resident(rows.shape)],
        out_specs=pl.BlockSpec((N_GROUPS, 2 * STRIDES, HEAD_LANES), lambda i: (0, 0, 0)),
        out_shape=jax.ShapeDtypeStruct((N_GROUPS, 2 * STRIDES, HEAD_LANES), F32),
        compiler_params=_params("arbitrary"),
        name="bands",
    )(rows)


def _attn_body(qkv_ref, prev_ref, band_ref, o_ref, lse_ref, *, dil, chunks):
    first = pl.program_id(1) == 0
    lane_head = lax.broadcasted_iota(jnp.int32, (1, GROUP_COLS), 1) // HEAD_DIM

    def keys_of(r):
        return jnp.concatenate([prev_ref[0, r, :, GROUP_COLS:2 * GROUP_COLS],
                                qkv_ref[0, r, :, GROUP_COLS:2 * GROUP_COLS]], axis=0)

    def scores(k_all, r, i):
        q = qkv_ref[0, r, i * STRIDES:(i + 1) * STRIDES, 0:GROUP_COLS]
        qm = jnp.concatenate(
            [jnp.where(lane_head == h, q, jnp.zeros_like(q)) for h in range(HEADS)], axis=0)
        return _dot_t(k_all[i * STRIDES:(i + 2) * STRIDES], qm)

    order = [(r, i) for r in range(dil) for i in range(chunks)]
    k_all = keys_of(0)
    st_next = scores(k_all, 0, 0)
    for idx, (r, i) in enumerate(order):
        st = st_next
        if i == 0:
            v_all = jnp.concatenate([prev_ref[0, r, :, 2 * GROUP_COLS:],
                                     qkv_ref[0, r, :, 2 * GROUP_COLS:]], axis=0)
            vt_all = v_all.astype(F32).T.astype(BF16)
        if idx + 1 < len(order):
            r2, i2 = order[idx + 1]
            if i2 == 0:
                k_all = keys_of(r2)
            st_next = scores(k_all, r2, i2)
        if True:
            vt2 = vt_all[:, i * STRIDES:(i + 2) * STRIDES]
            o_parts, lse_parts = [], []
            for h in range(HEADS):
                hl = slice(h * STRIDES, (h + 1) * STRIDES)
                s_h = st[:, hl] + band_ref[:, hl]
                if i == 0:
                    s_h = jnp.concatenate(
                        [jnp.where(first, NEG_INF, s_h[:STRIDES]), s_h[STRIDES:]], axis=0)
                m = jnp.max(s_h, axis=0, keepdims=True)
                e = jnp.exp2(s_h - m)
                l = jnp.sum(e, axis=0, keepdims=True)
                ot = _dot(vt2[h * HEAD_DIM:(h + 1) * HEAD_DIM], e.astype(BF16))
                o_parts.append(ot * (1.0 / l))
                lse_parts.append(jnp.broadcast_to(m * LN2 + jnp.log(l), (HEAD_DIM, STRIDES)))
            rows = pl.ds(i * STRIDES * dil + r, STRIDES, stride=dil)
            o_rows = jnp.concatenate(o_parts, axis=0).T
            lse_rows = jnp.concatenate(lse_parts, axis=0).T
            for c in range(GROUP_COLS // LANES):
                o_ref[c, rows, :] = o_rows[:, c * LANES:(c + 1) * LANES]
                lse_ref[c, rows, :] = lse_rows[:, c * LANES:(c + 1) * LANES]


def _attn(qkv, bands, g, n, s, dil):
    chunks = ATTN_CHUNKS // dil
    span = chunks * STRIDES
    steps = s // (span * dil)
    slabs = GROUP_COLS // LANES
    out_spec = pl.BlockSpec((slabs, span * dil, LANES), lambda b, k: (0, b * steps + k, 0))
    out_sds = jax.ShapeDtypeStruct((slabs, n * s, LANES), F32)
    return pl.pallas_call(
        functools.partial(_attn_body, dil=dil, chunks=chunks),
        grid=(n, steps),
        in_specs=[pl.BlockSpec((1, dil, span, QKV_COLS), lambda b, k: (b, 0, k, 0)),
                  pl.BlockSpec((1, dil, STRIDES, QKV_COLS),
                               lambda b, k: (b, 0, jnp.maximum(k * chunks - 1, 0), 0)),
                  pl.BlockSpec((None, 2 * STRIDES, HEAD_LANES), lambda b, k: (g, 0, 0))],
        out_specs=[out_spec, out_spec],
        out_shape=[out_sds, out_sds],
        compiler_params=_params("parallel", "parallel"),
        name="attn_d%d" % dil,
    )(qkv, qkv, bands)


def _memkv_body(mem_ref, g_ref, w_ref, o_ref):
    o_ref[...] = _dot(_rms(mem_ref[...], g_ref[...]).astype(BF16), w_ref[...])


def _memkv(mem, g, w):
    m = mem.shape[0]
    row = pl.BlockSpec((MEM_LEN, D_MODEL), lambda i: (i, 0))
    return pl.pallas_call(
        _memkv_body,
        grid=(m // MEM_LEN,),
        in_specs=[row, _resident((1, D_MODEL)), _resident((D_MODEL, 2 * MEM_WIDTH))],
        out_specs=pl.BlockSpec((MEM_LEN, 2 * MEM_WIDTH), lambda i: (i, 0)),
        out_shape=jax.ShapeDtypeStruct((m, 2 * MEM_WIDTH), F32),
        compiler_params=_params("parallel"),
        name="memkv",
    )(mem, g, w)


def _merge_math(a, pooled, c, x, gmix_ref, win_ref, wpool_ref, scale_ref, woa_ref, wob_ref,
                woc_ref, wout_ref):
    u = _rms(x, gmix_ref[...]).astype(BF16)
    mixed = [_dot(pooled[gi].astype(BF16), wpool_ref[gi]) for gi in range(len(POOL_WINDOWS))]
    b = jnp.concatenate(mixed, axis=1) * scale_ref[...]
    m = None
    for k, (branch, wo_ref) in enumerate(((a, woa_ref), (b, wob_ref), (c, woc_ref))):
        lo = GATE_OFF + k * D_MODEL
        gate = jax.nn.sigmoid(_dot(u, win_ref[:, lo:lo + D_MODEL]))
        term = gate * _dot(branch.astype(BF16), wo_ref[...])
        m = term if m is None else m + term
    return x + _dot(m.astype(BF16), wout_ref[...])


def _mix_body(o0, o1, o2, l0, l1, l2, z_ref, halo_ref, qm_ref, mkv_ref, x_ref, *rest):
    merge_refs, out_ref = rest[:-1], rest[-1]
    j = pl.program_id(1)
    tm = x_ref.shape[0]
    mkv = mkv_ref[...].astype(BF16)
    qm = qm_ref[...]
    head = lambda h: slice(h * MEM_HEAD_DIM, (h + 1) * MEM_HEAD_DIM)
    scores = [_dot_t(qm[:, head(h)], mkv[:, head(h)]) * (1.0 / math.sqrt(MEM_HEAD_DIM))
              for h in range(MEM_HEADS)]
    probs = []
    for h in range(MEM_HEADS):
        mm = jnp.max(scores[h], axis=-1, keepdims=True)
        p = jnp.exp(scores[h] - mm)
        probs.append((p.astype(BF16), jnp.sum(p, axis=-1, keepdims=True)))
    c = jnp.concatenate(
        [_dot(p, mkv[:, MEM_WIDTH + h * MEM_HEAD_DIM:MEM_WIDTH + (h + 1) * MEM_HEAD_DIM]) / l
         for h, (p, l) in enumerate(probs)], axis=1)
    unslab = lambda ref: jnp.concatenate([ref[c] for c in range(ref.shape[0])], axis=1)
    lses = [unslab(l0), unslab(l1), unslab(l2)]
    mx = jnp.maximum(jnp.maximum(lses[0], lses[1]), lses[2])
    es = [jnp.exp(l - mx) for l in lses]
    a = ((es[0] * unslab(o0) + es[1] * unslab(o1) + es[2] * unslab(o2))
         / (es[0] + es[1] + es[2]))
    z = z_ref[...]
    halo = jnp.where(j == 0, 0.0, halo_ref[...])
    zc = jnp.concatenate([halo, z], axis=0)
    pos = j * tm + lax.broadcasted_iota(jnp.int32, (tm, 1), 0)
    pooled = []
    for gi, kw in enumerate(POOL_WINDOWS):
        cs = slice(gi * POOL_GROUP, (gi + 1) * POOL_GROUP)
        run = zc[:, cs]
        width = 1
        while width < kw:
            run = run[width:] + run[:-width]
            width *= 2
        first = POOL_HALO - (kw - 1)
        cnt = jnp.minimum(kw, pos + 1).astype(F32)
        pooled.append(run[first:first + tm] / cnt - z[:, cs])
    out_ref[...] = _merge_math(a, pooled, c, x_ref[...], *merge_refs)


def _merge_weight_specs():
    return [_resident((1, D_MODEL)), _resident((D_MODEL, IN_COLS)),
            _resident((len(POOL_WINDOWS), POOL_GROUP, POOL_GROUP)), _resident((1, POOL_WIDTH)),
            _resident((GROUP_COLS, D_MODEL)), _resident((POOL_WIDTH, D_MODEL)),
            _resident((MEM_WIDTH, D_MODEL)), _resident((D_MODEL, D_MODEL))]


def _mix_prompt(os, lses, z, qm, mkv, x, weights, n, s):
    tm = TOKEN_TILE
    tj = s // tm
    halo_per_tile = tm // POOL_HALO

    def tok(cols):
        return pl.BlockSpec((tm, cols), lambda b, j: (b * tj + j, 0))

    halo = pl.BlockSpec((POOL_HALO, POOL_WIDTH),
                        lambda b, j: (jnp.maximum((b * tj + j) * halo_per_tile - 1, 0), 0))
    slab = pl.BlockSpec((GROUP_COLS // LANES, tm, LANES), lambda b, j: (0, b * tj + j, 0))
    in_specs = ([slab] * 6 + [tok(POOL_WIDTH), halo, tok(MEM_WIDTH),
                pl.BlockSpec((MEM_LEN, 2 * MEM_WIDTH), lambda b, j: (b, 0)),
                tok(D_MODEL)] + _merge_weight_specs())
    return pl.pallas_call(
        _mix_body,
        grid=(n, tj),
        in_specs=in_specs,
        out_specs=tok(D_MODEL),
        out_shape=jax.ShapeDtypeStruct((n * s, D_MODEL), F32),
        compiler_params=_params("parallel", "parallel"),
        name="mix_prompt",
    )(*os, *lses, z, z, qm, mkv, x, *weights)


N_SAMPLE_INPUTS = 14


def _sample_branch_math(req, qkv0, qkv1, qkv2, c0, c1, c2, b0, b1, b2, bias0_ref, z_ref, st_ref,
                        qm_ref, cm_ref, a_ref, pooled_ref, c_ref):
    row = pl.ds(req, 1)
    eye = (lax.broadcasted_iota(jnp.int32, (HEAD_DIM, HEAD_DIM), 0)
           == lax.broadcasted_iota(jnp.int32, (HEAD_DIM, HEAD_DIM), 1))
    cube = (HEADS, HEAD_DIM, HEAD_DIM)

    def heads_of(vec, lo):
        return jnp.stack([vec[:, lo + h * HEAD_DIM:lo + (h + 1) * HEAD_DIM]
                          for h in range(HEADS)], axis=0)

    outs, lses = [], []
    for g, (qkv_ref, cache_ref, bias_ref) in enumerate(
            ((qkv0, c0, b0), (qkv1, c1, b1), (qkv2, c2, b2))):
        qkv = qkv_ref[row, :]
        q, kn, vn = heads_of(qkv, 0), heads_of(qkv, GROUP_COLS), heads_of(qkv, 2 * GROUP_COLS)
        q_col = jnp.sum(jnp.where(eye, jnp.broadcast_to(q, cube), 0.0), axis=2, keepdims=True)
        s = jnp.sum(cache_ref[0, 0] * q_col, axis=1, keepdims=True) + bias_ref[...]
        sn = jnp.sum(kn * q, axis=2, keepdims=True) + bias0_ref[g]
        m = jnp.maximum(jnp.max(s, axis=2, keepdims=True), sn)
        p = jnp.exp(s - m)
        pn = jnp.exp(sn - m)
        l = jnp.sum(p, axis=2, keepdims=True) + pn
        pv = jnp.sum(cache_ref[0, 1] * p, axis=2, keepdims=True)
        pv_row = jnp.sum(jnp.where(eye, jnp.broadcast_to(pv, cube), 0.0), axis=1, keepdims=True)
        outs.append((pv_row + pn * vn) / l)
        lses.append(m + jnp.log(l))
    mx = jnp.maximum(jnp.maximum(lses[0], lses[1]), lses[2])
    es = [jnp.exp(lse - mx) for lse in lses]
    a = (es[0] * outs[0] + es[1] * outs[1] + es[2] * outs[2]) / (es[0] + es[1] + es[2])
    a_ref[row, :] = jnp.concatenate([a[h] for h in range(HEADS)], axis=1)

    zn = z_ref[row, :]
    st = st_ref[:, row, :]
    pooled = []
    for gi, kw in enumerate(POOL_WINDOWS):
        cs = slice(gi * POOL_GROUP, (gi + 1) * POOL_GROUP)
        tot = jnp.sum(st[POOL_STATE - (kw - 1):, :, cs], axis=0) + zn[:, cs]
        pooled.append(tot / float(min(kw, PAST_LEN + 1)) - zn[:, cs])
    pooled_ref[row, :] = jnp.concatenate(pooled, axis=1)

    qm_row = qm_ref[row, :]
    qm = jnp.concatenate([qm_row[:, h * MEM_HEAD_DIM:(h + 1) * MEM_HEAD_DIM]
                          for h in range(MEM_HEADS)], axis=0)
    km = cm_ref[0, :, 0]
    vm = cm_ref[0, :, 1]
    s = jnp.sum(km * qm[None], axis=-1, keepdims=True) * (1.0 / math.sqrt(MEM_HEAD_DIM))
    m = jnp.max(s, axis=0)
    p = jnp.exp(s - m[None])
    c = jnp.sum(p * vm, axis=0) / jnp.sum(p, axis=0)
    c_ref[row, :] = jnp.concatenate([c[h:h + 1] for h in range(MEM_HEADS)], axis=1)


def _sample_branch_specs(operands, request_of):
    assert len(operands) == N_SAMPLE_INPUTS
    nb = operands[0].shape[0]

    def per_request(t):
        zeros = (0,) * (t.ndim - 1)
        return pl.BlockSpec((1,) + t.shape[1:], lambda *g: (request_of(*g),) + zeros)

    def whole(shape):
        zeros = (0,) * len(shape)
        return pl.BlockSpec(shape, lambda *g: zeros)

    streamed = (3, 4, 5, 13)
    in_specs = [per_request(t) if k in streamed else whole(t.shape)
                for k, t in enumerate(operands)]
    widths = (GROUP_COLS, POOL_WIDTH, MEM_WIDTH)
    return (in_specs, [whole((nb, w)) for w in widths],
            [jax.ShapeDtypeStruct((nb, w), F32) for w in widths])


def _sample_merge_body(a_ref, pooled_ref, c_ref, x_ref, *rest):
    merge_refs, out_ref = rest[:-1], rest[-1]
    pooled_all = pooled_ref[...]
    pooled = [pooled_all[:, gi * POOL_GROUP:(gi + 1) * POOL_GROUP]
              for gi in range(len(POOL_WINDOWS))]
    out_ref[...] = _merge_math(a_ref[...], pooled, c_ref[...], x_ref[...], *merge_refs)


def _sample_merge(a, pooled, c, x, weights):
    nb = x.shape[0]
    full = lambda cols: _resident((nb, cols))
    return pl.pallas_call(
        _sample_merge_body,
        grid=(1,),
        in_specs=[full(GROUP_COLS), full(POOL_WIDTH), full(MEM_WIDTH), full(D_MODEL)]
        + _merge_weight_specs(),
        out_specs=pl.BlockSpec((nb, D_MODEL), lambda i: (0, 0)),
        out_shape=jax.ShapeDtypeStruct((nb, D_MODEL), F32),
        compiler_params=_params("arbitrary"),
        name="sample_merge",
    )(a, pooled, c, x, *weights)


def _rel_bucket(n):
    max_exact = N_BUCKETS // 2
    nf = jnp.maximum(n, 1).astype(F32)
    large = max_exact + (jnp.log(nf / max_exact) / math.log(MAX_DISTANCE / max_exact)
                         * (N_BUCKETS - max_exact)).astype(jnp.int32)
    large = jnp.minimum(large, N_BUCKETS - 1)
    return jnp.where(n < max_exact, n, large)


def _stride_bias(rel_bias, g, dil):
    j = jnp.arange(STRIDES + 1, dtype=jnp.int32)
    return rel_bias[_rel_bucket(j * dil)][:, g * HEADS:(g + 1) * HEADS].astype(F32)


def _band_row(bias_j):
    row = jnp.concatenate([bias_j[::-1], jnp.full((STRIDES - 1, HEADS), NEG_INF, F32)], axis=0)
    return row.T.reshape(HEADS, 1, 2 * STRIDES)


def _cache_bias(bias_j, dil):
    on_grid = bias_j[STRIDES:0:-1].T
    full = jnp.full((HEADS, STRIDES, dil), NEG_INF, F32).at[:, :, 0].set(on_grid)
    return full.reshape(HEADS, 1, STRIDES * dil)


def kernel(x_prompt, x_sample, cache_win0_kv, cache_win1_kv, cache_win2_kv, state_pool, cache_mem_kv, mem_prompt, rel_bias, g_ffn1, w1_gate, w1_up, w1_down, g_mix, w_in, w_pool, pool_scale, g_mem, w_mem_kv, w_oa, w_ob, w_oc, w_out, g_ffn2, w2_gate, w2_up, w2_down, g_final):
    n, s, _ = x_prompt.shape
    nb = x_sample.shape[0]
    depth = g_ffn1.shape[0]
    win_caches = (cache_win0_kv, cache_win1_kv, cache_win2_kv)
    bias_js = [_stride_bias(rel_bias, g, dil) for g, (_, dil) in enumerate(DIL_GROUPS)]
    bands = _bands(jnp.stack([_band_row(b) for b in bias_js]))
    bias_cache = [_cache_bias(b, dil) for b, (_, dil) in zip(bias_js, DIL_GROUPS)]
    bias_new = jnp.stack([b[0] for b in bias_js]).reshape(N_GROUPS, HEADS, 1, 1)
    gfin = g_final.reshape(1, D_MODEL)

    xp = x_prompt.reshape(n * s, D_MODEL)
    xs = x_sample.reshape(nb, D_MODEL)
    st_p = [[] for _ in range(5)]
    st_s = [[] for _ in range(4)]
    for l in range(depth):
        last = l == depth - 1
        bf = lambda w: w[l].astype(BF16)
        vec = lambda v: v[l].reshape(1, -1)
        w1 = (bf(w1_gate), bf(w1_up), bf(w1_down))
        q_scale = 1.0 / math.sqrt(HEAD_DIM)

        later = [w2_gate[l], w2_up[l], w2_down[l], w_in[l], w_oa[l], w_ob[l], w_oc[l], w_out[l],
                 w_mem_kv[l], w_pool[l].reshape(len(POOL_WINDOWS) * POOL_GROUP, POOL_GROUP)]
        xp, later = _ffn(xp, vec(g_ffn1), *w1, casts=later)
        w2, (win_l, woa_l, wob_l, woc_l, wout_l, wmem_l, wpool_l) = later[0:3], later[3:]
        merge_w = (vec(g_mix), win_l, wpool_l.reshape(w_pool.shape[1:]), vec(pool_scale),
                   woa_l, wob_l, woc_l, wout_l)

        xs = _ffn(xs, vec(g_ffn1), *w1)
        res = _inproj(xs, vec(g_mix), win_l, 1, nb, None, [1] * N_GROUPS, F32, q_scale)
        s_qkvs, s_z, s_qm = [t.reshape(nb, QKV_COLS) for t in res[0:3]], res[3], res[4]
        rider = (s_qkvs + [jnp.transpose(cw[l], (0, 2, 3, 4, 1)) for cw in win_caches]
                 + bias_cache + [bias_new, s_z, jnp.transpose(state_pool[l], (1, 0, 2)), s_qm,
                                 cache_mem_kv[l]])

        res = _inproj(xp, vec(g_mix), win_l, n, s, [min(w, s) for w, _ in DIL_GROUPS],
                      [d for _, d in DIL_GROUPS], BF16, q_scale * LOG2E, rider)
        qkvs, kvwins, z, qm = res[0:3], res[3:6], res[6], res[7]
        s_a, s_pooled, s_c = res[8:11]
        os, lses = [], []
        for g, (_, dil) in enumerate(DIL_GROUPS):
            o, lse = _attn(qkvs[g], bands, g, n, s, dil)
            os.append(o)
            lses.append(lse)
        mkv = _memkv(mem_prompt.reshape(n * MEM_LEN, D_MODEL), vec(g_mem), wmem_l)
        xp = _mix_prompt(os, lses, z, qm, mkv, xp, merge_w, n, s)
        xp = _ffn(xp, vec(g_ffn2), *w2, g_final=gfin if last else None)
        for g, (win, _) in enumerate(DIL_GROUPS):
            kv_t = kvwins[g].reshape(n, 2, HEADS, HEAD_DIM, min(win, s))
            st_p[g].append(jnp.transpose(kv_t, (0, 4, 1, 2, 3)))
        st_p[3].append(z.reshape(n, s, POOL_WIDTH)[:, s - POOL_STATE:])
        st_p[4].append(mkv.reshape(n, MEM_LEN, 2, MEM_HEADS, MEM_HEAD_DIM))

        xs = _sample_merge(s_a, s_pooled, s_c, xs, merge_w)
        xs = _ffn(xs, vec(g_ffn2), *w2, g_final=gfin if last else None)
        for g in range(N_GROUPS):
            st_s[g].append(s_qkvs[g][:, GROUP_COLS:].reshape(nb, 1, 2, HEADS, HEAD_DIM))
        st_s[3].append(s_z.reshape(nb, 1, POOL_WIDTH))

    y_prompt = xp.reshape(n, s, D_MODEL)
    y_sample = xs.reshape(nb, 1, D_MODEL)
    stack = lambda ts: jnp.stack(ts, axis=0)
    return (y_prompt, y_sample, stack(st_p[0]), stack(st_p[1]), stack(st_p[2]), stack(st_p[3]),
            stack(st_p[4]), stack(st_s[0]), stack(st_s[1]), stack(st_s[2]), stack(st_s[3]))
```

```python
import functools
import math

import jax
import jax.numpy as jnp
from jax import lax
from jax.experimental import pallas as pl
from jax.experimental.pallas import tpu as pltpu

F32 = jnp.float32
BF16 = jnp.bfloat16

D_MODEL = 1024
D_FF = 2816
LANES = 128
HEAD_DIM = 64
HEADS = 4
DIL_GROUPS = ((128, 1), (512, 4), (2048, 16))
N_GROUPS = 3
GROUP_COLS = HEADS * HEAD_DIM
QKV_COLS = 3 * GROUP_COLS
STRIDES = 128
POOL_WINDOWS = (2, 4, 8, 16)
POOL_GROUP = 128
POOL_WIDTH = 512
POOL_STATE = 15
POOL_HALO = 16
MEM_LEN = 256
MEM_HEADS = 4
MEM_HEAD_DIM = 128
MEM_WIDTH = 512
N_BUCKETS = 32
MAX_DISTANCE = 2048
N_BRANCH = 3
EPS = 1e-6
NEG_INF = -1e30
LOG2E = math.log2(math.e)
LN2 = math.log(2.0)
PAST_LEN = 8192

V7X_VMEM_LIMIT_BYTES = 56 * 1024 * 1024
TOKEN_TILE = 512


def _params(*sem):
    return pltpu.CompilerParams(dimension_semantics=sem,
                                vmem_limit_bytes=V7X_VMEM_LIMIT_BYTES)


def _resident(shape):
    zeros = (0,) * len(shape)
    return pl.BlockSpec(shape, lambda *_: zeros, pipeline_mode=pl.Buffered(1))


def _rms(x, g):
    return x * lax.rsqrt(jnp.mean(x * x, axis=-1, keepdims=True) + EPS) * g


def _dot(a, b):
    return jnp.dot(a, b, preferred_element_type=F32)


def _dot_t(a, b):
    return lax.dot_general(a, b, (((1,), (1,)), ((), ())), preferred_element_type=F32)


BF16_SUBLANES = 16


def _ffn_body(x_ref, g_ref, wg_ref, wu_ref, wd_ref, *rest, final, n_casts):
    rest, cast_out = (rest[:-n_casts], rest[-n_casts:]) if n_casts else (rest, ())
    cast_in, o_ref = rest[len(rest) - 1 - n_casts:-1], rest[-1]
    x = x_ref[...]
    h = _rms(x, g_ref[...]).astype(BF16)
    a = _dot(h, wg_ref[...])
    b = _dot(h, wu_ref[...])
    act = (a * jax.nn.sigmoid(a) * b).astype(BF16)
    y = x + 0.5 * _dot(act, wd_ref[...])
    if final:
        y = _rms(y, rest[0][...])
    o_ref[...] = y
    for src, dst in zip(cast_in, cast_out):
        dst[...] = src[...].astype(BF16)


def _ffn(x, g, wg, wu, wd, g_final=None, casts=(), rider=()):
    m = x.shape[0]
    tm = min(TOKEN_TILE, m)
    steps = m // tm
    final = g_final is not None
    row = pl.BlockSpec((tm, D_MODEL), lambda i: (i, 0))
    in_specs = [row, _resident((1, D_MODEL)), _resident((D_MODEL, D_FF)),
                _resident((D_MODEL, D_FF)), _resident((D_FF, D_MODEL))]
    args = [x, g, wg, wu, wd]
    if final:
        in_specs.append(_resident((1, D_MODEL)))
        args.append(g_final)
    cast_specs = []
    for w in casts:
        blocks = math.gcd(steps, w.shape[0] // BF16_SUBLANES)
        per = steps // blocks
        cast_specs.append(pl.BlockSpec((w.shape[0] // blocks, w.shape[1]),
                                       lambda i, per=per: (i // per, 0)))
    in_specs, out_specs = in_specs + cast_specs, [row] + cast_specs
    out_shape = ([jax.ShapeDtypeStruct((m, D_MODEL), F32)]
                 + [jax.ShapeDtypeStruct(w.shape, BF16) for w in casts])
    body = functools.partial(_ffn_body, final=final, n_casts=len(casts))
    n_own_in, n_own_out = len(in_specs), len(out_specs)
    if rider:
        assert rider[0].shape[0] == steps
        r_in, r_out, r_shapes = _sample_branch_specs(rider, lambda i: i)
        in_specs, out_specs, out_shape = in_specs + r_in, out_specs + r_out, out_shape + r_shapes

        def body(*refs, own=body):
            ins, r_ins = refs[:n_own_in], refs[n_own_in:n_own_in + N_SAMPLE_INPUTS]
            outs = refs[n_own_in + N_SAMPLE_INPUTS:]
            own(*ins, *outs[:n_own_out])
            _sample_branch_math(pl.program_id(0), *r_ins, *outs[n_own_out:])

    out = pl.pallas_call(
        body,
        grid=(steps,),
        in_specs=in_specs,
        out_specs=out_specs,
        out_shape=out_shape,
        compiler_params=_params("arbitrary" if casts or rider else "parallel"),
        name="ffn_final" if final else "ffn",
    )(*args, *casts, *rider)
    if not casts and not rider:
        return out[0]
    return (out[0],) + ((out[1:n_own_out],) if casts else ()) + ((out[n_own_out:],) if rider else ())


Z_OFF = N_GROUPS * QKV_COLS
QM_OFF = Z_OFF + POOL_WIDTH
GATE_OFF = QM_OFF + MEM_WIDTH
IN_COLS = GATE_OFF + N_BRANCH * D_MODEL


def _inproj_body(x_ref, g_ref, w_ref, *rest, q_scale, with_windows):
    qkv_refs = rest[0:N_GROUPS]
    kv_refs = rest[N_GROUPS:2 * N_GROUPS] if with_windows else (None,) * N_GROUPS
    z_ref, qm_ref, p_scr = rest[-3:]
    u = _rms(x_ref[...], g_ref[...]).astype(BF16)
    tm = u.shape[0]
    width = N_GROUPS * GROUP_COLS
    p = _dot(u, w_ref[:, 0:3 * width])
    for g, (qkv_ref, kv_ref) in enumerate(zip(qkv_refs, kv_refs)):
        dil, per_class = qkv_ref.shape[1], qkv_ref.shape[2]
        q, k, v = [p[:, t * width + g * GROUP_COLS:t * width + (g + 1) * GROUP_COLS]
                   for t in range(3)]
        qkv = jnp.concatenate([q * q_scale, k, v], axis=1)
        if dil == 1:
            qkv_ref[0, 0] = qkv.astype(qkv_ref.dtype)
        else:
            for c in range(QKV_COLS // LANES):
                p_scr[c] = qkv[:, c * LANES:(c + 1) * LANES]
            for r in range(dil):
                blk = jnp.concatenate([p_scr[c, pl.ds(r, per_class, stride=dil), :]
                                       for c in range(QKV_COLS // LANES)], axis=1)
                qkv_ref[0, r] = blk.astype(qkv_ref.dtype)
        if with_windows:
            rows = kv_ref.shape[2]
            kv_ref[0] = jnp.concatenate([k[tm - rows:], v[tm - rows:]], axis=1).T
    z_ref[...] = _dot(u, w_ref[:, Z_OFF:QM_OFF])
    qm_ref[...] = _dot(u, w_ref[:, QM_OFF:GATE_OFF]).astype(qm_ref.dtype)


def _inproj(x, g, w, n, s, windows, dils, q_dtype, q_scale):
    tm = min(TOKEN_TILE, s)
    tj = s // tm

    def tok(cols):
        return pl.BlockSpec((tm, cols), lambda b, j: (b * tj + j, 0))

    qkv_specs = [pl.BlockSpec((1, d, tm // d, QKV_COLS), lambda b, j: (b, 0, j, 0)) for d in dils]
    qkv_shapes = [jax.ShapeDtypeStruct((n, d, s // d, QKV_COLS), q_dtype) for d in dils]

    kv_specs, kv_shapes = [], []
    for win in windows or ():
        rows = min(win, tm)
        assert tm % rows == 0 and win % rows == 0
        first = (s - win) // tm
        if win >= tm:
            spec = pl.BlockSpec((1, 2 * GROUP_COLS, rows),
                                lambda b, j, first=first: (b, 0, jnp.maximum(j - first, 0)))
        else:
            spec = pl.BlockSpec((1, 2 * GROUP_COLS, rows), lambda b, j: (b, 0, 0))
        kv_specs.append(spec)
        kv_shapes.append(jax.ShapeDtypeStruct((n, 2 * GROUP_COLS, win), F32))
    m = n * s
    out_shape = (qkv_shapes + kv_shapes + [
        jax.ShapeDtypeStruct((m, POOL_WIDTH), F32),
        jax.ShapeDtypeStruct((m, MEM_WIDTH), q_dtype)])
    out_specs = qkv_specs + kv_specs + [tok(POOL_WIDTH), tok(MEM_WIDTH)]
    return pl.pallas_call(
        functools.partial(_inproj_body, q_scale=q_scale, with_windows=bool(windows)),
        grid=(n, tj),
        in_specs=[tok(D_MODEL), _resident((1, D_MODEL)),
                  pl.BlockSpec((D_MODEL, GATE_OFF), lambda b, j: (0, 0),
                               pipeline_mode=pl.Buffered(1))],
        out_specs=out_specs,
        out_shape=out_shape,
        scratch_shapes=[pltpu.VMEM((QKV_COLS // LANES, tm, LANES), F32)],
        compiler_params=_params("arbitrary", "arbitrary"),
        name="inproj",
    )(x, g, w)


ATTN_CHUNKS = 16
HEAD_LANES = HEADS * STRIDES


def _band_body(row_ref, band_ref):
    for g in range(N_GROUPS):
        for h in range(HEADS):
            rows = jnp.broadcast_to(row_ref[g, h] * LOG2E, (STRIDES, 2 * STRIDES))
            band = pltpu.roll(rows, 0, 1, stride=1, stride_axis=0).T
            band_ref[g, :, h * STRIDES:(h + 1) * STRIDES] = band


def _bands(rows):
    return pl.pallas_call(
        _band_body,
        grid=(1,),
        in_specs=[_resident(rows.shape)],
        out_specs=pl.BlockSpec((N_GROUPS, 2 * STRIDES, HEAD_LANES), lambda i: (0, 0, 0)),
        out_shape=jax.ShapeDtypeStruct((N_GROUPS, 2 * STRIDES, HEAD_LANES), F32),
        compiler_params=_params("arbitrary"),
        name="bands",
    )(rows)


def _attn_body(qkv_ref, prev_ref, band_ref, o_ref, lse_ref, *, dil, chunks):
    first = pl.program_id(1) == 0
    lane_head = lax.broadcasted_iota(jnp.int32, (1, GROUP_COLS), 1) // HEAD_DIM

    def keys_of(r):
        return jnp.concatenate([prev_ref[0, r, :, GROUP_COLS:2 * GROUP_COLS],
                                qkv_ref[0, r, :, GROUP_COLS:2 * GROUP_COLS]], axis=0)

    def scores(k_all, r, i):
        q = qkv_ref[0, r, i * STRIDES:(i + 1) * STRIDES, 0:GROUP_COLS]
        qm = jnp.concatenate(
            [jnp.where(lane_head == h, q, jnp.zeros_like(q)) for h in range(HEADS)], axis=0)
        return _dot_t(k_all[i * STRIDES:(i + 2) * STRIDES], qm)

    order = [(r, i) for r in range(dil) for i in range(chunks)]
    k_all = keys_of(0)
    st_next = scores(k_all, 0, 0)
    for idx, (r, i) in enumerate(order):
        st = st_next
        if i == 0:
            v_all = jnp.concatenate([prev_ref[0, r, :, 2 * GROUP_COLS:],
                                     qkv_ref[0, r, :, 2 * GROUP_COLS:]], axis=0)
            vt_all = v_all.astype(F32).T.astype(BF16)
        if idx + 1 < len(order):
            r2, i2 = order[idx + 1]
            if i2 == 0:
                k_all = keys_of(r2)
            st_next = scores(k_all, r2, i2)
        if True:
            vt2 = vt_all[:, i * STRIDES:(i + 2) * STRIDES]
            o_parts, lse_parts = [], []
            for h in range(HEADS):
                hl = slice(h * STRIDES, (h + 1) * STRIDES)
                s_h = st[:, hl] + band_ref[:, hl]
                if i == 0:
                    s_h = jnp.concatenate(
                        [jnp.where(first, NEG_INF, s_h[:STRIDES]), s_h[STRIDES:]], axis=0)
                m = jnp.max(s_h, axis=0, keepdims=True)
                e = jnp.exp2(s_h - m)
                l = jnp.sum(e, axis=0, keepdims=True)
                ot = _dot(vt2[h * HEAD_DIM:(h + 1) * HEAD_DIM], e.astype(BF16))
                o_parts.append(ot * (1.0 / l))
                lse_parts.append(jnp.broadcast_to(m * LN2 + jnp.log(l), (HEAD_DIM, STRIDES)))
            rows = pl.ds(i * STRIDES * dil + r, STRIDES, stride=dil)
            o_rows = jnp.concatenate(o_parts, axis=0).T
            lse_rows = jnp.concatenate(lse_parts, axis=0).T
            for c in range(GROUP_COLS // LANES):
                o_ref[c, rows, :] = o_rows[:, c * LANES:(c + 1) * LANES]
                lse_ref[c, rows, :] = lse_rows[:, c * LANES:(c + 1) * LANES]


def _attn(qkv, bands, g, n, s, dil):
    chunks = ATTN_CHUNKS // dil
    span = chunks * STRIDES
    steps = s // (span * dil)
    slabs = GROUP_COLS // LANES
    out_spec = pl.BlockSpec((slabs, span * dil, LANES), lambda b, k: (0, b * steps + k, 0))
    out_sds = jax.ShapeDtypeStruct((slabs, n * s, LANES), F32)
    return pl.pallas_call(
        functools.partial(_attn_body, dil=dil, chunks=chunks),
        grid=(n, steps),
        in_specs=[pl.BlockSpec((1, dil, span, QKV_COLS), lambda b, k: (b, 0, k, 0)),
                  pl.BlockSpec((1, dil, STRIDES, QKV_COLS),
                               lambda b, k: (b, 0, jnp.maximum(k * chunks - 1, 0), 0)),
                  pl.BlockSpec((None, 2 * STRIDES, HEAD_LANES), lambda b, k: (g, 0, 0))],
        out_specs=[out_spec, out_spec],
        out_shape=[out_sds, out_sds],
        compiler_params=_params("parallel", "parallel"),
        name="attn_d%d" % dil,
    )(qkv, qkv, bands)


def _memkv_body(mem_ref, g_ref, w_ref, o_ref):
    o_ref[...] = _dot(_rms(mem_ref[...], g_ref[...]).astype(BF16), w_ref[...])


def _memkv(mem, g, w):
    m = mem.shape[0]
    row = pl.BlockSpec((MEM_LEN, D_MODEL), lambda i: (i, 0))
    return pl.pallas_call(
        _memkv_body,
        grid=(m // MEM_LEN,),
        in_specs=[row, _resident((1, D_MODEL)), _resident((D_MODEL, 2 * MEM_WIDTH))],
        out_specs=pl.BlockSpec((MEM_LEN, 2 * MEM_WIDTH), lambda i: (i, 0)),
        out_shape=jax.ShapeDtypeStruct((m, 2 * MEM_WIDTH), F32),
        compiler_params=_params("parallel"),
        name="memkv",
    )(mem, g, w)


def _merge_math(a, pooled, c, x, gmix_ref, win_ref, wpool_ref, scale_ref, woa_ref, wob_ref,
                woc_ref, wout_ref):
    u = _rms(x, gmix_ref[...]).astype(BF16)
    mixed = [_dot(pooled[gi].astype(BF16), wpool_ref[gi]) for gi in range(len(POOL_WINDOWS))]
    b = jnp.concatenate(mixed, axis=1) * scale_ref[...]
    m = None
    for k, (branch, wo_ref) in enumerate(((a, woa_ref), (b, wob_ref), (c, woc_ref))):
        lo = GATE_OFF + k * D_MODEL
        gate = jax.nn.sigmoid(_dot(u, win_ref[:, lo:lo + D_MODEL]))
        term = gate * _dot(branch.astype(BF16), wo_ref[...])
        m = term if m is None else m + term
    return x + _dot(m.astype(BF16), wout_ref[...])


def _mix_body(o0, o1, o2, l0, l1, l2, z_ref, halo_ref, qm_ref, mkv_ref, x_ref, *rest):
    merge_refs, out_ref = rest[:-1], rest[-1]
    j = pl.program_id(1)
    tm = x_ref.shape[0]
    unslab = lambda ref: jnp.concatenate([ref[c] for c in range(ref.shape[0])], axis=1)
    lses = [unslab(l0), unslab(l1), unslab(l2)]
    mx = jnp.maximum(jnp.maximum(lses[0], lses[1]), lses[2])
    es = [jnp.exp(l - mx) for l in lses]
    a = ((es[0] * unslab(o0) + es[1] * unslab(o1) + es[2] * unslab(o2))
         / (es[0] + es[1] + es[2]))
    z = z_ref[...]
    halo = jnp.where(j == 0, 0.0, halo_ref[...])
    zc = jnp.concatenate([halo, z], axis=0)
    pos = j * tm + lax.broadcasted_iota(jnp.int32, (tm, 1), 0)
    pooled = []
    for gi, kw in enumerate(POOL_WINDOWS):
        cs = slice(gi * POOL_GROUP, (gi + 1) * POOL_GROUP)
        run = zc[:, cs]
        width = 1
        while width < kw:
            run = run[width:] + run[:-width]
            width *= 2
        first = POOL_HALO - (kw - 1)
        cnt = jnp.minimum(kw, pos + 1).astype(F32)
        pooled.append(run[first:first + tm] / cnt - z[:, cs])
    mkv = mkv_ref[...].astype(BF16)
    qm = qm_ref[...]
    cs_out = []
    for h in range(MEM_HEADS):
        hs = slice(h * MEM_HEAD_DIM, (h + 1) * MEM_HEAD_DIM)
        vs = slice(MEM_WIDTH + h * MEM_HEAD_DIM, MEM_WIDTH + (h + 1) * MEM_HEAD_DIM)
        s = _dot_t(qm[:, hs], mkv[:, hs]) * (1.0 / math.sqrt(MEM_HEAD_DIM))
        mm = jnp.max(s, axis=-1, keepdims=True)
        p = jnp.exp(s - mm)
        l = jnp.sum(p, axis=-1, keepdims=True)
        cs_out.append(_dot(p.astype(BF16), mkv[:, vs]) / l)
    c = jnp.concatenate(cs_out, axis=1)
    out_ref[...] = _merge_math(a, pooled, c, x_ref[...], *merge_refs)


def _merge_weight_specs():
    return [_resident((1, D_MODEL)), _resident((D_MODEL, IN_COLS)),
            _resident((len(POOL_WINDOWS), POOL_GROUP, POOL_GROUP)), _resident((1, POOL_WIDTH)),
            _resident((GROUP_COLS, D_MODEL)), _resident((POOL_WIDTH, D_MODEL)),
            _resident((MEM_WIDTH, D_MODEL)), _resident((D_MODEL, D_MODEL))]


def _mix_prompt(os, lses, z, qm, mkv, x, weights, n, s):
    tm = TOKEN_TILE
    tj = s // tm
    halo_per_tile = tm // POOL_HALO

    def tok(cols):
        return pl.BlockSpec((tm, cols), lambda b, j: (b * tj + j, 0))

    halo = pl.BlockSpec((POOL_HALO, POOL_WIDTH),
                        lambda b, j: (jnp.maximum((b * tj + j) * halo_per_tile - 1, 0), 0))
    slab = pl.BlockSpec((GROUP_COLS // LANES, tm, LANES), lambda b, j: (0, b * tj + j, 0))
    in_specs = ([slab] * 6 + [tok(POOL_WIDTH), halo, tok(MEM_WIDTH),
                pl.BlockSpec((MEM_LEN, 2 * MEM_WIDTH), lambda b, j: (b, 0)),
                tok(D_MODEL)] + _merge_weight_specs())
    return pl.pallas_call(
        _mix_body,
        grid=(n, tj),
        in_specs=in_specs,
        out_specs=tok(D_MODEL),
        out_shape=jax.ShapeDtypeStruct((n * s, D_MODEL), F32),
        compiler_params=_params("parallel", "parallel"),
        name="mix_prompt",
    )(*os, *lses, z, z, qm, mkv, x, *weights)


N_SAMPLE_INPUTS = 14


def _sample_branch_math(req, qkv0, qkv1, qkv2, c0, c1, c2, b0, b1, b2, bias0_ref, z_ref, st_ref,
                        qm_ref, cm_ref, a_ref, pooled_ref, c_ref):
    row = pl.ds(req, 1)
    eye = (lax.broadcasted_iota(jnp.int32, (HEAD_DIM, HEAD_DIM), 0)
           == lax.broadcasted_iota(jnp.int32, (HEAD_DIM, HEAD_DIM), 1))
    cube = (HEADS, HEAD_DIM, HEAD_DIM)

    def heads_of(vec, lo):
        return jnp.stack([vec[:, lo + h * HEAD_DIM:lo + (h + 1) * HEAD_DIM]
                          for h in range(HEADS)], axis=0)

    outs, lses = [], []
    for g, (qkv_ref, cache_ref, bias_ref) in enumerate(
            ((qkv0, c0, b0), (qkv1, c1, b1), (qkv2, c2, b2))):
        qkv = qkv_ref[row, :]
        q, kn, vn = heads_of(qkv, 0), heads_of(qkv, GROUP_COLS), heads_of(qkv, 2 * GROUP_COLS)
        q_col = jnp.sum(jnp.where(eye, jnp.broadcast_to(q, cube), 0.0), axis=2, keepdims=True)
        s = jnp.sum(cache_ref[0, 0] * q_col, axis=1, keepdims=True) + bias_ref[...]
        sn = jnp.sum(kn * q, axis=2, keepdims=True) + bias0_ref[g]
        m = jnp.maximum(jnp.max(s, axis=2, keepdims=True), sn)
        p = jnp.exp(s - m)
        pn = jnp.exp(sn - m)
        l = jnp.sum(p, axis=2, keepdims=True) + pn
        pv = jnp.sum(cache_ref[0, 1] * p, axis=2, keepdims=True)
        pv_row = jnp.sum(jnp.where(eye, jnp.broadcast_to(pv, cube), 0.0), axis=1, keepdims=True)
        outs.append((pv_row + pn * vn) / l)
        lses.append(m + jnp.log(l))
    mx = jnp.maximum(jnp.maximum(lses[0], lses[1]), lses[2])
    es = [jnp.exp(lse - mx) for lse in lses]
    a = (es[0] * outs[0] + es[1] * outs[1] + es[2] * outs[2]) / (es[0] + es[1] + es[2])
    a_ref[row, :] = jnp.concatenate([a[h] for h in range(HEADS)], axis=1)

    zn = z_ref[row, :]
    st = st_ref[:, row, :]
    pooled = []
    for gi, kw in enumerate(POOL_WINDOWS):
        cs = slice(gi * POOL_GROUP, (gi + 1) * POOL_GROUP)
        tot = jnp.sum(st[POOL_STATE - (kw - 1):, :, cs], axis=0) + zn[:, cs]
        pooled.append(tot / float(min(kw, PAST_LEN + 1)) - zn[:, cs])
    pooled_ref[row, :] = jnp.concatenate(pooled, axis=1)

    qm_row = qm_ref[row, :]
    qm = jnp.concatenate([qm_row[:, h * MEM_HEAD_DIM:(h + 1) * MEM_HEAD_DIM]
                          for h in range(MEM_HEADS)], axis=0)
    km = cm_ref[0, :, 0]
    vm = cm_ref[0, :, 1]
    s = jnp.sum(km * qm[None], axis=-1, keepdims=True) * (1.0 / math.sqrt(MEM_HEAD_DIM))
    m = jnp.max(s, axis=0)
    p = jnp.exp(s - m[None])
    c = jnp.sum(p * vm, axis=0) / jnp.sum(p, axis=0)
    c_ref[row, :] = jnp.concatenate([c[h:h + 1] for h in range(MEM_HEADS)], axis=1)


def _sample_branch_specs(operands, request_of):
    assert len(operands) == N_SAMPLE_INPUTS
    nb = operands[0].shape[0]

    def per_request(t):
        zeros = (0,) * (t.ndim - 1)
        return pl.BlockSpec((1,) + t.shape[1:], lambda *g: (request_of(*g),) + zeros)

    def whole(shape):
        zeros = (0,) * len(shape)
        return pl.BlockSpec(shape, lambda *g: zeros)

    streamed = (3, 4, 5, 13)
    in_specs = [per_request(t) if k in streamed else whole(t.shape)
                for k, t in enumerate(operands)]
    widths = (GROUP_COLS, POOL_WIDTH, MEM_WIDTH)
    return (in_specs, [whole((nb, w)) for w in widths],
            [jax.ShapeDtypeStruct((nb, w), F32) for w in widths])


def _sample_merge_body(a_ref, pooled_ref, c_ref, x_ref, *rest):
    merge_refs, out_ref = rest[:-1], rest[-1]
    pooled_all = pooled_ref[...]
    pooled = [pooled_all[:, gi * POOL_GROUP:(gi + 1) * POOL_GROUP]
              for gi in range(len(POOL_WINDOWS))]
    out_ref[...] = _merge_math(a_ref[...], pooled, c_ref[...], x_ref[...], *merge_refs)


def _sample_merge(a, pooled, c, x, weights):
    nb = x.shape[0]
    full = lambda cols: _resident((nb, cols))
    return pl.pallas_call(
        _sample_merge_body,
        grid=(1,),
        in_specs=[full(GROUP_COLS), full(POOL_WIDTH), full(MEM_WIDTH), full(D_MODEL)]
        + _merge_weight_specs(),
        out_specs=pl.BlockSpec((nb, D_MODEL), lambda i: (0, 0)),
        out_shape=jax.ShapeDtypeStruct((nb, D_MODEL), F32),
        compiler_params=_params("arbitrary"),
        name="sample_merge",
    )(a, pooled, c, x, *weights)


def _rel_bucket(n):
    max_exact = N_BUCKETS // 2
    nf = jnp.maximum(n, 1).astype(F32)
    large = max_exact + (jnp.log(nf / max_exact) / math.log(MAX_DISTANCE / max_exact)
                         * (N_BUCKETS - max_exact)).astype(jnp.int32)
    large = jnp.minimum(large, N_BUCKETS - 1)
    return jnp.where(n < max_exact, n, large)


def _stride_bias(rel_bias, g, dil):
    j = jnp.arange(STRIDES + 1, dtype=jnp.int32)
    return rel_bias[_rel_bucket(j * dil)][:, g * HEADS:(g + 1) * HEADS].astype(F32)


def _band_row(bias_j):
    row = jnp.concatenate([bias_j[::-1], jnp.full((STRIDES - 1, HEADS), NEG_INF, F32)], axis=0)
    return row.T.reshape(HEADS, 1, 2 * STRIDES)


def _cache_bias(bias_j, dil):
    on_grid = bias_j[STRIDES:0:-1].T
    full = jnp.full((HEADS, STRIDES, dil), NEG_INF, F32).at[:, :, 0].set(on_grid)
    return full.reshape(HEADS, 1, STRIDES * dil)


def kernel(x_prompt, x_sample, cache_win0_kv, cache_win1_kv, cache_win2_kv, state_pool, cache_mem_kv, mem_prompt, rel_bias, g_ffn1, w1_gate, w1_up, w1_down, g_mix, w_in, w_pool, pool_scale, g_mem, w_mem_kv, w_oa, w_ob, w_oc, w_out, g_ffn2, w2_gate, w2_up, w2_down, g_final):
    n, s, _ = x_prompt.shape
    nb = x_sample.shape[0]
    depth = g_ffn1.shape[0]
    win_caches = (cache_win0_kv, cache_win1_kv, cache_win2_kv)
    bias_js = [_stride_bias(rel_bias, g, dil) for g, (_, dil) in enumerate(DIL_GROUPS)]
    bands = _bands(jnp.stack([_band_row(b) for b in bias_js]))
    bias_cache = [_cache_bias(b, dil) for b, (_, dil) in zip(bias_js, DIL_GROUPS)]
    bias_new = jnp.stack([b[0] for b in bias_js]).reshape(N_GROUPS, HEADS, 1, 1)
    gfin = g_final.reshape(1, D_MODEL)

    xp = x_prompt.reshape(n * s, D_MODEL)
    xs = x_sample.reshape(nb, D_MODEL)
    st_p = [[] for _ in range(5)]
    st_s = [[] for _ in range(4)]
    for l in range(depth):
        last = l == depth - 1
        bf = lambda w: w[l].astype(BF16)
        vec = lambda v: v[l].reshape(1, -1)
        w1 = (bf(w1_gate), bf(w1_up), bf(w1_down))
        q_scale = 1.0 / math.sqrt(HEAD_DIM)

        later = [w2_gate[l], w2_up[l], w2_down[l], w_in[l], w_oa[l], w_ob[l], w_oc[l], w_out[l],
                 w_mem_kv[l], w_pool[l].reshape(len(POOL_WINDOWS) * POOL_GROUP, POOL_GROUP)]
        xp, later = _ffn(xp, vec(g_ffn1), *w1, casts=later)
        w2, (win_l, woa_l, wob_l, woc_l, wout_l, wmem_l, wpool_l) = later[0:3], later[3:]
        merge_w = (vec(g_mix), win_l, wpool_l.reshape(w_pool.shape[1:]), vec(pool_scale),
                   woa_l, wob_l, woc_l, wout_l)

        xs = _ffn(xs, vec(g_ffn1), *w1)
        res = _inproj(xs, vec(g_mix), win_l, 1, nb, None, [1] * N_GROUPS, F32, q_scale)
        s_qkvs, s_z, s_qm = [t.reshape(nb, QKV_COLS) for t in res[0:3]], res[3], res[4]
        rider = (s_qkvs + [jnp.transpose(cw[l], (0, 2, 3, 4, 1)) for cw in win_caches]
                 + bias_cache + [bias_new, s_z, jnp.transpose(state_pool[l], (1, 0, 2)), s_qm,
                                 cache_mem_kv[l]])

        res = _inproj(xp, vec(g_mix), win_l, n, s, [min(w, s) for w, _ in DIL_GROUPS],
                      [d for _, d in DIL_GROUPS], BF16, q_scale * LOG2E)
        qkvs, kvwins, z, qm = res[0:3], res[3:6], res[6], res[7]
        os, lses = [], []
        for g, (_, dil) in enumerate(DIL_GROUPS):
            o, lse = _attn(qkvs[g], bands, g, n, s, dil)
            os.append(o)
            lses.append(lse)
        mkv = _memkv(mem_prompt.reshape(n * MEM_LEN, D_MODEL), vec(g_mem), wmem_l)
        xp = _mix_prompt(os, lses, z, qm, mkv, xp, merge_w, n, s)
        xp, (s_a, s_pooled, s_c) = _ffn(xp, vec(g_ffn2), *w2, g_final=gfin if last else None,
                                        rider=rider)
        for g, (win, _) in enumerate(DIL_GROUPS):
            kv_t = kvwins[g].reshape(n, 2, HEADS, HEAD_DIM, min(win, s))
            st_p[g].append(jnp.transpose(kv_t, (0, 4, 1, 2, 3)))
        st_p[3].append(z.reshape(n, s, POOL_WIDTH)[:, s - POOL_STATE:])
        st_p[4].append(mkv.reshape(n, MEM_LEN, 2, MEM_HEADS, MEM_HEAD_DIM))

        xs = _sample_merge(s_a, s_pooled, s_c, xs, merge_w)
        xs = _ffn(xs, vec(g_ffn2), *w2, g_final=gfin if last else None)
        for g in range(N_GROUPS):
            st_s[g].append(s_qkvs[g][:, GROUP_COLS:].reshape(nb, 1, 2, HEADS, HEAD_DIM))
        st_s[3].append(s_z.reshape(nb, 1, POOL_WIDTH))

    y_prompt = xp.reshape(n, s, D_MODEL)
    y_sample = xs.reshape(nb, 1, D_MODEL)
    stack = lambda ts: jnp.stack(ts, axis=0)
    return (y_prompt, y_sample, stack(st_p[0]), stack(st_p[1]), stack(st_p[2]), stack(st_p[3]),
            stack(st_p[4]), stack(st_s[0]), stack(st_s[1]), stack(st_s[2]), stack(st_s[3]))
```

```python
import functools
import math

import jax
import jax.numpy as jnp
from jax import lax
from jax.experimental import pallas as pl
from jax.experimental.pallas import tpu as pltpu

F32 = jnp.float32
BF16 = jnp.bfloat16

D_MODEL = 1024
D_FF = 2816
LANES = 128
HEAD_DIM = 64
HEADS = 4
DIL_GROUPS = ((128, 1), (512, 4), (2048, 16))
N_GROUPS = 3
GROUP_COLS = HEADS * HEAD_DIM
QKV_COLS = 3 * GROUP_COLS
STRIDES = 128
POOL_WINDOWS = (2, 4, 8, 16)
POOL_GROUP = 128
POOL_WIDTH = 512
POOL_STATE = 15
POOL_HALO = 16
MEM_LEN = 256
MEM_HEADS = 4
MEM_HEAD_DIM = 128
MEM_WIDTH = 512
N_BUCKETS = 32
MAX_DISTANCE = 2048
N_BRANCH = 3
EPS = 1e-6
NEG_INF = -1e30
LOG2E = math.log2(math.e)
LN2 = math.log(2.0)
PAST_LEN = 8192

V7X_VMEM_LIMIT_BYTES = 56 * 1024 * 1024
TOKEN_TILE = 512


def _params(*sem):
    return pltpu.CompilerParams(dimension_semantics=sem,
                                vmem_limit_bytes=V7X_VMEM_LIMIT_BYTES)


def _resident(shape):
    zeros = (0,) * len(shape)
    return pl.BlockSpec(shape, lambda *_: zeros, pipeline_mode=pl.Buffered(1))


def _rms(x, g):
    return x * lax.rsqrt(jnp.mean(x * x, axis=-1, keepdims=True) + EPS) * g


def _dot(a, b):
    return jnp.dot(a, b, preferred_element_type=F32)


def _dot_t(a, b):
    return lax.dot_general(a, b, (((1,), (1,)), ((), ())), preferred_element_type=F32)


BF16_SUBLANES = 16


def _ffn_math(x, g_ref, wg_ref, wu_ref, wd_ref, gf_ref):
    h = _rms(x, g_ref[...]).astype(BF16)
    a = _dot(h, wg_ref[...])
    b = _dot(h, wu_ref[...])
    act = (a * jax.nn.sigmoid(a) * b).astype(BF16)
    y = x + 0.5 * _dot(act, wd_ref[...])
    return y if gf_ref is None else _rms(y, gf_ref[...])


def _ffn_body(*refs, final, n_casts, steps, has_tail):
    refs = iter(refs)
    x_ref = next(refs)
    weights = [next(refs) for _ in range(4)] + [next(refs) if final else None]
    cast_in = [next(refs) for _ in range(n_casts)]
    tail_in = next(refs) if has_tail else None
    o_ref = next(refs)
    cast_out = [next(refs) for _ in range(n_casts)]

    def tile():
        o_ref[...] = _ffn_math(x_ref[...], *weights)
        for src, dst in zip(cast_in, cast_out):
            dst[...] = src[...].astype(BF16)

    if not has_tail:
        tile()
        return
    tail_out = next(refs)
    pl.when(pl.program_id(0) < steps)(tile)

    @pl.when(pl.program_id(0) == steps)
    def _():
        tail_out[...] = _ffn_math(tail_in[...], *weights)


def _ffn(x, g, wg, wu, wd, g_final=None, casts=(), tail=None):
    m = x.shape[0]
    tm = min(TOKEN_TILE, m)
    steps = m // tm
    final = g_final is not None
    has_tail = tail is not None
    last = steps - 1
    row = pl.BlockSpec((tm, D_MODEL), lambda i: (jnp.minimum(i, last), 0))
    in_specs = [row, _resident((1, D_MODEL)), _resident((D_MODEL, D_FF)),
                _resident((D_MODEL, D_FF)), _resident((D_FF, D_MODEL))]
    args = [x, g, wg, wu, wd]
    if final:
        in_specs.append(_resident((1, D_MODEL)))
        args.append(g_final)
    cast_specs = []
    for w in casts:
        blocks = math.gcd(steps, w.shape[0] // BF16_SUBLANES)
        per = steps // blocks
        cast_specs.append(pl.BlockSpec((w.shape[0] // blocks, w.shape[1]),
                                       lambda i, per=per: (jnp.minimum(i, last) // per, 0)))
    in_specs, out_specs = in_specs + cast_specs, [row] + cast_specs
    out_shape = ([jax.ShapeDtypeStruct((m, D_MODEL), F32)]
                 + [jax.ShapeDtypeStruct(w.shape, BF16) for w in casts])
    if has_tail:
        in_specs.append(_resident(tail.shape))
        out_specs.append(pl.BlockSpec(tail.shape, lambda i: (0, 0)))
        out_shape.append(jax.ShapeDtypeStruct(tail.shape, F32))
    out = pl.pallas_call(
        functools.partial(_ffn_body, final=final, n_casts=len(casts), steps=steps,
                          has_tail=has_tail),
        grid=(steps + has_tail,),
        in_specs=in_specs,
        out_specs=out_specs,
        out_shape=out_shape,
        compiler_params=_params("arbitrary" if casts or has_tail else "parallel"),
        name="ffn_final" if final else "ffn",
    )(*args, *casts, *([tail] if has_tail else []))
    if not casts and not has_tail:
        return out[0]
    n_casts = len(casts)
    return ((out[0],) + ((out[1:1 + n_casts],) if casts else ())
            + ((out[1 + n_casts],) if has_tail else ()))


Z_OFF = N_GROUPS * QKV_COLS
QM_OFF = Z_OFF + POOL_WIDTH
GATE_OFF = QM_OFF + MEM_WIDTH
IN_COLS = GATE_OFF + N_BRANCH * D_MODEL


def _inproj_body(x_ref, g_ref, w_ref, *rest, q_scale, with_windows):
    qkv_refs = rest[0:N_GROUPS]
    kv_refs = rest[N_GROUPS:2 * N_GROUPS] if with_windows else (None,) * N_GROUPS
    z_ref, qm_ref, p_scr = rest[-3:]
    u = _rms(x_ref[...], g_ref[...]).astype(BF16)
    tm = u.shape[0]
    width = N_GROUPS * GROUP_COLS
    p = _dot(u, w_ref[:, 0:3 * width])
    for g, (qkv_ref, kv_ref) in enumerate(zip(qkv_refs, kv_refs)):
        dil, per_class = qkv_ref.shape[1], qkv_ref.shape[2]
        q, k, v = [p[:, t * width + g * GROUP_COLS:t * width + (g + 1) * GROUP_COLS]
                   for t in range(3)]
        qkv = jnp.concatenate([q * q_scale, k, v], axis=1)
        if dil == 1:
            qkv_ref[0, 0] = qkv.astype(qkv_ref.dtype)
        else:
            for c in range(QKV_COLS // LANES):
                p_scr[c] = qkv[:, c * LANES:(c + 1) * LANES]
            for r in range(dil):
                blk = jnp.concatenate([p_scr[c, pl.ds(r, per_class, stride=dil), :]
                                       for c in range(QKV_COLS // LANES)], axis=1)
                qkv_ref[0, r] = blk.astype(qkv_ref.dtype)
        if with_windows:
            rows = kv_ref.shape[2]
            kv_ref[0] = jnp.concatenate([k[tm - rows:], v[tm - rows:]], axis=1).T
    z_ref[...] = _dot(u, w_ref[:, Z_OFF:QM_OFF])
    qm_ref[...] = _dot(u, w_ref[:, QM_OFF:GATE_OFF]).astype(qm_ref.dtype)


def _inproj(x, g, w, n, s, windows, dils, q_dtype, q_scale, rider=()):
    tm = min(TOKEN_TILE, s)
    tj = s // tm

    def tok(cols):
        return pl.BlockSpec((tm, cols), lambda b, j: (b * tj + j, 0))

    qkv_specs = [pl.BlockSpec((1, d, tm // d, QKV_COLS), lambda b, j: (b, 0, j, 0)) for d in dils]
    qkv_shapes = [jax.ShapeDtypeStruct((n, d, s // d, QKV_COLS), q_dtype) for d in dils]

    kv_specs, kv_shapes = [], []
    for win in windows or ():
        rows = min(win, tm)
        assert tm % rows == 0 and win % rows == 0
        first = (s - win) // tm
        if win >= tm:
            spec = pl.BlockSpec((1, 2 * GROUP_COLS, rows),
                                lambda b, j, first=first: (b, 0, jnp.maximum(j - first, 0)))
        else:
            spec = pl.BlockSpec((1, 2 * GROUP_COLS, rows), lambda b, j: (b, 0, 0))
        kv_specs.append(spec)
        kv_shapes.append(jax.ShapeDtypeStruct((n, 2 * GROUP_COLS, win), F32))
    m = n * s
    out_shape = (qkv_shapes + kv_shapes + [
        jax.ShapeDtypeStruct((m, POOL_WIDTH), F32),
        jax.ShapeDtypeStruct((m, MEM_WIDTH), q_dtype)])
    out_specs = qkv_specs + kv_specs + [tok(POOL_WIDTH), tok(MEM_WIDTH)]
    in_specs = [tok(D_MODEL), _resident((1, D_MODEL)),
                pl.BlockSpec((D_MODEL, GATE_OFF), lambda b, j: (0, 0),
                             pipeline_mode=pl.Buffered(1))]
    body = functools.partial(_inproj_body, q_scale=q_scale, with_windows=bool(windows))
    n_own_out = len(out_specs)
    if rider:
        assert rider[0].shape[0] == n * tj
        r_in, r_out, r_shapes = _sample_branch_specs(rider, lambda b, j: b * tj + j)
        in_specs, out_specs, out_shape = in_specs + r_in, out_specs + r_out, out_shape + r_shapes

        def body(*refs, own=body):
            ins, r_ins = refs[:3], refs[3:3 + N_SAMPLE_INPUTS]
            outs, scratch = refs[3 + N_SAMPLE_INPUTS:-1], refs[-1:]
            own(*ins, *outs[:n_own_out], *scratch)
            _sample_branch_math(pl.program_id(0) * tj + pl.program_id(1), *r_ins,
                                *outs[n_own_out:])

    return pl.pallas_call(
        body,
        grid=(n, tj),
        in_specs=in_specs,
        out_specs=out_specs,
        out_shape=out_shape,
        scratch_shapes=[pltpu.VMEM((QKV_COLS // LANES, tm, LANES), F32)],
        compiler_params=_params("arbitrary", "arbitrary"),
        name="inproj",
    )(x, g, w, *rider)


ATTN_CHUNKS = 16
HEAD_LANES = HEADS * STRIDES


def _band_body(row_ref, band_ref):
    for g in range(N_GROUPS):
        for h in range(HEADS):
            rows = jnp.broadcast_to(row_ref[g, h] * LOG2E, (STRIDES, 2 * STRIDES))
            band = pltpu.roll(rows, 0, 1, stride=1, stride_axis=0).T
            band_ref[g, :, h * STRIDES:(h + 1) * STRIDES] = band


def _bands(rows):
    return pl.pallas_call(
        _band_body,
        grid=(1,),
        in_specs=[_resident(rows.shape)],
        out_specs=pl.BlockSpec((N_GROUPS, 2 * STRIDES, HEAD_LANES), lambda i: (0, 0, 0)),
        out_shape=jax.ShapeDtypeStruct((N_GROUPS, 2 * STRIDES, HEAD_LANES), F32),
        compiler_params=_params("arbitrary"),
        name="bands",
    )(rows)


def _attn_body(qkv_ref, prev_ref, band_ref, o_ref, lse_ref, *, dil, chunks):
    first = pl.program_id(1) == 0
    lane_head = lax.broadcasted_iota(jnp.int32, (1, GROUP_COLS), 1) // HEAD_DIM

    def keys_of(r):
        return jnp.concatenate([prev_ref[0, r, :, GROUP_COLS:2 * GROUP_COLS],
                                qkv_ref[0, r, :, GROUP_COLS:2 * GROUP_COLS]], axis=0)

    def scores(k_all, r, i):
        q = qkv_ref[0, r, i * STRIDES:(i + 1) * STRIDES, 0:GROUP_COLS]
        qm = jnp.concatenate(
            [jnp.where(lane_head == h, q, jnp.zeros_like(q)) for h in range(HEADS)], axis=0)
        return _dot_t(k_all[i * STRIDES:(i + 2) * STRIDES], qm)

    ones_row = jnp.where(
        lax.broadcasted_iota(jnp.int32, (BF16_SUBLANES, 2 * STRIDES), 0) == 0, 1.0, 0.0
    ).astype(BF16)

    order = [(r, i) for r in range(dil) for i in range(chunks)]
    k_all = keys_of(0)
    st_next = scores(k_all, 0, 0)
    for idx, (r, i) in enumerate(order):
        st = st_next
        if i == 0:
            v_all = jnp.concatenate([prev_ref[0, r, :, 2 * GROUP_COLS:],
                                     qkv_ref[0, r, :, 2 * GROUP_COLS:]], axis=0)
            vt_all = v_all.astype(F32).T.astype(BF16)
        if idx + 1 < len(order):
            r2, i2 = order[idx + 1]
            if i2 == 0:
                k_all = keys_of(r2)
            st_next = scores(k_all, r2, i2)
        vt2 = vt_all[:, i * STRIDES:(i + 2) * STRIDES]
        o_parts, lse_parts = [], []
        for h in range(HEADS):
            hl = slice(h * STRIDES, (h + 1) * STRIDES)
            s_h = st[:, hl] + band_ref[:, hl]
            if i == 0:
                s_h = jnp.concatenate(
                    [jnp.where(first, NEG_INF, s_h[:STRIDES]), s_h[STRIDES:]], axis=0)
            m = jnp.max(s_h, axis=0, keepdims=True)
            e = jnp.exp2(s_h - m).astype(BF16)
            ot = _dot(jnp.concatenate([vt2[h * HEAD_DIM:(h + 1) * HEAD_DIM], ones_row], axis=0), e)
            l = ot[HEAD_DIM:HEAD_DIM + 1]
            o_parts.append(ot[:HEAD_DIM] * (1.0 / l))
            lse_parts.append(jnp.broadcast_to(m * LN2 + jnp.log(l), (HEAD_DIM, STRIDES)))
        rows = pl.ds(i * STRIDES * dil + r, STRIDES, stride=dil)
        o_rows = jnp.concatenate(o_parts, axis=0).T
        lse_rows = jnp.concatenate(lse_parts, axis=0).T
        for c in range(GROUP_COLS // LANES):
            o_ref[c, rows, :] = o_rows[:, c * LANES:(c + 1) * LANES]
            lse_ref[c, rows, :] = lse_rows[:, c * LANES:(c + 1) * LANES]


def _attn(qkv, bands, g, n, s, dil):
    chunks = ATTN_CHUNKS // dil
    span = chunks * STRIDES
    steps = s // (span * dil)
    slabs = GROUP_COLS // LANES
    out_spec = pl.BlockSpec((slabs, span * dil, LANES), lambda b, k: (0, b * steps + k, 0))
    out_sds = jax.ShapeDtypeStruct((slabs, n * s, LANES), F32)
    return pl.pallas_call(
        functools.partial(_attn_body, dil=dil, chunks=chunks),
        grid=(n, steps),
        in_specs=[pl.BlockSpec((1, dil, span, QKV_COLS), lambda b, k: (b, 0, k, 0)),
                  pl.BlockSpec((1, dil, STRIDES, QKV_COLS),
                               lambda b, k: (b, 0, jnp.maximum(k * chunks - 1, 0), 0)),
                  pl.BlockSpec((None, 2 * STRIDES, HEAD_LANES), lambda b, k: (g, 0, 0))],
        out_specs=[out_spec, out_spec],
        out_shape=[out_sds, out_sds],
        compiler_params=_params("parallel", "parallel"),
        name="attn_d%d" % dil,
    )(qkv, qkv, bands)


def _memkv_body(mem_ref, g_ref, w_ref, o_ref):
    o_ref[...] = _dot(_rms(mem_ref[...], g_ref[...]).astype(BF16), w_ref[...])


def _memkv(mem, g, w):
    m = mem.shape[0]
    row = pl.BlockSpec((MEM_LEN, D_MODEL), lambda i: (i, 0))
    return pl.pallas_call(
        _memkv_body,
        grid=(m // MEM_LEN,),
        in_specs=[row, _resident((1, D_MODEL)), _resident((D_MODEL, 2 * MEM_WIDTH))],
        out_specs=pl.BlockSpec((MEM_LEN, 2 * MEM_WIDTH), lambda i: (i, 0)),
        out_shape=jax.ShapeDtypeStruct((m, 2 * MEM_WIDTH), F32),
        compiler_params=_params("parallel"),
        name="memkv",
    )(mem, g, w)


def _merge_math(a, pooled, c, x, gmix_ref, win_ref, wpool_ref, scale_ref, woa_ref, wob_ref,
                woc_ref, wout_ref):
    u = _rms(x, gmix_ref[...]).astype(BF16)
    mixed = [_dot(pooled[gi].astype(BF16), wpool_ref[gi]) for gi in range(len(POOL_WINDOWS))]
    b = jnp.concatenate(mixed, axis=1) * scale_ref[...]
    m = None
    for k, (branch, wo_ref) in enumerate(((a, woa_ref), (b, wob_ref), (c, woc_ref))):
        lo = GATE_OFF + k * D_MODEL
        gate = jax.nn.sigmoid(_dot(u, win_ref[:, lo:lo + D_MODEL]))
        term = gate * _dot(branch.astype(BF16), wo_ref[...])
        m = term if m is None else m + term
    return x + _dot(m.astype(BF16), wout_ref[...])


def _mix_body(o0, o1, o2, l0, l1, l2, z_ref, halo_ref, qm_ref, mkv_ref, x_ref, *rest,
              tile_in_seq):
    merge_refs, out_ref = rest[:-1], rest[-1]
    j = tile_in_seq
    tm = x_ref.shape[0]
    unslab = lambda ref: jnp.concatenate([ref[c] for c in range(ref.shape[0])], axis=1)
    lses = [unslab(l0), unslab(l1), unslab(l2)]
    mx = jnp.maximum(jnp.maximum(lses[0], lses[1]), lses[2])
    es = [jnp.exp(l - mx) for l in lses]
    a = ((es[0] * unslab(o0) + es[1] * unslab(o1) + es[2] * unslab(o2))
         / (es[0] + es[1] + es[2]))
    z = z_ref[...]
    halo = jnp.where(j == 0, 0.0, halo_ref[...])
    zc = jnp.concatenate([halo, z], axis=0)
    pos = j * tm + lax.broadcasted_iota(jnp.int32, (tm, 1), 0)
    pooled = []
    for gi, kw in enumerate(POOL_WINDOWS):
        cs = slice(gi * POOL_GROUP, (gi + 1) * POOL_GROUP)
        run = zc[:, cs]
        width = 1
        while width < kw:
            run = run[width:] + run[:-width]
            width *= 2
        first = POOL_HALO - (kw - 1)
        cnt = jnp.minimum(kw, pos + 1).astype(F32)
        pooled.append(run[first:first + tm] / cnt - z[:, cs])
    mkv = mkv_ref[...].astype(BF16)
    qm = qm_ref[...]
    cs_out = []
    for h in range(MEM_HEADS):
        hs = slice(h * MEM_HEAD_DIM, (h + 1) * MEM_HEAD_DIM)
        vs = slice(MEM_WIDTH + h * MEM_HEAD_DIM, MEM_WIDTH + (h + 1) * MEM_HEAD_DIM)
        s = _dot_t(qm[:, hs], mkv[:, hs]) * (1.0 / math.sqrt(MEM_HEAD_DIM))
        mm = jnp.max(s, axis=-1, keepdims=True)
        p = jnp.exp(s - mm)
        l = jnp.sum(p, axis=-1, keepdims=True)
        cs_out.append(_dot(p.astype(BF16), mkv[:, vs]) / l)
    c = jnp.concatenate(cs_out, axis=1)
    out_ref[...] = _merge_math(a, pooled, c, x_ref[...], *merge_refs)


def _merge_weight_specs():
    return [_resident((1, D_MODEL)), _resident((D_MODEL, IN_COLS)),
            _resident((len(POOL_WINDOWS), POOL_GROUP, POOL_GROUP)), _resident((1, POOL_WIDTH)),
            _resident((GROUP_COLS, D_MODEL)), _resident((POOL_WIDTH, D_MODEL)),
            _resident((MEM_WIDTH, D_MODEL)), _resident((D_MODEL, D_MODEL))]


N_MERGE_WEIGHTS = 8


def _mix_with_tail_body(*refs, tiles_per_seq, steps):
    n_tile_in = 11
    tile_in, merge_refs = refs[:n_tile_in], refs[n_tile_in:n_tile_in + N_MERGE_WEIGHTS]
    a_ref, pooled_ref, c_ref, xs_ref, out_ref, tail_out = refs[n_tile_in + N_MERGE_WEIGHTS:]
    t = pl.program_id(0)

    @pl.when(t < steps)
    def _():
        _mix_body(*tile_in, *merge_refs, out_ref, tile_in_seq=lax.rem(t, tiles_per_seq))

    @pl.when(t == steps)
    def _():
        pooled_all = pooled_ref[...]
        pooled = [pooled_all[:, gi * POOL_GROUP:(gi + 1) * POOL_GROUP]
                  for gi in range(len(POOL_WINDOWS))]
        tail_out[...] = _merge_math(a_ref[...], pooled, c_ref[...], xs_ref[...], *merge_refs)


def _mix(os, lses, z, qm, mkv, x, weights, n, s, tail):
    tm = TOKEN_TILE
    tj = s // tm
    steps = n * tj
    last = steps - 1
    halo_per_tile = tm // POOL_HALO
    tile = lambda t: jnp.minimum(t, last)

    def tok(cols):
        return pl.BlockSpec((tm, cols), lambda t: (tile(t), 0))

    halo = pl.BlockSpec((POOL_HALO, POOL_WIDTH),
                        lambda t: (jnp.maximum(tile(t) * halo_per_tile - 1, 0), 0))
    slab = pl.BlockSpec((GROUP_COLS // LANES, tm, LANES), lambda t: (0, tile(t), 0))
    weight_specs = _merge_weight_specs()
    assert len(weight_specs) == N_MERGE_WEIGHTS
    in_specs = ([slab] * 6 + [tok(POOL_WIDTH), halo, tok(MEM_WIDTH),
                pl.BlockSpec((MEM_LEN, 2 * MEM_WIDTH), lambda t: (tile(t) // tj, 0)),
                tok(D_MODEL)] + weight_specs + [_resident(t.shape) for t in tail])
    xs = tail[-1]
    return pl.pallas_call(
        functools.partial(_mix_with_tail_body, tiles_per_seq=tj, steps=steps),
        grid=(steps + 1,),
        in_specs=in_specs,
        out_specs=[tok(D_MODEL), pl.BlockSpec(xs.shape, lambda t: (0, 0))],
        out_shape=[jax.ShapeDtypeStruct((n * s, D_MODEL), F32),
                   jax.ShapeDtypeStruct(xs.shape, F32)],
        compiler_params=_params("arbitrary"),
        name="mix",
    )(*os, *lses, z, z, qm, mkv, x, *weights, *tail)


N_SAMPLE_INPUTS = 14


def _sample_branch_math(req, qkv0, qkv1, qkv2, c0, c1, c2, b0, b1, b2, bias0_ref, z_ref, st_ref,
                        qm_ref, cm_ref, a_ref, pooled_ref, c_ref):
    row = pl.ds(req, 1)
    eye = (lax.broadcasted_iota(jnp.int32, (HEAD_DIM, HEAD_DIM), 0)
           == lax.broadcasted_iota(jnp.int32, (HEAD_DIM, HEAD_DIM), 1))
    cube = (HEADS, HEAD_DIM, HEAD_DIM)

    def heads_of(vec, lo):
        return jnp.stack([vec[:, lo + h * HEAD_DIM:lo + (h + 1) * HEAD_DIM]
                          for h in range(HEADS)], axis=0)

    outs, lses = [], []
    for g, (qkv_ref, cache_ref, bias_ref) in enumerate(
            ((qkv0, c0, b0), (qkv1, c1, b1), (qkv2, c2, b2))):
        qkv = qkv_ref[row, :]
        q, kn, vn = heads_of(qkv, 0), heads_of(qkv, GROUP_COLS), heads_of(qkv, 2 * GROUP_COLS)
        q_col = jnp.sum(jnp.where(eye, jnp.broadcast_to(q, cube), 0.0), axis=2, keepdims=True)
        s = jnp.sum(cache_ref[0, 0] * q_col, axis=1, keepdims=True) + bias_ref[...]
        sn = jnp.sum(kn * q, axis=2, keepdims=True) + bias0_ref[g]
        m = jnp.maximum(jnp.max(s, axis=2, keepdims=True), sn)
        p = jnp.exp(s - m)
        pn = jnp.exp(sn - m)
        l = jnp.sum(p, axis=2, keepdims=True) + pn
        pv = jnp.sum(cache_ref[0, 1] * p, axis=2, keepdims=True)
        pv_row = jnp.sum(jnp.where(eye, jnp.broadcast_to(pv, cube), 0.0), axis=1, keepdims=True)
        outs.append((pv_row + pn * vn) / l)
        lses.append(m + jnp.log(l))
    mx = jnp.maximum(jnp.maximum(lses[0], lses[1]), lses[2])
    es = [jnp.exp(lse - mx) for lse in lses]
    a = (es[0] * outs[0] + es[1] * outs[1] + es[2] * outs[2]) / (es[0] + es[1] + es[2])
    a_ref[row, :] = jnp.concatenate([a[h] for h in range(HEADS)], axis=1)

    zn = z_ref[row, :]
    st = st_ref[:, row, :]
    pooled = []
    for gi, kw in enumerate(POOL_WINDOWS):
        cs = slice(gi * POOL_GROUP, (gi + 1) * POOL_GROUP)
        tot = jnp.sum(st[POOL_STATE - (kw - 1):, :, cs], axis=0) + zn[:, cs]
        pooled.append(tot / float(min(kw, PAST_LEN + 1)) - zn[:, cs])
    pooled_ref[row, :] = jnp.concatenate(pooled, axis=1)

    qm_row = qm_ref[row, :]
    qm = jnp.concatenate([qm_row[:, h * MEM_HEAD_DIM:(h + 1) * MEM_HEAD_DIM]
                          for h in range(MEM_HEADS)], axis=0)
    km = cm_ref[0, :, 0]
    vm = cm_ref[0, :, 1]
    s = jnp.sum(km * qm[None], axis=-1, keepdims=True) * (1.0 / math.sqrt(MEM_HEAD_DIM))
    m = jnp.max(s, axis=0)
    p = jnp.exp(s - m[None])
    c = jnp.sum(p * vm, axis=0) / jnp.sum(p, axis=0)
    c_ref[row, :] = jnp.concatenate([c[h:h + 1] for h in range(MEM_HEADS)], axis=1)


def _sample_branch_specs(operands, request_of):
    assert len(operands) == N_SAMPLE_INPUTS
    nb = operands[0].shape[0]

    def per_request(t):
        zeros = (0,) * (t.ndim - 1)
        return pl.BlockSpec((1,) + t.shape[1:], lambda *g: (request_of(*g),) + zeros)

    def whole(shape):
        zeros = (0,) * len(shape)
        return pl.BlockSpec(shape, lambda *g: zeros)

    streamed = (3, 4, 5, 13)
    in_specs = [per_request(t) if k in streamed else whole(t.shape)
                for k, t in enumerate(operands)]
    widths = (GROUP_COLS, POOL_WIDTH, MEM_WIDTH)
    return (in_specs, [whole((nb, w)) for w in widths],
            [jax.ShapeDtypeStruct((nb, w), F32) for w in widths])


def _rel_bucket(n):
    max_exact = N_BUCKETS // 2
    nf = jnp.maximum(n, 1).astype(F32)
    large = max_exact + (jnp.log(nf / max_exact) / math.log(MAX_DISTANCE / max_exact)
                         * (N_BUCKETS - max_exact)).astype(jnp.int32)
    large = jnp.minimum(large, N_BUCKETS - 1)
    return jnp.where(n < max_exact, n, large)


def _stride_bias(rel_bias, g, dil):
    j = jnp.arange(STRIDES + 1, dtype=jnp.int32)
    return rel_bias[_rel_bucket(j * dil)][:, g * HEADS:(g + 1) * HEADS].astype(F32)


def _band_row(bias_j):
    row = jnp.concatenate([bias_j[::-1], jnp.full((STRIDES - 1, HEADS), NEG_INF, F32)], axis=0)
    return row.T.reshape(HEADS, 1, 2 * STRIDES)


def _cache_bias(bias_j, dil):
    on_grid = bias_j[STRIDES:0:-1].T
    full = jnp.full((HEADS, STRIDES, dil), NEG_INF, F32).at[:, :, 0].set(on_grid)
    return full.reshape(HEADS, 1, STRIDES * dil)


def kernel(x_prompt, x_sample, cache_win0_kv, cache_win1_kv, cache_win2_kv, state_pool, cache_mem_kv, mem_prompt, rel_bias, g_ffn1, w1_gate, w1_up, w1_down, g_mix, w_in, w_pool, pool_scale, g_mem, w_mem_kv, w_oa, w_ob, w_oc, w_out, g_ffn2, w2_gate, w2_up, w2_down, g_final):
    n, s, _ = x_prompt.shape
    nb = x_sample.shape[0]
    depth = g_ffn1.shape[0]
    win_caches = (cache_win0_kv, cache_win1_kv, cache_win2_kv)
    bias_js = [_stride_bias(rel_bias, g, dil) for g, (_, dil) in enumerate(DIL_GROUPS)]
    bands = _bands(jnp.stack([_band_row(b) for b in bias_js]))
    bias_cache = [_cache_bias(b, dil) for b, (_, dil) in zip(bias_js, DIL_GROUPS)]
    bias_new = jnp.stack([b[0] for b in bias_js]).reshape(N_GROUPS, HEADS, 1, 1)
    gfin = g_final.reshape(1, D_MODEL)

    xp = x_prompt.reshape(n * s, D_MODEL)
    xs = x_sample.reshape(nb, D_MODEL)
    st_p = [[] for _ in range(5)]
    st_s = [[] for _ in range(4)]
    for l in range(depth):
        last = l == depth - 1
        bf = lambda w: w[l].astype(BF16)
        vec = lambda v: v[l].reshape(1, -1)
        w1 = (bf(w1_gate), bf(w1_up), bf(w1_down))
        q_scale = 1.0 / math.sqrt(HEAD_DIM)

        later = [w2_gate[l], w2_up[l], w2_down[l], w_in[l], w_oa[l], w_ob[l], w_oc[l], w_out[l],
                 w_mem_kv[l], w_pool[l].reshape(len(POOL_WINDOWS) * POOL_GROUP, POOL_GROUP)]
        xp, later, xs = _ffn(xp, vec(g_ffn1), *w1, casts=later, tail=xs)
        w2, (win_l, woa_l, wob_l, woc_l, wout_l, wmem_l, wpool_l) = later[0:3], later[3:]
        merge_w = (vec(g_mix), win_l, wpool_l.reshape(w_pool.shape[1:]), vec(pool_scale),
                   woa_l, wob_l, woc_l, wout_l)

        res = _inproj(xs, vec(g_mix), win_l, 1, nb, None, [1] * N_GROUPS, F32, q_scale)
        s_qkvs, s_z, s_qm = [t.reshape(nb, QKV_COLS) for t in res[0:3]], res[3], res[4]
        rider = (s_qkvs + [jnp.transpose(cw[l], (0, 2, 3, 4, 1)) for cw in win_caches]
                 + bias_cache + [bias_new, s_z, jnp.transpose(state_pool[l], (1, 0, 2)), s_qm,
                                 cache_mem_kv[l]])
        res = _inproj(xp, vec(g_mix), win_l, n, s, [min(w, s) for w, _ in DIL_GROUPS],
                      [d for _, d in DIL_GROUPS], BF16, q_scale * LOG2E, rider)
        qkvs, kvwins, z, qm = res[0:3], res[3:6], res[6], res[7]
        s_branches = res[8:11]
        os, lses = [], []
        for g, (_, dil) in enumerate(DIL_GROUPS):
            o, lse = _attn(qkvs[g], bands, g, n, s, dil)
            os.append(o)
            lses.append(lse)
        mkv = _memkv(mem_prompt.reshape(n * MEM_LEN, D_MODEL), vec(g_mem), wmem_l)
        xp, xs = _mix(os, lses, z, qm, mkv, xp, merge_w, n, s, tail=(*s_branches, xs))
        xp, xs = _ffn(xp, vec(g_ffn2), *w2, g_final=gfin if last else None, tail=xs)
        for g, (win, _) in enumerate(DIL_GROUPS):
            kv_t = kvwins[g].reshape(n, 2, HEADS, HEAD_DIM, min(win, s))
            st_p[g].append(jnp.transpose(kv_t, (0, 4, 1, 2, 3)))
        st_p[3].append(z.reshape(n, s, POOL_WIDTH)[:, s - POOL_STATE:])
        st_p[4].append(mkv.reshape(n, MEM_LEN, 2, MEM_HEADS, MEM_HEAD_DIM))

        for g in range(N_GROUPS):
            st_s[g].append(s_qkvs[g][:, GROUP_COLS:].reshape(nb, 1, 2, HEADS, HEAD_DIM))
        st_s[3].append(s_z.reshape(nb, 1, POOL_WIDTH))

    y_prompt = xp.reshape(n, s, D_MODEL)
    y_sample = xs.reshape(nb, 1, D_MODEL)
    stack = lambda ts: jnp.stack(ts, axis=0)
    return (y_prompt, y_sample, stack(st_p[0]), stack(st_p[1]), stack(st_p[2]), stack(st_p[3]),
            stack(st_p[4]), stack(st_s[0]), stack(st_s[1]), stack(st_s[2]), stack(st_s[3]))
```

```python
import functools
import math

import jax
import jax.numpy as jnp
from jax import lax
from jax.experimental import pallas as pl
from jax.experimental.pallas import tpu as pltpu

F32 = jnp.float32
BF16 = jnp.bfloat16

D_MODEL = 1024
D_FF = 2816
LANES = 128
HEAD_DIM = 64
HEADS = 4
DIL_GROUPS = ((128, 1), (512, 4), (2048, 16))
N_GROUPS = 3
GROUP_COLS = HEADS * HEAD_DIM
QKV_COLS = 3 * GROUP_COLS
STRIDES = 128
POOL_WINDOWS = (2, 4, 8, 16)
POOL_GROUP = 128
POOL_WIDTH = 512
POOL_STATE = 15
POOL_HALO = 16
MEM_LEN = 256
MEM_HEADS = 4
MEM_HEAD_DIM = 128
MEM_WIDTH = 512
N_BUCKETS = 32
MAX_DISTANCE = 2048
N_BRANCH = 3
EPS = 1e-6
NEG_INF = -1e30
LOG2E = math.log2(math.e)
LN2 = math.log(2.0)
PAST_LEN = 8192

V7X_VMEM_LIMIT_BYTES = 56 * 1024 * 1024
TOKEN_TILE = 512


def _params(*sem):
    return pltpu.CompilerParams(dimension_semantics=sem,
                                vmem_limit_bytes=V7X_VMEM_LIMIT_BYTES)


def _resident(shape):
    zeros = (0,) * len(shape)
    return pl.BlockSpec(shape, lambda *_: zeros, pipeline_mode=pl.Buffered(1))


def _rms(x, g):
    return x * lax.rsqrt(jnp.mean(x * x, axis=-1, keepdims=True) + EPS) * g


def _dot(a, b):
    return jnp.dot(a, b, preferred_element_type=F32)


def _dot_t(a, b):
    return lax.dot_general(a, b, (((1,), (1,)), ((), ())), preferred_element_type=F32)


BF16_SUBLANES = 16


def _ffn_math(x, g_ref, wg_ref, wu_ref, wd_ref, gf_ref):
    h = _rms(x, g_ref[...]).astype(BF16)
    a = _dot(h, wg_ref[...])
    b = _dot(h, wu_ref[...])
    act = (a * jax.nn.sigmoid(a) * b).astype(BF16)
    y = x + 0.5 * _dot(act, wd_ref[...])
    return y if gf_ref is None else _rms(y, gf_ref[...])


def _ffn_body(*refs, final, n_casts, steps, has_tail):
    refs = iter(refs)
    x_ref = next(refs)
    weights = [next(refs) for _ in range(4)] + [next(refs) if final else None]
    cast_in = [next(refs) for _ in range(n_casts)]
    tail_in = next(refs) if has_tail else None
    o_ref = next(refs)
    cast_out = [next(refs) for _ in range(n_casts)]

    def tile():
        o_ref[...] = _ffn_math(x_ref[...], *weights)
        for src, dst in zip(cast_in, cast_out):
            dst[...] = src[...].astype(BF16)

    if not has_tail:
        tile()
        return
    tail_out = next(refs)
    pl.when(pl.program_id(0) < steps)(tile)

    @pl.when(pl.program_id(0) == steps)
    def _():
        tail_out[...] = _ffn_math(tail_in[...], *weights)


def _ffn(x, g, wg, wu, wd, g_final=None, casts=(), tail=None):
    m = x.shape[0]
    tm = min(TOKEN_TILE, m)
    steps = m // tm
    final = g_final is not None
    has_tail = tail is not None
    last = steps - 1
    row = pl.BlockSpec((tm, D_MODEL), lambda i: (jnp.minimum(i, last), 0))
    in_specs = [row, _resident((1, D_MODEL)), _resident((D_MODEL, D_FF)),
                _resident((D_MODEL, D_FF)), _resident((D_FF, D_MODEL))]
    args = [x, g, wg, wu, wd]
    if final:
        in_specs.append(_resident((1, D_MODEL)))
        args.append(g_final)
    cast_specs = []
    for w in casts:
        blocks = math.gcd(steps, w.shape[0] // BF16_SUBLANES)
        per = steps // blocks
        cast_specs.append(pl.BlockSpec((w.shape[0] // blocks, w.shape[1]),
                                       lambda i, per=per: (jnp.minimum(i, last) // per, 0)))
    in_specs, out_specs = in_specs + cast_specs, [row] + cast_specs
    out_shape = ([jax.ShapeDtypeStruct((m, D_MODEL), F32)]
                 + [jax.ShapeDtypeStruct(w.shape, BF16) for w in casts])
    if has_tail:
        in_specs.append(_resident(tail.shape))
        out_specs.append(pl.BlockSpec(tail.shape, lambda i: (0, 0)))
        out_shape.append(jax.ShapeDtypeStruct(tail.shape, F32))
    out = pl.pallas_call(
        functools.partial(_ffn_body, final=final, n_casts=len(casts), steps=steps,
                          has_tail=has_tail),
        grid=(steps + has_tail,),
        in_specs=in_specs,
        out_specs=out_specs,
        out_shape=out_shape,
        compiler_params=_params("arbitrary" if casts or has_tail else "parallel"),
        name="ffn_final" if final else "ffn",
    )(*args, *casts, *([tail] if has_tail else []))
    if not casts and not has_tail:
        return out[0]
    n_casts = len(casts)
    return ((out[0],) + ((out[1:1 + n_casts],) if casts else ())
            + ((out[1 + n_casts],) if has_tail else ()))


Z_OFF = N_GROUPS * QKV_COLS
QM_OFF = Z_OFF + POOL_WIDTH
GATE_OFF = QM_OFF + MEM_WIDTH
IN_COLS = GATE_OFF + N_BRANCH * D_MODEL


def _inproj_body(x_ref, g_ref, w_ref, *rest, q_scale, with_windows):
    qkv_refs = rest[0:N_GROUPS]
    kv_refs = rest[N_GROUPS:2 * N_GROUPS] if with_windows else (None,) * N_GROUPS
    z_ref, qm_ref, u_ref, p_scr = rest[-4:]
    u = _rms(x_ref[...], g_ref[...]).astype(BF16)
    u_ref[...] = u
    tm = u.shape[0]
    width = N_GROUPS * GROUP_COLS
    p = _dot(u, w_ref[:, 0:3 * width])
    for g, (qkv_ref, kv_ref) in enumerate(zip(qkv_refs, kv_refs)):
        dil, per_class = qkv_ref.shape[1], qkv_ref.shape[2]
        q, k, v = [p[:, t * width + g * GROUP_COLS:t * width + (g + 1) * GROUP_COLS]
                   for t in range(3)]
        qkv = jnp.concatenate([q * q_scale, k, v], axis=1)
        if dil == 1:
            qkv_ref[0, 0] = qkv.astype(qkv_ref.dtype)
        else:
            for c in range(QKV_COLS // LANES):
                p_scr[c] = qkv[:, c * LANES:(c + 1) * LANES]
            for r in range(dil):
                blk = jnp.concatenate([p_scr[c, pl.ds(r, per_class, stride=dil), :]
                                       for c in range(QKV_COLS // LANES)], axis=1)
                qkv_ref[0, r] = blk.astype(qkv_ref.dtype)
        if with_windows:
            rows = kv_ref.shape[2]
            kv_ref[0] = jnp.concatenate([k[tm - rows:], v[tm - rows:]], axis=1).T
    z_ref[...] = _dot(u, w_ref[:, Z_OFF:QM_OFF])
    qm_ref[...] = _dot(u, w_ref[:, QM_OFF:GATE_OFF]).astype(qm_ref.dtype)


def _inproj(x, g, w, n, s, windows, dils, q_dtype, q_scale, rider=()):
    tm = min(TOKEN_TILE, s)
    tj = s // tm

    def tok(cols):
        return pl.BlockSpec((tm, cols), lambda b, j: (b * tj + j, 0))

    qkv_specs = [pl.BlockSpec((1, d, tm // d, QKV_COLS), lambda b, j: (b, 0, j, 0)) for d in dils]
    qkv_shapes = [jax.ShapeDtypeStruct((n, d, s // d, QKV_COLS), q_dtype) for d in dils]

    kv_specs, kv_shapes = [], []
    for win in windows or ():
        rows = min(win, tm)
        assert tm % rows == 0 and win % rows == 0
        first = (s - win) // tm
        if win >= tm:
            spec = pl.BlockSpec((1, 2 * GROUP_COLS, rows),
                                lambda b, j, first=first: (b, 0, jnp.maximum(j - first, 0)))
        else:
            spec = pl.BlockSpec((1, 2 * GROUP_COLS, rows), lambda b, j: (b, 0, 0))
        kv_specs.append(spec)
        kv_shapes.append(jax.ShapeDtypeStruct((n, 2 * GROUP_COLS, win), F32))
    m = n * s
    out_shape = (qkv_shapes + kv_shapes + [
        jax.ShapeDtypeStruct((m, POOL_WIDTH), F32),
        jax.ShapeDtypeStruct((m, MEM_WIDTH), q_dtype),
        jax.ShapeDtypeStruct((m, D_MODEL), BF16)])
    out_specs = qkv_specs + kv_specs + [tok(POOL_WIDTH), tok(MEM_WIDTH), tok(D_MODEL)]
    in_specs = [tok(D_MODEL), _resident((1, D_MODEL)),
                pl.BlockSpec((D_MODEL, GATE_OFF), lambda b, j: (0, 0),
                             pipeline_mode=pl.Buffered(1))]
    body = functools.partial(_inproj_body, q_scale=q_scale, with_windows=bool(windows))
    n_own_out = len(out_specs)
    if rider:
        assert rider[0].shape[0] == n * tj
        r_in, r_out, r_shapes = _sample_branch_specs(rider, lambda b, j: b * tj + j)
        in_specs, out_specs, out_shape = in_specs + r_in, out_specs + r_out, out_shape + r_shapes

        def body(*refs, own=body):
            ins, r_ins = refs[:3], refs[3:3 + N_SAMPLE_INPUTS]
            outs, scratch = refs[3 + N_SAMPLE_INPUTS:-1], refs[-1:]
            own(*ins, *outs[:n_own_out], *scratch)
            _sample_branch_math(pl.program_id(0) * tj + pl.program_id(1), *r_ins,
                                *outs[n_own_out:])

    return pl.pallas_call(
        body,
        grid=(n, tj),
        in_specs=in_specs,
        out_specs=out_specs,
        out_shape=out_shape,
        scratch_shapes=[pltpu.VMEM((QKV_COLS // LANES, tm, LANES), F32)],
        compiler_params=_params("arbitrary", "arbitrary"),
        name="inproj",
    )(x, g, w, *rider)


ATTN_CHUNKS = 16
HEAD_LANES = HEADS * STRIDES


def _band_body(row_ref, band_ref):
    for g in range(N_GROUPS):
        for h in range(HEADS):
            rows = jnp.broadcast_to(row_ref[g, h] * LOG2E, (STRIDES, 2 * STRIDES))
            band = pltpu.roll(rows, 0, 1, stride=1, stride_axis=0).T
            band_ref[g, :, h * STRIDES:(h + 1) * STRIDES] = band


def _bands(rows):
    return pl.pallas_call(
        _band_body,
        grid=(1,),
        in_specs=[_resident(rows.shape)],
        out_specs=pl.BlockSpec((N_GROUPS, 2 * STRIDES, HEAD_LANES), lambda i: (0, 0, 0)),
        out_shape=jax.ShapeDtypeStruct((N_GROUPS, 2 * STRIDES, HEAD_LANES), F32),
        compiler_params=_params("arbitrary"),
        name="bands",
    )(rows)


def _attn_body(qkv_ref, prev_ref, band_ref, o_ref, lse_ref, *, dil, chunks):
    first = pl.program_id(1) == 0
    lane_head = lax.broadcasted_iota(jnp.int32, (1, GROUP_COLS), 1) // HEAD_DIM

    def keys_of(r):
        return jnp.concatenate([prev_ref[0, r, :, GROUP_COLS:2 * GROUP_COLS],
                                qkv_ref[0, r, :, GROUP_COLS:2 * GROUP_COLS]], axis=0)

    def scores(k_all, r, i):
        q = qkv_ref[0, r, i * STRIDES:(i + 1) * STRIDES, 0:GROUP_COLS]
        qm = jnp.concatenate(
            [jnp.where(lane_head == h, q, jnp.zeros_like(q)) for h in range(HEADS)], axis=0)
        return _dot_t(k_all[i * STRIDES:(i + 2) * STRIDES], qm)

    ones_row = jnp.where(
        lax.broadcasted_iota(jnp.int32, (BF16_SUBLANES, 2 * STRIDES), 0) == 0, 1.0, 0.0
    ).astype(BF16)

    order = [(r, i) for r in range(dil) for i in range(chunks)]
    k_all = keys_of(0)
    st_next = scores(k_all, 0, 0)
    for idx, (r, i) in enumerate(order):
        st = st_next
        if i == 0:
            v_all = jnp.concatenate([prev_ref[0, r, :, 2 * GROUP_COLS:],
                                     qkv_ref[0, r, :, 2 * GROUP_COLS:]], axis=0)
            vt_all = v_all.astype(F32).T.astype(BF16)
        if idx + 1 < len(order):
            r2, i2 = order[idx + 1]
            if i2 == 0:
                k_all = keys_of(r2)
            st_next = scores(k_all, r2, i2)
        vt2 = vt_all[:, i * STRIDES:(i + 2) * STRIDES]
        o_parts, lse_parts = [], []
        for h in range(HEADS):
            hl = slice(h * STRIDES, (h + 1) * STRIDES)
            s_h = st[:, hl] + band_ref[:, hl]
            if i == 0:
                s_h = jnp.concatenate(
                    [jnp.where(first, NEG_INF, s_h[:STRIDES]), s_h[STRIDES:]], axis=0)
            m = jnp.max(s_h, axis=0, keepdims=True)
            e = jnp.exp2(s_h - m).astype(BF16)
            ot = _dot(jnp.concatenate([vt2[h * HEAD_DIM:(h + 1) * HEAD_DIM], ones_row], axis=0), e)
            l = ot[HEAD_DIM:HEAD_DIM + 1]
            o_parts.append(ot[:HEAD_DIM] * (1.0 / l))
            lse_parts.append(jnp.broadcast_to(m * LN2 + jnp.log(l), (HEAD_DIM, STRIDES)))
        rows = pl.ds(i * STRIDES * dil + r, STRIDES, stride=dil)
        o_rows = jnp.concatenate(o_parts, axis=0).T
        lse_rows = jnp.concatenate(lse_parts, axis=0).T
        for c in range(GROUP_COLS // LANES):
            o_ref[c, rows, :] = o_rows[:, c * LANES:(c + 1) * LANES]
            lse_ref[c, rows, :] = lse_rows[:, c * LANES:(c + 1) * LANES]


def _attn(qkv, bands, g, n, s, dil):
    chunks = ATTN_CHUNKS // dil
    span = chunks * STRIDES
    steps = s // (span * dil)
    slabs = GROUP_COLS // LANES
    out_spec = pl.BlockSpec((slabs, span * dil, LANES), lambda b, k: (0, b * steps + k, 0))
    out_sds = jax.ShapeDtypeStruct((slabs, n * s, LANES), F32)
    return pl.pallas_call(
        functools.partial(_attn_body, dil=dil, chunks=chunks),
        grid=(n, steps),
        in_specs=[pl.BlockSpec((1, dil, span, QKV_COLS), lambda b, k: (b, 0, k, 0)),
                  pl.BlockSpec((1, dil, STRIDES, QKV_COLS),
                               lambda b, k: (b, 0, jnp.maximum(k * chunks - 1, 0), 0)),
                  pl.BlockSpec((None, 2 * STRIDES, HEAD_LANES), lambda b, k: (g, 0, 0))],
        out_specs=[out_spec, out_spec],
        out_shape=[out_sds, out_sds],
        compiler_params=_params("parallel", "parallel"),
        name="attn_d%d" % dil,
    )(qkv, qkv, bands)


def _memkv_body(mem_ref, g_ref, w_ref, o_ref):
    o_ref[...] = _dot(_rms(mem_ref[...], g_ref[...]).astype(BF16), w_ref[...])


def _memkv(mem, g, w):
    m = mem.shape[0]
    row = pl.BlockSpec((MEM_LEN, D_MODEL), lambda i: (i, 0))
    return pl.pallas_call(
        _memkv_body,
        grid=(m // MEM_LEN,),
        in_specs=[row, _resident((1, D_MODEL)), _resident((D_MODEL, 2 * MEM_WIDTH))],
        out_specs=pl.BlockSpec((MEM_LEN, 2 * MEM_WIDTH), lambda i: (i, 0)),
        out_shape=jax.ShapeDtypeStruct((m, 2 * MEM_WIDTH), F32),
        compiler_params=_params("parallel"),
        name="memkv",
    )(mem, g, w)


def _merge_math(a, pooled, c, x, u, win_ref, wpool_ref, scale_ref, woa_ref, wob_ref,
                woc_ref, wout_ref):
    mixed = [_dot(pooled[gi].astype(BF16), wpool_ref[gi]) for gi in range(len(POOL_WINDOWS))]
    b = jnp.concatenate(mixed, axis=1) * scale_ref[...]
    m = None
    for k, (branch, wo_ref) in enumerate(((a, woa_ref), (b, wob_ref), (c, woc_ref))):
        lo = GATE_OFF + k * D_MODEL
        gate = jax.nn.sigmoid(_dot(u, win_ref[:, lo:lo + D_MODEL]))
        term = gate * _dot(branch.astype(BF16), wo_ref[...])
        m = term if m is None else m + term
    return x + _dot(m.astype(BF16), wout_ref[...])


def _mix_body(o0, o1, o2, l0, l1, l2, z_ref, halo_ref, qm_ref, mkv_ref, x_ref, u_ref, *rest,
              tile_in_seq):
    merge_refs, out_ref = rest[:-1], rest[-1]
    j = tile_in_seq
    tm = x_ref.shape[0]
    unslab = lambda ref: jnp.concatenate([ref[c] for c in range(ref.shape[0])], axis=1)
    lses = [unslab(l0), unslab(l1), unslab(l2)]
    mx = jnp.maximum(jnp.maximum(lses[0], lses[1]), lses[2])
    es = [jnp.exp(l - mx) for l in lses]
    a = ((es[0] * unslab(o0) + es[1] * unslab(o1) + es[2] * unslab(o2))
         / (es[0] + es[1] + es[2]))
    z = z_ref[...]
    halo = jnp.where(j == 0, 0.0, halo_ref[...])
    zc = jnp.concatenate([halo, z], axis=0)
    pos = j * tm + lax.broadcasted_iota(jnp.int32, (tm, 1), 0)
    pooled = []
    for gi, kw in enumerate(POOL_WINDOWS):
        cs = slice(gi * POOL_GROUP, (gi + 1) * POOL_GROUP)
        run = zc[:, cs]
        width = 1
        while width < kw:
            run = run[width:] + run[:-width]
            width *= 2
        first = POOL_HALO - (kw - 1)
        cnt = jnp.minimum(kw, pos + 1).astype(F32)
        pooled.append(run[first:first + tm] / cnt - z[:, cs])
    mkv = mkv_ref[...].astype(BF16)
    qm = qm_ref[...]
    cs_out = []
    for h in range(MEM_HEADS):
        hs = slice(h * MEM_HEAD_DIM, (h + 1) * MEM_HEAD_DIM)
        vs = slice(MEM_WIDTH + h * MEM_HEAD_DIM, MEM_WIDTH + (h + 1) * MEM_HEAD_DIM)
        s = _dot_t(qm[:, hs], mkv[:, hs]) * (1.0 / math.sqrt(MEM_HEAD_DIM))
        mm = jnp.max(s, axis=-1, keepdims=True)
        p = jnp.exp(s - mm)
        l = jnp.sum(p, axis=-1, keepdims=True)
        cs_out.append(_dot(p.astype(BF16), mkv[:, vs]) / l)
    c = jnp.concatenate(cs_out, axis=1)
    out_ref[...] = _merge_math(a, pooled, c, x_ref[...], u_ref[...], *merge_refs)


def _merge_weight_specs():
    return [_resident((D_MODEL, IN_COLS)),
            _resident((len(POOL_WINDOWS), POOL_GROUP, POOL_GROUP)), _resident((1, POOL_WIDTH)),
            _resident((GROUP_COLS, D_MODEL)), _resident((POOL_WIDTH, D_MODEL)),
            _resident((MEM_WIDTH, D_MODEL)), _resident((D_MODEL, D_MODEL))]


N_MERGE_WEIGHTS = 7


def _mix_with_tail_body(*refs, tiles_per_seq, steps):
    n_tile_in = 12
    tile_in, merge_refs = refs[:n_tile_in], refs[n_tile_in:n_tile_in + N_MERGE_WEIGHTS]
    (a_ref, pooled_ref, c_ref, xs_ref, us_ref,
     out_ref, tail_out) = refs[n_tile_in + N_MERGE_WEIGHTS:]
    t = pl.program_id(0)

    @pl.when(t < steps)
    def _():
        _mix_body(*tile_in, *merge_refs, out_ref, tile_in_seq=lax.rem(t, tiles_per_seq))

    @pl.when(t == steps)
    def _():
        pooled_all = pooled_ref[...]
        pooled = [pooled_all[:, gi * POOL_GROUP:(gi + 1) * POOL_GROUP]
                  for gi in range(len(POOL_WINDOWS))]
        tail_out[...] = _merge_math(a_ref[...], pooled, c_ref[...], xs_ref[...], us_ref[...],
                                    *merge_refs)


def _mix(os, lses, z, qm, mkv, x, u, weights, n, s, tail):
    tm = TOKEN_TILE
    tj = s // tm
    steps = n * tj
    last = steps - 1
    halo_per_tile = tm // POOL_HALO
    tile = lambda t: jnp.minimum(t, last)

    def tok(cols):
        return pl.BlockSpec((tm, cols), lambda t: (tile(t), 0))

    halo = pl.BlockSpec((POOL_HALO, POOL_WIDTH),
                        lambda t: (jnp.maximum(tile(t) * halo_per_tile - 1, 0), 0))
    slab = pl.BlockSpec((GROUP_COLS // LANES, tm, LANES), lambda t: (0, tile(t), 0))
    weight_specs = _merge_weight_specs()
    assert len(weight_specs) == N_MERGE_WEIGHTS
    in_specs = ([slab] * 6 + [tok(POOL_WIDTH), halo, tok(MEM_WIDTH),
                pl.BlockSpec((MEM_LEN, 2 * MEM_WIDTH), lambda t: (tile(t) // tj, 0)),
                tok(D_MODEL), tok(D_MODEL)] + weight_specs + [_resident(t.shape) for t in tail])
    xs = tail[3]
    return pl.pallas_call(
        functools.partial(_mix_with_tail_body, tiles_per_seq=tj, steps=steps),
        grid=(steps + 1,),
        in_specs=in_specs,
        out_specs=[tok(D_MODEL), pl.BlockSpec(xs.shape, lambda t: (0, 0))],
        out_shape=[jax.ShapeDtypeStruct((n * s, D_MODEL), F32),
                   jax.ShapeDtypeStruct(xs.shape, F32)],
        compiler_params=_params("arbitrary"),
        name="mix",
    )(*os, *lses, z, z, qm, mkv, x, u, *weights, *tail)


N_SAMPLE_INPUTS = 14


def _sample_branch_math(req, qkv0, qkv1, qkv2, c0, c1, c2, b0, b1, b2, bias0_ref, z_ref, st_ref,
                        qm_ref, cm_ref, a_ref, pooled_ref, c_ref):
    row = pl.ds(req, 1)
    eye = (lax.broadcasted_iota(jnp.int32, (HEAD_DIM, HEAD_DIM), 0)
           == lax.broadcasted_iota(jnp.int32, (HEAD_DIM, HEAD_DIM), 1))
    cube = (HEADS, HEAD_DIM, HEAD_DIM)

    def heads_of(vec, lo):
        return jnp.stack([vec[:, lo + h * HEAD_DIM:lo + (h + 1) * HEAD_DIM]
                          for h in range(HEADS)], axis=0)

    outs, lses = [], []
    for g, (qkv_ref, cache_ref, bias_ref) in enumerate(
            ((qkv0, c0, b0), (qkv1, c1, b1), (qkv2, c2, b2))):
        qkv = qkv_ref[row, :]
        q, kn, vn = heads_of(qkv, 0), heads_of(qkv, GROUP_COLS), heads_of(qkv, 2 * GROUP_COLS)
        q_col = jnp.sum(jnp.where(eye, jnp.broadcast_to(q, cube), 0.0), axis=2, keepdims=True)
        s = jnp.sum(cache_ref[0, 0] * q_col, axis=1, keepdims=True) + bias_ref[...]
        sn = jnp.sum(kn * q, axis=2, keepdims=True) + bias0_ref[g]
        m = jnp.maximum(jnp.max(s, axis=2, keepdims=True), sn)
        p = jnp.exp(s - m)
        pn = jnp.exp(sn - m)
        l = jnp.sum(p, axis=2, keepdims=True) + pn
        pv = jnp.sum(cache_ref[0, 1] * p, axis=2, keepdims=True)
        pv_row = jnp.sum(jnp.where(eye, jnp.broadcast_to(pv, cube), 0.0), axis=1, keepdims=True)
        outs.append((pv_row + pn * vn) / l)
        lses.append(m + jnp.log(l))
    mx = jnp.maximum(jnp.maximum(lses[0], lses[1]), lses[2])
    es = [jnp.exp(lse - mx) for lse in lses]
    a = (es[0] * outs[0] + es[1] * outs[1] + es[2] * outs[2]) / (es[0] + es[1] + es[2])
    a_ref[row, :] = jnp.concatenate([a[h] for h in range(HEADS)], axis=1)

    zn = z_ref[row, :]
    st = st_ref[:, row, :]
    pooled = []
    for gi, kw in enumerate(POOL_WINDOWS):
        cs = slice(gi * POOL_GROUP, (gi + 1) * POOL_GROUP)
        tot = jnp.sum(st[POOL_STATE - (kw - 1):, :, cs], axis=0) + zn[:, cs]
        pooled.append(tot / float(min(kw, PAST_LEN + 1)) - zn[:, cs])
    pooled_ref[row, :] = jnp.concatenate(pooled, axis=1)

    qm_row = qm_ref[row, :]
    qm = jnp.concatenate([qm_row[:, h * MEM_HEAD_DIM:(h + 1) * MEM_HEAD_DIM]
                          for h in range(MEM_HEADS)], axis=0)
    km = cm_ref[0, :, 0]
    vm = cm_ref[0, :, 1]
    s = jnp.sum(km * qm[None], axis=-1, keepdims=True) * (1.0 / math.sqrt(MEM_HEAD_DIM))
    m = jnp.max(s, axis=0)
    p = jnp.exp(s - m[None])
    c = jnp.sum(p * vm, axis=0) / jnp.sum(p, axis=0)
    c_ref[row, :] = jnp.concatenate([c[h:h + 1] for h in range(MEM_HEADS)], axis=1)


def _sample_branch_specs(operands, request_of):
    assert len(operands) == N_SAMPLE_INPUTS
    nb = operands[0].shape[0]

    def per_request(t):
        zeros = (0,) * (t.ndim - 1)
        return pl.BlockSpec((1,) + t.shape[1:], lambda *g: (request_of(*g),) + zeros)

    def whole(shape):
        zeros = (0,) * len(shape)
        return pl.BlockSpec(shape, lambda *g: zeros)

    streamed = (3, 4, 5, 13)
    in_specs = [per_request(t) if k in streamed else whole(t.shape)
                for k, t in enumerate(operands)]
    widths = (GROUP_COLS, POOL_WIDTH, MEM_WIDTH)
    return (in_specs, [whole((nb, w)) for w in widths],
            [jax.ShapeDtypeStruct((nb, w), F32) for w in widths])


def _rel_bucket(n):
    max_exact = N_BUCKETS // 2
    nf = jnp.maximum(n, 1).astype(F32)
    large = max_exact + (jnp.log(nf / max_exact) / math.log(MAX_DISTANCE / max_exact)
                         * (N_BUCKETS - max_exact)).astype(jnp.int32)
    large = jnp.minimum(large, N_BUCKETS - 1)
    return jnp.where(n < max_exact, n, large)


def _stride_bias(rel_bias, g, dil):
    j = jnp.arange(STRIDES + 1, dtype=jnp.int32)
    return rel_bias[_rel_bucket(j * dil)][:, g * HEADS:(g + 1) * HEADS].astype(F32)


def _band_row(bias_j):
    row = jnp.concatenate([bias_j[::-1], jnp.full((STRIDES - 1, HEADS), NEG_INF, F32)], axis=0)
    return row.T.reshape(HEADS, 1, 2 * STRIDES)


def _cache_bias(bias_j, dil):
    on_grid = bias_j[STRIDES:0:-1].T
    full = jnp.full((HEADS, STRIDES, dil), NEG_INF, F32).at[:, :, 0].set(on_grid)
    return full.reshape(HEADS, 1, STRIDES * dil)


def kernel(x_prompt, x_sample, cache_win0_kv, cache_win1_kv, cache_win2_kv, state_pool, cache_mem_kv, mem_prompt, rel_bias, g_ffn1, w1_gate, w1_up, w1_down, g_mix, w_in, w_pool, pool_scale, g_mem, w_mem_kv, w_oa, w_ob, w_oc, w_out, g_ffn2, w2_gate, w2_up, w2_down, g_final):
    n, s, _ = x_prompt.shape
    nb = x_sample.shape[0]
    depth = g_ffn1.shape[0]
    win_caches = (cache_win0_kv, cache_win1_kv, cache_win2_kv)
    bias_js = [_stride_bias(rel_bias, g, dil) for g, (_, dil) in enumerate(DIL_GROUPS)]
    bands = _bands(jnp.stack([_band_row(b) for b in bias_js]))
    bias_cache = [_cache_bias(b, dil) for b, (_, dil) in zip(bias_js, DIL_GROUPS)]
    bias_new = jnp.stack([b[0] for b in bias_js]).reshape(N_GROUPS, HEADS, 1, 1)
    gfin = g_final.reshape(1, D_MODEL)

    xp = x_prompt.reshape(n * s, D_MODEL)
    xs = x_sample.reshape(nb, D_MODEL)
    st_p = [[] for _ in range(5)]
    st_s = [[] for _ in range(4)]
    for l in range(depth):
        last = l == depth - 1
        bf = lambda w: w[l].astype(BF16)
        vec = lambda v: v[l].reshape(1, -1)
        w1 = (bf(w1_gate), bf(w1_up), bf(w1_down))
        q_scale = 1.0 / math.sqrt(HEAD_DIM)

        later = [w2_gate[l], w2_up[l], w2_down[l], w_in[l], w_oa[l], w_ob[l], w_oc[l], w_out[l],
                 w_mem_kv[l], w_pool[l].reshape(len(POOL_WINDOWS) * POOL_GROUP, POOL_GROUP)]
        xp, later, xs = _ffn(xp, vec(g_ffn1), *w1, casts=later, tail=xs)
        w2, (win_l, woa_l, wob_l, woc_l, wout_l, wmem_l, wpool_l) = later[0:3], later[3:]
        merge_w = (win_l, wpool_l.reshape(w_pool.shape[1:]), vec(pool_scale),
                   woa_l, wob_l, woc_l, wout_l)

        res = _inproj(xs, vec(g_mix), win_l, 1, nb, None, [1] * N_GROUPS, F32, q_scale)
        s_qkvs, s_z, s_qm, s_u = [t.reshape(nb, QKV_COLS) for t in res[0:3]], *res[3:6]
        rider = (s_qkvs + [jnp.transpose(cw[l], (0, 2, 3, 4, 1)) for cw in win_caches]
                 + bias_cache + [bias_new, s_z, jnp.transpose(state_pool[l], (1, 0, 2)), s_qm,
                                 cache_mem_kv[l]])
        res = _inproj(xp, vec(g_mix), win_l, n, s, [min(w, s) for w, _ in DIL_GROUPS],
                      [d for _, d in DIL_GROUPS], BF16, q_scale * LOG2E, rider)
        qkvs, kvwins, z, qm, u = res[0:3], res[3:6], res[6], res[7], res[8]
        s_branches = res[9:12]
        os, lses = [], []
        for g, (_, dil) in enumerate(DIL_GROUPS):
            o, lse = _attn(qkvs[g], bands, g, n, s, dil)
            os.append(o)
            lses.append(lse)
        mkv = _memkv(mem_prompt.reshape(n * MEM_LEN, D_MODEL), vec(g_mem), wmem_l)
        xp, xs = _mix(os, lses, z, qm, mkv, xp, u, merge_w, n, s, tail=(*s_branches, xs, s_u))
        xp, xs = _ffn(xp, vec(g_ffn2), *w2, g_final=gfin if last else None, tail=xs)
        for g, (win, _) in enumerate(DIL_GROUPS):
            kv_t = kvwins[g].reshape(n, 2, HEADS, HEAD_DIM, min(win, s))
            st_p[g].append(jnp.transpose(kv_t, (0, 4, 1, 2, 3)))
        st_p[3].append(z.reshape(n, s, POOL_WIDTH)[:, s - POOL_STATE:])
        st_p[4].append(mkv.reshape(n, MEM_LEN, 2, MEM_HEADS, MEM_HEAD_DIM))

        for g in range(N_GROUPS):
            st_s[g].append(s_qkvs[g][:, GROUP_COLS:].reshape(nb, 1, 2, HEADS, HEAD_DIM))
        st_s[3].append(s_z.reshape(nb, 1, POOL_WIDTH))

    y_prompt = xp.reshape(n, s, D_MODEL)
    y_sample = xs.reshape(nb, 1, D_MODEL)
    stack = lambda ts: jnp.stack(ts, axis=0)
    return (y_prompt, y_sample, stack(st_p[0]), stack(st_p[1]), stack(st_p[2]), stack(st_p[3]),
            stack(st_p[4]), stack(st_s[0]), stack(st_s[1]), stack(st_s[2]), stack(st_s[3]))
```

```python
import functools
import math

import jax
import jax.numpy as jnp
from jax import lax
from jax.experimental import pallas as pl
from jax.experimental.pallas import tpu as pltpu

F32 = jnp.float32
BF16 = jnp.bfloat16

D_MODEL = 1024
D_FF = 2816
LANES = 128
HEAD_DIM = 64
HEADS = 4
DIL_GROUPS = ((128, 1), (512, 4), (2048, 16))
N_GROUPS = 3
GROUP_COLS = HEADS * HEAD_DIM
QKV_COLS = 3 * GROUP_COLS
STRIDES = 128
POOL_WINDOWS = (2, 4, 8, 16)
POOL_GROUP = 128
POOL_WIDTH = 512
POOL_STATE = 15
POOL_HALO = 16
MEM_LEN = 256
MEM_HEADS = 4
MEM_HEAD_DIM = 128
MEM_WIDTH = 512
N_BUCKETS = 32
MAX_DISTANCE = 2048
N_BRANCH = 3
EPS = 1e-6
NEG_INF = -1e30
LOG2E = math.log2(math.e)
LN2 = math.log(2.0)
PAST_LEN = 8192

V7X_VMEM_LIMIT_BYTES = 56 * 1024 * 1024
TOKEN_TILE = 512


def _params(*sem):
    return pltpu.CompilerParams(dimension_semantics=sem,
                                vmem_limit_bytes=V7X_VMEM_LIMIT_BYTES)


def _resident(shape):
    zeros = (0,) * len(shape)
    return pl.BlockSpec(shape, lambda *_: zeros, pipeline_mode=pl.Buffered(1))


def _rms(x, g):
    return x * lax.rsqrt(jnp.mean(x * x, axis=-1, keepdims=True) + EPS) * g


def _dot(a, b):
    return jnp.dot(a, b, preferred_element_type=F32)


def _dot_t(a, b):
    return lax.dot_general(a, b, (((1,), (1,)), ((), ())), preferred_element_type=F32)


BF16_SUBLANES = 16


def _ffn_math(x, g_ref, wg_ref, wu_ref, wd_ref, gf_ref):
    h = _rms(x, g_ref[...]).astype(BF16)
    a = _dot(h, wg_ref[...])
    b = _dot(h, wu_ref[...])
    act = (a * jax.nn.sigmoid(a) * b).astype(BF16)
    y = x + 0.5 * _dot(act, wd_ref[...])
    return y if gf_ref is None else _rms(y, gf_ref[...])


def _ffn_body(*refs, final, n_casts, steps, has_tail):
    refs = iter(refs)
    x_ref = next(refs)
    weights = [next(refs) for _ in range(4)] + [next(refs) if final else None]
    cast_in = [next(refs) for _ in range(n_casts)]
    tail_in = next(refs) if has_tail else None
    o_ref = next(refs)
    cast_out = [next(refs) for _ in range(n_casts)]

    def tile():
        o_ref[...] = _ffn_math(x_ref[...], *weights)
        for src, dst in zip(cast_in, cast_out):
            dst[...] = src[...].astype(BF16)

    if not has_tail:
        tile()
        return
    tail_out = next(refs)
    pl.when(pl.program_id(0) < steps)(tile)

    @pl.when(pl.program_id(0) == steps)
    def _():
        tail_out[...] = _ffn_math(tail_in[...], *weights)


def _ffn(x, g, wg, wu, wd, g_final=None, casts=(), tail=None):
    m = x.shape[0]
    tm = min(TOKEN_TILE, m)
    steps = m // tm
    final = g_final is not None
    has_tail = tail is not None
    last = steps - 1
    row = pl.BlockSpec((tm, D_MODEL), lambda i: (jnp.minimum(i, last), 0))
    in_specs = [row, _resident((1, D_MODEL)), _resident((D_MODEL, D_FF)),
                _resident((D_MODEL, D_FF)), _resident((D_FF, D_MODEL))]
    args = [x, g, wg, wu, wd]
    if final:
        in_specs.append(_resident((1, D_MODEL)))
        args.append(g_final)
    cast_specs = []
    for w in casts:
        blocks = math.gcd(steps, w.shape[0] // BF16_SUBLANES)
        per = steps // blocks
        cast_specs.append(pl.BlockSpec((w.shape[0] // blocks, w.shape[1]),
                                       lambda i, per=per: (jnp.minimum(i, last) // per, 0)))
    in_specs, out_specs = in_specs + cast_specs, [row] + cast_specs
    out_shape = ([jax.ShapeDtypeStruct((m, D_MODEL), F32)]
                 + [jax.ShapeDtypeStruct(w.shape, BF16) for w in casts])
    if has_tail:
        in_specs.append(_resident(tail.shape))
        out_specs.append(pl.BlockSpec(tail.shape, lambda i: (0, 0)))
        out_shape.append(jax.ShapeDtypeStruct(tail.shape, F32))
    out = pl.pallas_call(
        functools.partial(_ffn_body, final=final, n_casts=len(casts), steps=steps,
                          has_tail=has_tail),
        grid=(steps + has_tail,),
        in_specs=in_specs,
        out_specs=out_specs,
        out_shape=out_shape,
        compiler_params=_params("arbitrary" if casts or has_tail else "parallel"),
        name="ffn_final" if final else "ffn",
    )(*args, *casts, *([tail] if has_tail else []))
    if not casts and not has_tail:
        return out[0]
    n_casts = len(casts)
    return ((out[0],) + ((out[1:1 + n_casts],) if casts else ())
            + ((out[1 + n_casts],) if has_tail else ()))


Z_OFF = N_GROUPS * QKV_COLS
QM_OFF = Z_OFF + POOL_WIDTH
GATE_OFF = QM_OFF + MEM_WIDTH
IN_COLS = GATE_OFF + N_BRANCH * D_MODEL


def _inproj_body(x_ref, g_ref, w_ref, *rest, q_scale, with_windows, under_matmul=None):
    qkv_refs = rest[0:N_GROUPS]
    kv_refs = rest[N_GROUPS:2 * N_GROUPS] if with_windows else (None,) * N_GROUPS
    z_ref, qm_ref, u_ref, p_scr = rest[-4:]
    u = _rms(x_ref[...], g_ref[...]).astype(BF16)
    u_ref[...] = u
    tm = u.shape[0]
    width = N_GROUPS * GROUP_COLS
    p = _dot(u, w_ref[:, 0:3 * width])
    if under_matmul is not None:
        under_matmul()
    for g, (qkv_ref, kv_ref) in enumerate(zip(qkv_refs, kv_refs)):
        dil, per_class = qkv_ref.shape[1], qkv_ref.shape[2]
        q, k, v = [p[:, t * width + g * GROUP_COLS:t * width + (g + 1) * GROUP_COLS]
                   for t in range(3)]
        qkv = jnp.concatenate([q * q_scale, k, v], axis=1)
        if dil == 1:
            qkv_ref[0, 0] = qkv.astype(qkv_ref.dtype)
        else:
            for c in range(QKV_COLS // LANES):
                p_scr[c] = qkv[:, c * LANES:(c + 1) * LANES]
            for r in range(dil):
                blk = jnp.concatenate([p_scr[c, pl.ds(r, per_class, stride=dil), :]
                                       for c in range(QKV_COLS // LANES)], axis=1)
                qkv_ref[0, r] = blk.astype(qkv_ref.dtype)
        if with_windows:
            rows = kv_ref.shape[2]
            kv_ref[0] = jnp.concatenate([k[tm - rows:], v[tm - rows:]], axis=1).T
    z_ref[...] = _dot(u, w_ref[:, Z_OFF:QM_OFF])
    qm_ref[...] = _dot(u, w_ref[:, QM_OFF:GATE_OFF]).astype(qm_ref.dtype)


def _inproj(x, g, w, n, s, windows, dils, q_dtype, q_scale, rider=()):
    tm = min(TOKEN_TILE, s)
    tj = s // tm

    def tok(cols):
        return pl.BlockSpec((tm, cols), lambda b, j: (b * tj + j, 0))

    qkv_specs = [pl.BlockSpec((1, d, tm // d, QKV_COLS), lambda b, j: (b, 0, j, 0)) for d in dils]
    qkv_shapes = [jax.ShapeDtypeStruct((n, d, s // d, QKV_COLS), q_dtype) for d in dils]

    kv_specs, kv_shapes = [], []
    for win in windows or ():
        rows = min(win, tm)
        assert tm % rows == 0 and win % rows == 0
        first = (s - win) // tm
        if win >= tm:
            spec = pl.BlockSpec((1, 2 * GROUP_COLS, rows),
                                lambda b, j, first=first: (b, 0, jnp.maximum(j - first, 0)))
        else:
            spec = pl.BlockSpec((1, 2 * GROUP_COLS, rows), lambda b, j: (b, 0, 0))
        kv_specs.append(spec)
        kv_shapes.append(jax.ShapeDtypeStruct((n, 2 * GROUP_COLS, win), F32))
    m = n * s
    out_shape = (qkv_shapes + kv_shapes + [
        jax.ShapeDtypeStruct((m, POOL_WIDTH), F32),
        jax.ShapeDtypeStruct((m, MEM_WIDTH), q_dtype),
        jax.ShapeDtypeStruct((m, D_MODEL), BF16)])
    out_specs = qkv_specs + kv_specs + [tok(POOL_WIDTH), tok(MEM_WIDTH), tok(D_MODEL)]
    in_specs = [tok(D_MODEL), _resident((1, D_MODEL)),
                pl.BlockSpec((D_MODEL, GATE_OFF), lambda b, j: (0, 0),
                             pipeline_mode=pl.Buffered(1))]
    body = functools.partial(_inproj_body, q_scale=q_scale, with_windows=bool(windows))
    n_own_out = len(out_specs)
    if rider:
        assert rider[0].shape[0] == n * tj
        r_in, r_out, r_shapes = _sample_branch_specs(rider, lambda b, j: b * tj + j)
        in_specs, out_specs, out_shape = in_specs + r_in, out_specs + r_out, out_shape + r_shapes

        def body(*refs, own=body):
            ins, r_ins = refs[:3], refs[3:3 + N_SAMPLE_INPUTS]
            outs, scratch = refs[3 + N_SAMPLE_INPUTS:-1], refs[-1:]
            own(*ins, *outs[:n_own_out], *scratch, under_matmul=lambda: _sample_branch_math(
                pl.program_id(0) * tj + pl.program_id(1), *r_ins, *outs[n_own_out:]))

    return pl.pallas_call(
        body,
        grid=(n, tj),
        in_specs=in_specs,
        out_specs=out_specs,
        out_shape=out_shape,
        scratch_shapes=[pltpu.VMEM((QKV_COLS // LANES, tm, LANES), F32)],
        compiler_params=_params("arbitrary", "arbitrary"),
        name="inproj",
    )(x, g, w, *rider)


ATTN_CHUNKS = 32
HEAD_LANES = HEADS * STRIDES


def _band_body(row_ref, band_ref):
    for g in range(N_GROUPS):
        for h in range(HEADS):
            rows = jnp.broadcast_to(row_ref[g, h] * LOG2E, (STRIDES, 2 * STRIDES))
            band = pltpu.roll(rows, 0, 1, stride=1, stride_axis=0).T
            band_ref[g, :, h * STRIDES:(h + 1) * STRIDES] = band


def _bands(rows):
    return pl.pallas_call(
        _band_body,
        grid=(1,),
        in_specs=[_resident(rows.shape)],
        out_specs=pl.BlockSpec((N_GROUPS, 2 * STRIDES, HEAD_LANES), lambda i: (0, 0, 0)),
        out_shape=jax.ShapeDtypeStruct((N_GROUPS, 2 * STRIDES, HEAD_LANES), F32),
        compiler_params=_params("arbitrary"),
        name="bands",
    )(rows)


def _attn_body(qkv_ref, prev_ref, band_ref, o_ref, lse_ref, *, dil, chunks):
    first = pl.program_id(1) == 0
    lane_head = lax.broadcasted_iota(jnp.int32, (1, GROUP_COLS), 1) // HEAD_DIM

    def keys_of(r):
        return jnp.concatenate([prev_ref[0, r, :, GROUP_COLS:2 * GROUP_COLS],
                                qkv_ref[0, r, :, GROUP_COLS:2 * GROUP_COLS]], axis=0)

    def scores(k_all, r, i):
        q = qkv_ref[0, r, i * STRIDES:(i + 1) * STRIDES, 0:GROUP_COLS]
        qm = jnp.concatenate(
            [jnp.where(lane_head == h, q, jnp.zeros_like(q)) for h in range(HEADS)], axis=0)
        return _dot_t(k_all[i * STRIDES:(i + 2) * STRIDES], qm)

    ones_row = jnp.where(
        lax.broadcasted_iota(jnp.int32, (BF16_SUBLANES, 2 * STRIDES), 0) == 0, 1.0, 0.0
    ).astype(BF16)

    order = [(r, i) for r in range(dil) for i in range(chunks)]
    k_all = keys_of(0)
    st_next = scores(k_all, 0, 0)
    for idx, (r, i) in enumerate(order):
        st = st_next
        if i == 0:
            v_all = jnp.concatenate([prev_ref[0, r, :, 2 * GROUP_COLS:],
                                     qkv_ref[0, r, :, 2 * GROUP_COLS:]], axis=0)
            vt_all = v_all.astype(F32).T.astype(BF16)
        if idx + 1 < len(order):
            r2, i2 = order[idx + 1]
            if i2 == 0:
                k_all = keys_of(r2)
            st_next = scores(k_all, r2, i2)
        vt2 = vt_all[:, i * STRIDES:(i + 2) * STRIDES]
        o_parts, lse_parts = [], []
        for h in range(HEADS):
            hl = slice(h * STRIDES, (h + 1) * STRIDES)
            s_h = st[:, hl] + band_ref[:, hl]
            if i == 0:
                s_h = jnp.concatenate(
                    [jnp.where(first, NEG_INF, s_h[:STRIDES]), s_h[STRIDES:]], axis=0)
            m = jnp.max(s_h, axis=0, keepdims=True)
            e = jnp.exp2(s_h - m).astype(BF16)
            ot = _dot(jnp.concatenate([vt2[h * HEAD_DIM:(h + 1) * HEAD_DIM], ones_row], axis=0), e)
            l = ot[HEAD_DIM:HEAD_DIM + 1]
            o_parts.append(ot[:HEAD_DIM] * (1.0 / l))
            lse_parts.append(jnp.broadcast_to(m * LN2 + jnp.log(l), (HEAD_DIM, STRIDES)))
        rows = pl.ds(i * STRIDES * dil + r, STRIDES, stride=dil)
        o_rows = jnp.concatenate(o_parts, axis=0).T
        lse_rows = jnp.concatenate(lse_parts, axis=0).T
        for c in range(GROUP_COLS // LANES):
            o_ref[c, rows, :] = o_rows[:, c * LANES:(c + 1) * LANES]
            lse_ref[c, rows, :] = lse_rows[:, c * LANES:(c + 1) * LANES]


def _attn(qkv, bands, g, n, s, dil):
    chunks = ATTN_CHUNKS // dil
    span = chunks * STRIDES
    steps = s // (span * dil)
    slabs = GROUP_COLS // LANES
    out_spec = pl.BlockSpec((slabs, span * dil, LANES), lambda b, k: (0, b * steps + k, 0))
    out_sds = jax.ShapeDtypeStruct((slabs, n * s, LANES), F32)
    return pl.pallas_call(
        functools.partial(_attn_body, dil=dil, chunks=chunks),
        grid=(n, steps),
        in_specs=[pl.BlockSpec((1, dil, span, QKV_COLS), lambda b, k: (b, 0, k, 0)),
                  pl.BlockSpec((1, dil, STRIDES, QKV_COLS),
                               lambda b, k: (b, 0, jnp.maximum(k * chunks - 1, 0), 0)),
                  pl.BlockSpec((None, 2 * STRIDES, HEAD_LANES), lambda b, k: (g, 0, 0))],
        out_specs=[out_spec, out_spec],
        out_shape=[out_sds, out_sds],
        compiler_params=_params("parallel", "parallel"),
        name="attn_d%d" % dil,
    )(qkv, qkv, bands)


def _memkv_body(mem_ref, g_ref, w_ref, o_ref):
    o_ref[...] = _dot(_rms(mem_ref[...], g_ref[...]).astype(BF16), w_ref[...])


def _memkv(mem, g, w):
    m = mem.shape[0]
    row = pl.BlockSpec((MEM_LEN, D_MODEL), lambda i: (i, 0))
    return pl.pallas_call(
        _memkv_body,
        grid=(m // MEM_LEN,),
        in_specs=[row, _resident((1, D_MODEL)), _resident((D_MODEL, 2 * MEM_WIDTH))],
        out_specs=pl.BlockSpec((MEM_LEN, 2 * MEM_WIDTH), lambda i: (i, 0)),
        out_shape=jax.ShapeDtypeStruct((m, 2 * MEM_WIDTH), F32),
        compiler_params=_params("parallel"),
        name="memkv",
    )(mem, g, w)


def _merge_math(a, pooled, c, x, u, win_ref, wpool_ref, scale_ref, woa_ref, wob_ref,
                woc_ref, wout_ref):
    mixed = [_dot(pooled[gi].astype(BF16), wpool_ref[gi]) for gi in range(len(POOL_WINDOWS))]
    b = jnp.concatenate(mixed, axis=1) * scale_ref[...]
    m = None
    for k, (branch, wo_ref) in enumerate(((a, woa_ref), (b, wob_ref), (c, woc_ref))):
        lo = GATE_OFF + k * D_MODEL
        gate = jax.nn.sigmoid(_dot(u, win_ref[:, lo:lo + D_MODEL]))
        term = gate * _dot(branch.astype(BF16), wo_ref[...])
        m = term if m is None else m + term
    return x + _dot(m.astype(BF16), wout_ref[...])


def _mix_body(o0, o1, o2, l0, l1, l2, z_ref, halo_ref, qm_ref, mkv_ref, x_ref, u_ref, *rest,
              tile_in_seq):
    merge_refs, out_ref = rest[:-1], rest[-1]
    j = tile_in_seq
    tm = x_ref.shape[0]
    unslab = lambda ref: jnp.concatenate([ref[c] for c in range(ref.shape[0])], axis=1)
    lses = [unslab(l0), unslab(l1), unslab(l2)]
    mx = jnp.maximum(jnp.maximum(lses[0], lses[1]), lses[2])
    es = [jnp.exp(l - mx) for l in lses]
    a = ((es[0] * unslab(o0) + es[1] * unslab(o1) + es[2] * unslab(o2))
         / (es[0] + es[1] + es[2]))
    z = z_ref[...]
    halo = jnp.where(j == 0, 0.0, halo_ref[...])
    zc = jnp.concatenate([halo, z], axis=0)
    pos = j * tm + lax.broadcasted_iota(jnp.int32, (tm, 1), 0)
    pooled = []
    for gi, kw in enumerate(POOL_WINDOWS):
        cs = slice(gi * POOL_GROUP, (gi + 1) * POOL_GROUP)
        run = zc[:, cs]
        width = 1
        while width < kw:
            run = run[width:] + run[:-width]
            width *= 2
        first = POOL_HALO - (kw - 1)
        cnt = jnp.minimum(kw, pos + 1).astype(F32)
        pooled.append(run[first:first + tm] / cnt - z[:, cs])
    mkv = mkv_ref[...].astype(BF16)
    qm = qm_ref[...]
    cs_out = []
    for h in range(MEM_HEADS):
        hs = slice(h * MEM_HEAD_DIM, (h + 1) * MEM_HEAD_DIM)
        vs = slice(MEM_WIDTH + h * MEM_HEAD_DIM, MEM_WIDTH + (h + 1) * MEM_HEAD_DIM)
        s = _dot_t(qm[:, hs], mkv[:, hs]) * (1.0 / math.sqrt(MEM_HEAD_DIM))
        mm = jnp.max(s, axis=-1, keepdims=True)
        p = jnp.exp(s - mm)
        l = jnp.sum(p, axis=-1, keepdims=True)
        cs_out.append(_dot(p.astype(BF16), mkv[:, vs]) / l)
    c = jnp.concatenate(cs_out, axis=1)
    out_ref[...] = _merge_math(a, pooled, c, x_ref[...], u_ref[...], *merge_refs)


def _merge_weight_specs():
    return [_resident((D_MODEL, IN_COLS)),
            _resident((len(POOL_WINDOWS), POOL_GROUP, POOL_GROUP)), _resident((1, POOL_WIDTH)),
            _resident((GROUP_COLS, D_MODEL)), _resident((POOL_WIDTH, D_MODEL)),
            _resident((MEM_WIDTH, D_MODEL)), _resident((D_MODEL, D_MODEL))]


N_MERGE_WEIGHTS = 7


def _mix_with_tail_body(*refs, tiles_per_seq, steps):
    n_tile_in = 12
    tile_in, merge_refs = refs[:n_tile_in], refs[n_tile_in:n_tile_in + N_MERGE_WEIGHTS]
    (a_ref, pooled_ref, c_ref, xs_ref, us_ref,
     out_ref, tail_out) = refs[n_tile_in + N_MERGE_WEIGHTS:]
    t = pl.program_id(0)

    @pl.when(t < steps)
    def _():
        _mix_body(*tile_in, *merge_refs, out_ref, tile_in_seq=lax.rem(t, tiles_per_seq))

    @pl.when(t == steps)
    def _():
        pooled_all = pooled_ref[...]
        pooled = [pooled_all[:, gi * POOL_GROUP:(gi + 1) * POOL_GROUP]
                  for gi in range(len(POOL_WINDOWS))]
        tail_out[...] = _merge_math(a_ref[...], pooled, c_ref[...], xs_ref[...], us_ref[...],
                                    *merge_refs)


def _mix(os, lses, z, qm, mkv, x, u, weights, n, s, tail):
    tm = TOKEN_TILE
    tj = s // tm
    steps = n * tj
    last = steps - 1
    halo_per_tile = tm // POOL_HALO
    tile = lambda t: jnp.minimum(t, last)

    def tok(cols):
        return pl.BlockSpec((tm, cols), lambda t: (tile(t), 0))

    halo = pl.BlockSpec((POOL_HALO, POOL_WIDTH),
                        lambda t: (jnp.maximum(tile(t) * halo_per_tile - 1, 0), 0))
    slab = pl.BlockSpec((GROUP_COLS // LANES, tm, LANES), lambda t: (0, tile(t), 0))
    weight_specs = _merge_weight_specs()
    assert len(weight_specs) == N_MERGE_WEIGHTS
    in_specs = ([slab] * 6 + [tok(POOL_WIDTH), halo, tok(MEM_WIDTH),
                pl.BlockSpec((MEM_LEN, 2 * MEM_WIDTH), lambda t: (tile(t) // tj, 0)),
                tok(D_MODEL), tok(D_MODEL)] + weight_specs + [_resident(t.shape) for t in tail])
    xs = tail[3]
    return pl.pallas_call(
        functools.partial(_mix_with_tail_body, tiles_per_seq=tj, steps=steps),
        grid=(steps + 1,),
        in_specs=in_specs,
        out_specs=[tok(D_MODEL), pl.BlockSpec(xs.shape, lambda t: (0, 0))],
        out_shape=[jax.ShapeDtypeStruct((n * s, D_MODEL), F32),
                   jax.ShapeDtypeStruct(xs.shape, F32)],
        compiler_params=_params("arbitrary"),
        name="mix",
    )(*os, *lses, z, z, qm, mkv, x, u, *weights, *tail)


N_SAMPLE_INPUTS = 14


def _sample_branch_math(req, qkv0, qkv1, qkv2, c0, c1, c2, b0, b1, b2, bias0_ref, z_ref, st_ref,
                        qm_ref, cm_ref, a_ref, pooled_ref, c_ref):
    row = pl.ds(req, 1)
    eye = (lax.broadcasted_iota(jnp.int32, (HEAD_DIM, HEAD_DIM), 0)
           == lax.broadcasted_iota(jnp.int32, (HEAD_DIM, HEAD_DIM), 1))
    cube = (HEADS, HEAD_DIM, HEAD_DIM)

    def heads_of(vec, lo):
        return jnp.stack([vec[:, lo + h * HEAD_DIM:lo + (h + 1) * HEAD_DIM]
                          for h in range(HEADS)], axis=0)

    outs, lses = [], []
    for g, (qkv_ref, cache_ref, bias_ref) in enumerate(
            ((qkv0, c0, b0), (qkv1, c1, b1), (qkv2, c2, b2))):
        qkv = qkv_ref[row, :]
        q, kn, vn = heads_of(qkv, 0), heads_of(qkv, GROUP_COLS), heads_of(qkv, 2 * GROUP_COLS)
        q_col = jnp.sum(jnp.where(eye, jnp.broadcast_to(q, cube), 0.0), axis=2, keepdims=True)
        s = jnp.sum(cache_ref[0, 0] * q_col, axis=1, keepdims=True) + bias_ref[...]
        sn = jnp.sum(kn * q, axis=2, keepdims=True) + bias0_ref[g]
        m = jnp.maximum(jnp.max(s, axis=2, keepdims=True), sn)
        p = jnp.exp(s - m)
        pn = jnp.exp(sn - m)
        l = jnp.sum(p, axis=2, keepdims=True) + pn
        pv = jnp.sum(cache_ref[0, 1] * p, axis=2, keepdims=True)
        pv_row = jnp.sum(jnp.where(eye, jnp.broadcast_to(pv, cube), 0.0), axis=1, keepdims=True)
        outs.append((pv_row + pn * vn) / l)
        lses.append(m + jnp.log(l))
    mx = jnp.maximum(jnp.maximum(lses[0], lses[1]), lses[2])
    es = [jnp.exp(lse - mx) for lse in lses]
    a = (es[0] * outs[0] + es[1] * outs[1] + es[2] * outs[2]) / (es[0] + es[1] + es[2])
    a_ref[row, :] = jnp.concatenate([a[h] for h in range(HEADS)], axis=1)

    zn = z_ref[row, :]
    st = st_ref[:, row, :]
    pooled = []
    for gi, kw in enumerate(POOL_WINDOWS):
        cs = slice(gi * POOL_GROUP, (gi + 1) * POOL_GROUP)
        tot = jnp.sum(st[POOL_STATE - (kw - 1):, :, cs], axis=0) + zn[:, cs]
        pooled.append(tot / float(min(kw, PAST_LEN + 1)) - zn[:, cs])
    pooled_ref[row, :] = jnp.concatenate(pooled, axis=1)

    qm_row = qm_ref[row, :]
    qm = jnp.concatenate([qm_row[:, h * MEM_HEAD_DIM:(h + 1) * MEM_HEAD_DIM]
                          for h in range(MEM_HEADS)], axis=0)
    km = cm_ref[0, :, 0]
    vm = cm_ref[0, :, 1]
    s = jnp.sum(km * qm[None], axis=-1, keepdims=True) * (1.0 / math.sqrt(MEM_HEAD_DIM))
    m = jnp.max(s, axis=0)
    p = jnp.exp(s - m[None])
    c = jnp.sum(p * vm, axis=0) / jnp.sum(p, axis=0)
    c_ref[row, :] = jnp.concatenate([c[h:h + 1] for h in range(MEM_HEADS)], axis=1)


def _sample_branch_specs(operands, request_of):
    assert len(operands) == N_SAMPLE_INPUTS
    nb = operands[0].shape[0]

    def per_request(t):
        zeros = (0,) * (t.ndim - 1)
        return pl.BlockSpec((1,) + t.shape[1:], lambda *g: (request_of(*g),) + zeros)

    def whole(shape):
        zeros = (0,) * len(shape)
        return pl.BlockSpec(shape, lambda *g: zeros)

    streamed = (3, 4, 5, 13)
    in_specs = [per_request(t) if k in streamed else whole(t.shape)
                for k, t in enumerate(operands)]
    widths = (GROUP_COLS, POOL_WIDTH, MEM_WIDTH)
    return (in_specs, [whole((nb, w)) for w in widths],
            [jax.ShapeDtypeStruct((nb, w), F32) for w in widths])


def _rel_bucket(n):
    max_exact = N_BUCKETS // 2
    nf = jnp.maximum(n, 1).astype(F32)
    large = max_exact + (jnp.log(nf / max_exact) / math.log(MAX_DISTANCE / max_exact)
                         * (N_BUCKETS - max_exact)).astype(jnp.int32)
    large = jnp.minimum(large, N_BUCKETS - 1)
    return jnp.where(n < max_exact, n, large)


def _stride_bias(rel_bias, g, dil):
    j = jnp.arange(STRIDES + 1, dtype=jnp.int32)
    return rel_bias[_rel_bucket(j * dil)][:, g * HEADS:(g + 1) * HEADS].astype(F32)


def _band_row(bias_j):
    row = jnp.concatenate([bias_j[::-1], jnp.full((STRIDES - 1, HEADS), NEG_INF, F32)], axis=0)
    return row.T.reshape(HEADS, 1, 2 * STRIDES)


def _cache_bias(bias_j, dil):
    on_grid = bias_j[STRIDES:0:-1].T
    full = jnp.full((HEADS, STRIDES, dil), NEG_INF, F32).at[:, :, 0].set(on_grid)
    return full.reshape(HEADS, 1, STRIDES * dil)


def kernel(x_prompt, x_sample, cache_win0_kv, cache_win1_kv, cache_win2_kv, state_pool, cache_mem_kv, mem_prompt, rel_bias, g_ffn1, w1_gate, w1_up, w1_down, g_mix, w_in, w_pool, pool_scale, g_mem, w_mem_kv, w_oa, w_ob, w_oc, w_out, g_ffn2, w2_gate, w2_up, w2_down, g_final):
    n, s, _ = x_prompt.shape
    nb = x_sample.shape[0]
    depth = g_ffn1.shape[0]
    win_caches = (cache_win0_kv, cache_win1_kv, cache_win2_kv)
    bias_js = [_stride_bias(rel_bias, g, dil) for g, (_, dil) in enumerate(DIL_GROUPS)]
    bands = _bands(jnp.stack([_band_row(b) for b in bias_js]))
    bias_cache = [_cache_bias(b, dil) for b, (_, dil) in zip(bias_js, DIL_GROUPS)]
    bias_new = jnp.stack([b[0] for b in bias_js]).reshape(N_GROUPS, HEADS, 1, 1)
    gfin = g_final.reshape(1, D_MODEL)

    xp = x_prompt.reshape(n * s, D_MODEL)
    xs = x_sample.reshape(nb, D_MODEL)
    st_p = [[] for _ in range(5)]
    st_s = [[] for _ in range(4)]
    for l in range(depth):
        last = l == depth - 1
        bf = lambda w: w[l].astype(BF16)
        vec = lambda v: v[l].reshape(1, -1)
        w1 = (bf(w1_gate), bf(w1_up), bf(w1_down))
        q_scale = 1.0 / math.sqrt(HEAD_DIM)

        later = [w2_gate[l], w2_up[l], w2_down[l], w_in[l], w_oa[l], w_ob[l], w_oc[l], w_out[l],
                 w_mem_kv[l], w_pool[l].reshape(len(POOL_WINDOWS) * POOL_GROUP, POOL_GROUP)]
        xp, later, xs = _ffn(xp, vec(g_ffn1), *w1, casts=later, tail=xs)
        w2, (win_l, woa_l, wob_l, woc_l, wout_l, wmem_l, wpool_l) = later[0:3], later[3:]
        merge_w = (win_l, wpool_l.reshape(w_pool.shape[1:]), vec(pool_scale),
                   woa_l, wob_l, woc_l, wout_l)

        res = _inproj(xs, vec(g_mix), win_l, 1, nb, None, [1] * N_GROUPS, F32, q_scale)
        s_qkvs, s_z, s_qm, s_u = [t.reshape(nb, QKV_COLS) for t in res[0:3]], *res[3:6]
        rider = (s_qkvs + [jnp.transpose(cw[l], (0, 2, 3, 4, 1)) for cw in win_caches]
                 + bias_cache + [bias_new, s_z, jnp.transpose(state_pool[l], (1, 0, 2)), s_qm,
                                 cache_mem_kv[l]])
        res = _inproj(xp, vec(g_mix), win_l, n, s, [min(w, s) for w, _ in DIL_GROUPS],
                      [d for _, d in DIL_GROUPS], BF16, q_scale * LOG2E, rider)
        qkvs, kvwins, z, qm, u = res[0:3], res[3:6], res[6], res[7], res[8]
        s_branches = res[9:12]
        os, lses = [], []
        for g, (_, dil) in enumerate(DIL_GROUPS):
            o, lse = _attn(qkvs[g], bands, g, n, s, dil)
            os.append(o)
            lses.append(lse)
        mkv = _memkv(mem_prompt.reshape(n * MEM_LEN, D_MODEL), vec(g_mem), wmem_l)
        xp, xs = _mix(os, lses, z, qm, mkv, xp, u, merge_w, n, s, tail=(*s_branches, xs, s_u))
        xp, xs = _ffn(xp, vec(g_ffn2), *w2, g_final=gfin if last else None, tail=xs)
        for g, (win, _) in enumerate(DIL_GROUPS):
            kv_t = kvwins[g].reshape(n, 2, HEADS, HEAD_DIM, min(win, s))
            st_p[g].append(jnp.transpose(kv_t, (0, 4, 1, 2, 3)))
        st_p[3].append(z.reshape(n, s, POOL_WIDTH)[:, s - POOL_STATE:])
        st_p[4].append(mkv.reshape(n, MEM_LEN, 2, MEM_HEADS, MEM_HEAD_DIM))

        for g in range(N_GROUPS):
            st_s[g].append(s_qkvs[g][:, GROUP_COLS:].reshape(nb, 1, 2, HEADS, HEAD_DIM))
        st_s[3].append(s_z.reshape(nb, 1, POOL_WIDTH))

    y_prompt = xp.reshape(n, s, D_MODEL)
    y_sample = xs.reshape(nb, 1, D_MODEL)
    stack = lambda ts: jnp.stack(ts, axis=0)
    return (y_prompt, y_sample, stack(st_p[0]), stack(st_p[1]), stack(st_p[2]), stack(st_p[3]),
            stack(st_p[4]), stack(st_s[0]), stack(st_s[1]), stack(st_s[2]), stack(st_s[3]))
```

```python
import functools
import math

import jax
import jax.numpy as jnp
from jax import lax
from jax.experimental import pallas as pl
from jax.experimental.pallas import tpu as pltpu

F32 = jnp.float32
BF16 = jnp.bfloat16

D_MODEL = 1024
D_FF = 2816
LANES = 128
HEAD_DIM = 64
HEADS = 4
DIL_GROUPS = ((128, 1), (512, 4), (2048, 16))
N_GROUPS = 3
GROUP_COLS = HEADS * HEAD_DIM
QKV_COLS = 3 * GROUP_COLS
STRIDES = 128
POOL_WINDOWS = (2, 4, 8, 16)
POOL_GROUP = 128
POOL_WIDTH = 512
POOL_STATE = 15
POOL_HALO = 16
MEM_LEN = 256
MEM_HEADS = 4
MEM_HEAD_DIM = 128
MEM_WIDTH = 512
N_BUCKETS = 32
MAX_DISTANCE = 2048
N_BRANCH = 3
EPS = 1e-6
NEG_INF = -1e30
LOG2E = math.log2(math.e)
LN2 = math.log(2.0)
PAST_LEN = 8192

V7X_VMEM_LIMIT_BYTES = 56 * 1024 * 1024
TOKEN_TILE = 512
FFN_TILE = 2 * TOKEN_TILE


def _params(*sem):
    return pltpu.CompilerParams(dimension_semantics=sem,
                                vmem_limit_bytes=V7X_VMEM_LIMIT_BYTES)


def _resident(shape):
    zeros = (0,) * len(shape)
    return pl.BlockSpec(shape, lambda *_: zeros, pipeline_mode=pl.Buffered(1))


def _rms(x, g):
    return x * lax.rsqrt(jnp.mean(x * x, axis=-1, keepdims=True) + EPS) * g


def _dot(a, b):
    return jnp.dot(a, b, preferred_element_type=F32)


def _dot_t(a, b):
    return lax.dot_general(a, b, (((1,), (1,)), ((), ())), preferred_element_type=F32)


BF16_SUBLANES = 16


def _ffn_math(x, g_ref, wg_ref, wu_ref, wd_ref, gf_ref):
    h = _rms(x, g_ref[...]).astype(BF16)
    a = _dot(h, wg_ref[...])
    b = _dot(h, wu_ref[...])
    act = (a * jax.nn.sigmoid(a) * b).astype(BF16)
    y = x + 0.5 * _dot(act, wd_ref[...])
    return y if gf_ref is None else _rms(y, gf_ref[...])


def _ffn_body(*refs, final, n_casts, steps, has_tail):
    refs = iter(refs)
    x_ref = next(refs)
    weights = [next(refs) for _ in range(4)] + [next(refs) if final else None]
    cast_in = [next(refs) for _ in range(n_casts)]
    tail_in = next(refs) if has_tail else None
    o_ref = next(refs)
    cast_out = [next(refs) for _ in range(n_casts)]

    def tile():
        for lo in range(0, x_ref.shape[0], TOKEN_TILE):
            rows = slice(lo, lo + TOKEN_TILE)
            o_ref[rows, :] = _ffn_math(x_ref[rows, :], *weights)
        for src, dst in zip(cast_in, cast_out):
            dst[...] = src[...].astype(BF16)

    if not has_tail:
        tile()
        return
    tail_out = next(refs)
    pl.when(pl.program_id(0) < steps)(tile)

    @pl.when(pl.program_id(0) == steps)
    def _():
        tail_out[...] = _ffn_math(tail_in[...], *weights)


def _ffn(x, g, wg, wu, wd, g_final=None, casts=(), tail=None):
    m = x.shape[0]
    tm = min(FFN_TILE, m)
    steps = m // tm
    final = g_final is not None
    has_tail = tail is not None
    last = steps - 1
    row = pl.BlockSpec((tm, D_MODEL), lambda i: (jnp.minimum(i, last), 0))
    in_specs = [row, _resident((1, D_MODEL)), _resident((D_MODEL, D_FF)),
                _resident((D_MODEL, D_FF)), _resident((D_FF, D_MODEL))]
    args = [x, g, wg, wu, wd]
    if final:
        in_specs.append(_resident((1, D_MODEL)))
        args.append(g_final)
    cast_specs = []
    for w in casts:
        blocks = math.gcd(steps, w.shape[0] // BF16_SUBLANES)
        per = steps // blocks
        cast_specs.append(pl.BlockSpec((w.shape[0] // blocks, w.shape[1]),
                                       lambda i, per=per: (jnp.minimum(i, last) // per, 0)))
    in_specs, out_specs = in_specs + cast_specs, [row] + cast_specs
    out_shape = ([jax.ShapeDtypeStruct((m, D_MODEL), F32)]
                 + [jax.ShapeDtypeStruct(w.shape, BF16) for w in casts])
    if has_tail:
        in_specs.append(_resident(tail.shape))
        out_specs.append(pl.BlockSpec(tail.shape, lambda i: (0, 0)))
        out_shape.append(jax.ShapeDtypeStruct(tail.shape, F32))
    out = pl.pallas_call(
        functools.partial(_ffn_body, final=final, n_casts=len(casts), steps=steps,
                          has_tail=has_tail),
        grid=(steps + has_tail,),
        in_specs=in_specs,
        out_specs=out_specs,
        out_shape=out_shape,
        compiler_params=_params("arbitrary" if casts or has_tail else "parallel"),
        name="ffn_final" if final else "ffn",
    )(*args, *casts, *([tail] if has_tail else []))
    if not casts and not has_tail:
        return out[0]
    n_casts = len(casts)
    return ((out[0],) + ((out[1:1 + n_casts],) if casts else ())
            + ((out[1 + n_casts],) if has_tail else ()))


Z_OFF = N_GROUPS * QKV_COLS
QM_OFF = Z_OFF + POOL_WIDTH
GATE_OFF = QM_OFF + MEM_WIDTH
IN_COLS = GATE_OFF + N_BRANCH * D_MODEL


def _inproj_body(x_ref, g_ref, w_ref, *rest, q_scale, with_windows, under_matmul=None):
    qkv_refs = rest[0:N_GROUPS]
    kv_refs = rest[N_GROUPS:2 * N_GROUPS] if with_windows else (None,) * N_GROUPS
    z_ref, qm_ref, u_ref, p_scr = rest[-4:]
    u = _rms(x_ref[...], g_ref[...]).astype(BF16)
    u_ref[...] = u
    tm = u.shape[0]
    width = N_GROUPS * GROUP_COLS
    p = _dot(u, w_ref[:, 0:3 * width])
    if under_matmul is not None:
        under_matmul()
    for g, (qkv_ref, kv_ref) in enumerate(zip(qkv_refs, kv_refs)):
        dil, per_class = qkv_ref.shape[1], qkv_ref.shape[2]
        q, k, v = [p[:, t * width + g * GROUP_COLS:t * width + (g + 1) * GROUP_COLS]
                   for t in range(3)]
        qkv = jnp.concatenate([q * q_scale, k, v], axis=1)
        if dil == 1:
            qkv_ref[0, 0] = qkv.astype(qkv_ref.dtype)
        else:
            for c in range(QKV_COLS // LANES):
                p_scr[c] = qkv[:, c * LANES:(c + 1) * LANES]
            for r in range(dil):
                blk = jnp.concatenate([p_scr[c, pl.ds(r, per_class, stride=dil), :]
                                       for c in range(QKV_COLS // LANES)], axis=1)
                qkv_ref[0, r] = blk.astype(qkv_ref.dtype)
        if with_windows:
            rows = kv_ref.shape[2]
            kv_ref[0] = jnp.concatenate([k[tm - rows:], v[tm - rows:]], axis=1).T
    z_ref[...] = _dot(u, w_ref[:, Z_OFF:QM_OFF])
    qm_ref[...] = _dot(u, w_ref[:, QM_OFF:GATE_OFF]).astype(qm_ref.dtype)


def _inproj(x, g, w, n, s, windows, dils, q_dtype, q_scale, rider=()):
    tm = min(TOKEN_TILE, s)
    tj = s // tm

    def tok(cols):
        return pl.BlockSpec((tm, cols), lambda b, j: (b * tj + j, 0))

    qkv_specs = [pl.BlockSpec((1, d, tm // d, QKV_COLS), lambda b, j: (b, 0, j, 0)) for d in dils]
    qkv_shapes = [jax.ShapeDtypeStruct((n, d, s // d, QKV_COLS), q_dtype) for d in dils]

    kv_specs, kv_shapes = [], []
    for win in windows or ():
        rows = min(win, tm)
        assert tm % rows == 0 and win % rows == 0
        first = (s - win) // tm
        if win >= tm:
            spec = pl.BlockSpec((1, 2 * GROUP_COLS, rows),
                                lambda b, j, first=first: (b, 0, jnp.maximum(j - first, 0)))
        else:
            spec = pl.BlockSpec((1, 2 * GROUP_COLS, rows), lambda b, j: (b, 0, 0))
        kv_specs.append(spec)
        kv_shapes.append(jax.ShapeDtypeStruct((n, 2 * GROUP_COLS, win), F32))
    m = n * s
    out_shape = (qkv_shapes + kv_shapes + [
        jax.ShapeDtypeStruct((m, POOL_WIDTH), F32),
        jax.ShapeDtypeStruct((m, MEM_WIDTH), q_dtype),
        jax.ShapeDtypeStruct((m, D_MODEL), BF16)])
    out_specs = qkv_specs + kv_specs + [tok(POOL_WIDTH), tok(MEM_WIDTH), tok(D_MODEL)]
    in_specs = [tok(D_MODEL), _resident((1, D_MODEL)),
                pl.BlockSpec((D_MODEL, GATE_OFF), lambda b, j: (0, 0),
                             pipeline_mode=pl.Buffered(1))]
    body = functools.partial(_inproj_body, q_scale=q_scale, with_windows=bool(windows))
    n_own_out = len(out_specs)
    if rider:
        assert rider[0].shape[0] == n * tj
        r_in, r_out, r_shapes = _sample_branch_specs(rider, lambda b, j: b * tj + j)
        in_specs, out_specs, out_shape = in_specs + r_in, out_specs + r_out, out_shape + r_shapes

        def body(*refs, own=body):
            ins, r_ins = refs[:3], refs[3:3 + N_SAMPLE_INPUTS]
            outs, scratch = refs[3 + N_SAMPLE_INPUTS:-1], refs[-1:]
            own(*ins, *outs[:n_own_out], *scratch, under_matmul=lambda: _sample_branch_math(
                pl.program_id(0) * tj + pl.program_id(1), *r_ins, *outs[n_own_out:]))

    return pl.pallas_call(
        body,
        grid=(n, tj),
        in_specs=in_specs,
        out_specs=out_specs,
        out_shape=out_shape,
        scratch_shapes=[pltpu.VMEM((QKV_COLS // LANES, tm, LANES), F32)],
        compiler_params=_params("arbitrary", "arbitrary"),
        name="inproj",
    )(x, g, w, *rider)


ATTN_CHUNKS = 32
HEAD_LANES = HEADS * STRIDES


def _band_body(row_ref, band_ref):
    for g in range(N_GROUPS):
        for h in range(HEADS):
            rows = jnp.broadcast_to(row_ref[g, h] * LOG2E, (STRIDES, 2 * STRIDES))
            band = pltpu.roll(rows, 0, 1, stride=1, stride_axis=0).T
            band_ref[g, :, h * STRIDES:(h + 1) * STRIDES] = band


def _bands(rows):
    return pl.pallas_call(
        _band_body,
        grid=(1,),
        in_specs=[_resident(rows.shape)],
        out_specs=pl.BlockSpec((N_GROUPS, 2 * STRIDES, HEAD_LANES), lambda i: (0, 0, 0)),
        out_shape=jax.ShapeDtypeStruct((N_GROUPS, 2 * STRIDES, HEAD_LANES), F32),
        compiler_params=_params("arbitrary"),
        name="bands",
    )(rows)


def _attn_body(qkv_ref, prev_ref, band_ref, o_ref, lse_ref, *, dil, chunks):
    first = pl.program_id(1) == 0
    lane_head = lax.broadcasted_iota(jnp.int32, (1, GROUP_COLS), 1) // HEAD_DIM

    def keys_of(r):
        return jnp.concatenate([prev_ref[0, r, :, GROUP_COLS:2 * GROUP_COLS],
                                qkv_ref[0, r, :, GROUP_COLS:2 * GROUP_COLS]], axis=0)

    def scores(k_all, r, i):
        q = qkv_ref[0, r, i * STRIDES:(i + 1) * STRIDES, 0:GROUP_COLS]
        qm = jnp.concatenate(
            [jnp.where(lane_head == h, q, jnp.zeros_like(q)) for h in range(HEADS)], axis=0)
        return _dot_t(k_all[i * STRIDES:(i + 2) * STRIDES], qm)

    ones_row = jnp.where(
        lax.broadcasted_iota(jnp.int32, (BF16_SUBLANES, 2 * STRIDES), 0) == 0, 1.0, 0.0
    ).astype(BF16)

    order = [(r, i) for r in range(dil) for i in range(chunks)]
    k_all = keys_of(0)
    st_next = scores(k_all, 0, 0)
    for idx, (r, i) in enumerate(order):
        st = st_next
        if i == 0:
            v_all = jnp.concatenate([prev_ref[0, r, :, 2 * GROUP_COLS:],
                                     qkv_ref[0, r, :, 2 * GROUP_COLS:]], axis=0)
            vt_all = v_all.astype(F32).T.astype(BF16)
        if idx + 1 < len(order):
            r2, i2 = order[idx + 1]
            if i2 == 0:
                k_all = keys_of(r2)
            st_next = scores(k_all, r2, i2)
        vt2 = vt_all[:, i * STRIDES:(i + 2) * STRIDES]
        o_parts, lse_parts = [], []
        for h in range(HEADS):
            hl = slice(h * STRIDES, (h + 1) * STRIDES)
            s_h = st[:, hl] + band_ref[:, hl]
            if i == 0:
                s_h = jnp.concatenate(
                    [jnp.where(first, NEG_INF, s_h[:STRIDES]), s_h[STRIDES:]], axis=0)
            m = jnp.max(s_h, axis=0, keepdims=True)
            e = jnp.exp2(s_h - m).astype(BF16)
            ot = _dot(jnp.concatenate([vt2[h * HEAD_DIM:(h + 1) * HEAD_DIM], ones_row], axis=0), e)
            l = ot[HEAD_DIM:HEAD_DIM + 1]
            o_parts.append(ot[:HEAD_DIM] * (1.0 / l))
            lse_parts.append(jnp.broadcast_to(m * LN2 + jnp.log(l), (HEAD_DIM, STRIDES)))
        rows = pl.ds(i * STRIDES * dil + r, STRIDES, stride=dil)
        o_rows = jnp.concatenate(o_parts, axis=0).T
        lse_rows = jnp.concatenate(lse_parts, axis=0).T
        for c in range(GROUP_COLS // LANES):
            o_ref[c, rows, :] = o_rows[:, c * LANES:(c + 1) * LANES]
            lse_ref[c, rows, :] = lse_rows[:, c * LANES:(c + 1) * LANES]


def _attn(qkv, bands, g, n, s, dil):
    chunks = ATTN_CHUNKS // dil
    span = chunks * STRIDES
    steps = s // (span * dil)
    slabs = GROUP_COLS // LANES
    out_spec = pl.BlockSpec((slabs, span * dil, LANES), lambda b, k: (0, b * steps + k, 0))
    out_sds = jax.ShapeDtypeStruct((slabs, n * s, LANES), F32)
    return pl.pallas_call(
        functools.partial(_attn_body, dil=dil, chunks=chunks),
        grid=(n, steps),
        in_specs=[pl.BlockSpec((1, dil, span, QKV_COLS), lambda b, k: (b, 0, k, 0)),
                  pl.BlockSpec((1, dil, STRIDES, QKV_COLS),
                               lambda b, k: (b, 0, jnp.maximum(k * chunks - 1, 0), 0)),
                  pl.BlockSpec((None, 2 * STRIDES, HEAD_LANES), lambda b, k: (g, 0, 0))],
        out_specs=[out_spec, out_spec],
        out_shape=[out_sds, out_sds],
        compiler_params=_params("parallel", "parallel"),
        name="attn_d%d" % dil,
    )(qkv, qkv, bands)


def _memkv_body(mem_ref, g_ref, w_ref, o_ref):
    o_ref[...] = _dot(_rms(mem_ref[...], g_ref[...]).astype(BF16), w_ref[...])


def _memkv(mem, g, w):
    m = mem.shape[0]
    row = pl.BlockSpec((MEM_LEN, D_MODEL), lambda i: (i, 0))
    return pl.pallas_call(
        _memkv_body,
        grid=(m // MEM_LEN,),
        in_specs=[row, _resident((1, D_MODEL)), _resident((D_MODEL, 2 * MEM_WIDTH))],
        out_specs=pl.BlockSpec((MEM_LEN, 2 * MEM_WIDTH), lambda i: (i, 0)),
        out_shape=jax.ShapeDtypeStruct((m, 2 * MEM_WIDTH), F32),
        compiler_params=_params("parallel"),
        name="memkv",
    )(mem, g, w)


def _merge_math(a, pooled, c, x, u, win_ref, wpool_ref, scale_ref, woa_ref, wob_ref,
                woc_ref, wout_ref):
    mixed = [_dot(pooled[gi].astype(BF16), wpool_ref[gi]) for gi in range(len(POOL_WINDOWS))]
    b = jnp.concatenate(mixed, axis=1) * scale_ref[...]
    m = None
    for k, (branch, wo_ref) in enumerate(((a, woa_ref), (b, wob_ref), (c, woc_ref))):
        lo = GATE_OFF + k * D_MODEL
        gate = jax.nn.sigmoid(_dot(u, win_ref[:, lo:lo + D_MODEL]))
        term = gate * _dot(branch.astype(BF16), wo_ref[...])
        m = term if m is None else m + term
    return x + _dot(m.astype(BF16), wout_ref[...])


def _mix_body(o0, o1, o2, l0, l1, l2, z_ref, halo_ref, qm_ref, mkv_ref, x_ref, u_ref, *rest,
              tile_in_seq):
    merge_refs, out_ref = rest[:-1], rest[-1]
    j = tile_in_seq
    tm = x_ref.shape[0]
    unslab = lambda ref: jnp.concatenate([ref[c] for c in range(ref.shape[0])], axis=1)
    lses = [unslab(l0), unslab(l1), unslab(l2)]
    mx = jnp.maximum(jnp.maximum(lses[0], lses[1]), lses[2])
    es = [jnp.exp(l - mx) for l in lses]
    a = ((es[0] * unslab(o0) + es[1] * unslab(o1) + es[2] * unslab(o2))
         / (es[0] + es[1] + es[2]))
    z = z_ref[...]
    halo = jnp.where(j == 0, 0.0, halo_ref[...])
    zc = jnp.concatenate([halo, z], axis=0)
    pos = j * tm + lax.broadcasted_iota(jnp.int32, (tm, 1), 0)
    pooled = []
    for gi, kw in enumerate(POOL_WINDOWS):
        cs = slice(gi * POOL_GROUP, (gi + 1) * POOL_GROUP)
        run = zc[:, cs]
        width = 1
        while width < kw:
            run = run[width:] + run[:-width]
            width *= 2
        first = POOL_HALO - (kw - 1)
        cnt = jnp.minimum(kw, pos + 1).astype(F32)
        pooled.append(run[first:first + tm] / cnt - z[:, cs])
    mkv = mkv_ref[...].astype(BF16)
    qm = qm_ref[...]
    cs_out = []
    for h in range(MEM_HEADS):
        hs = slice(h * MEM_HEAD_DIM, (h + 1) * MEM_HEAD_DIM)
        vs = slice(MEM_WIDTH + h * MEM_HEAD_DIM, MEM_WIDTH + (h + 1) * MEM_HEAD_DIM)
        s = _dot_t(qm[:, hs], mkv[:, hs]) * (1.0 / math.sqrt(MEM_HEAD_DIM))
        mm = jnp.max(s, axis=-1, keepdims=True)
        p = jnp.exp(s - mm)
        l = jnp.sum(p, axis=-1, keepdims=True)
        cs_out.append(_dot(p.astype(BF16), mkv[:, vs]) / l)
    c = jnp.concatenate(cs_out, axis=1)
    out_ref[...] = _merge_math(a, pooled, c, x_ref[...], u_ref[...], *merge_refs)


def _merge_weight_specs():
    return [_resident((D_MODEL, IN_COLS)),
            _resident((len(POOL_WINDOWS), POOL_GROUP, POOL_GROUP)), _resident((1, POOL_WIDTH)),
            _resident((GROUP_COLS, D_MODEL)), _resident((POOL_WIDTH, D_MODEL)),
            _resident((MEM_WIDTH, D_MODEL)), _resident((D_MODEL, D_MODEL))]


N_MERGE_WEIGHTS = 7


def _mix_with_tail_body(*refs, tiles_per_seq, steps):
    n_tile_in = 12
    tile_in, merge_refs = refs[:n_tile_in], refs[n_tile_in:n_tile_in + N_MERGE_WEIGHTS]
    (a_ref, pooled_ref, c_ref, xs_ref, us_ref,
     out_ref, tail_out) = refs[n_tile_in + N_MERGE_WEIGHTS:]
    t = pl.program_id(0)

    @pl.when(t < steps)
    def _():
        _mix_body(*tile_in, *merge_refs, out_ref, tile_in_seq=lax.rem(t, tiles_per_seq))

    @pl.when(t == steps)
    def _():
        pooled_all = pooled_ref[...]
        pooled = [pooled_all[:, gi * POOL_GROUP:(gi + 1) * POOL_GROUP]
                  for gi in range(len(POOL_WINDOWS))]
        tail_out[...] = _merge_math(a_ref[...], pooled, c_ref[...], xs_ref[...], us_ref[...],
                                    *merge_refs)


def _mix(os, lses, z, qm, mkv, x, u, weights, n, s, tail):
    tm = TOKEN_TILE
    tj = s // tm
    steps = n * tj
    last = steps - 1
    halo_per_tile = tm // POOL_HALO
    tile = lambda t: jnp.minimum(t, last)

    def tok(cols):
        return pl.BlockSpec((tm, cols), lambda t: (tile(t), 0))

    halo = pl.BlockSpec((POOL_HALO, POOL_WIDTH),
                        lambda t: (jnp.maximum(tile(t) * halo_per_tile - 1, 0), 0))
    slab = pl.BlockSpec((GROUP_COLS // LANES, tm, LANES), lambda t: (0, tile(t), 0))
    weight_specs = _merge_weight_specs()
    assert len(weight_specs) == N_MERGE_WEIGHTS
    in_specs = ([slab] * 6 + [tok(POOL_WIDTH), halo, tok(MEM_WIDTH),
                pl.BlockSpec((MEM_LEN, 2 * MEM_WIDTH), lambda t: (tile(t) // tj, 0)),
                tok(D_MODEL), tok(D_MODEL)] + weight_specs + [_resident(t.shape) for t in tail])
    xs = tail[3]
    return pl.pallas_call(
        functools.partial(_mix_with_tail_body, tiles_per_seq=tj, steps=steps),
        grid=(steps + 1,),
        in_specs=in_specs,
        out_specs=[tok(D_MODEL), pl.BlockSpec(xs.shape, lambda t: (0, 0))],
        out_shape=[jax.ShapeDtypeStruct((n * s, D_MODEL), F32),
                   jax.ShapeDtypeStruct(xs.shape, F32)],
        compiler_params=_params("arbitrary"),
        name="mix",
    )(*os, *lses, z, z, qm, mkv, x, u, *weights, *tail)


N_SAMPLE_INPUTS = 14


def _sample_branch_math(req, qkv0, qkv1, qkv2, c0, c1, c2, b0, b1, b2, bias0_ref, z_ref, st_ref,
                        qm_ref, cm_ref, a_ref, pooled_ref, c_ref):
    row = pl.ds(req, 1)
    eye = (lax.broadcasted_iota(jnp.int32, (HEAD_DIM, HEAD_DIM), 0)
           == lax.broadcasted_iota(jnp.int32, (HEAD_DIM, HEAD_DIM), 1))
    cube = (HEADS, HEAD_DIM, HEAD_DIM)

    def heads_of(vec, lo):
        return jnp.stack([vec[:, lo + h * HEAD_DIM:lo + (h + 1) * HEAD_DIM]
                          for h in range(HEADS)], axis=0)

    outs, lses = [], []
    for g, (qkv_ref, cache_ref, bias_ref) in enumerate(
            ((qkv0, c0, b0), (qkv1, c1, b1), (qkv2, c2, b2))):
        qkv = qkv_ref[row, :]
        q, kn, vn = heads_of(qkv, 0), heads_of(qkv, GROUP_COLS), heads_of(qkv, 2 * GROUP_COLS)
        q_col = jnp.sum(jnp.where(eye, jnp.broadcast_to(q, cube), 0.0), axis=2, keepdims=True)
        s = jnp.sum(cache_ref[0, 0] * q_col, axis=1, keepdims=True) + bias_ref[...]
        sn = jnp.sum(kn * q, axis=2, keepdims=True) + bias0_ref[g]
        m = jnp.maximum(jnp.max(s, axis=2, keepdims=True), sn)
        p = jnp.exp(s - m)
        pn = jnp.exp(sn - m)
        l = jnp.sum(p, axis=2, keepdims=True) + pn
        pv = jnp.sum(cache_ref[0, 1] * p, axis=2, keepdims=True)
        pv_row = jnp.sum(jnp.where(eye, jnp.broadcast_to(pv, cube), 0.0), axis=1, keepdims=True)
        outs.append((pv_row + pn * vn) / l)
        lses.append(m + jnp.log(l))
    mx = jnp.maximum(jnp.maximum(lses[0], lses[1]), lses[2])
    es = [jnp.exp(lse - mx) for lse in lses]
    a = (es[0] * outs[0] + es[1] * outs[1] + es[2] * outs[2]) / (es[0] + es[1] + es[2])
    a_ref[row, :] = jnp.concatenate([a[h] for h in range(HEADS)], axis=1)

    zn = z_ref[row, :]
    st = st_ref[:, row, :]
    pooled = []
    for gi, kw in enumerate(POOL_WINDOWS):
        cs = slice(gi * POOL_GROUP, (gi + 1) * POOL_GROUP)
        tot = jnp.sum(st[POOL_STATE - (kw - 1):, :, cs], axis=0) + zn[:, cs]
        pooled.append(tot / float(min(kw, PAST_LEN + 1)) - zn[:, cs])
    pooled_ref[row, :] = jnp.concatenate(pooled, axis=1)

    qm_row = qm_ref[row, :]
    qm = jnp.concatenate([qm_row[:, h * MEM_HEAD_DIM:(h + 1) * MEM_HEAD_DIM]
                          for h in range(MEM_HEADS)], axis=0)
    km = cm_ref[0, :, 0]
    vm = cm_ref[0, :, 1]
    s = jnp.sum(km * qm[None], axis=-1, keepdims=True) * (1.0 / math.sqrt(MEM_HEAD_DIM))
    m = jnp.max(s, axis=0)
    p = jnp.exp(s - m[None])
    c = jnp.sum(p * vm, axis=0) / jnp.sum(p, axis=0)
    c_ref[row, :] = jnp.concatenate([c[h:h + 1] for h in range(MEM_HEADS)], axis=1)


def _sample_branch_specs(operands, request_of):
    assert len(operands) == N_SAMPLE_INPUTS
    nb = operands[0].shape[0]

    def per_request(t):
        zeros = (0,) * (t.ndim - 1)
        return pl.BlockSpec((1,) + t.shape[1:], lambda *g: (request_of(*g),) + zeros)

    def whole(shape):
        zeros = (0,) * len(shape)
        return pl.BlockSpec(shape, lambda *g: zeros)

    streamed = (3, 4, 5, 13)
    in_specs = [per_request(t) if k in streamed else whole(t.shape)
                for k, t in enumerate(operands)]
    widths = (GROUP_COLS, POOL_WIDTH, MEM_WIDTH)
    return (in_specs, [whole((nb, w)) for w in widths],
            [jax.ShapeDtypeStruct((nb, w), F32) for w in widths])


def _rel_bucket(n):
    max_exact = N_BUCKETS // 2
    nf = jnp.maximum(n, 1).astype(F32)
    large = max_exact + (jnp.log(nf / max_exact) / math.log(MAX_DISTANCE / max_exact)
                         * (N_BUCKETS - max_exact)).astype(jnp.int32)
    large = jnp.minimum(large, N_BUCKETS - 1)
    return jnp.where(n < max_exact, n, large)


def _stride_bias(rel_bias, g, dil):
    j = jnp.arange(STRIDES + 1, dtype=jnp.int32)
    return rel_bias[_rel_bucket(j * dil)][:, g * HEADS:(g + 1) * HEADS].astype(F32)


def _band_row(bias_j):
    row = jnp.concatenate([bias_j[::-1], jnp.full((STRIDES - 1, HEADS), NEG_INF, F32)], axis=0)
    return row.T.reshape(HEADS, 1, 2 * STRIDES)


def _cache_bias(bias_j, dil):
    on_grid = bias_j[STRIDES:0:-1].T
    full = jnp.full((HEADS, STRIDES, dil), NEG_INF, F32).at[:, :, 0].set(on_grid)
    return full.reshape(HEADS, 1, STRIDES * dil)


def kernel(x_prompt, x_sample, cache_win0_kv, cache_win1_kv, cache_win2_kv, state_pool, cache_mem_kv, mem_prompt, rel_bias, g_ffn1, w1_gate, w1_up, w1_down, g_mix, w_in, w_pool, pool_scale, g_mem, w_mem_kv, w_oa, w_ob, w_oc, w_out, g_ffn2, w2_gate, w2_up, w2_down, g_final):
    n, s, _ = x_prompt.shape
    nb = x_sample.shape[0]
    depth = g_ffn1.shape[0]
    win_caches = (cache_win0_kv, cache_win1_kv, cache_win2_kv)
    bias_js = [_stride_bias(rel_bias, g, dil) for g, (_, dil) in enumerate(DIL_GROUPS)]
    bands = _bands(jnp.stack([_band_row(b) for b in bias_js]))
    bias_cache = [_cache_bias(b, dil) for b, (_, dil) in zip(bias_js, DIL_GROUPS)]
    bias_new = jnp.stack([b[0] for b in bias_js]).reshape(N_GROUPS, HEADS, 1, 1)
    gfin = g_final.reshape(1, D_MODEL)

    xp = x_prompt.reshape(n * s, D_MODEL)
    xs = x_sample.reshape(nb, D_MODEL)
    st_p = [[] for _ in range(5)]
    st_s = [[] for _ in range(4)]
    for l in range(depth):
        last = l == depth - 1
        bf = lambda w: w[l].astype(BF16)
        vec = lambda v: v[l].reshape(1, -1)
        w1 = (bf(w1_gate), bf(w1_up), bf(w1_down))
        q_scale = 1.0 / math.sqrt(HEAD_DIM)

        later = [w2_gate[l], w2_up[l], w2_down[l], w_in[l], w_oa[l], w_ob[l], w_oc[l], w_out[l],
                 w_mem_kv[l], w_pool[l].reshape(len(POOL_WINDOWS) * POOL_GROUP, POOL_GROUP)]
        xp, later, xs = _ffn(xp, vec(g_ffn1), *w1, casts=later, tail=xs)
        w2, (win_l, woa_l, wob_l, woc_l, wout_l, wmem_l, wpool_l) = later[0:3], later[3:]
        merge_w = (win_l, wpool_l.reshape(w_pool.shape[1:]), vec(pool_scale),
                   woa_l, wob_l, woc_l, wout_l)

        res = _inproj(xs, vec(g_mix), win_l, 1, nb, None, [1] * N_GROUPS, F32, q_scale)
        s_qkvs, s_z, s_qm, s_u = [t.reshape(nb, QKV_COLS) for t in res[0:3]], *res[3:6]
        rider = (s_qkvs + [jnp.transpose(cw[l], (0, 2, 3, 4, 1)) for cw in win_caches]
                 + bias_cache + [bias_new, s_z, jnp.transpose(state_pool[l], (1, 0, 2)), s_qm,
                                 cache_mem_kv[l]])
        res = _inproj(xp, vec(g_mix), win_l, n, s, [min(w, s) for w, _ in DIL_GROUPS],
                      [d for _, d in DIL_GROUPS], BF16, q_scale * LOG2E, rider)
        qkvs, kvwins, z, qm, u = res[0:3], res[3:6], res[6], res[7], res[8]
        s_branches = res[9:12]
        os, lses = [], []
        for g, (_, dil) in enumerate(DIL_GROUPS):
            o, lse = _attn(qkvs[g], bands, g, n, s, dil)
            os.append(o)
            lses.append(lse)
        mkv = _memkv(mem_prompt.reshape(n * MEM_LEN, D_MODEL), vec(g_mem), wmem_l)
        xp, xs = _mix(os, lses, z, qm, mkv, xp, u, merge_w, n, s, tail=(*s_branches, xs, s_u))
        xp, xs = _ffn(xp, vec(g_ffn2), *w2, g_final=gfin if last else None, tail=xs)
        for g, (win, _) in enumerate(DIL_GROUPS):
            kv_t = kvwins[g].reshape(n, 2, HEADS, HEAD_DIM, min(win, s))
            st_p[g].append(jnp.transpose(kv_t, (0, 4, 1, 2, 3)))
        st_p[3].append(z.reshape(n, s, POOL_WIDTH)[:, s - POOL_STATE:])
        st_p[4].append(mkv.reshape(n, MEM_LEN, 2, MEM_HEADS, MEM_HEAD_DIM))

        for g in range(N_GROUPS):
            st_s[g].append(s_qkvs[g][:, GROUP_COLS:].reshape(nb, 1, 2, HEADS, HEAD_DIM))
        st_s[3].append(s_z.reshape(nb, 1, POOL_WIDTH))

    y_prompt = xp.reshape(n, s, D_MODEL)
    y_sample = xs.reshape(nb, 1, D_MODEL)
    stack = lambda ts: jnp.stack(ts, axis=0)
    return (y_prompt, y_sample, stack(st_p[0]), stack(st_p[1]), stack(st_p[2]), stack(st_p[3]),
            stack(st_p[4]), stack(st_s[0]), stack(st_s[1]), stack(st_s[2]), stack(st_s[3]))
```

```python
import functools
import math

import jax
import jax.numpy as jnp
from jax import lax
from jax.experimental import pallas as pl
from jax.experimental.pallas import tpu as pltpu

F32 = jnp.float32
BF16 = jnp.bfloat16

D_MODEL = 1024
D_FF = 2816
LANES = 128
HEAD_DIM = 64
HEADS = 4
DIL_GROUPS = ((128, 1), (512, 4), (2048, 16))
N_GROUPS = 3
GROUP_COLS = HEADS * HEAD_DIM
QKV_COLS = 3 * GROUP_COLS
STRIDES = 128
POOL_WINDOWS = (2, 4, 8, 16)
POOL_GROUP = 128
POOL_WIDTH = 512
POOL_STATE = 15
POOL_HALO = 16
MEM_LEN = 256
MEM_HEADS = 4
MEM_HEAD_DIM = 128
MEM_WIDTH = 512
N_BUCKETS = 32
MAX_DISTANCE = 2048
N_BRANCH = 3
EPS = 1e-6
NEG_INF = -1e30
LOG2E = math.log2(math.e)
LN2 = math.log(2.0)
PAST_LEN = 8192

V7X_VMEM_LIMIT_BYTES = 56 * 1024 * 1024
TOKEN_TILE = 512


def _params(*sem):
    return pltpu.CompilerParams(dimension_semantics=sem,
                                vmem_limit_bytes=V7X_VMEM_LIMIT_BYTES)


def _resident(shape):
    zeros = (0,) * len(shape)
    return pl.BlockSpec(shape, lambda *_: zeros, pipeline_mode=pl.Buffered(1))


def _rms(x, g):
    return x * lax.rsqrt(jnp.mean(x * x, axis=-1, keepdims=True) + EPS) * g


def _dot(a, b):
    return jnp.dot(a, b, preferred_element_type=F32)


def _dot_t(a, b):
    return lax.dot_general(a, b, (((1,), (1,)), ((), ())), preferred_element_type=F32)


BF16_SUBLANES = 16


FF_CHUNK = 256
FF_CHUNKS = D_FF // FF_CHUNK


def _ffn_math(x, g_ref, chunk, gf_ref):
    h = _rms(x, g_ref[...]).astype(BF16)
    acc = None
    for c in range(FF_CHUNKS):
        wg, wu, wd = chunk(c)
        a = _dot(h, wg)
        b = _dot(h, wu)
        part = _dot((a * jax.nn.sigmoid(a) * b).astype(BF16), wd)
        acc = part if acc is None else acc + part
    y = x + 0.5 * acc
    return y if gf_ref is None else _rms(y, gf_ref[...])


def _ffn_body(*refs, final, n_casts, steps, own_weights):
    refs = iter(refs)
    x_ref, g_ref, wg_ref, wu_ref, wd_ref = (next(refs) for _ in range(5))
    gf_ref = next(refs) if final else None
    cast_in = [next(refs) for _ in range(n_casts)]
    tail_in = next(refs)
    o_ref = next(refs)
    cast_out = [next(refs) for _ in range(n_casts)]
    tail_out = next(refs)
    t = pl.program_id(0)
    if own_weights:
        wg_s, wu_s, wd_s = (next(refs) for _ in range(3))
        start = FF_CHUNKS

        @pl.when(t < start)
        def _():
            wg_s[t] = wg_ref[...].astype(BF16)
            wu_s[t] = wu_ref[...].astype(BF16)
            wd_s[t] = wd_ref[...].astype(BF16)

        chunk = lambda c: (wg_s[c], wu_s[c], wd_s[c])
    else:
        start = 0
        cols = lambda c: slice(c * FF_CHUNK, (c + 1) * FF_CHUNK)
        chunk = lambda c: (wg_ref[:, cols(c)], wu_ref[:, cols(c)], wd_ref[cols(c), :])

    @pl.when(jnp.logical_and(t >= start, t < start + steps))
    def _():
        o_ref[...] = _ffn_math(x_ref[...], g_ref, chunk, gf_ref)
        for src, dst in zip(cast_in, cast_out):
            dst[...] = src[...].astype(BF16)

    @pl.when(t == start + steps)
    def _():
        tail_out[...] = _ffn_math(tail_in[...], g_ref, chunk, gf_ref)


def _ffn(x, g, wg, wu, wd, tail, g_final=None, casts=()):
    m = x.shape[0]
    tm = min(TOKEN_TILE, m)
    steps = m // tm
    final = g_final is not None
    own_weights = wg.dtype == F32
    start = FF_CHUNKS if own_weights else 0
    last = steps - 1
    tile = lambda t: jnp.clip(t - start, 0, last)
    row = pl.BlockSpec((tm, D_MODEL), lambda t: (tile(t), 0))
    if own_weights:
        load = lambda t: jnp.minimum(t, FF_CHUNKS - 1)
        weight_specs = [pl.BlockSpec((D_MODEL, FF_CHUNK), lambda t: (0, load(t))),
                        pl.BlockSpec((D_MODEL, FF_CHUNK), lambda t: (0, load(t))),
                        pl.BlockSpec((FF_CHUNK, D_MODEL), lambda t: (load(t), 0))]
        scratch = [pltpu.VMEM((FF_CHUNKS, D_MODEL, FF_CHUNK), BF16),
                   pltpu.VMEM((FF_CHUNKS, D_MODEL, FF_CHUNK), BF16),
                   pltpu.VMEM((FF_CHUNKS, FF_CHUNK, D_MODEL), BF16)]
    else:
        weight_specs = [_resident((D_MODEL, D_FF)), _resident((D_MODEL, D_FF)),
                        _resident((D_FF, D_MODEL))]
        scratch = []
    in_specs = [row, _resident((1, D_MODEL))] + weight_specs
    args = [x, g, wg, wu, wd]
    if final:
        in_specs.append(_resident((1, D_MODEL)))
        args.append(g_final)
    cast_specs = []
    for w in casts:
        blocks = math.gcd(steps, w.shape[0] // BF16_SUBLANES)
        per = steps // blocks
        cast_specs.append(pl.BlockSpec((w.shape[0] // blocks, w.shape[1]),
                                       lambda t, per=per: (tile(t) // per, 0)))
    n_casts = len(casts)
    out = pl.pallas_call(
        functools.partial(_ffn_body, final=final, n_casts=n_casts, steps=steps,
                          own_weights=own_weights),
        grid=(start + steps + 1,),
        in_specs=in_specs + cast_specs + [_resident(tail.shape)],
        out_specs=[row] + cast_specs + [pl.BlockSpec(tail.shape, lambda t: (0, 0))],
        out_shape=[jax.ShapeDtypeStruct((m, D_MODEL), F32)]
        + [jax.ShapeDtypeStruct(w.shape, BF16) for w in casts]
        + [jax.ShapeDtypeStruct(tail.shape, F32)],
        scratch_shapes=scratch,
        compiler_params=_params("arbitrary"),
        name="ffn_final" if final else "ffn",
    )(*args, *casts, tail)
    return (out[0], out[1:1 + n_casts], out[1 + n_casts]) if casts else (out[0], out[1])


Z_OFF = N_GROUPS * QKV_COLS
QM_OFF = Z_OFF + POOL_WIDTH
GATE_OFF = QM_OFF + MEM_WIDTH
IN_COLS = GATE_OFF + N_BRANCH * D_MODEL


def _inproj_body(x_ref, g_ref, w_ref, *rest, q_scale, with_windows, under_matmul=None):
    qkv_refs = rest[0:N_GROUPS]
    kv_refs = rest[N_GROUPS:2 * N_GROUPS] if with_windows else (None,) * N_GROUPS
    z_ref, qm_ref, u_ref, p_scr = rest[-4:]
    u = _rms(x_ref[...], g_ref[...]).astype(BF16)
    u_ref[...] = u
    tm = u.shape[0]
    width = N_GROUPS * GROUP_COLS
    p = _dot(u, w_ref[:, 0:3 * width])
    if under_matmul is not None:
        under_matmul()
    for g, (qkv_ref, kv_ref) in enumerate(zip(qkv_refs, kv_refs)):
        dil, per_class = qkv_ref.shape[1], qkv_ref.shape[2]
        q, k, v = [p[:, t * width + g * GROUP_COLS:t * width + (g + 1) * GROUP_COLS]
                   for t in range(3)]
        qkv = jnp.concatenate([q * q_scale, k, v], axis=1)
        if dil == 1:
            qkv_ref[0, 0] = qkv.astype(qkv_ref.dtype)
        else:
            for c in range(QKV_COLS // LANES):
                p_scr[c] = qkv[:, c * LANES:(c + 1) * LANES]
            for r in range(dil):
                blk = jnp.concatenate([p_scr[c, pl.ds(r, per_class, stride=dil), :]
                                       for c in range(QKV_COLS // LANES)], axis=1)
                qkv_ref[0, r] = blk.astype(qkv_ref.dtype)
        if with_windows:
            rows = kv_ref.shape[2]
            kv_ref[0] = jnp.concatenate([k[tm - rows:], v[tm - rows:]], axis=1).T
    z_ref[...] = _dot(u, w_ref[:, Z_OFF:QM_OFF])
    qm_ref[...] = _dot(u, w_ref[:, QM_OFF:GATE_OFF]).astype(qm_ref.dtype)


def _inproj(x, g, w, n, s, windows, dils, q_dtype, q_scale, rider=()):
    tm = min(TOKEN_TILE, s)
    tj = s // tm

    def tok(cols):
        return pl.BlockSpec((tm, cols), lambda b, j: (b * tj + j, 0))

    qkv_specs = [pl.BlockSpec((1, d, tm // d, QKV_COLS), lambda b, j: (b, 0, j, 0)) for d in dils]
    qkv_shapes = [jax.ShapeDtypeStruct((n, d, s // d, QKV_COLS), q_dtype) for d in dils]

    kv_specs, kv_shapes = [], []
    for win in windows or ():
        rows = min(win, tm)
        assert tm % rows == 0 and win % rows == 0
        first = (s - win) // tm
        if win >= tm:
            spec = pl.BlockSpec((1, 2 * GROUP_COLS, rows),
                                lambda b, j, first=first: (b, 0, jnp.maximum(j - first, 0)))
        else:
            spec = pl.BlockSpec((1, 2 * GROUP_COLS, rows), lambda b, j: (b, 0, 0))
        kv_specs.append(spec)
        kv_shapes.append(jax.ShapeDtypeStruct((n, 2 * GROUP_COLS, win), F32))
    m = n * s
    out_shape = (qkv_shapes + kv_shapes + [
        jax.ShapeDtypeStruct((m, POOL_WIDTH), F32),
        jax.ShapeDtypeStruct((m, MEM_WIDTH), q_dtype),
        jax.ShapeDtypeStruct((m, D_MODEL), BF16)])
    out_specs = qkv_specs + kv_specs + [tok(POOL_WIDTH), tok(MEM_WIDTH), tok(D_MODEL)]
    in_specs = [tok(D_MODEL), _resident((1, D_MODEL)),
                pl.BlockSpec((D_MODEL, GATE_OFF), lambda b, j: (0, 0),
                             pipeline_mode=pl.Buffered(1))]
    body = functools.partial(_inproj_body, q_scale=q_scale, with_windows=bool(windows))
    n_own_out = len(out_specs)
    if rider:
        assert rider[0].shape[0] == n * tj
        r_in, r_out, r_shapes = _sample_branch_specs(rider, lambda b, j: b * tj + j)
        in_specs, out_specs, out_shape = in_specs + r_in, out_specs + r_out, out_shape + r_shapes

        def body(*refs, own=body):
            ins, r_ins = refs[:3], refs[3:3 + N_SAMPLE_INPUTS]
            outs, scratch = refs[3 + N_SAMPLE_INPUTS:-1], refs[-1:]
            own(*ins, *outs[:n_own_out], *scratch, under_matmul=lambda: _sample_branch_math(
                pl.program_id(0) * tj + pl.program_id(1), *r_ins, *outs[n_own_out:]))

    return pl.pallas_call(
        body,
        grid=(n, tj),
        in_specs=in_specs,
        out_specs=out_specs,
        out_shape=out_shape,
        scratch_shapes=[pltpu.VMEM((QKV_COLS // LANES, tm, LANES), F32)],
        compiler_params=_params("arbitrary", "arbitrary"),
        name="inproj",
    )(x, g, w, *rider)


ATTN_CHUNKS = 32
HEAD_LANES = HEADS * STRIDES


def _band_body(row_ref, band_ref):
    for g in range(N_GROUPS):
        for h in range(HEADS):
            rows = jnp.broadcast_to(row_ref[g, h] * LOG2E, (STRIDES, 2 * STRIDES))
            band = pltpu.roll(rows, 0, 1, stride=1, stride_axis=0).T
            band_ref[g, :, h * STRIDES:(h + 1) * STRIDES] = band


def _bands(rows):
    return pl.pallas_call(
        _band_body,
        grid=(1,),
        in_specs=[_resident(rows.shape)],
        out_specs=pl.BlockSpec((N_GROUPS, 2 * STRIDES, HEAD_LANES), lambda i: (0, 0, 0)),
        out_shape=jax.ShapeDtypeStruct((N_GROUPS, 2 * STRIDES, HEAD_LANES), F32),
        compiler_params=_params("arbitrary"),
        name="bands",
    )(rows)


def _attn_body(qkv_ref, prev_ref, band_ref, o_ref, lse_ref, *, dil, chunks):
    first = pl.program_id(1) == 0
    lane_head = lax.broadcasted_iota(jnp.int32, (1, GROUP_COLS), 1) // HEAD_DIM

    def keys_of(r):
        return jnp.concatenate([prev_ref[0, r, :, GROUP_COLS:2 * GROUP_COLS],
                                qkv_ref[0, r, :, GROUP_COLS:2 * GROUP_COLS]], axis=0)

    def scores(k_all, r, i):
        q = qkv_ref[0, r, i * STRIDES:(i + 1) * STRIDES, 0:GROUP_COLS]
        qm = jnp.concatenate(
            [jnp.where(lane_head == h, q, jnp.zeros_like(q)) for h in range(HEADS)], axis=0)
        return _dot_t(k_all[i * STRIDES:(i + 2) * STRIDES], qm)

    ones_row = jnp.where(
        lax.broadcasted_iota(jnp.int32, (BF16_SUBLANES, 2 * STRIDES), 0) == 0, 1.0, 0.0
    ).astype(BF16)

    order = [(r, i) for r in range(dil) for i in range(chunks)]
    k_all = keys_of(0)
    st_next = scores(k_all, 0, 0)
    for idx, (r, i) in enumerate(order):
        st = st_next
        if i == 0:
            v_all = jnp.concatenate([prev_ref[0, r, :, 2 * GROUP_COLS:],
                                     qkv_ref[0, r, :, 2 * GROUP_COLS:]], axis=0)
            vt_all = v_all.astype(F32).T.astype(BF16)
        if idx + 1 < len(order):
            r2, i2 = order[idx + 1]
            if i2 == 0:
                k_all = keys_of(r2)
            st_next = scores(k_all, r2, i2)
        vt2 = vt_all[:, i * STRIDES:(i + 2) * STRIDES]
        o_parts, lse_parts = [], []
        for h in range(HEADS):
            hl = slice(h * STRIDES, (h + 1) * STRIDES)
            s_h = st[:, hl] + band_ref[:, hl]
            if i == 0:
                s_h = jnp.concatenate(
                    [jnp.where(first, NEG_INF, s_h[:STRIDES]), s_h[STRIDES:]], axis=0)
            m = jnp.max(s_h, axis=0, keepdims=True)
            e = jnp.exp2(s_h - m).astype(BF16)
            ot = _dot(jnp.concatenate([vt2[h * HEAD_DIM:(h + 1) * HEAD_DIM], ones_row], axis=0), e)
            l = ot[HEAD_DIM:HEAD_DIM + 1]
            o_parts.append(ot[:HEAD_DIM] * (1.0 / l))
            lse_parts.append(jnp.broadcast_to(m * LN2 + jnp.log(l), (HEAD_DIM, STRIDES)))
        rows = pl.ds(i * STRIDES * dil + r, STRIDES, stride=dil)
        o_rows = jnp.concatenate(o_parts, axis=0).T
        lse_rows = jnp.concatenate(lse_parts, axis=0).T
        for c in range(GROUP_COLS // LANES):
            o_ref[c, rows, :] = o_rows[:, c * LANES:(c + 1) * LANES]
            lse_ref[c, rows, :] = lse_rows[:, c * LANES:(c + 1) * LANES]


def _attn(qkv, bands, g, n, s, dil):
    chunks = ATTN_CHUNKS // dil
    span = chunks * STRIDES
    steps = s // (span * dil)
    slabs = GROUP_COLS // LANES
    out_spec = pl.BlockSpec((slabs, span * dil, LANES), lambda b, k: (0, b * steps + k, 0))
    out_sds = jax.ShapeDtypeStruct((slabs, n * s, LANES), F32)
    return pl.pallas_call(
        functools.partial(_attn_body, dil=dil, chunks=chunks),
        grid=(n, steps),
        in_specs=[pl.BlockSpec((1, dil, span, QKV_COLS), lambda b, k: (b, 0, k, 0)),
                  pl.BlockSpec((1, dil, STRIDES, QKV_COLS),
                               lambda b, k: (b, 0, jnp.maximum(k * chunks - 1, 0), 0)),
                  pl.BlockSpec((None, 2 * STRIDES, HEAD_LANES), lambda b, k: (g, 0, 0))],
        out_specs=[out_spec, out_spec],
        out_shape=[out_sds, out_sds],
        compiler_params=_params("parallel", "parallel"),
        name="attn_d%d" % dil,
    )(qkv, qkv, bands)


def _memkv_body(mem_ref, g_ref, w_ref, o_ref):
    o_ref[...] = _dot(_rms(mem_ref[...], g_ref[...]).astype(BF16), w_ref[...])


def _memkv(mem, g, w):
    m = mem.shape[0]
    row = pl.BlockSpec((MEM_LEN, D_MODEL), lambda i: (i, 0))
    return pl.pallas_call(
        _memkv_body,
        grid=(m // MEM_LEN,),
        in_specs=[row, _resident((1, D_MODEL)), _resident((D_MODEL, 2 * MEM_WIDTH))],
        out_specs=pl.BlockSpec((MEM_LEN, 2 * MEM_WIDTH), lambda i: (i, 0)),
        out_shape=jax.ShapeDtypeStruct((m, 2 * MEM_WIDTH), F32),
        compiler_params=_params("parallel"),
        name="memkv",
    )(mem, g, w)


def _merge_math(a, pooled, c, x, u, win_ref, wpool_ref, scale_ref, woa_ref, wob_ref,
                woc_ref, wout_ref):
    mixed = [_dot(pooled[gi].astype(BF16), wpool_ref[gi]) for gi in range(len(POOL_WINDOWS))]
    b = jnp.concatenate(mixed, axis=1) * scale_ref[...]
    m = None
    for k, (branch, wo_ref) in enumerate(((a, woa_ref), (b, wob_ref), (c, woc_ref))):
        lo = GATE_OFF + k * D_MODEL
        gate = jax.nn.sigmoid(_dot(u, win_ref[:, lo:lo + D_MODEL]))
        term = gate * _dot(branch.astype(BF16), wo_ref[...])
        m = term if m is None else m + term
    return x + _dot(m.astype(BF16), wout_ref[...])


def _mix_body(o0, o1, o2, l0, l1, l2, z_ref, halo_ref, qm_ref, mkv_ref, x_ref, u_ref, *rest,
              tile_in_seq):
    merge_refs, out_ref = rest[:-1], rest[-1]
    j = tile_in_seq
    tm = x_ref.shape[0]
    unslab = lambda ref: jnp.concatenate([ref[c] for c in range(ref.shape[0])], axis=1)
    lses = [unslab(l0), unslab(l1), unslab(l2)]
    mx = jnp.maximum(jnp.maximum(lses[0], lses[1]), lses[2])
    es = [jnp.exp(l - mx) for l in lses]
    a = ((es[0] * unslab(o0) + es[1] * unslab(o1) + es[2] * unslab(o2))
         / (es[0] + es[1] + es[2]))
    z = z_ref[...]
    halo = jnp.where(j == 0, 0.0, halo_ref[...])
    zc = jnp.concatenate([halo, z], axis=0)
    pos = j * tm + lax.broadcasted_iota(jnp.int32, (tm, 1), 0)
    pooled = []
    for gi, kw in enumerate(POOL_WINDOWS):
        cs = slice(gi * POOL_GROUP, (gi + 1) * POOL_GROUP)
        run = zc[:, cs]
        width = 1
        while width < kw:
            run = run[width:] + run[:-width]
            width *= 2
        first = POOL_HALO - (kw - 1)
        cnt = jnp.minimum(kw, pos + 1).astype(F32)
        pooled.append(run[first:first + tm] / cnt - z[:, cs])
    mkv = mkv_ref[...].astype(BF16)
    qm = qm_ref[...]
    cs_out = []
    for h in range(MEM_HEADS):
        hs = slice(h * MEM_HEAD_DIM, (h + 1) * MEM_HEAD_DIM)
        vs = slice(MEM_WIDTH + h * MEM_HEAD_DIM, MEM_WIDTH + (h + 1) * MEM_HEAD_DIM)
        s = _dot_t(qm[:, hs], mkv[:, hs]) * (1.0 / math.sqrt(MEM_HEAD_DIM))
        mm = jnp.max(s, axis=-1, keepdims=True)
        p = jnp.exp(s - mm)
        l = jnp.sum(p, axis=-1, keepdims=True)
        cs_out.append(_dot(p.astype(BF16), mkv[:, vs]) / l)
    c = jnp.concatenate(cs_out, axis=1)
    out_ref[...] = _merge_math(a, pooled, c, x_ref[...], u_ref[...], *merge_refs)


def _merge_weight_specs():
    return [_resident((D_MODEL, IN_COLS)),
            _resident((len(POOL_WINDOWS), POOL_GROUP, POOL_GROUP)), _resident((1, POOL_WIDTH)),
            _resident((GROUP_COLS, D_MODEL)), _resident((POOL_WIDTH, D_MODEL)),
            _resident((MEM_WIDTH, D_MODEL)), _resident((D_MODEL, D_MODEL))]


N_MERGE_WEIGHTS = 7


def _mix_with_tail_body(*refs, tiles_per_seq, steps):
    n_tile_in = 12
    tile_in, merge_refs = refs[:n_tile_in], refs[n_tile_in:n_tile_in + N_MERGE_WEIGHTS]
    (a_ref, pooled_ref, c_ref, xs_ref, us_ref,
     out_ref, tail_out) = refs[n_tile_in + N_MERGE_WEIGHTS:]
    t = pl.program_id(0)

    @pl.when(t < steps)
    def _():
        _mix_body(*tile_in, *merge_refs, out_ref, tile_in_seq=lax.rem(t, tiles_per_seq))

    @pl.when(t == steps)
    def _():
        pooled_all = pooled_ref[...]
        pooled = [pooled_all[:, gi * POOL_GROUP:(gi + 1) * POOL_GROUP]
                  for gi in range(len(POOL_WINDOWS))]
        tail_out[...] = _merge_math(a_ref[...], pooled, c_ref[...], xs_ref[...], us_ref[...],
                                    *merge_refs)


def _mix(os, lses, z, qm, mkv, x, u, weights, n, s, tail):
    tm = TOKEN_TILE
    tj = s // tm
    steps = n * tj
    last = steps - 1
    halo_per_tile = tm // POOL_HALO
    tile = lambda t: jnp.minimum(t, last)

    def tok(cols):
        return pl.BlockSpec((tm, cols), lambda t: (tile(t), 0))

    halo = pl.BlockSpec((POOL_HALO, POOL_WIDTH),
                        lambda t: (jnp.maximum(tile(t) * halo_per_tile - 1, 0), 0))
    slab = pl.BlockSpec((GROUP_COLS // LANES, tm, LANES), lambda t: (0, tile(t), 0))
    weight_specs = _merge_weight_specs()
    assert len(weight_specs) == N_MERGE_WEIGHTS
    in_specs = ([slab] * 6 + [tok(POOL_WIDTH), halo, tok(MEM_WIDTH),
                pl.BlockSpec((MEM_LEN, 2 * MEM_WIDTH), lambda t: (tile(t) // tj, 0)),
                tok(D_MODEL), tok(D_MODEL)] + weight_specs + [_resident(t.shape) for t in tail])
    xs = tail[3]
    return pl.pallas_call(
        functools.partial(_mix_with_tail_body, tiles_per_seq=tj, steps=steps),
        grid=(steps + 1,),
        in_specs=in_specs,
        out_specs=[tok(D_MODEL), pl.BlockSpec(xs.shape, lambda t: (0, 0))],
        out_shape=[jax.ShapeDtypeStruct((n * s, D_MODEL), F32),
                   jax.ShapeDtypeStruct(xs.shape, F32)],
        compiler_params=_params("arbitrary"),
        name="mix",
    )(*os, *lses, z, z, qm, mkv, x, u, *weights, *tail)


N_SAMPLE_INPUTS = 14


def _sample_branch_math(req, qkv0, qkv1, qkv2, c0, c1, c2, b0, b1, b2, bias0_ref, z_ref, st_ref,
                        qm_ref, cm_ref, a_ref, pooled_ref, c_ref):
    row = pl.ds(req, 1)
    eye = (lax.broadcasted_iota(jnp.int32, (HEAD_DIM, HEAD_DIM), 0)
           == lax.broadcasted_iota(jnp.int32, (HEAD_DIM, HEAD_DIM), 1))
    cube = (HEADS, HEAD_DIM, HEAD_DIM)

    def heads_of(vec, lo):
        return jnp.stack([vec[:, lo + h * HEAD_DIM:lo + (h + 1) * HEAD_DIM]
                          for h in range(HEADS)], axis=0)

    outs, lses = [], []
    for g, (qkv_ref, cache_ref, bias_ref) in enumerate(
            ((qkv0, c0, b0), (qkv1, c1, b1), (qkv2, c2, b2))):
        qkv = qkv_ref[row, :]
        q, kn, vn = heads_of(qkv, 0), heads_of(qkv, GROUP_COLS), heads_of(qkv, 2 * GROUP_COLS)
        q_col = jnp.sum(jnp.where(eye, jnp.broadcast_to(q, cube), 0.0), axis=2, keepdims=True)
        s = jnp.sum(cache_ref[0, 0] * q_col, axis=1, keepdims=True) + bias_ref[...]
        sn = jnp.sum(kn * q, axis=2, keepdims=True) + bias0_ref[g]
        m = jnp.maximum(jnp.max(s, axis=2, keepdims=True), sn)
        p = jnp.exp(s - m)
        pn = jnp.exp(sn - m)
        l = jnp.sum(p, axis=2, keepdims=True) + pn
        pv = jnp.sum(cache_ref[0, 1] * p, axis=2, keepdims=True)
        pv_row = jnp.sum(jnp.where(eye, jnp.broadcast_to(pv, cube), 0.0), axis=1, keepdims=True)
        outs.append((pv_row + pn * vn) / l)
        lses.append(m + jnp.log(l))
    mx = jnp.maximum(jnp.maximum(lses[0], lses[1]), lses[2])
    es = [jnp.exp(lse - mx) for lse in lses]
    a = (es[0] * outs[0] + es[1] * outs[1] + es[2] * outs[2]) / (es[0] + es[1] + es[2])
    a_ref[row, :] = jnp.concatenate([a[h] for h in range(HEADS)], axis=1)

    zn = z_ref[row, :]
    st = st_ref[:, row, :]
    pooled = []
    for gi, kw in enumerate(POOL_WINDOWS):
        cs = slice(gi * POOL_GROUP, (gi + 1) * POOL_GROUP)
        tot = jnp.sum(st[POOL_STATE - (kw - 1):, :, cs], axis=0) + zn[:, cs]
        pooled.append(tot / float(min(kw, PAST_LEN + 1)) - zn[:, cs])
    pooled_ref[row, :] = jnp.concatenate(pooled, axis=1)

    qm_row = qm_ref[row, :]
    qm = jnp.concatenate([qm_row[:, h * MEM_HEAD_DIM:(h + 1) * MEM_HEAD_DIM]
                          for h in range(MEM_HEADS)], axis=0)
    km = cm_ref[0, :, 0]
    vm = cm_ref[0, :, 1]
    s = jnp.sum(km * qm[None], axis=-1, keepdims=True) * (1.0 / math.sqrt(MEM_HEAD_DIM))
    m = jnp.max(s, axis=0)
    p = jnp.exp(s - m[None])
    c = jnp.sum(p * vm, axis=0) / jnp.sum(p, axis=0)
    c_ref[row, :] = jnp.concatenate([c[h:h + 1] for h in range(MEM_HEADS)], axis=1)


def _sample_branch_specs(operands, request_of):
    assert len(operands) == N_SAMPLE_INPUTS
    nb = operands[0].shape[0]

    def per_request(t):
        zeros = (0,) * (t.ndim - 1)
        return pl.BlockSpec((1,) + t.shape[1:], lambda *g: (request_of(*g),) + zeros)

    def whole(shape):
        zeros = (0,) * len(shape)
        return pl.BlockSpec(shape, lambda *g: zeros)

    streamed = (3, 4, 5, 13)
    in_specs = [per_request(t) if k in streamed else whole(t.shape)
                for k, t in enumerate(operands)]
    widths = (GROUP_COLS, POOL_WIDTH, MEM_WIDTH)
    return (in_specs, [whole((nb, w)) for w in widths],
            [jax.ShapeDtypeStruct((nb, w), F32) for w in widths])


def _rel_bucket(n):
    max_exact = N_BUCKETS // 2
    nf = jnp.maximum(n, 1).astype(F32)
    large = max_exact + (jnp.log(nf / max_exact) / math.log(MAX_DISTANCE / max_exact)
                         * (N_BUCKETS - max_exact)).astype(jnp.int32)
    large = jnp.minimum(large, N_BUCKETS - 1)
    return jnp.where(n < max_exact, n, large)


def _stride_bias(rel_bias, g, dil):
    j = jnp.arange(STRIDES + 1, dtype=jnp.int32)
    return rel_bias[_rel_bucket(j * dil)][:, g * HEADS:(g + 1) * HEADS].astype(F32)


def _band_row(bias_j):
    row = jnp.concatenate([bias_j[::-1], jnp.full((STRIDES - 1, HEADS), NEG_INF, F32)], axis=0)
    return row.T.reshape(HEADS, 1, 2 * STRIDES)


def _cache_bias(bias_j, dil):
    on_grid = bias_j[STRIDES:0:-1].T
    full = jnp.full((HEADS, STRIDES, dil), NEG_INF, F32).at[:, :, 0].set(on_grid)
    return full.reshape(HEADS, 1, STRIDES * dil)


def kernel(x_prompt, x_sample, cache_win0_kv, cache_win1_kv, cache_win2_kv, state_pool, cache_mem_kv, mem_prompt, rel_bias, g_ffn1, w1_gate, w1_up, w1_down, g_mix, w_in, w_pool, pool_scale, g_mem, w_mem_kv, w_oa, w_ob, w_oc, w_out, g_ffn2, w2_gate, w2_up, w2_down, g_final):
    n, s, _ = x_prompt.shape
    nb = x_sample.shape[0]
    depth = g_ffn1.shape[0]
    win_caches = (cache_win0_kv, cache_win1_kv, cache_win2_kv)
    bias_js = [_stride_bias(rel_bias, g, dil) for g, (_, dil) in enumerate(DIL_GROUPS)]
    bands = _bands(jnp.stack([_band_row(b) for b in bias_js]))
    bias_cache = [_cache_bias(b, dil) for b, (_, dil) in zip(bias_js, DIL_GROUPS)]
    bias_new = jnp.stack([b[0] for b in bias_js]).reshape(N_GROUPS, HEADS, 1, 1)
    gfin = g_final.reshape(1, D_MODEL)

    xp = x_prompt.reshape(n * s, D_MODEL)
    xs = x_sample.reshape(nb, D_MODEL)
    st_p = [[] for _ in range(5)]
    st_s = [[] for _ in range(4)]
    for l in range(depth):
        last = l == depth - 1
        vec = lambda v: v[l].reshape(1, -1)
        q_scale = 1.0 / math.sqrt(HEAD_DIM)

        later = [w2_gate[l], w2_up[l], w2_down[l], w_in[l], w_oa[l], w_ob[l], w_oc[l], w_out[l],
                 w_mem_kv[l], w_pool[l].reshape(len(POOL_WINDOWS) * POOL_GROUP, POOL_GROUP)]
        xp, later, xs = _ffn(xp, vec(g_ffn1), w1_gate[l], w1_up[l], w1_down[l], xs, casts=later)
        w2, (win_l, woa_l, wob_l, woc_l, wout_l, wmem_l, wpool_l) = later[0:3], later[3:]
        merge_w = (win_l, wpool_l.reshape(w_pool.shape[1:]), vec(pool_scale),
                   woa_l, wob_l, woc_l, wout_l)

        res = _inproj(xs, vec(g_mix), win_l, 1, nb, None, [1] * N_GROUPS, F32, q_scale)
        s_qkvs, s_z, s_qm, s_u = [t.reshape(nb, QKV_COLS) for t in res[0:3]], *res[3:6]
        rider = (s_qkvs + [jnp.transpose(cw[l], (0, 2, 3, 4, 1)) for cw in win_caches]
                 + bias_cache + [bias_new, s_z, jnp.transpose(state_pool[l], (1, 0, 2)), s_qm,
                                 cache_mem_kv[l]])
        res = _inproj(xp, vec(g_mix), win_l, n, s, [min(w, s) for w, _ in DIL_GROUPS],
                      [d for _, d in DIL_GROUPS], BF16, q_scale * LOG2E, rider)
        qkvs, kvwins, z, qm, u = res[0:3], res[3:6], res[6], res[7], res[8]
        s_branches = res[9:12]
        os, lses = [], []
        for g, (_, dil) in enumerate(DIL_GROUPS):
            o, lse = _attn(qkvs[g], bands, g, n, s, dil)
            os.append(o)
            lses.append(lse)
        mkv = _memkv(mem_prompt.reshape(n * MEM_LEN, D_MODEL), vec(g_mem), wmem_l)
        xp, xs = _mix(os, lses, z, qm, mkv, xp, u, merge_w, n, s, tail=(*s_branches, xs, s_u))
        xp, xs = _ffn(xp, vec(g_ffn2), *w2, xs, g_final=gfin if last else None)
        for g, (win, _) in enumerate(DIL_GROUPS):
            kv_t = kvwins[g].reshape(n, 2, HEADS, HEAD_DIM, min(win, s))
            st_p[g].append(jnp.transpose(kv_t, (0, 4, 1, 2, 3)))
        st_p[3].append(z.reshape(n, s, POOL_WIDTH)[:, s - POOL_STATE:])
        st_p[4].append(mkv.reshape(n, MEM_LEN, 2, MEM_HEADS, MEM_HEAD_DIM))

        for g in range(N_GROUPS):
            st_s[g].append(s_qkvs[g][:, GROUP_COLS:].reshape(nb, 1, 2, HEADS, HEAD_DIM))
        st_s[3].append(s_z.reshape(nb, 1, POOL_WIDTH))

    y_prompt = xp.reshape(n, s, D_MODEL)
    y_sample = xs.reshape(nb, 1, D_MODEL)
    stack = lambda ts: jnp.stack(ts, axis=0)
    return (y_prompt, y_sample, stack(st_p[0]), stack(st_p[1]), stack(st_p[2]), stack(st_p[3]),
            stack(st_p[4]), stack(st_s[0]), stack(st_s[1]), stack(st_s[2]), stack(st_s[3]))
```

```python
import functools
import math

import jax
import jax.numpy as jnp
from jax import lax
from jax.experimental import pallas as pl
from jax.experimental.pallas import tpu as pltpu

F32 = jnp.float32
BF16 = jnp.bfloat16

D_MODEL = 1024
D_FF = 2816
LANES = 128
HEAD_DIM = 64
HEADS = 4
DIL_GROUPS = ((128, 1), (512, 4), (2048, 16))
N_GROUPS = 3
GROUP_COLS = HEADS * HEAD_DIM
QKV_COLS = 3 * GROUP_COLS
STRIDES = 128
POOL_WINDOWS = (2, 4, 8, 16)
POOL_GROUP = 128
POOL_WIDTH = 512
POOL_STATE = 15
POOL_HALO = 16
MEM_LEN = 256
MEM_HEADS = 4
MEM_HEAD_DIM = 128
MEM_WIDTH = 512
N_BUCKETS = 32
MAX_DISTANCE = 2048
N_BRANCH = 3
EPS = 1e-6
NEG_INF = -1e30
LOG2E = math.log2(math.e)
LN2 = math.log(2.0)
PAST_LEN = 8192

V7X_VMEM_LIMIT_BYTES = 56 * 1024 * 1024
TOKEN_TILE = 512


def _params(*sem):
    return pltpu.CompilerParams(dimension_semantics=sem,
                                vmem_limit_bytes=V7X_VMEM_LIMIT_BYTES)


def _resident(shape):
    zeros = (0,) * len(shape)
    return pl.BlockSpec(shape, lambda *_: zeros, pipeline_mode=pl.Buffered(1))


def _rms(x, g):
    return x * lax.rsqrt(jnp.mean(x * x, axis=-1, keepdims=True) + EPS) * g


def _dot(a, b):
    return jnp.dot(a, b, preferred_element_type=F32)


def _dot_t(a, b):
    return lax.dot_general(a, b, (((1,), (1,)), ((), ())), preferred_element_type=F32)


BF16_SUBLANES = 16


FF_CHUNK = 256
FF_CHUNKS = D_FF // FF_CHUNK


def _ffn_math(x, g_ref, chunk, gf_ref):
    h = _rms(x, g_ref[...]).astype(BF16)
    acc = None
    for c in range(FF_CHUNKS):
        wg, wu, wd = chunk(c)
        a = _dot(h, wg)
        b = _dot(h, wu)
        part = _dot((a * jax.nn.sigmoid(a) * b).astype(BF16), wd)
        acc = part if acc is None else acc + part
    y = x + 0.5 * acc
    return y if gf_ref is None else _rms(y, gf_ref[...])


def _ffn_body(*refs, final, n_casts, steps, own_weights):
    refs = iter(refs)
    x_ref, g_ref, wg_ref, wu_ref, wd_ref = (next(refs) for _ in range(5))
    gf_ref = next(refs) if final else None
    cast_in = [next(refs) for _ in range(n_casts)]
    tail_in = next(refs)
    o_ref = next(refs)
    cast_out = [next(refs) for _ in range(n_casts)]
    tail_out = next(refs)
    t = pl.program_id(0)
    if own_weights:
        wg_s, wu_s, wd_s = (next(refs) for _ in range(3))
        start = FF_CHUNKS

        @pl.when(t < start)
        def _():
            wg_s[t] = wg_ref[...].astype(BF16)
            wu_s[t] = wu_ref[...].astype(BF16)
            wd_s[t] = wd_ref[...].astype(BF16)

        chunk = lambda c: (wg_s[c], wu_s[c], wd_s[c])
    else:
        start = 0
        cols = lambda c: slice(c * FF_CHUNK, (c + 1) * FF_CHUNK)
        chunk = lambda c: (wg_ref[:, cols(c)], wu_ref[:, cols(c)], wd_ref[cols(c), :])

    @pl.when(jnp.logical_and(t >= start, t < start + steps))
    def _():
        o_ref[...] = _ffn_math(x_ref[...], g_ref, chunk, gf_ref)
        for src, dst in zip(cast_in, cast_out):
            dst[...] = src[...].astype(BF16)

    @pl.when(t == start + steps)
    def _():
        tail_out[...] = _ffn_math(tail_in[...], g_ref, chunk, gf_ref)


def _ffn(x, g, wg, wu, wd, tail, g_final=None, casts=()):
    m = x.shape[0]
    tm = min(TOKEN_TILE, m)
    steps = m // tm
    final = g_final is not None
    own_weights = wg.dtype == F32
    start = FF_CHUNKS if own_weights else 0
    last = steps - 1
    tile = lambda t: jnp.clip(t - start, 0, last)
    row = pl.BlockSpec((tm, D_MODEL), lambda t: (tile(t), 0))
    if own_weights:
        load = lambda t: jnp.minimum(t, FF_CHUNKS - 1)
        weight_specs = [pl.BlockSpec((D_MODEL, FF_CHUNK), lambda t: (0, load(t))),
                        pl.BlockSpec((D_MODEL, FF_CHUNK), lambda t: (0, load(t))),
                        pl.BlockSpec((FF_CHUNK, D_MODEL), lambda t: (load(t), 0))]
        scratch = [pltpu.VMEM((FF_CHUNKS, D_MODEL, FF_CHUNK), BF16),
                   pltpu.VMEM((FF_CHUNKS, D_MODEL, FF_CHUNK), BF16),
                   pltpu.VMEM((FF_CHUNKS, FF_CHUNK, D_MODEL), BF16)]
    else:
        weight_specs = [_resident((D_MODEL, D_FF)), _resident((D_MODEL, D_FF)),
                        _resident((D_FF, D_MODEL))]
        scratch = []
    in_specs = [row, _resident((1, D_MODEL))] + weight_specs
    args = [x, g, wg, wu, wd]
    if final:
        in_specs.append(_resident((1, D_MODEL)))
        args.append(g_final)
    cast_specs = []
    for w in casts:
        blocks = math.gcd(steps, w.shape[0] // BF16_SUBLANES)
        per = steps // blocks
        cast_specs.append(pl.BlockSpec((w.shape[0] // blocks, w.shape[1]),
                                       lambda t, per=per: (tile(t) // per, 0)))
    n_casts = len(casts)
    out = pl.pallas_call(
        functools.partial(_ffn_body, final=final, n_casts=n_casts, steps=steps,
                          own_weights=own_weights),
        grid=(start + steps + 1,),
        in_specs=in_specs + cast_specs + [_resident(tail.shape)],
        out_specs=[row] + cast_specs + [pl.BlockSpec(tail.shape, lambda t: (0, 0))],
        out_shape=[jax.ShapeDtypeStruct((m, D_MODEL), F32)]
        + [jax.ShapeDtypeStruct(w.shape, BF16) for w in casts]
        + [jax.ShapeDtypeStruct(tail.shape, F32)],
        scratch_shapes=scratch,
        compiler_params=_params("arbitrary"),
        name="ffn_final" if final else "ffn",
    )(*args, *casts, tail)
    return (out[0], out[1:1 + n_casts], out[1 + n_casts]) if casts else (out[0], out[1])


Z_OFF = N_GROUPS * QKV_COLS
QM_OFF = Z_OFF + POOL_WIDTH
GATE_OFF = QM_OFF + MEM_WIDTH
IN_COLS = GATE_OFF + N_BRANCH * D_MODEL


def _inproj_body(x_ref, g_ref, w_ref, *rest, q_scale, with_windows, under_matmul=None):
    qkv_refs = rest[0:N_GROUPS]
    kv_refs = rest[N_GROUPS:2 * N_GROUPS] if with_windows else (None,) * N_GROUPS
    z_ref, qm_ref, u_ref, p_scr = rest[-4:]
    u = _rms(x_ref[...], g_ref[...]).astype(BF16)
    u_ref[...] = u
    tm = u.shape[0]
    width = N_GROUPS * GROUP_COLS
    p = _dot(u, w_ref[:, 0:3 * width])
    if under_matmul is not None:
        under_matmul()
    for g, (qkv_ref, kv_ref) in enumerate(zip(qkv_refs, kv_refs)):
        dil, per_class = qkv_ref.shape[1], qkv_ref.shape[2]
        q, k, v = [p[:, t * width + g * GROUP_COLS:t * width + (g + 1) * GROUP_COLS]
                   for t in range(3)]
        qkv = jnp.concatenate([q * q_scale, k, v], axis=1)
        if dil == 1:
            qkv_ref[0, 0] = qkv.astype(qkv_ref.dtype)
        else:
            for c in range(QKV_COLS // LANES):
                p_scr[c] = qkv[:, c * LANES:(c + 1) * LANES]
            for r in range(dil):
                blk = jnp.concatenate([p_scr[c, pl.ds(r, per_class, stride=dil), :]
                                       for c in range(QKV_COLS // LANES)], axis=1)
                qkv_ref[0, r] = blk.astype(qkv_ref.dtype)
        if with_windows:
            rows = kv_ref.shape[2]
            kv_ref[0] = jnp.concatenate([k[tm - rows:], v[tm - rows:]], axis=1).T
    z_ref[...] = _dot(u, w_ref[:, Z_OFF:QM_OFF])
    qm_ref[...] = _dot(u, w_ref[:, QM_OFF:GATE_OFF]).astype(qm_ref.dtype)


def _inproj(x, g, w, n, s, windows, dils, q_dtype, q_scale, rider=()):
    tm = min(TOKEN_TILE, s)
    tj = s // tm

    def tok(cols):
        return pl.BlockSpec((tm, cols), lambda b, j: (b * tj + j, 0))

    qkv_specs = [pl.BlockSpec((1, d, tm // d, QKV_COLS), lambda b, j: (b, 0, j, 0)) for d in dils]
    qkv_shapes = [jax.ShapeDtypeStruct((n, d, s // d, QKV_COLS), q_dtype) for d in dils]

    kv_specs, kv_shapes = [], []
    for win in windows or ():
        rows = min(win, tm)
        assert tm % rows == 0 and win % rows == 0
        first = (s - win) // tm
        if win >= tm:
            spec = pl.BlockSpec((1, 2 * GROUP_COLS, rows),
                                lambda b, j, first=first: (b, 0, jnp.maximum(j - first, 0)))
        else:
            spec = pl.BlockSpec((1, 2 * GROUP_COLS, rows), lambda b, j: (b, 0, 0))
        kv_specs.append(spec)
        kv_shapes.append(jax.ShapeDtypeStruct((n, 2 * GROUP_COLS, win), F32))
    m = n * s
    out_shape = (qkv_shapes + kv_shapes + [
        jax.ShapeDtypeStruct((m, POOL_WIDTH), F32),
        jax.ShapeDtypeStruct((m, MEM_WIDTH), q_dtype),
        jax.ShapeDtypeStruct((m, D_MODEL), BF16)])
    out_specs = qkv_specs + kv_specs + [tok(POOL_WIDTH), tok(MEM_WIDTH), tok(D_MODEL)]
    in_specs = [tok(D_MODEL), _resident((1, D_MODEL)),
                pl.BlockSpec((D_MODEL, GATE_OFF), lambda b, j: (0, 0),
                             pipeline_mode=pl.Buffered(1))]
    body = functools.partial(_inproj_body, q_scale=q_scale, with_windows=bool(windows))
    n_own_out = len(out_specs)
    if rider:
        assert rider[0].shape[0] == n * tj
        r_in, r_out, r_shapes = _sample_branch_specs(rider, lambda b, j: b * tj + j)
        in_specs, out_specs, out_shape = in_specs + r_in, out_specs + r_out, out_shape + r_shapes

        def body(*refs, own=body):
            ins, r_ins = refs[:3], refs[3:3 + N_SAMPLE_INPUTS]
            outs, scratch = refs[3 + N_SAMPLE_INPUTS:-1], refs[-1:]
            own(*ins, *outs[:n_own_out], *scratch, under_matmul=lambda: _sample_branch_math(
                pl.program_id(0) * tj + pl.program_id(1), *r_ins, *outs[n_own_out:]))

    return pl.pallas_call(
        body,
        grid=(n, tj),
        in_specs=in_specs,
        out_specs=out_specs,
        out_shape=out_shape,
        scratch_shapes=[pltpu.VMEM((QKV_COLS // LANES, tm, LANES), F32)],
        compiler_params=_params("arbitrary", "arbitrary"),
        name="inproj",
    )(x, g, w, *rider)


ATTN_CHUNKS = 32
HEAD_LANES = HEADS * STRIDES


def _band_body(row_ref, band_ref):
    for g in range(N_GROUPS):
        for h in range(HEADS):
            rows = jnp.broadcast_to(row_ref[g, h] * LOG2E, (STRIDES, 2 * STRIDES))
            band = pltpu.roll(rows, 0, 1, stride=1, stride_axis=0).T
            band_ref[g, :, h * STRIDES:(h + 1) * STRIDES] = band


def _bands(rows):
    return pl.pallas_call(
        _band_body,
        grid=(1,),
        in_specs=[_resident(rows.shape)],
        out_specs=pl.BlockSpec((N_GROUPS, 2 * STRIDES, HEAD_LANES), lambda i: (0, 0, 0)),
        out_shape=jax.ShapeDtypeStruct((N_GROUPS, 2 * STRIDES, HEAD_LANES), F32),
        compiler_params=_params("arbitrary"),
        name="bands",
    )(rows)


def _attn_body(qkv_ref, band_ref, o_ref, lse_ref, *, dil, chunks):
    lane_head = lax.broadcasted_iota(jnp.int32, (1, GROUP_COLS), 1) // HEAD_DIM

    def keys(i):
        return slice(max(i - 1, 0) * STRIDES, (i + 1) * STRIDES)

    def scores(r, i):
        q = qkv_ref[0, r, i * STRIDES:(i + 1) * STRIDES, 0:GROUP_COLS]
        qm = jnp.concatenate(
            [jnp.where(lane_head == h, q, jnp.zeros_like(q)) for h in range(HEADS)], axis=0)
        return _dot_t(qkv_ref[0, r, keys(i), GROUP_COLS:2 * GROUP_COLS], qm)

    def ones_row(n_keys):
        return jnp.where(lax.broadcasted_iota(jnp.int32, (BF16_SUBLANES, n_keys), 0) == 0,
                         1.0, 0.0).astype(BF16)

    order = [(r, i) for r in range(dil) for i in range(chunks)]
    st_next = scores(0, 0)
    for idx, (r, i) in enumerate(order):
        st = st_next
        if i == 0:
            vt_cls = qkv_ref[0, r, :, 2 * GROUP_COLS:].astype(F32).T.astype(BF16)
        if idx + 1 < len(order):
            st_next = scores(*order[idx + 1])
        vt2 = vt_cls[:, keys(i)]
        band_lo = STRIDES if i == 0 else 0
        o_parts, lse_parts = [], []
        for h in range(HEADS):
            hl = slice(h * STRIDES, (h + 1) * STRIDES)
            s_h = st[:, hl] + band_ref[band_lo:, hl]
            m = jnp.max(s_h, axis=0, keepdims=True)
            e = jnp.exp2(s_h - m).astype(BF16)
            ot = _dot(jnp.concatenate([vt2[h * HEAD_DIM:(h + 1) * HEAD_DIM],
                                       ones_row(vt2.shape[1])], axis=0), e)
            l = ot[HEAD_DIM:HEAD_DIM + 1]
            o_parts.append(ot[:HEAD_DIM] * (1.0 / l))
            lse_parts.append(jnp.broadcast_to(m * LN2 + jnp.log(l), (HEAD_DIM, STRIDES)))
        rows = pl.ds(i * STRIDES * dil + r, STRIDES, stride=dil)
        o_rows = jnp.concatenate(o_parts, axis=0).T
        lse_rows = jnp.concatenate(lse_parts, axis=0).T
        for c in range(GROUP_COLS // LANES):
            o_ref[c, rows, :] = o_rows[:, c * LANES:(c + 1) * LANES]
            lse_ref[c, rows, :] = lse_rows[:, c * LANES:(c + 1) * LANES]


def _attn(qkv, bands, g, n, s, dil):
    chunks = s // (STRIDES * dil)
    assert chunks * dil == ATTN_CHUNKS
    span = chunks * STRIDES
    slabs = GROUP_COLS // LANES
    out_spec = pl.BlockSpec((slabs, s, LANES), lambda b: (0, b, 0))
    out_sds = jax.ShapeDtypeStruct((slabs, n * s, LANES), F32)
    return pl.pallas_call(
        functools.partial(_attn_body, dil=dil, chunks=chunks),
        grid=(n,),
        in_specs=[pl.BlockSpec((1, dil, span, QKV_COLS), lambda b: (b, 0, 0, 0)),
                  pl.BlockSpec((None, 2 * STRIDES, HEAD_LANES), lambda b: (g, 0, 0))],
        out_specs=[out_spec, out_spec],
        out_shape=[out_sds, out_sds],
        compiler_params=_params("parallel"),
        name="attn_d%d" % dil,
    )(qkv, bands)


def _memkv_body(mem_ref, g_ref, w_ref, o_ref):
    o_ref[...] = _dot(_rms(mem_ref[...], g_ref[...]).astype(BF16), w_ref[...])


def _memkv(mem, g, w):
    m = mem.shape[0]
    row = pl.BlockSpec((MEM_LEN, D_MODEL), lambda i: (i, 0))
    return pl.pallas_call(
        _memkv_body,
        grid=(m // MEM_LEN,),
        in_specs=[row, _resident((1, D_MODEL)), _resident((D_MODEL, 2 * MEM_WIDTH))],
        out_specs=pl.BlockSpec((MEM_LEN, 2 * MEM_WIDTH), lambda i: (i, 0)),
        out_shape=jax.ShapeDtypeStruct((m, 2 * MEM_WIDTH), F32),
        compiler_params=_params("parallel"),
        name="memkv",
    )(mem, g, w)


def _merge_math(a, pooled, c, x, u, win_ref, wpool_ref, scale_ref, woa_ref, wob_ref,
                woc_ref, wout_ref):
    mixed = [_dot(pooled[gi].astype(BF16), wpool_ref[gi]) for gi in range(len(POOL_WINDOWS))]
    b = jnp.concatenate(mixed, axis=1) * scale_ref[...]
    m = None
    for k, (branch, wo_ref) in enumerate(((a, woa_ref), (b, wob_ref), (c, woc_ref))):
        lo = GATE_OFF + k * D_MODEL
        gate = jax.nn.sigmoid(_dot(u, win_ref[:, lo:lo + D_MODEL]))
        term = gate * _dot(branch.astype(BF16), wo_ref[...])
        m = term if m is None else m + term
    return x + _dot(m.astype(BF16), wout_ref[...])


def _mix_body(o0, o1, o2, l0, l1, l2, z_ref, halo_ref, qm_ref, mkv_ref, x_ref, u_ref, *rest,
              tile_in_seq):
    merge_refs, out_ref = rest[:-1], rest[-1]
    j = tile_in_seq
    tm = x_ref.shape[0]
    unslab = lambda ref: jnp.concatenate([ref[c] for c in range(ref.shape[0])], axis=1)
    lses = [unslab(l0), unslab(l1), unslab(l2)]
    mx = jnp.maximum(jnp.maximum(lses[0], lses[1]), lses[2])
    es = [jnp.exp(l - mx) for l in lses]
    a = ((es[0] * unslab(o0) + es[1] * unslab(o1) + es[2] * unslab(o2))
         / (es[0] + es[1] + es[2]))
    z = z_ref[...]
    halo = jnp.where(j == 0, 0.0, halo_ref[...])
    zc = jnp.concatenate([halo, z], axis=0)
    pos = j * tm + lax.broadcasted_iota(jnp.int32, (tm, 1), 0)
    pooled = []
    for gi, kw in enumerate(POOL_WINDOWS):
        cs = slice(gi * POOL_GROUP, (gi + 1) * POOL_GROUP)
        run = zc[:, cs]
        width = 1
        while width < kw:
            run = run[width:] + run[:-width]
            width *= 2
        first = POOL_HALO - (kw - 1)
        cnt = jnp.minimum(kw, pos + 1).astype(F32)
        pooled.append(run[first:first + tm] / cnt - z[:, cs])
    mkv = mkv_ref[...].astype(BF16)
    qm = qm_ref[...]
    cs_out = []
    for h in range(MEM_HEADS):
        hs = slice(h * MEM_HEAD_DIM, (h + 1) * MEM_HEAD_DIM)
        vs = slice(MEM_WIDTH + h * MEM_HEAD_DIM, MEM_WIDTH + (h + 1) * MEM_HEAD_DIM)
        s = _dot_t(qm[:, hs], mkv[:, hs]) * (1.0 / math.sqrt(MEM_HEAD_DIM))
        mm = jnp.max(s, axis=-1, keepdims=True)
        p = jnp.exp(s - mm)
        l = jnp.sum(p, axis=-1, keepdims=True)
        cs_out.append(_dot(p.astype(BF16), mkv[:, vs]) / l)
    c = jnp.concatenate(cs_out, axis=1)
    out_ref[...] = _merge_math(a, pooled, c, x_ref[...], u_ref[...], *merge_refs)


def _merge_weight_specs():
    return [_resident((D_MODEL, IN_COLS)),
            _resident((len(POOL_WINDOWS), POOL_GROUP, POOL_GROUP)), _resident((1, POOL_WIDTH)),
            _resident((GROUP_COLS, D_MODEL)), _resident((POOL_WIDTH, D_MODEL)),
            _resident((MEM_WIDTH, D_MODEL)), _resident((D_MODEL, D_MODEL))]


N_MERGE_WEIGHTS = 7


def _mix_with_tail_body(*refs, tiles_per_seq, steps):
    n_tile_in = 12
    tile_in, merge_refs = refs[:n_tile_in], refs[n_tile_in:n_tile_in + N_MERGE_WEIGHTS]
    (a_ref, pooled_ref, c_ref, xs_ref, us_ref,
     out_ref, tail_out) = refs[n_tile_in + N_MERGE_WEIGHTS:]
    t = pl.program_id(0)

    @pl.when(t < steps)
    def _():
        _mix_body(*tile_in, *merge_refs, out_ref, tile_in_seq=lax.rem(t, tiles_per_seq))

    @pl.when(t == steps)
    def _():
        pooled_all = pooled_ref[...]
        pooled = [pooled_all[:, gi * POOL_GROUP:(gi + 1) * POOL_GROUP]
                  for gi in range(len(POOL_WINDOWS))]
        tail_out[...] = _merge_math(a_ref[...], pooled, c_ref[...], xs_ref[...], us_ref[...],
                                    *merge_refs)


def _mix(os, lses, z, qm, mkv, x, u, weights, n, s, tail):
    tm = TOKEN_TILE
    tj = s // tm
    steps = n * tj
    last = steps - 1
    halo_per_tile = tm // POOL_HALO
    tile = lambda t: jnp.minimum(t, last)

    def tok(cols):
        return pl.BlockSpec((tm, cols), lambda t: (tile(t), 0))

    halo = pl.BlockSpec((POOL_HALO, POOL_WIDTH),
                        lambda t: (jnp.maximum(tile(t) * halo_per_tile - 1, 0), 0))
    slab = pl.BlockSpec((GROUP_COLS // LANES, tm, LANES), lambda t: (0, tile(t), 0))
    weight_specs = _merge_weight_specs()
    assert len(weight_specs) == N_MERGE_WEIGHTS
    in_specs = ([slab] * 6 + [tok(POOL_WIDTH), halo, tok(MEM_WIDTH),
                pl.BlockSpec((MEM_LEN, 2 * MEM_WIDTH), lambda t: (tile(t) // tj, 0)),
                tok(D_MODEL), tok(D_MODEL)] + weight_specs + [_resident(t.shape) for t in tail])
    xs = tail[3]
    return pl.pallas_call(
        functools.partial(_mix_with_tail_body, tiles_per_seq=tj, steps=steps),
        grid=(steps + 1,),
        in_specs=in_specs,
        out_specs=[tok(D_MODEL), pl.BlockSpec(xs.shape, lambda t: (0, 0))],
        out_shape=[jax.ShapeDtypeStruct((n * s, D_MODEL), F32),
                   jax.ShapeDtypeStruct(xs.shape, F32)],
        compiler_params=_params("arbitrary"),
        name="mix",
    )(*os, *lses, z, z, qm, mkv, x, u, *weights, *tail)


N_SAMPLE_INPUTS = 14


def _sample_branch_math(req, qkv0, qkv1, qkv2, c0, c1, c2, b0, b1, b2, bias0_ref, z_ref, st_ref,
                        qm_ref, cm_ref, a_ref, pooled_ref, c_ref):
    row = pl.ds(req, 1)
    eye = (lax.broadcasted_iota(jnp.int32, (HEAD_DIM, HEAD_DIM), 0)
           == lax.broadcasted_iota(jnp.int32, (HEAD_DIM, HEAD_DIM), 1))
    cube = (HEADS, HEAD_DIM, HEAD_DIM)

    def heads_of(vec, lo):
        return jnp.stack([vec[:, lo + h * HEAD_DIM:lo + (h + 1) * HEAD_DIM]
                          for h in range(HEADS)], axis=0)

    outs, lses = [], []
    for g, (qkv_ref, cache_ref, bias_ref) in enumerate(
            ((qkv0, c0, b0), (qkv1, c1, b1), (qkv2, c2, b2))):
        qkv = qkv_ref[row, :]
        q, kn, vn = heads_of(qkv, 0), heads_of(qkv, GROUP_COLS), heads_of(qkv, 2 * GROUP_COLS)
        q_col = jnp.sum(jnp.where(eye, jnp.broadcast_to(q, cube), 0.0), axis=2, keepdims=True)
        s = jnp.sum(cache_ref[0, 0] * q_col, axis=1, keepdims=True) + bias_ref[...]
        sn = jnp.sum(kn * q, axis=2, keepdims=True) + bias0_ref[g]
        m = jnp.maximum(jnp.max(s, axis=2, keepdims=True), sn)
        p = jnp.exp(s - m)
        pn = jnp.exp(sn - m)
        l = jnp.sum(p, axis=2, keepdims=True) + pn
        pv = jnp.sum(cache_ref[0, 1] * p, axis=2, keepdims=True)
        pv_row = jnp.sum(jnp.where(eye, jnp.broadcast_to(pv, cube), 0.0), axis=1, keepdims=True)
        outs.append((pv_row + pn * vn) / l)
        lses.append(m + jnp.log(l))
    mx = jnp.maximum(jnp.maximum(lses[0], lses[1]), lses[2])
    es = [jnp.exp(lse - mx) for lse in lses]
    a = (es[0] * outs[0] + es[1] * outs[1] + es[2] * outs[2]) / (es[0] + es[1] + es[2])
    a_ref[row, :] = jnp.concatenate([a[h] for h in range(HEADS)], axis=1)

    zn = z_ref[row, :]
    st = st_ref[:, row, :]
    pooled = []
    for gi, kw in enumerate(POOL_WINDOWS):
        cs = slice(gi * POOL_GROUP, (gi + 1) * POOL_GROUP)
        tot = jnp.sum(st[POOL_STATE - (kw - 1):, :, cs], axis=0) + zn[:, cs]
        pooled.append(tot / float(min(kw, PAST_LEN + 1)) - zn[:, cs])
    pooled_ref[row, :] = jnp.concatenate(pooled, axis=1)

    qm_row = qm_ref[row, :]
    qm = jnp.concatenate([qm_row[:, h * MEM_HEAD_DIM:(h + 1) * MEM_HEAD_DIM]
                          for h in range(MEM_HEADS)], axis=0)
    km = cm_ref[0, :, 0]
    vm = cm_ref[0, :, 1]
    s = jnp.sum(km * qm[None], axis=-1, keepdims=True) * (1.0 / math.sqrt(MEM_HEAD_DIM))
    m = jnp.max(s, axis=0)
    p = jnp.exp(s - m[None])
    c = jnp.sum(p * vm, axis=0) / jnp.sum(p, axis=0)
    c_ref[row, :] = jnp.concatenate([c[h:h + 1] for h in range(MEM_HEADS)], axis=1)


def _sample_branch_specs(operands, request_of):
    assert len(operands) == N_SAMPLE_INPUTS
    nb = operands[0].shape[0]

    def per_request(t):
        zeros = (0,) * (t.ndim - 1)
        return pl.BlockSpec((1,) + t.shape[1:], lambda *g: (request_of(*g),) + zeros)

    def whole(shape):
        zeros = (0,) * len(shape)
        return pl.BlockSpec(shape, lambda *g: zeros)

    streamed = (3, 4, 5, 13)
    in_specs = [per_request(t) if k in streamed else whole(t.shape)
                for k, t in enumerate(operands)]
    widths = (GROUP_COLS, POOL_WIDTH, MEM_WIDTH)
    return (in_specs, [whole((nb, w)) for w in widths],
            [jax.ShapeDtypeStruct((nb, w), F32) for w in widths])


def _rel_bucket(n):
    max_exact = N_BUCKETS // 2
    nf = jnp.maximum(n, 1).astype(F32)
    large = max_exact + (jnp.log(nf / max_exact) / math.log(MAX_DISTANCE / max_exact)
                         * (N_BUCKETS - max_exact)).astype(jnp.int32)
    large = jnp.minimum(large, N_BUCKETS - 1)
    return jnp.where(n < max_exact, n, large)


def _stride_bias(rel_bias, g, dil):
    j = jnp.arange(STRIDES + 1, dtype=jnp.int32)
    return rel_bias[_rel_bucket(j * dil)][:, g * HEADS:(g + 1) * HEADS].astype(F32)


def _band_row(bias_j):
    row = jnp.concatenate([bias_j[::-1], jnp.full((STRIDES - 1, HEADS), NEG_INF, F32)], axis=0)
    return row.T.reshape(HEADS, 1, 2 * STRIDES)


def _cache_bias(bias_j, dil):
    on_grid = bias_j[STRIDES:0:-1].T
    full = jnp.full((HEADS, STRIDES, dil), NEG_INF, F32).at[:, :, 0].set(on_grid)
    return full.reshape(HEADS, 1, STRIDES * dil)


def kernel(x_prompt, x_sample, cache_win0_kv, cache_win1_kv, cache_win2_kv, state_pool, cache_mem_kv, mem_prompt, rel_bias, g_ffn1, w1_gate, w1_up, w1_down, g_mix, w_in, w_pool, pool_scale, g_mem, w_mem_kv, w_oa, w_ob, w_oc, w_out, g_ffn2, w2_gate, w2_up, w2_down, g_final):
    n, s, _ = x_prompt.shape
    nb = x_sample.shape[0]
    depth = g_ffn1.shape[0]
    win_caches = (cache_win0_kv, cache_win1_kv, cache_win2_kv)
    bias_js = [_stride_bias(rel_bias, g, dil) for g, (_, dil) in enumerate(DIL_GROUPS)]
    bands = _bands(jnp.stack([_band_row(b) for b in bias_js]))
    bias_cache = [_cache_bias(b, dil) for b, (_, dil) in zip(bias_js, DIL_GROUPS)]
    bias_new = jnp.stack([b[0] for b in bias_js]).reshape(N_GROUPS, HEADS, 1, 1)
    gfin = g_final.reshape(1, D_MODEL)

    xp = x_prompt.reshape(n * s, D_MODEL)
    xs = x_sample.reshape(nb, D_MODEL)
    st_p = [[] for _ in range(5)]
    st_s = [[] for _ in range(4)]
    for l in range(depth):
        last = l == depth - 1
        vec = lambda v: v[l].reshape(1, -1)
        q_scale = 1.0 / math.sqrt(HEAD_DIM)

        later = [w2_gate[l], w2_up[l], w2_down[l], w_in[l], w_oa[l], w_ob[l], w_oc[l], w_out[l],
                 w_mem_kv[l], w_pool[l].reshape(len(POOL_WINDOWS) * POOL_GROUP, POOL_GROUP)]
        xp, later, xs = _ffn(xp, vec(g_ffn1), w1_gate[l], w1_up[l], w1_down[l], xs, casts=later)
        w2, (win_l, woa_l, wob_l, woc_l, wout_l, wmem_l, wpool_l) = later[0:3], later[3:]
        merge_w = (win_l, wpool_l.reshape(w_pool.shape[1:]), vec(pool_scale),
                   woa_l, wob_l, woc_l, wout_l)

        res = _inproj(xs, vec(g_mix), win_l, 1, nb, None, [1] * N_GROUPS, F32, q_scale)
        s_qkvs, s_z, s_qm, s_u = [t.reshape(nb, QKV_COLS) for t in res[0:3]], *res[3:6]
        rider = (s_qkvs + [jnp.transpose(cw[l], (0, 2, 3, 4, 1)) for cw in win_caches]
                 + bias_cache + [bias_new, s_z, jnp.transpose(state_pool[l], (1, 0, 2)), s_qm,
                                 cache_mem_kv[l]])
        res = _inproj(xp, vec(g_mix), win_l, n, s, [min(w, s) for w, _ in DIL_GROUPS],
                      [d for _, d in DIL_GROUPS], BF16, q_scale * LOG2E, rider)
        qkvs, kvwins, z, qm, u = res[0:3], res[3:6], res[6], res[7], res[8]
        s_branches = res[9:12]
        os, lses = [], []
        for g, (_, dil) in enumerate(DIL_GROUPS):
            o, lse = _attn(qkvs[g], bands, g, n, s, dil)
            os.append(o)
            lses.append(lse)
        mkv = _memkv(mem_prompt.reshape(n * MEM_LEN, D_MODEL), vec(g_mem), wmem_l)
        xp, xs = _mix(os, lses, z, qm, mkv, xp, u, merge_w, n, s, tail=(*s_branches, xs, s_u))
        xp, xs = _ffn(xp, vec(g_ffn2), *w2, xs, g_final=gfin if last else None)
        for g, (win, _) in enumerate(DIL_GROUPS):
            kv_t = kvwins[g].reshape(n, 2, HEADS, HEAD_DIM, min(win, s))
            st_p[g].append(jnp.transpose(kv_t, (0, 4, 1, 2, 3)))
        st_p[3].append(z.reshape(n, s, POOL_WIDTH)[:, s - POOL_STATE:])
        st_p[4].append(mkv.reshape(n, MEM_LEN, 2, MEM_HEADS, MEM_HEAD_DIM))

        for g in range(N_GROUPS):
            st_s[g].append(s_qkvs[g][:, GROUP_COLS:].reshape(nb, 1, 2, HEADS, HEAD_DIM))
        st_s[3].append(s_z.reshape(nb, 1, POOL_WIDTH))

    y_prompt = xp.reshape(n, s, D_MODEL)
    y_sample = xs.reshape(nb, 1, D_MODEL)
    stack = lambda ts: jnp.stack(ts, axis=0)
    return (y_prompt, y_sample, stack(st_p[0]), stack(st_p[1]), stack(st_p[2]), stack(st_p[3]),
            stack(st_p[4]), stack(st_s[0]), stack(st_s[1]), stack(st_s[2]), stack(st_s[3]))
```

```python
import functools
import math

import jax
import jax.numpy as jnp
from jax import lax
from jax.experimental import pallas as pl
from jax.experimental.pallas import tpu as pltpu

F32 = jnp.float32
BF16 = jnp.bfloat16

D_MODEL = 1024
D_FF = 2816
LANES = 128
HEAD_DIM = 64
HEADS = 4
DIL_GROUPS = ((128, 1), (512, 4), (2048, 16))
N_GROUPS = 3
GROUP_COLS = HEADS * HEAD_DIM
QKV_COLS = 3 * GROUP_COLS
STRIDES = 128
POOL_WINDOWS = (2, 4, 8, 16)
POOL_GROUP = 128
POOL_WIDTH = 512
POOL_STATE = 15
POOL_HALO = 16
MEM_LEN = 256
MEM_HEADS = 4
MEM_HEAD_DIM = 128
MEM_WIDTH = 512
N_BUCKETS = 32
MAX_DISTANCE = 2048
N_BRANCH = 3
EPS = 1e-6
NEG_INF = -1e30
LOG2E = math.log2(math.e)
LN2 = math.log(2.0)
PAST_LEN = 8192

V7X_VMEM_LIMIT_BYTES = 56 * 1024 * 1024
TOKEN_TILE = 512


def _params(*sem):
    return pltpu.CompilerParams(dimension_semantics=sem,
                                vmem_limit_bytes=V7X_VMEM_LIMIT_BYTES)


def _resident(shape):
    zeros = (0,) * len(shape)
    return pl.BlockSpec(shape, lambda *_: zeros, pipeline_mode=pl.Buffered(1))


def _rms(x, g):
    return x * lax.rsqrt(jnp.mean(x * x, axis=-1, keepdims=True) + EPS) * g


def _dot(a, b):
    return jnp.dot(a, b, preferred_element_type=F32)


def _dot_t(a, b):
    return lax.dot_general(a, b, (((1,), (1,)), ((), ())), preferred_element_type=F32)


BF16_SUBLANES = 16


FF_CHUNK = 256
FF_CHUNKS = D_FF // FF_CHUNK


def _ffn_math(x, g_ref, chunk, gf_ref):
    h = _rms(x, g_ref[...]).astype(BF16)
    acc = None
    for c in range(FF_CHUNKS):
        wg, wu, wd = chunk(c)
        a = _dot(h, wg)
        b = _dot(h, wu)
        part = _dot((a * jax.nn.sigmoid(a) * b).astype(BF16), wd)
        acc = part if acc is None else acc + part
    y = x + 0.5 * acc
    return y if gf_ref is None else _rms(y, gf_ref[...])


def _ffn_body(*refs, final, n_casts, steps, own_weights):
    refs = iter(refs)
    x_ref, g_ref, wg_ref, wu_ref, wd_ref = (next(refs) for _ in range(5))
    gf_ref = next(refs) if final else None
    cast_in = [next(refs) for _ in range(n_casts)]
    tail_in = next(refs)
    o_ref = next(refs)
    cast_out = [next(refs) for _ in range(n_casts)]
    tail_out = next(refs)
    t = pl.program_id(0)

    def convert_slices():
        for src, dst in zip(cast_in, cast_out):
            dst[...] = src[...].astype(BF16)

    if own_weights:
        wg_s, wu_s, wd_s, h_s, acc_s = (next(refs) for _ in range(5))
        base = FF_CHUNKS - 1
        full_from = base + 1

        @pl.when(t <= base)
        def _():
            wg_s[t] = wg_ref[...].astype(BF16)
            wu_s[t] = wu_ref[...].astype(BF16)
            wd_s[t] = wd_ref[...].astype(BF16)

            @pl.when(t == 0)
            def _():
                h_s[...] = _rms(x_ref[...], g_ref[...]).astype(BF16)
                acc_s[...] = jnp.zeros_like(acc_s)

            h = h_s[...]
            a = _dot(h, wg_s[t])
            b = _dot(h, wu_s[t])
            acc_s[...] += _dot((a * jax.nn.sigmoid(a) * b).astype(BF16), wd_s[t])

            @pl.when(t == base)
            def _():
                y = x_ref[...] + 0.5 * acc_s[...]
                o_ref[...] = y if gf_ref is None else _rms(y, gf_ref[...])
                convert_slices()

        chunk = lambda c: (wg_s[c], wu_s[c], wd_s[c])
    else:
        base = full_from = 0
        cols = lambda c: slice(c * FF_CHUNK, (c + 1) * FF_CHUNK)
        chunk = lambda c: (wg_ref[:, cols(c)], wu_ref[:, cols(c)], wd_ref[cols(c), :])

    @pl.when(jnp.logical_and(t >= full_from, t < base + steps))
    def _():
        o_ref[...] = _ffn_math(x_ref[...], g_ref, chunk, gf_ref)
        convert_slices()

    @pl.when(t == base + steps)
    def _():
        tail_out[...] = _ffn_math(tail_in[...], g_ref, chunk, gf_ref)


def _ffn(x, g, wg, wu, wd, tail, g_final=None, casts=()):
    m = x.shape[0]
    tm = min(TOKEN_TILE, m)
    steps = m // tm
    final = g_final is not None
    own_weights = wg.dtype == F32
    base = FF_CHUNKS - 1 if own_weights else 0
    last = steps - 1
    tile = lambda t: jnp.clip(t - base, 0, last)
    row = pl.BlockSpec((tm, D_MODEL), lambda t: (tile(t), 0))
    if own_weights:
        load = lambda t: jnp.minimum(t, FF_CHUNKS - 1)
        weight_specs = [pl.BlockSpec((D_MODEL, FF_CHUNK), lambda t: (0, load(t))),
                        pl.BlockSpec((D_MODEL, FF_CHUNK), lambda t: (0, load(t))),
                        pl.BlockSpec((FF_CHUNK, D_MODEL), lambda t: (load(t), 0))]
        scratch = [pltpu.VMEM((FF_CHUNKS, D_MODEL, FF_CHUNK), BF16),
                   pltpu.VMEM((FF_CHUNKS, D_MODEL, FF_CHUNK), BF16),
                   pltpu.VMEM((FF_CHUNKS, FF_CHUNK, D_MODEL), BF16),
                   pltpu.VMEM((tm, D_MODEL), BF16),
                   pltpu.VMEM((tm, D_MODEL), F32)]
    else:
        weight_specs = [_resident((D_MODEL, D_FF)), _resident((D_MODEL, D_FF)),
                        _resident((D_FF, D_MODEL))]
        scratch = []
    in_specs = [row, _resident((1, D_MODEL))] + weight_specs
    args = [x, g, wg, wu, wd]
    if final:
        in_specs.append(_resident((1, D_MODEL)))
        args.append(g_final)
    cast_specs = []
    for w in casts:
        blocks = math.gcd(steps, w.shape[0] // BF16_SUBLANES)
        per = steps // blocks
        cast_specs.append(pl.BlockSpec((w.shape[0] // blocks, w.shape[1]),
                                       lambda t, per=per: (tile(t) // per, 0)))
    n_casts = len(casts)
    out = pl.pallas_call(
        functools.partial(_ffn_body, final=final, n_casts=n_casts, steps=steps,
                          own_weights=own_weights),
        grid=(base + steps + 1,),
        in_specs=in_specs + cast_specs + [_resident(tail.shape)],
        out_specs=[row] + cast_specs + [pl.BlockSpec(tail.shape, lambda t: (0, 0))],
        out_shape=[jax.ShapeDtypeStruct((m, D_MODEL), F32)]
        + [jax.ShapeDtypeStruct(w.shape, BF16) for w in casts]
        + [jax.ShapeDtypeStruct(tail.shape, F32)],
        scratch_shapes=scratch,
        compiler_params=_params("arbitrary"),
        name="ffn_final" if final else "ffn",
    )(*args, *casts, tail)
    return (out[0], out[1:1 + n_casts], out[1 + n_casts]) if casts else (out[0], out[1])


Z_OFF = N_GROUPS * QKV_COLS
QM_OFF = Z_OFF + POOL_WIDTH
GATE_OFF = QM_OFF + MEM_WIDTH
IN_COLS = GATE_OFF + N_BRANCH * D_MODEL


def _inproj_body(x_ref, g_ref, w_ref, *rest, q_scale, with_windows, under_matmul=None):
    qkv_refs = rest[0:N_GROUPS]
    kv_refs = rest[N_GROUPS:2 * N_GROUPS] if with_windows else (None,) * N_GROUPS
    z_ref, qm_ref, u_ref, p_scr = rest[-4:]
    u = _rms(x_ref[...], g_ref[...]).astype(BF16)
    u_ref[...] = u
    tm = u.shape[0]
    width = N_GROUPS * GROUP_COLS
    p = _dot(u, w_ref[...])
    if under_matmul is not None:
        under_matmul()
    for g, (qkv_ref, kv_ref) in enumerate(zip(qkv_refs, kv_refs)):
        dil, per_class = qkv_ref.shape[1], qkv_ref.shape[2]
        q, k, v = [p[:, t * width + g * GROUP_COLS:t * width + (g + 1) * GROUP_COLS]
                   for t in range(3)]
        qkv = jnp.concatenate([q * q_scale, k, v], axis=1)
        if dil == 1:
            qkv_ref[0, 0] = qkv.astype(qkv_ref.dtype)
        else:
            for c in range(QKV_COLS // LANES):
                p_scr[c] = qkv[:, c * LANES:(c + 1) * LANES]
            for r in range(dil):
                blk = jnp.concatenate([p_scr[c, pl.ds(r, per_class, stride=dil), :]
                                       for c in range(QKV_COLS // LANES)], axis=1)
                qkv_ref[0, r] = blk.astype(qkv_ref.dtype)
        if with_windows:
            rows = kv_ref.shape[2]
            kv_ref[0] = jnp.concatenate([k[tm - rows:], v[tm - rows:]], axis=1).T
    z_ref[...] = p[:, Z_OFF:QM_OFF]
    qm_ref[...] = p[:, QM_OFF:GATE_OFF].astype(qm_ref.dtype)


def _inproj(x, g, w, n, s, windows, dils, q_dtype, q_scale, rider=()):
    tm = min(TOKEN_TILE, s)
    tj = s // tm

    def tok(cols):
        return pl.BlockSpec((tm, cols), lambda b, j: (b * tj + j, 0))

    qkv_specs = [pl.BlockSpec((1, d, tm // d, QKV_COLS), lambda b, j: (b, 0, j, 0)) for d in dils]
    qkv_shapes = [jax.ShapeDtypeStruct((n, d, s // d, QKV_COLS), q_dtype) for d in dils]

    kv_specs, kv_shapes = [], []
    for win in windows or ():
        rows = min(win, tm)
        assert tm % rows == 0 and win % rows == 0
        first = (s - win) // tm
        if win >= tm:
            spec = pl.BlockSpec((1, 2 * GROUP_COLS, rows),
                                lambda b, j, first=first: (b, 0, jnp.maximum(j - first, 0)))
        else:
            spec = pl.BlockSpec((1, 2 * GROUP_COLS, rows), lambda b, j: (b, 0, 0))
        kv_specs.append(spec)
        kv_shapes.append(jax.ShapeDtypeStruct((n, 2 * GROUP_COLS, win), F32))
    m = n * s
    out_shape = (qkv_shapes + kv_shapes + [
        jax.ShapeDtypeStruct((m, POOL_WIDTH), F32),
        jax.ShapeDtypeStruct((m, MEM_WIDTH), q_dtype),
        jax.ShapeDtypeStruct((m, D_MODEL), BF16)])
    out_specs = qkv_specs + kv_specs + [tok(POOL_WIDTH), tok(MEM_WIDTH), tok(D_MODEL)]
    in_specs = [tok(D_MODEL), _resident((1, D_MODEL)),
                pl.BlockSpec((D_MODEL, GATE_OFF), lambda b, j: (0, 0),
                             pipeline_mode=pl.Buffered(1))]
    body = functools.partial(_inproj_body, q_scale=q_scale, with_windows=bool(windows))
    n_own_out = len(out_specs)
    if rider:
        assert rider[0].shape[0] == n * tj
        r_in, r_out, r_shapes = _sample_branch_specs(rider, lambda b, j: b * tj + j)
        in_specs, out_specs, out_shape = in_specs + r_in, out_specs + r_out, out_shape + r_shapes

        def body(*refs, own=body):
            ins, r_ins = refs[:3], refs[3:3 + N_SAMPLE_INPUTS]
            outs, scratch = refs[3 + N_SAMPLE_INPUTS:-1], refs[-1:]
            own(*ins, *outs[:n_own_out], *scratch, under_matmul=lambda: _sample_branch_math(
                pl.program_id(0) * tj + pl.program_id(1), *r_ins, *outs[n_own_out:]))

    return pl.pallas_call(
        body,
        grid=(n, tj),
        in_specs=in_specs,
        out_specs=out_specs,
        out_shape=out_shape,
        scratch_shapes=[pltpu.VMEM((QKV_COLS // LANES, tm, LANES), F32)],
        compiler_params=_params("arbitrary", "arbitrary"),
        name="inproj",
    )(x, g, w, *rider)


ATTN_CHUNKS = 32
HEAD_LANES = HEADS * STRIDES


def _band_body(row_ref, band_ref):
    for g in range(N_GROUPS):
        for h in range(HEADS):
            rows = jnp.broadcast_to(row_ref[g, h] * LOG2E, (STRIDES, 2 * STRIDES))
            band = pltpu.roll(rows, 0, 1, stride=1, stride_axis=0).T
            band_ref[g, :, h * STRIDES:(h + 1) * STRIDES] = band


def _bands(rows):
    return pl.pallas_call(
        _band_body,
        grid=(1,),
        in_specs=[_resident(rows.shape)],
        out_specs=pl.BlockSpec((N_GROUPS, 2 * STRIDES, HEAD_LANES), lambda i: (0, 0, 0)),
        out_shape=jax.ShapeDtypeStruct((N_GROUPS, 2 * STRIDES, HEAD_LANES), F32),
        compiler_params=_params("arbitrary"),
        name="bands",
    )(rows)


def _attn_body(qkv_ref, band_ref, o_ref, lse_ref, *, dil, chunks):
    lane_head = lax.broadcasted_iota(jnp.int32, (1, GROUP_COLS), 1) // HEAD_DIM

    def keys(i):
        return slice(max(i - 1, 0) * STRIDES, (i + 1) * STRIDES)

    def scores(r, i):
        q = qkv_ref[0, r, i * STRIDES:(i + 1) * STRIDES, 0:GROUP_COLS]
        qm = jnp.concatenate(
            [jnp.where(lane_head == h, q, jnp.zeros_like(q)) for h in range(HEADS)], axis=0)
        return _dot_t(qkv_ref[0, r, keys(i), GROUP_COLS:2 * GROUP_COLS], qm)

    def ones_row(n_keys):
        return jnp.where(lax.broadcasted_iota(jnp.int32, (BF16_SUBLANES, n_keys), 0) == 0,
                         1.0, 0.0).astype(BF16)

    order = [(r, i) for r in range(dil) for i in range(chunks)]
    st_next = scores(0, 0)
    for idx, (r, i) in enumerate(order):
        st = st_next
        if i == 0:
            vt_cls = qkv_ref[0, r, :, 2 * GROUP_COLS:].T
        if idx + 1 < len(order):
            st_next = scores(*order[idx + 1])
        vt2 = vt_cls[:, keys(i)]
        band_lo = STRIDES if i == 0 else 0
        o_parts, lse_parts = [], []
        for h in range(HEADS):
            hl = slice(h * STRIDES, (h + 1) * STRIDES)
            s_h = st[:, hl] + band_ref[band_lo:, hl]
            m = jnp.max(s_h, axis=0, keepdims=True)
            e = jnp.exp2(s_h - m).astype(BF16)
            ot = _dot(jnp.concatenate([vt2[h * HEAD_DIM:(h + 1) * HEAD_DIM],
                                       ones_row(vt2.shape[1])], axis=0), e)
            l = ot[HEAD_DIM:HEAD_DIM + 1]
            o_parts.append(ot[:HEAD_DIM] * (1.0 / l))
            lse_parts.append(jnp.broadcast_to(m * LN2 + jnp.log(l), (HEAD_DIM, STRIDES)))
        rows = pl.ds(i * STRIDES * dil + r, STRIDES, stride=dil)
        o_rows = jnp.concatenate(o_parts, axis=0).T
        lse_rows = jnp.concatenate(lse_parts, axis=0).T
        for c in range(GROUP_COLS // LANES):
            o_ref[c, rows, :] = o_rows[:, c * LANES:(c + 1) * LANES]
            lse_ref[c, rows, :] = lse_rows[:, c * LANES:(c + 1) * LANES]


def _attn(qkv, bands, g, n, s, dil):
    chunks = s // (STRIDES * dil)
    assert chunks * dil == ATTN_CHUNKS
    span = chunks * STRIDES
    slabs = GROUP_COLS // LANES
    out_spec = pl.BlockSpec((slabs, s, LANES), lambda b: (0, b, 0))
    out_sds = jax.ShapeDtypeStruct((slabs, n * s, LANES), F32)
    return pl.pallas_call(
        functools.partial(_attn_body, dil=dil, chunks=chunks),
        grid=(n,),
        in_specs=[pl.BlockSpec((1, dil, span, QKV_COLS), lambda b: (b, 0, 0, 0)),
                  pl.BlockSpec((None, 2 * STRIDES, HEAD_LANES), lambda b: (g, 0, 0))],
        out_specs=[out_spec, out_spec],
        out_shape=[out_sds, out_sds],
        compiler_params=_params("parallel"),
        name="attn_d%d" % dil,
    )(qkv, bands)


def _memkv_body(mem_ref, g_ref, w_ref, o_ref):
    o_ref[...] = _dot(_rms(mem_ref[...], g_ref[...]).astype(BF16), w_ref[...])


def _memkv(mem, g, w):
    m = mem.shape[0]
    row = pl.BlockSpec((MEM_LEN, D_MODEL), lambda i: (i, 0))
    return pl.pallas_call(
        _memkv_body,
        grid=(m // MEM_LEN,),
        in_specs=[row, _resident((1, D_MODEL)), _resident((D_MODEL, 2 * MEM_WIDTH))],
        out_specs=pl.BlockSpec((MEM_LEN, 2 * MEM_WIDTH), lambda i: (i, 0)),
        out_shape=jax.ShapeDtypeStruct((m, 2 * MEM_WIDTH), F32),
        compiler_params=_params("parallel"),
        name="memkv",
    )(mem, g, w)


def _merge_math(a, pooled, c, x, u, win_ref, wpool_ref, scale_ref, woa_ref, wob_ref,
                woc_ref, wout_ref):
    mixed = [_dot(pooled[gi].astype(BF16), wpool_ref[gi]) for gi in range(len(POOL_WINDOWS))]
    b = jnp.concatenate(mixed, axis=1) * scale_ref[...]
    m = None
    for k, (branch, wo_ref) in enumerate(((a, woa_ref), (b, wob_ref), (c, woc_ref))):
        lo = GATE_OFF + k * D_MODEL
        gate = jax.nn.sigmoid(_dot(u, win_ref[:, lo:lo + D_MODEL]))
        term = gate * _dot(branch.astype(BF16), wo_ref[...])
        m = term if m is None else m + term
    return x + _dot(m.astype(BF16), wout_ref[...])


def _mix_body(o0, o1, o2, l0, l1, l2, z_ref, halo_ref, qm_ref, mkv_ref, x_ref, u_ref, *rest,
              tile_in_seq):
    merge_refs, out_ref = rest[:-1], rest[-1]
    j = tile_in_seq
    tm = x_ref.shape[0]
    unslab = lambda ref: jnp.concatenate([ref[c] for c in range(ref.shape[0])], axis=1)
    lses = [unslab(l0), unslab(l1), unslab(l2)]
    mx = jnp.maximum(jnp.maximum(lses[0], lses[1]), lses[2])
    es = [jnp.exp(l - mx) for l in lses]
    a = ((es[0] * unslab(o0) + es[1] * unslab(o1) + es[2] * unslab(o2))
         / (es[0] + es[1] + es[2]))
    z = z_ref[...]
    halo = jnp.where(j == 0, 0.0, halo_ref[...])
    zc = jnp.concatenate([halo, z], axis=0)
    pos = j * tm + lax.broadcasted_iota(jnp.int32, (tm, 1), 0)
    pooled = []
    for gi, kw in enumerate(POOL_WINDOWS):
        cs = slice(gi * POOL_GROUP, (gi + 1) * POOL_GROUP)
        run = zc[:, cs]
        width = 1
        while width < kw:
            run = run[width:] + run[:-width]
            width *= 2
        first = POOL_HALO - (kw - 1)
        cnt = jnp.minimum(kw, pos + 1).astype(F32)
        pooled.append(run[first:first + tm] / cnt - z[:, cs])
    mkv = mkv_ref[...].astype(BF16)
    qm = qm_ref[...]
    cs_out = []
    for h in range(MEM_HEADS):
        hs = slice(h * MEM_HEAD_DIM, (h + 1) * MEM_HEAD_DIM)
        vs = slice(MEM_WIDTH + h * MEM_HEAD_DIM, MEM_WIDTH + (h + 1) * MEM_HEAD_DIM)
        s = _dot_t(qm[:, hs], mkv[:, hs]) * (1.0 / math.sqrt(MEM_HEAD_DIM))
        mm = jnp.max(s, axis=-1, keepdims=True)
        p = jnp.exp(s - mm)
        l = jnp.sum(p, axis=-1, keepdims=True)
        cs_out.append(_dot(p.astype(BF16), mkv[:, vs]) / l)
    c = jnp.concatenate(cs_out, axis=1)
    out_ref[...] = _merge_math(a, pooled, c, x_ref[...], u_ref[...], *merge_refs)


def _merge_weight_specs():
    return [_resident((D_MODEL, IN_COLS)),
            _resident((len(POOL_WINDOWS), POOL_GROUP, POOL_GROUP)), _resident((1, POOL_WIDTH)),
            _resident((GROUP_COLS, D_MODEL)), _resident((POOL_WIDTH, D_MODEL)),
            _resident((MEM_WIDTH, D_MODEL)), _resident((D_MODEL, D_MODEL))]


N_MERGE_WEIGHTS = 7


def _mix_with_tail_body(*refs, tiles_per_seq, steps):
    n_tile_in = 12
    tile_in, merge_refs = refs[:n_tile_in], refs[n_tile_in:n_tile_in + N_MERGE_WEIGHTS]
    (a_ref, pooled_ref, c_ref, xs_ref, us_ref,
     out_ref, tail_out) = refs[n_tile_in + N_MERGE_WEIGHTS:]
    t = pl.program_id(0)

    @pl.when(t < steps)
    def _():
        _mix_body(*tile_in, *merge_refs, out_ref, tile_in_seq=lax.rem(t, tiles_per_seq))

    @pl.when(t == steps)
    def _():
        pooled_all = pooled_ref[...]
        pooled = [pooled_all[:, gi * POOL_GROUP:(gi + 1) * POOL_GROUP]
                  for gi in range(len(POOL_WINDOWS))]
        tail_out[...] = _merge_math(a_ref[...], pooled, c_ref[...], xs_ref[...], us_ref[...],
                                    *merge_refs)


def _mix(os, lses, z, qm, mkv, x, u, weights, n, s, tail):
    tm = TOKEN_TILE
    tj = s // tm
    steps = n * tj
    last = steps - 1
    halo_per_tile = tm // POOL_HALO
    tile = lambda t: jnp.minimum(t, last)

    def tok(cols):
        return pl.BlockSpec((tm, cols), lambda t: (tile(t), 0))

    halo = pl.BlockSpec((POOL_HALO, POOL_WIDTH),
                        lambda t: (jnp.maximum(tile(t) * halo_per_tile - 1, 0), 0))
    slab = pl.BlockSpec((GROUP_COLS // LANES, tm, LANES), lambda t: (0, tile(t), 0))
    weight_specs = _merge_weight_specs()
    assert len(weight_specs) == N_MERGE_WEIGHTS
    in_specs = ([slab] * 6 + [tok(POOL_WIDTH), halo, tok(MEM_WIDTH),
                pl.BlockSpec((MEM_LEN, 2 * MEM_WIDTH), lambda t: (tile(t) // tj, 0)),
                tok(D_MODEL), tok(D_MODEL)] + weight_specs + [_resident(t.shape) for t in tail])
    xs = tail[3]
    return pl.pallas_call(
        functools.partial(_mix_with_tail_body, tiles_per_seq=tj, steps=steps),
        grid=(steps + 1,),
        in_specs=in_specs,
        out_specs=[tok(D_MODEL), pl.BlockSpec(xs.shape, lambda t: (0, 0))],
        out_shape=[jax.ShapeDtypeStruct((n * s, D_MODEL), F32),
                   jax.ShapeDtypeStruct(xs.shape, F32)],
        compiler_params=_params("arbitrary"),
        name="mix",
    )(*os, *lses, z, z, qm, mkv, x, u, *weights, *tail)


N_SAMPLE_INPUTS = 14


def _sample_branch_math(req, qkv0, qkv1, qkv2, c0, c1, c2, b0, b1, b2, bias0_ref, z_ref, st_ref,
                        qm_ref, cm_ref, a_ref, pooled_ref, c_ref):
    row = pl.ds(req, 1)
    eye = (lax.broadcasted_iota(jnp.int32, (HEAD_DIM, HEAD_DIM), 0)
           == lax.broadcasted_iota(jnp.int32, (HEAD_DIM, HEAD_DIM), 1))
    cube = (HEADS, HEAD_DIM, HEAD_DIM)

    def heads_of(vec, lo):
        return jnp.stack([vec[:, lo + h * HEAD_DIM:lo + (h + 1) * HEAD_DIM]
                          for h in range(HEADS)], axis=0)

    outs, lses = [], []
    for g, (qkv_ref, cache_ref, bias_ref) in enumerate(
            ((qkv0, c0, b0), (qkv1, c1, b1), (qkv2, c2, b2))):
        qkv = qkv_ref[row, :]
        q, kn, vn = heads_of(qkv, 0), heads_of(qkv, GROUP_COLS), heads_of(qkv, 2 * GROUP_COLS)
        q_col = jnp.sum(jnp.where(eye, jnp.broadcast_to(q, cube), 0.0), axis=2, keepdims=True)
        s = jnp.sum(cache_ref[0, 0] * q_col, axis=1, keepdims=True) + bias_ref[...]
        sn = jnp.sum(kn * q, axis=2, keepdims=True) + bias0_ref[g]
        m = jnp.maximum(jnp.max(s, axis=2, keepdims=True), sn)
        p = jnp.exp(s - m)
        pn = jnp.exp(sn - m)
        l = jnp.sum(p, axis=2, keepdims=True) + pn
        pv = jnp.sum(cache_ref[0, 1] * p, axis=2, keepdims=True)
        pv_row = jnp.sum(jnp.where(eye, jnp.broadcast_to(pv, cube), 0.0), axis=1, keepdims=True)
        outs.append((pv_row + pn * vn) / l)
        lses.append(m + jnp.log(l))
    mx = jnp.maximum(jnp.maximum(lses[0], lses[1]), lses[2])
    es = [jnp.exp(lse - mx) for lse in lses]
    a = (es[0] * outs[0] + es[1] * outs[1] + es[2] * outs[2]) / (es[0] + es[1] + es[2])
    a_ref[row, :] = jnp.concatenate([a[h] for h in range(HEADS)], axis=1)

    zn = z_ref[row, :]
    st = st_ref[:, row, :]
    pooled = []
    for gi, kw in enumerate(POOL_WINDOWS):
        cs = slice(gi * POOL_GROUP, (gi + 1) * POOL_GROUP)
        tot = jnp.sum(st[POOL_STATE - (kw - 1):, :, cs], axis=0) + zn[:, cs]
        pooled.append(tot / float(min(kw, PAST_LEN + 1)) - zn[:, cs])
    pooled_ref[row, :] = jnp.concatenate(pooled, axis=1)

    qm_row = qm_ref[row, :]
    qm = jnp.concatenate([qm_row[:, h * MEM_HEAD_DIM:(h + 1) * MEM_HEAD_DIM]
                          for h in range(MEM_HEADS)], axis=0)
    km = cm_ref[0, :, 0]
    vm = cm_ref[0, :, 1]
    s = jnp.sum(km * qm[None], axis=-1, keepdims=True) * (1.0 / math.sqrt(MEM_HEAD_DIM))
    m = jnp.max(s, axis=0)
    p = jnp.exp(s - m[None])
    c = jnp.sum(p * vm, axis=0) / jnp.sum(p, axis=0)
    c_ref[row, :] = jnp.concatenate([c[h:h + 1] for h in range(MEM_HEADS)], axis=1)


def _sample_branch_specs(operands, request_of):
    assert len(operands) == N_SAMPLE_INPUTS
    nb = operands[0].shape[0]

    def per_request(t):
        zeros = (0,) * (t.ndim - 1)
        return pl.BlockSpec((1,) + t.shape[1:], lambda *g: (request_of(*g),) + zeros)

    def whole(shape):
        zeros = (0,) * len(shape)
        return pl.BlockSpec(shape, lambda *g: zeros)

    streamed = (3, 4, 5, 13)
    in_specs = [per_request(t) if k in streamed else whole(t.shape)
                for k, t in enumerate(operands)]
    widths = (GROUP_COLS, POOL_WIDTH, MEM_WIDTH)
    return (in_specs, [whole((nb, w)) for w in widths],
            [jax.ShapeDtypeStruct((nb, w), F32) for w in widths])


def _rel_bucket(n):
    max_exact = N_BUCKETS // 2
    nf = jnp.maximum(n, 1).astype(F32)
    large = max_exact + (jnp.log(nf / max_exact) / math.log(MAX_DISTANCE / max_exact)
                         * (N_BUCKETS - max_exact)).astype(jnp.int32)
    large = jnp.minimum(large, N_BUCKETS - 1)
    return jnp.where(n < max_exact, n, large)


def _stride_bias(rel_bias, g, dil):
    j = jnp.arange(STRIDES + 1, dtype=jnp.int32)
    return rel_bias[_rel_bucket(j * dil)][:, g * HEADS:(g + 1) * HEADS].astype(F32)


def _band_row(bias_j):
    row = jnp.concatenate([bias_j[::-1], jnp.full((STRIDES - 1, HEADS), NEG_INF, F32)], axis=0)
    return row.T.reshape(HEADS, 1, 2 * STRIDES)


def _cache_bias(bias_j, dil):
    on_grid = bias_j[STRIDES:0:-1].T
    full = jnp.full((HEADS, STRIDES, dil), NEG_INF, F32).at[:, :, 0].set(on_grid)
    return full.reshape(HEADS, 1, STRIDES * dil)


def kernel(x_prompt, x_sample, cache_win0_kv, cache_win1_kv, cache_win2_kv, state_pool, cache_mem_kv, mem_prompt, rel_bias, g_ffn1, w1_gate, w1_up, w1_down, g_mix, w_in, w_pool, pool_scale, g_mem, w_mem_kv, w_oa, w_ob, w_oc, w_out, g_ffn2, w2_gate, w2_up, w2_down, g_final):
    n, s, _ = x_prompt.shape
    nb = x_sample.shape[0]
    depth = g_ffn1.shape[0]
    win_caches = (cache_win0_kv, cache_win1_kv, cache_win2_kv)
    bias_js = [_stride_bias(rel_bias, g, dil) for g, (_, dil) in enumerate(DIL_GROUPS)]
    bands = _bands(jnp.stack([_band_row(b) for b in bias_js]))
    bias_cache = [_cache_bias(b, dil) for b, (_, dil) in zip(bias_js, DIL_GROUPS)]
    bias_new = jnp.stack([b[0] for b in bias_js]).reshape(N_GROUPS, HEADS, 1, 1)
    gfin = g_final.reshape(1, D_MODEL)

    xp = x_prompt.reshape(n * s, D_MODEL)
    xs = x_sample.reshape(nb, D_MODEL)
    st_p = [[] for _ in range(5)]
    st_s = [[] for _ in range(4)]
    for l in range(depth):
        last = l == depth - 1
        vec = lambda v: v[l].reshape(1, -1)
        q_scale = 1.0 / math.sqrt(HEAD_DIM)

        later = [w2_gate[l], w2_up[l], w2_down[l], w_in[l], w_oa[l], w_ob[l], w_oc[l], w_out[l],
                 w_mem_kv[l], w_pool[l].reshape(len(POOL_WINDOWS) * POOL_GROUP, POOL_GROUP)]
        xp, later, xs = _ffn(xp, vec(g_ffn1), w1_gate[l], w1_up[l], w1_down[l], xs, casts=later)
        w2, (win_l, woa_l, wob_l, woc_l, wout_l, wmem_l, wpool_l) = later[0:3], later[3:]
        merge_w = (win_l, wpool_l.reshape(w_pool.shape[1:]), vec(pool_scale),
                   woa_l, wob_l, woc_l, wout_l)

        res = _inproj(xs, vec(g_mix), win_l, 1, nb, None, [1] * N_GROUPS, F32, q_scale)
        s_qkvs, s_z, s_qm, s_u = [t.reshape(nb, QKV_COLS) for t in res[0:3]], *res[3:6]
        rider = (s_qkvs + [jnp.transpose(cw[l], (0, 2, 3, 4, 1)) for cw in win_caches]
                 + bias_cache + [bias_new, s_z, jnp.transpose(state_pool[l], (1, 0, 2)), s_qm,
                                 cache_mem_kv[l]])
        res = _inproj(xp, vec(g_mix), win_l, n, s, [min(w, s) for w, _ in DIL_GROUPS],
                      [d for _, d in DIL_GROUPS], BF16, q_scale * LOG2E, rider)
        qkvs, kvwins, z, qm, u = res[0:3], res[3:6], res[6], res[7], res[8]
        s_branches = res[9:12]
        os, lses = [], []
        for g, (_, dil) in enumerate(DIL_GROUPS):
            o, lse = _attn(qkvs[g], bands, g, n, s, dil)
            os.append(o)
            lses.append(lse)
        mkv = _memkv(mem_prompt.reshape(n * MEM_LEN, D_MODEL), vec(g_mem), wmem_l)
        xp, xs = _mix(os, lses, z, qm, mkv, xp, u, merge_w, n, s, tail=(*s_branches, xs, s_u))
        xp, xs = _ffn(xp, vec(g_ffn2), *w2, xs, g_final=gfin if last else None)
        for g, (win, _) in enumerate(DIL_GROUPS):
            kv_t = kvwins[g].reshape(n, 2, HEADS, HEAD_DIM, min(win, s))
            st_p[g].append(jnp.transpose(kv_t, (0, 4, 1, 2, 3)))
        st_p[3].append(z.reshape(n, s, POOL_WIDTH)[:, s - POOL_STATE:])
        st_p[4].append(mkv.reshape(n, MEM_LEN, 2, MEM_HEADS, MEM_HEAD_DIM))

        for g in range(N_GROUPS):
            st_s[g].append(s_qkvs[g][:, GROUP_COLS:].reshape(nb, 1, 2, HEADS, HEAD_DIM))
        st_s[3].append(s_z.reshape(nb, 1, POOL_WIDTH))

    y_prompt = xp.reshape(n, s, D_MODEL)
    y_sample = xs.reshape(nb, 1, D_MODEL)
    stack = lambda ts: jnp.stack(ts, axis=0)
    return (y_prompt, y_sample, stack(st_p[0]), stack(st_p[1]), stack(st_p[2]), stack(st_p[3]),
            stack(st_p[4]), stack(st_s[0]), stack(st_s[1]), stack(st_s[2]), stack(st_s[3]))
```

```python
import functools
import math

import jax
import jax.numpy as jnp
from jax import lax
from jax.experimental import pallas as pl
from jax.experimental.pallas import tpu as pltpu

F32 = jnp.float32
BF16 = jnp.bfloat16

D_MODEL = 1024
D_FF = 2816
LANES = 128
HEAD_DIM = 64
HEADS = 4
DIL_GROUPS = ((128, 1), (512, 4), (2048, 16))
N_GROUPS = 3
GROUP_COLS = HEADS * HEAD_DIM
QKV_COLS = 3 * GROUP_COLS
STRIDES = 128
POOL_WINDOWS = (2, 4, 8, 16)
POOL_GROUP = 128
POOL_WIDTH = 512
POOL_STATE = 15
POOL_HALO = 16
MEM_LEN = 256
MEM_HEADS = 4
MEM_HEAD_DIM = 128
MEM_WIDTH = 512
N_BUCKETS = 32
MAX_DISTANCE = 2048
N_BRANCH = 3
EPS = 1e-6
NEG_INF = -1e30
LOG2E = math.log2(math.e)
LN2 = math.log(2.0)
PAST_LEN = 8192

V7X_VMEM_LIMIT_BYTES = 56 * 1024 * 1024
TOKEN_TILE = 512


def _params(*sem):
    return pltpu.CompilerParams(dimension_semantics=sem,
                                vmem_limit_bytes=V7X_VMEM_LIMIT_BYTES)


def _resident(shape):
    zeros = (0,) * len(shape)
    return pl.BlockSpec(shape, lambda *_: zeros, pipeline_mode=pl.Buffered(1))


def _rms(x, g):
    return x * lax.rsqrt(jnp.mean(x * x, axis=-1, keepdims=True) + EPS) * g


def _dot(a, b):
    return jnp.dot(a, b, preferred_element_type=F32)


def _dot_t(a, b):
    return lax.dot_general(a, b, (((1,), (1,)), ((), ())), preferred_element_type=F32)


BF16_SUBLANES = 16


FF_CHUNK = 256
FF_CHUNKS = D_FF // FF_CHUNK


def _ffn_math(x, g_ref, chunk, gf_ref):
    h = _rms(x, g_ref[...]).astype(BF16)
    acc = None
    for c in range(FF_CHUNKS):
        wg, wu, wd = chunk(c)
        a = _dot(h, wg)
        b = _dot(h, wu)
        part = _dot((a * jax.nn.sigmoid(a) * b).astype(BF16), wd)
        acc = part if acc is None else acc + part
    y = x + 0.5 * acc
    return y if gf_ref is None else _rms(y, gf_ref[...])


def _ffn_body(*refs, final, n_casts, steps, own_weights):
    refs = iter(refs)
    x_ref, x_next_ref, g_ref, wg_ref, wu_ref, wd_ref = (next(refs) for _ in range(6))
    gf_ref = next(refs) if final else None
    cast_in = [next(refs) for _ in range(n_casts)]
    tail_in = next(refs)
    o_ref = next(refs)
    cast_out = [next(refs) for _ in range(n_casts)]
    tail_out = next(refs)
    h_bufs = (next(refs), next(refs))
    a_s, b_s = next(refs), next(refs)
    t = pl.program_id(0)

    def convert_slices():
        for src, dst in zip(cast_in, cast_out):
            dst[...] = src[...].astype(BF16)

    def normalise(x_src_ref, h_ref):
        h = _rms(x_src_ref[...], g_ref[...]).astype(BF16)
        h_ref[...] = h
        return h

    def prime(x_src_ref, h_ref):
        h = normalise(x_src_ref, h_ref)
        wg, wu, _ = chunk(0)
        a_s[...] = _dot(h, wg)
        b_s[...] = _dot(h, wu)

    if own_weights:
        wg_s, wu_s, wd_s, h_s, acc_s = (next(refs) for _ in range(5))
        base = FF_CHUNKS - 1
        full_from = base + 1
        chunk = lambda c: (wg_s[c], wu_s[c], wd_s[c])

        @pl.when(t <= base)
        def _():
            wg_s[t] = wg_ref[...].astype(BF16)
            wu_s[t] = wu_ref[...].astype(BF16)
            wd_s[t] = wd_ref[...].astype(BF16)

            @pl.when(t == 0)
            def _():
                h_s[...] = _rms(x_ref[...], g_ref[...]).astype(BF16)
                acc_s[...] = jnp.zeros_like(acc_s)

            h = h_s[...]
            a = _dot(h, wg_s[t])
            b = _dot(h, wu_s[t])
            acc_s[...] += _dot((a * jax.nn.sigmoid(a) * b).astype(BF16), wd_s[t])

            @pl.when(t == base)
            def _():
                y = x_ref[...] + 0.5 * acc_s[...]
                o_ref[...] = y if gf_ref is None else _rms(y, gf_ref[...])
                convert_slices()
                prime(x_next_ref, h_bufs[full_from % 2])
    else:
        base = full_from = 0
        cols = lambda c: slice(c * FF_CHUNK, (c + 1) * FF_CHUNK)
        chunk = lambda c: (wg_ref[:, cols(c)], wu_ref[:, cols(c)], wd_ref[cols(c), :])

        @pl.when(t == 0)
        def _():
            prime(x_ref, h_bufs[0])

    def full_step(h_ref, h_next_ref):
        h = h_ref[...]
        h_next = normalise(x_next_ref, h_next_ref)
        a, b = a_s[...], b_s[...]
        acc = None
        for c in range(FF_CHUNKS):
            ahead_h, ahead_c = (h, c + 1) if c + 1 < FF_CHUNKS else (h_next, 0)
            wg, wu, _ = chunk(ahead_c)
            a_ahead, b_ahead = _dot(ahead_h, wg), _dot(ahead_h, wu)
            part = _dot((a * jax.nn.sigmoid(a) * b).astype(BF16), chunk(c)[2])
            acc = part if acc is None else acc + part
            a, b = a_ahead, b_ahead
        a_s[...] = a
        b_s[...] = b
        y = x_ref[...] + 0.5 * acc
        o_ref[...] = y if gf_ref is None else _rms(y, gf_ref[...])
        convert_slices()

    in_full = jnp.logical_and(t >= full_from, t < base + steps)
    for parity in range(2):
        pl.when(jnp.logical_and(in_full, t % 2 == parity))(
            functools.partial(full_step, h_bufs[parity], h_bufs[1 - parity]))

    @pl.when(t == base + steps)
    def _():
        tail_out[...] = _ffn_math(tail_in[...], g_ref, chunk, gf_ref)


def _ffn(x, g, wg, wu, wd, tail, g_final=None, casts=()):
    m = x.shape[0]
    tm = min(TOKEN_TILE, m)
    steps = m // tm
    final = g_final is not None
    own_weights = wg.dtype == F32
    base = FF_CHUNKS - 1 if own_weights else 0
    last = steps - 1
    tile = lambda t: jnp.clip(t - base, 0, last)
    row = pl.BlockSpec((tm, D_MODEL), lambda t: (tile(t), 0))
    next_row = pl.BlockSpec((tm, D_MODEL), lambda t: (tile(t + 1), 0))
    look_ahead = [pltpu.VMEM((tm, D_MODEL), BF16), pltpu.VMEM((tm, D_MODEL), BF16),
                  pltpu.VMEM((tm, FF_CHUNK), F32), pltpu.VMEM((tm, FF_CHUNK), F32)]
    if own_weights:
        load = lambda t: jnp.minimum(t, FF_CHUNKS - 1)
        weight_specs = [pl.BlockSpec((D_MODEL, FF_CHUNK), lambda t: (0, load(t))),
                        pl.BlockSpec((D_MODEL, FF_CHUNK), lambda t: (0, load(t))),
                        pl.BlockSpec((FF_CHUNK, D_MODEL), lambda t: (load(t), 0))]
        scratch = [pltpu.VMEM((FF_CHUNKS, D_MODEL, FF_CHUNK), BF16),
                   pltpu.VMEM((FF_CHUNKS, D_MODEL, FF_CHUNK), BF16),
                   pltpu.VMEM((FF_CHUNKS, FF_CHUNK, D_MODEL), BF16),
                   pltpu.VMEM((tm, D_MODEL), BF16),
                   pltpu.VMEM((tm, D_MODEL), F32)]
    else:
        weight_specs = [_resident((D_MODEL, D_FF)), _resident((D_MODEL, D_FF)),
                        _resident((D_FF, D_MODEL))]
        scratch = []
    in_specs = [row, next_row, _resident((1, D_MODEL))] + weight_specs
    args = [x, x, g, wg, wu, wd]
    if final:
        in_specs.append(_resident((1, D_MODEL)))
        args.append(g_final)
    cast_specs = []
    for w in casts:
        blocks = math.gcd(steps, w.shape[0] // BF16_SUBLANES)
        per = steps // blocks
        cast_specs.append(pl.BlockSpec((w.shape[0] // blocks, w.shape[1]),
                                       lambda t, per=per: (tile(t) // per, 0)))
    n_casts = len(casts)
    out = pl.pallas_call(
        functools.partial(_ffn_body, final=final, n_casts=n_casts, steps=steps,
                          own_weights=own_weights),
        grid=(base + steps + 1,),
        in_specs=in_specs + cast_specs + [_resident(tail.shape)],
        out_specs=[row] + cast_specs + [pl.BlockSpec(tail.shape, lambda t: (0, 0))],
        out_shape=[jax.ShapeDtypeStruct((m, D_MODEL), F32)]
        + [jax.ShapeDtypeStruct(w.shape, BF16) for w in casts]
        + [jax.ShapeDtypeStruct(tail.shape, F32)],
        scratch_shapes=look_ahead + scratch,
        compiler_params=_params("arbitrary"),
        name="ffn_final" if final else "ffn",
    )(*args, *casts, tail)
    return (out[0], out[1:1 + n_casts], out[1 + n_casts]) if casts else (out[0], out[1])


Z_OFF = N_GROUPS * QKV_COLS
QM_OFF = Z_OFF + POOL_WIDTH
GATE_OFF = QM_OFF + MEM_WIDTH
IN_COLS = GATE_OFF + N_BRANCH * D_MODEL


def _inproj_body(x_ref, g_ref, w_ref, *rest, q_scale, with_windows, under_matmul=None):
    qkv_refs = rest[0:N_GROUPS]
    kv_refs = rest[N_GROUPS:2 * N_GROUPS] if with_windows else (None,) * N_GROUPS
    z_ref, qm_ref, u_ref, p_scr = rest[-4:]
    u = _rms(x_ref[...], g_ref[...]).astype(BF16)
    u_ref[...] = u
    tm = u.shape[0]
    width = N_GROUPS * GROUP_COLS
    p = _dot(u, w_ref[...])
    if under_matmul is not None:
        under_matmul()
    for g, (qkv_ref, kv_ref) in enumerate(zip(qkv_refs, kv_refs)):
        dil, per_class = qkv_ref.shape[1], qkv_ref.shape[2]
        q, k, v = [p[:, t * width + g * GROUP_COLS:t * width + (g + 1) * GROUP_COLS]
                   for t in range(3)]
        qkv = jnp.concatenate([q * q_scale, k, v], axis=1)
        if dil == 1:
            qkv_ref[0, 0] = qkv.astype(qkv_ref.dtype)
        else:
            for c in range(QKV_COLS // LANES):
                p_scr[c] = qkv[:, c * LANES:(c + 1) * LANES]
            for r in range(dil):
                blk = jnp.concatenate([p_scr[c, pl.ds(r, per_class, stride=dil), :]
                                       for c in range(QKV_COLS // LANES)], axis=1)
                qkv_ref[0, r] = blk.astype(qkv_ref.dtype)
        if with_windows:
            rows = kv_ref.shape[2]
            kv_ref[0] = jnp.concatenate([k[tm - rows:], v[tm - rows:]], axis=1).T
    z_ref[...] = p[:, Z_OFF:QM_OFF]
    qm_ref[...] = p[:, QM_OFF:GATE_OFF].astype(qm_ref.dtype)


def _inproj(x, g, w, n, s, windows, dils, q_dtype, q_scale, rider=()):
    tm = min(TOKEN_TILE, s)
    tj = s // tm

    def tok(cols):
        return pl.BlockSpec((tm, cols), lambda b, j: (b * tj + j, 0))

    qkv_specs = [pl.BlockSpec((1, d, tm // d, QKV_COLS), lambda b, j: (b, 0, j, 0)) for d in dils]
    qkv_shapes = [jax.ShapeDtypeStruct((n, d, s // d, QKV_COLS), q_dtype) for d in dils]

    kv_specs, kv_shapes = [], []
    for win in windows or ():
        rows = min(win, tm)
        assert tm % rows == 0 and win % rows == 0
        first = (s - win) // tm
        if win >= tm:
            spec = pl.BlockSpec((1, 2 * GROUP_COLS, rows),
                                lambda b, j, first=first: (b, 0, jnp.maximum(j - first, 0)))
        else:
            spec = pl.BlockSpec((1, 2 * GROUP_COLS, rows), lambda b, j: (b, 0, 0))
        kv_specs.append(spec)
        kv_shapes.append(jax.ShapeDtypeStruct((n, 2 * GROUP_COLS, win), F32))
    m = n * s
    out_shape = (qkv_shapes + kv_shapes + [
        jax.ShapeDtypeStruct((m, POOL_WIDTH), F32),
        jax.ShapeDtypeStruct((m, MEM_WIDTH), q_dtype),
        jax.ShapeDtypeStruct((m, D_MODEL), BF16)])
    out_specs = qkv_specs + kv_specs + [tok(POOL_WIDTH), tok(MEM_WIDTH), tok(D_MODEL)]
    in_specs = [tok(D_MODEL), _resident((1, D_MODEL)),
                pl.BlockSpec((D_MODEL, GATE_OFF), lambda b, j: (0, 0),
                             pipeline_mode=pl.Buffered(1))]
    body = functools.partial(_inproj_body, q_scale=q_scale, with_windows=bool(windows))
    n_own_out = len(out_specs)
    if rider:
        assert rider[0].shape[0] == n * tj
        r_in, r_out, r_shapes = _sample_branch_specs(rider, lambda b, j: b * tj + j)
        in_specs, out_specs, out_shape = in_specs + r_in, out_specs + r_out, out_shape + r_shapes

        def body(*refs, own=body):
            ins, r_ins = refs[:3], refs[3:3 + N_SAMPLE_INPUTS]
            outs, scratch = refs[3 + N_SAMPLE_INPUTS:-1], refs[-1:]
            own(*ins, *outs[:n_own_out], *scratch, under_matmul=lambda: _sample_branch_math(
                pl.program_id(0) * tj + pl.program_id(1), *r_ins, *outs[n_own_out:]))

    return pl.pallas_call(
        body,
        grid=(n, tj),
        in_specs=in_specs,
        out_specs=out_specs,
        out_shape=out_shape,
        scratch_shapes=[pltpu.VMEM((QKV_COLS // LANES, tm, LANES), F32)],
        compiler_params=_params("arbitrary", "arbitrary"),
        name="inproj",
    )(x, g, w, *rider)


ATTN_CHUNKS = 32
HEAD_LANES = HEADS * STRIDES


def _band_body(row_ref, band_ref):
    for g in range(N_GROUPS):
        for h in range(HEADS):
            rows = jnp.broadcast_to(row_ref[g, h] * LOG2E, (STRIDES, 2 * STRIDES))
            band = pltpu.roll(rows, 0, 1, stride=1, stride_axis=0).T
            band_ref[g, :, h * STRIDES:(h + 1) * STRIDES] = band


def _bands(rows):
    return pl.pallas_call(
        _band_body,
        grid=(1,),
        in_specs=[_resident(rows.shape)],
        out_specs=pl.BlockSpec((N_GROUPS, 2 * STRIDES, HEAD_LANES), lambda i: (0, 0, 0)),
        out_shape=jax.ShapeDtypeStruct((N_GROUPS, 2 * STRIDES, HEAD_LANES), F32),
        compiler_params=_params("arbitrary"),
        name="bands",
    )(rows)


def _attn_body(qkv_ref, band_ref, o_ref, lse_ref, *, dil, chunks):
    lane_head = lax.broadcasted_iota(jnp.int32, (1, GROUP_COLS), 1) // HEAD_DIM

    def keys(i):
        return slice(max(i - 1, 0) * STRIDES, (i + 1) * STRIDES)

    def scores(r, i):
        q = qkv_ref[0, r, i * STRIDES:(i + 1) * STRIDES, 0:GROUP_COLS]
        qm = jnp.concatenate(
            [jnp.where(lane_head == h, q, jnp.zeros_like(q)) for h in range(HEADS)], axis=0)
        return _dot_t(qkv_ref[0, r, keys(i), GROUP_COLS:2 * GROUP_COLS], qm)

    def ones_row(n_keys):
        return jnp.where(lax.broadcasted_iota(jnp.int32, (BF16_SUBLANES, n_keys), 0) == 0,
                         1.0, 0.0).astype(BF16)

    order = [(r, i) for r in range(dil) for i in range(chunks)]
    st_next = scores(0, 0)
    for idx, (r, i) in enumerate(order):
        st = st_next
        if i == 0:
            vt_cls = qkv_ref[0, r, :, 2 * GROUP_COLS:].T
        if idx + 1 < len(order):
            st_next = scores(*order[idx + 1])
        vt2 = vt_cls[:, keys(i)]
        band_lo = STRIDES if i == 0 else 0
        o_parts, lse_parts = [], []
        for h in range(HEADS):
            hl = slice(h * STRIDES, (h + 1) * STRIDES)
            s_h = st[:, hl] + band_ref[band_lo:, hl]
            m = jnp.max(s_h, axis=0, keepdims=True)
            e = jnp.exp2(s_h - m).astype(BF16)
            ot = _dot(jnp.concatenate([vt2[h * HEAD_DIM:(h + 1) * HEAD_DIM],
                                       ones_row(vt2.shape[1])], axis=0), e)
            l = ot[HEAD_DIM:HEAD_DIM + 1]
            o_parts.append(ot[:HEAD_DIM] * (1.0 / l))
            lse_parts.append(jnp.broadcast_to(m * LN2 + jnp.log(l), (HEAD_DIM, STRIDES)))
        rows = pl.ds(i * STRIDES * dil + r, STRIDES, stride=dil)
        o_rows = jnp.concatenate(o_parts, axis=0).T
        lse_rows = jnp.concatenate(lse_parts, axis=0).T
        for c in range(GROUP_COLS // LANES):
            o_ref[c, rows, :] = o_rows[:, c * LANES:(c + 1) * LANES]
            lse_ref[c, rows, :] = lse_rows[:, c * LANES:(c + 1) * LANES]


def _attn(qkv, bands, g, n, s, dil):
    chunks = s // (STRIDES * dil)
    assert chunks * dil == ATTN_CHUNKS
    span = chunks * STRIDES
    slabs = GROUP_COLS // LANES
    out_spec = pl.BlockSpec((slabs, s, LANES), lambda b: (0, b, 0))
    out_sds = jax.ShapeDtypeStruct((slabs, n * s, LANES), F32)
    return pl.pallas_call(
        functools.partial(_attn_body, dil=dil, chunks=chunks),
        grid=(n,),
        in_specs=[pl.BlockSpec((1, dil, span, QKV_COLS), lambda b: (b, 0, 0, 0)),
                  pl.BlockSpec((None, 2 * STRIDES, HEAD_LANES), lambda b: (g, 0, 0))],
        out_specs=[out_spec, out_spec],
        out_shape=[out_sds, out_sds],
        compiler_params=_params("parallel"),
        name="attn_d%d" % dil,
    )(qkv, bands)


def _memkv_body(mem_ref, g_ref, w_ref, o_ref):
    o_ref[...] = _dot(_rms(mem_ref[...], g_ref[...]).astype(BF16), w_ref[...])


def _memkv(mem, g, w):
    m = mem.shape[0]
    row = pl.BlockSpec((MEM_LEN, D_MODEL), lambda i: (i, 0))
    return pl.pallas_call(
        _memkv_body,
        grid=(m // MEM_LEN,),
        in_specs=[row, _resident((1, D_MODEL)), _resident((D_MODEL, 2 * MEM_WIDTH))],
        out_specs=pl.BlockSpec((MEM_LEN, 2 * MEM_WIDTH), lambda i: (i, 0)),
        out_shape=jax.ShapeDtypeStruct((m, 2 * MEM_WIDTH), F32),
        compiler_params=_params("parallel"),
        name="memkv",
    )(mem, g, w)


def _merge_math(a, pooled, c, x, u, win_ref, wpool_ref, scale_ref, woa_ref, wob_ref,
                woc_ref, wout_ref):
    mixed = [_dot(pooled[gi].astype(BF16), wpool_ref[gi]) for gi in range(len(POOL_WINDOWS))]
    b = jnp.concatenate(mixed, axis=1) * scale_ref[...]
    m = None
    for k, (branch, wo_ref) in enumerate(((a, woa_ref), (b, wob_ref), (c, woc_ref))):
        lo = GATE_OFF + k * D_MODEL
        gate = jax.nn.sigmoid(_dot(u, win_ref[:, lo:lo + D_MODEL]))
        term = gate * _dot(branch.astype(BF16), wo_ref[...])
        m = term if m is None else m + term
    return x + _dot(m.astype(BF16), wout_ref[...])


def _mix_body(o0, o1, o2, l0, l1, l2, z_ref, halo_ref, qm_ref, mkv_ref, x_ref, u_ref, *rest,
              tile_in_seq):
    merge_refs, out_ref = rest[:-1], rest[-1]
    j = tile_in_seq
    tm = x_ref.shape[0]
    unslab = lambda ref: jnp.concatenate([ref[c] for c in range(ref.shape[0])], axis=1)
    lses = [unslab(l0), unslab(l1), unslab(l2)]
    mx = jnp.maximum(jnp.maximum(lses[0], lses[1]), lses[2])
    es = [jnp.exp(l - mx) for l in lses]
    a = ((es[0] * unslab(o0) + es[1] * unslab(o1) + es[2] * unslab(o2))
         / (es[0] + es[1] + es[2]))
    z = z_ref[...]
    halo = jnp.where(j == 0, 0.0, halo_ref[...])
    zc = jnp.concatenate([halo, z], axis=0)
    pos = j * tm + lax.broadcasted_iota(jnp.int32, (tm, 1), 0)
    pooled = []
    for gi, kw in enumerate(POOL_WINDOWS):
        cs = slice(gi * POOL_GROUP, (gi + 1) * POOL_GROUP)
        run = zc[:, cs]
        width = 1
        while width < kw:
            run = run[width:] + run[:-width]
            width *= 2
        first = POOL_HALO - (kw - 1)
        cnt = jnp.minimum(kw, pos + 1).astype(F32)
        pooled.append(run[first:first + tm] / cnt - z[:, cs])
    mkv = mkv_ref[...].astype(BF16)
    qm = qm_ref[...]
    cs_out = []
    for h in range(MEM_HEADS):
        hs = slice(h * MEM_HEAD_DIM, (h + 1) * MEM_HEAD_DIM)
        vs = slice(MEM_WIDTH + h * MEM_HEAD_DIM, MEM_WIDTH + (h + 1) * MEM_HEAD_DIM)
        s = _dot_t(qm[:, hs], mkv[:, hs]) * (1.0 / math.sqrt(MEM_HEAD_DIM))
        mm = jnp.max(s, axis=-1, keepdims=True)
        p = jnp.exp(s - mm)
        l = jnp.sum(p, axis=-1, keepdims=True)
        cs_out.append(_dot(p.astype(BF16), mkv[:, vs]) / l)
    c = jnp.concatenate(cs_out, axis=1)
    out_ref[...] = _merge_math(a, pooled, c, x_ref[...], u_ref[...], *merge_refs)


def _merge_weight_specs():
    return [_resident((D_MODEL, IN_COLS)),
            _resident((len(POOL_WINDOWS), POOL_GROUP, POOL_GROUP)), _resident((1, POOL_WIDTH)),
            _resident((GROUP_COLS, D_MODEL)), _resident((POOL_WIDTH, D_MODEL)),
            _resident((MEM_WIDTH, D_MODEL)), _resident((D_MODEL, D_MODEL))]


N_MERGE_WEIGHTS = 7


def _mix_with_tail_body(*refs, tiles_per_seq, steps):
    n_tile_in = 12
    tile_in, merge_refs = refs[:n_tile_in], refs[n_tile_in:n_tile_in + N_MERGE_WEIGHTS]
    (a_ref, pooled_ref, c_ref, xs_ref, us_ref,
     out_ref, tail_out) = refs[n_tile_in + N_MERGE_WEIGHTS:]
    t = pl.program_id(0)

    @pl.when(t < steps)
    def _():
        _mix_body(*tile_in, *merge_refs, out_ref, tile_in_seq=lax.rem(t, tiles_per_seq))

    @pl.when(t == steps)
    def _():
        pooled_all = pooled_ref[...]
        pooled = [pooled_all[:, gi * POOL_GROUP:(gi + 1) * POOL_GROUP]
                  for gi in range(len(POOL_WINDOWS))]
        tail_out[...] = _merge_math(a_ref[...], pooled, c_ref[...], xs_ref[...], us_ref[...],
                                    *merge_refs)


def _mix(os, lses, z, qm, mkv, x, u, weights, n, s, tail):
    tm = TOKEN_TILE
    tj = s // tm
    steps = n * tj
    last = steps - 1
    halo_per_tile = tm // POOL_HALO
    tile = lambda t: jnp.minimum(t, last)

    def tok(cols):
        return pl.BlockSpec((tm, cols), lambda t: (tile(t), 0))

    halo = pl.BlockSpec((POOL_HALO, POOL_WIDTH),
                        lambda t: (jnp.maximum(tile(t) * halo_per_tile - 1, 0), 0))
    slab = pl.BlockSpec((GROUP_COLS // LANES, tm, LANES), lambda t: (0, tile(t), 0))
    weight_specs = _merge_weight_specs()
    assert len(weight_specs) == N_MERGE_WEIGHTS
    in_specs = ([slab] * 6 + [tok(POOL_WIDTH), halo, tok(MEM_WIDTH),
                pl.BlockSpec((MEM_LEN, 2 * MEM_WIDTH), lambda t: (tile(t) // tj, 0)),
                tok(D_MODEL), tok(D_MODEL)] + weight_specs + [_resident(t.shape) for t in tail])
    xs = tail[3]
    return pl.pallas_call(
        functools.partial(_mix_with_tail_body, tiles_per_seq=tj, steps=steps),
        grid=(steps + 1,),
        in_specs=in_specs,
        out_specs=[tok(D_MODEL), pl.BlockSpec(xs.shape, lambda t: (0, 0))],
        out_shape=[jax.ShapeDtypeStruct((n * s, D_MODEL), F32),
                   jax.ShapeDtypeStruct(xs.shape, F32)],
        compiler_params=_params("arbitrary"),
        name="mix",
    )(*os, *lses, z, z, qm, mkv, x, u, *weights, *tail)


N_SAMPLE_INPUTS = 14


def _sample_branch_math(req, qkv0, qkv1, qkv2, c0, c1, c2, b0, b1, b2, bias0_ref, z_ref, st_ref,
                        qm_ref, cm_ref, a_ref, pooled_ref, c_ref):
    row = pl.ds(req, 1)
    eye = (lax.broadcasted_iota(jnp.int32, (HEAD_DIM, HEAD_DIM), 0)
           == lax.broadcasted_iota(jnp.int32, (HEAD_DIM, HEAD_DIM), 1))
    cube = (HEADS, HEAD_DIM, HEAD_DIM)

    def heads_of(vec, lo):
        return jnp.stack([vec[:, lo + h * HEAD_DIM:lo + (h + 1) * HEAD_DIM]
                          for h in range(HEADS)], axis=0)

    outs, lses = [], []
    for g, (qkv_ref, cache_ref, bias_ref) in enumerate(
            ((qkv0, c0, b0), (qkv1, c1, b1), (qkv2, c2, b2))):
        qkv = qkv_ref[row, :]
        q, kn, vn = heads_of(qkv, 0), heads_of(qkv, GROUP_COLS), heads_of(qkv, 2 * GROUP_COLS)
        q_col = jnp.sum(jnp.where(eye, jnp.broadcast_to(q, cube), 0.0), axis=2, keepdims=True)
        s = jnp.sum(cache_ref[0, 0] * q_col, axis=1, keepdims=True) + bias_ref[...]
        sn = jnp.sum(kn * q, axis=2, keepdims=True) + bias0_ref[g]
        m = jnp.maximum(jnp.max(s, axis=2, keepdims=True), sn)
        p = jnp.exp(s - m)
        pn = jnp.exp(sn - m)
        l = jnp.sum(p, axis=2, keepdims=True) + pn
        pv = jnp.sum(cache_ref[0, 1] * p, axis=2, keepdims=True)
        pv_row = jnp.sum(jnp.where(eye, jnp.broadcast_to(pv, cube), 0.0), axis=1, keepdims=True)
        outs.append((pv_row + pn * vn) / l)
        lses.append(m + jnp.log(l))
    mx = jnp.maximum(jnp.maximum(lses[0], lses[1]), lses[2])
    es = [jnp.exp(lse - mx) for lse in lses]
    a = (es[0] * outs[0] + es[1] * outs[1] + es[2] * outs[2]) / (es[0] + es[1] + es[2])
    a_ref[row, :] = jnp.concatenate([a[h] for h in range(HEADS)], axis=1)

    zn = z_ref[row, :]
    st = st_ref[:, row, :]
    pooled = []
    for gi, kw in enumerate(POOL_WINDOWS):
        cs = slice(gi * POOL_GROUP, (gi + 1) * POOL_GROUP)
        tot = jnp.sum(st[POOL_STATE - (kw - 1):, :, cs], axis=0) + zn[:, cs]
        pooled.append(tot / float(min(kw, PAST_LEN + 1)) - zn[:, cs])
    pooled_ref[row, :] = jnp.concatenate(pooled, axis=1)

    qm_row = qm_ref[row, :]
    qm = jnp.concatenate([qm_row[:, h * MEM_HEAD_DIM:(h + 1) * MEM_HEAD_DIM]
                          for h in range(MEM_HEADS)], axis=0)
    km = cm_ref[0, :, 0]
    vm = cm_ref[0, :, 1]
    s = jnp.sum(km * qm[None], axis=-1, keepdims=True) * (1.0 / math.sqrt(MEM_HEAD_DIM))
    m = jnp.max(s, axis=0)
    p = jnp.exp(s - m[None])
    c = jnp.sum(p * vm, axis=0) / jnp.sum(p, axis=0)
    c_ref[row, :] = jnp.concatenate([c[h:h + 1] for h in range(MEM_HEADS)], axis=1)


def _sample_branch_specs(operands, request_of):
    assert len(operands) == N_SAMPLE_INPUTS
    nb = operands[0].shape[0]

    def per_request(t):
        zeros = (0,) * (t.ndim - 1)
        return pl.BlockSpec((1,) + t.shape[1:], lambda *g: (request_of(*g),) + zeros)

    def whole(shape):
        zeros = (0,) * len(shape)
        return pl.BlockSpec(shape, lambda *g: zeros)

    streamed = (3, 4, 5, 13)
    in_specs = [per_request(t) if k in streamed else whole(t.shape)
                for k, t in enumerate(operands)]
    widths = (GROUP_COLS, POOL_WIDTH, MEM_WIDTH)
    return (in_specs, [whole((nb, w)) for w in widths],
            [jax.ShapeDtypeStruct((nb, w), F32) for w in widths])


def _rel_bucket(n):
    max_exact = N_BUCKETS // 2
    nf = jnp.maximum(n, 1).astype(F32)
    large = max_exact + (jnp.log(nf / max_exact) / math.log(MAX_DISTANCE / max_exact)
                         * (N_BUCKETS - max_exact)).astype(jnp.int32)
    large = jnp.minimum(large, N_BUCKETS - 1)
    return jnp.where(n < max_exact, n, large)


def _stride_bias(rel_bias, g, dil):
    j = jnp.arange(STRIDES + 1, dtype=jnp.int32)
    return rel_bias[_rel_bucket(j * dil)][:, g * HEADS:(g + 1) * HEADS].astype(F32)


def _band_row(bias_j):
    row = jnp.concatenate([bias_j[::-1], jnp.full((STRIDES - 1, HEADS), NEG_INF, F32)], axis=0)
    return row.T.reshape(HEADS, 1, 2 * STRIDES)


def _cache_bias(bias_j, dil):
    on_grid = bias_j[STRIDES:0:-1].T
    full = jnp.full((HEADS, STRIDES, dil), NEG_INF, F32).at[:, :, 0].set(on_grid)
    return full.reshape(HEADS, 1, STRIDES * dil)


def kernel(x_prompt, x_sample, cache_win0_kv, cache_win1_kv, cache_win2_kv, state_pool, cache_mem_kv, mem_prompt, rel_bias, g_ffn1, w1_gate, w1_up, w1_down, g_mix, w_in, w_pool, pool_scale, g_mem, w_mem_kv, w_oa, w_ob, w_oc, w_out, g_ffn2, w2_gate, w2_up, w2_down, g_final):
    n, s, _ = x_prompt.shape
    nb = x_sample.shape[0]
    depth = g_ffn1.shape[0]
    win_caches = (cache_win0_kv, cache_win1_kv, cache_win2_kv)
    bias_js = [_stride_bias(rel_bias, g, dil) for g, (_, dil) in enumerate(DIL_GROUPS)]
    bands = _bands(jnp.stack([_band_row(b) for b in bias_js]))
    bias_cache = [_cache_bias(b, dil) for b, (_, dil) in zip(bias_js, DIL_GROUPS)]
    bias_new = jnp.stack([b[0] for b in bias_js]).reshape(N_GROUPS, HEADS, 1, 1)
    gfin = g_final.reshape(1, D_MODEL)

    xp = x_prompt.reshape(n * s, D_MODEL)
    xs = x_sample.reshape(nb, D_MODEL)
    st_p = [[] for _ in range(5)]
    st_s = [[] for _ in range(4)]
    for l in range(depth):
        last = l == depth - 1
        vec = lambda v: v[l].reshape(1, -1)
        q_scale = 1.0 / math.sqrt(HEAD_DIM)

        later = [w2_gate[l], w2_up[l], w2_down[l], w_in[l], w_oa[l], w_ob[l], w_oc[l], w_out[l],
                 w_mem_kv[l], w_pool[l].reshape(len(POOL_WINDOWS) * POOL_GROUP, POOL_GROUP)]
        xp, later, xs = _ffn(xp, vec(g_ffn1), w1_gate[l], w1_up[l], w1_down[l], xs, casts=later)
        w2, (win_l, woa_l, wob_l, woc_l, wout_l, wmem_l, wpool_l) = later[0:3], later[3:]
        merge_w = (win_l, wpool_l.reshape(w_pool.shape[1:]), vec(pool_scale),
                   woa_l, wob_l, woc_l, wout_l)

        res = _inproj(xs, vec(g_mix), win_l, 1, nb, None, [1] * N_GROUPS, F32, q_scale)
        s_qkvs, s_z, s_qm, s_u = [t.reshape(nb, QKV_COLS) for t in res[0:3]], *res[3:6]
        rider = (s_qkvs + [jnp.transpose(cw[l], (0, 2, 3, 4, 1)) for cw in win_caches]
                 + bias_cache + [bias_new, s_z, jnp.transpose(state_pool[l], (1, 0, 2)), s_qm,
                                 cache_mem_kv[l]])
        res = _inproj(xp, vec(g_mix), win_l, n, s, [min(w, s) for w, _ in DIL_GROUPS],
                      [d for _, d in DIL_GROUPS], BF16, q_scale * LOG2E, rider)
        qkvs, kvwins, z, qm, u = res[0:3], res[3:6], res[6], res[7], res[8]
        s_branches = res[9:12]
        os, lses = [], []
        for g, (_, dil) in enumerate(DIL_GROUPS):
            o, lse = _attn(qkvs[g], bands, g, n, s, dil)
            os.append(o)
            lses.append(lse)
        mkv = _memkv(mem_prompt.reshape(n * MEM_LEN, D_MODEL), vec(g_mem), wmem_l)
        xp, xs = _mix(os, lses, z, qm, mkv, xp, u, merge_w, n, s, tail=(*s_branches, xs, s_u))
        xp, xs = _ffn(xp, vec(g_ffn2), *w2, xs, g_final=gfin if last else None)
        for g, (win, _) in enumerate(DIL_GROUPS):
            kv_t = kvwins[g].reshape(n, 2, HEADS, HEAD_DIM, min(win, s))
            st_p[g].append(jnp.transpose(kv_t, (0, 4, 1, 2, 3)))
        st_p[3].append(z.reshape(n, s, POOL_WIDTH)[:, s - POOL_STATE:])
        st_p[4].append(mkv.reshape(n, MEM_LEN, 2, MEM_HEADS, MEM_HEAD_DIM))

        for g in range(N_GROUPS):
            st_s[g].append(s_qkvs[g][:, GROUP_COLS:].reshape(nb, 1, 2, HEADS, HEAD_DIM))
        st_s[3].append(s_z.reshape(nb, 1, POOL_WIDTH))

    y_prompt = xp.reshape(n, s, D_MODEL)
    y_sample = xs.reshape(nb, 1, D_MODEL)
    stack = lambda ts: jnp.stack(ts, axis=0)
    return (y_prompt, y_sample, stack(st_p[0]), stack(st_p[1]), stack(st_p[2]), stack(st_p[3]),
            stack(st_p[4]), stack(st_s[0]), stack(st_s[1]), stack(st_s[2]), stack(st_s[3]))
```

```python
import functools
import math

import jax
import jax.numpy as jnp
from jax import lax
from jax.experimental import pallas as pl
from jax.experimental.pallas import tpu as pltpu

F32 = jnp.float32
BF16 = jnp.bfloat16

D_MODEL = 1024
D_FF = 2816
LANES = 128
HEAD_DIM = 64
HEADS = 4
DIL_GROUPS = ((128, 1), (512, 4), (2048, 16))
N_GROUPS = 3
GROUP_COLS = HEADS * HEAD_DIM
QKV_COLS = 3 * GROUP_COLS
STRIDES = 128
POOL_WINDOWS = (2, 4, 8, 16)
POOL_GROUP = 128
POOL_WIDTH = 512
POOL_STATE = 15
POOL_HALO = 16
MEM_LEN = 256
MEM_HEADS = 4
MEM_HEAD_DIM = 128
MEM_WIDTH = 512
N_BUCKETS = 32
MAX_DISTANCE = 2048
N_BRANCH = 3
EPS = 1e-6
NEG_INF = -1e30
LOG2E = math.log2(math.e)
LN2 = math.log(2.0)
PAST_LEN = 8192

V7X_VMEM_LIMIT_BYTES = 56 * 1024 * 1024
TOKEN_TILE = 512


def _params(*sem):
    return pltpu.CompilerParams(dimension_semantics=sem,
                                vmem_limit_bytes=V7X_VMEM_LIMIT_BYTES)


def _resident(shape):
    zeros = (0,) * len(shape)
    return pl.BlockSpec(shape, lambda *_: zeros, pipeline_mode=pl.Buffered(1))


def _rms(x, g):
    return x * lax.rsqrt(jnp.mean(x * x, axis=-1, keepdims=True) + EPS) * g


def _dot(a, b):
    return jnp.dot(a, b, preferred_element_type=F32)


def _dot_t(a, b):
    return lax.dot_general(a, b, (((1,), (1,)), ((), ())), preferred_element_type=F32)


BF16_SUBLANES = 16


FF_CHUNK = 256
FF_CHUNKS = D_FF // FF_CHUNK


def _ffn_math(x, g_ref, chunk, gf_ref):
    h = _rms(x, g_ref[...]).astype(BF16)
    acc = None
    for c in range(FF_CHUNKS):
        wg, wu, wd = chunk(c)
        a = _dot(h, wg)
        b = _dot(h, wu)
        part = _dot((a * jax.nn.sigmoid(a) * b).astype(BF16), wd)
        acc = part if acc is None else acc + part
    y = x + 0.5 * acc
    return y if gf_ref is None else _rms(y, gf_ref[...])


def _ffn_body(*refs, final, n_casts, steps, own_weights):
    refs = iter(refs)
    x_ref, g_ref, wg_ref, wu_ref, wd_ref = (next(refs) for _ in range(5))
    gf_ref = next(refs) if final else None
    cast_in = [next(refs) for _ in range(n_casts)]
    tail_in = next(refs)
    o_ref = next(refs)
    cast_out = [next(refs) for _ in range(n_casts)]
    tail_out = next(refs)
    t = pl.program_id(0)

    def convert_slices():
        for src, dst in zip(cast_in, cast_out):
            dst[...] = src[...].astype(BF16)

    if own_weights:
        wg_s, wu_s, wd_s, h_s, acc_s = (next(refs) for _ in range(5))
        base = FF_CHUNKS - 1
        full_from = base + 1

        @pl.when(t <= base)
        def _():
            wg_s[t] = wg_ref[...].astype(BF16)
            wu_s[t] = wu_ref[...].astype(BF16)
            wd_s[t] = wd_ref[...].astype(BF16)

            @pl.when(t == 0)
            def _():
                h_s[...] = _rms(x_ref[...], g_ref[...]).astype(BF16)
                acc_s[...] = jnp.zeros_like(acc_s)

            h = h_s[...]
            a = _dot(h, wg_s[t])
            b = _dot(h, wu_s[t])
            acc_s[...] += _dot((a * jax.nn.sigmoid(a) * b).astype(BF16), wd_s[t])

            @pl.when(t == base)
            def _():
                y = x_ref[...] + 0.5 * acc_s[...]
                o_ref[...] = y if gf_ref is None else _rms(y, gf_ref[...])
                convert_slices()

        chunk = lambda c: (wg_s[c], wu_s[c], wd_s[c])
    else:
        base = full_from = 0
        cols = lambda c: slice(c * FF_CHUNK, (c + 1) * FF_CHUNK)
        chunk = lambda c: (wg_ref[:, cols(c)], wu_ref[:, cols(c)], wd_ref[cols(c), :])

    last_step = base + steps - 1

    @pl.when(jnp.logical_and(t >= full_from, t < last_step))
    def _():
        o_ref[...] = _ffn_math(x_ref[...], g_ref, chunk, gf_ref)
        convert_slices()

    @pl.when(t == last_step)
    def _():
        rows = x_ref.shape[0]
        both = jnp.concatenate([x_ref[...], tail_in[...]], axis=0)
        y = _ffn_math(both, g_ref, chunk, gf_ref)
        o_ref[...] = y[:rows]
        tail_out[...] = y[rows:]
        convert_slices()


def _ffn(x, g, wg, wu, wd, tail, g_final=None, casts=()):
    m = x.shape[0]
    tm = min(TOKEN_TILE, m)
    steps = m // tm
    final = g_final is not None
    own_weights = wg.dtype == F32
    assert steps > 1 or not own_weights
    base = FF_CHUNKS - 1 if own_weights else 0
    last = steps - 1
    tile = lambda t: jnp.clip(t - base, 0, last)
    row = pl.BlockSpec((tm, D_MODEL), lambda t: (tile(t), 0))
    if own_weights:
        load = lambda t: jnp.minimum(t, FF_CHUNKS - 1)
        weight_specs = [pl.BlockSpec((D_MODEL, FF_CHUNK), lambda t: (0, load(t))),
                        pl.BlockSpec((D_MODEL, FF_CHUNK), lambda t: (0, load(t))),
                        pl.BlockSpec((FF_CHUNK, D_MODEL), lambda t: (load(t), 0))]
        scratch = [pltpu.VMEM((FF_CHUNKS, D_MODEL, FF_CHUNK), BF16),
                   pltpu.VMEM((FF_CHUNKS, D_MODEL, FF_CHUNK), BF16),
                   pltpu.VMEM((FF_CHUNKS, FF_CHUNK, D_MODEL), BF16),
                   pltpu.VMEM((tm, D_MODEL), BF16),
                   pltpu.VMEM((tm, D_MODEL), F32)]
    else:
        weight_specs = [_resident((D_MODEL, D_FF)), _resident((D_MODEL, D_FF)),
                        _resident((D_FF, D_MODEL))]
        scratch = []
    in_specs = [row, _resident((1, D_MODEL))] + weight_specs
    args = [x, g, wg, wu, wd]
    if final:
        in_specs.append(_resident((1, D_MODEL)))
        args.append(g_final)
    cast_specs = []
    for w in casts:
        blocks = math.gcd(steps, w.shape[0] // BF16_SUBLANES)
        per = steps // blocks
        cast_specs.append(pl.BlockSpec((w.shape[0] // blocks, w.shape[1]),
                                       lambda t, per=per: (tile(t) // per, 0)))
    n_casts = len(casts)
    out = pl.pallas_call(
        functools.partial(_ffn_body, final=final, n_casts=n_casts, steps=steps,
                          own_weights=own_weights),
        grid=(base + steps,),
        in_specs=in_specs + cast_specs + [_resident(tail.shape)],
        out_specs=[row] + cast_specs + [pl.BlockSpec(tail.shape, lambda t: (0, 0))],
        out_shape=[jax.ShapeDtypeStruct((m, D_MODEL), F32)]
        + [jax.ShapeDtypeStruct(w.shape, BF16) for w in casts]
        + [jax.ShapeDtypeStruct(tail.shape, F32)],
        scratch_shapes=scratch,
        compiler_params=_params("arbitrary"),
        name="ffn_final" if final else "ffn",
    )(*args, *casts, tail)
    return (out[0], out[1:1 + n_casts], out[1 + n_casts]) if casts else (out[0], out[1])


Z_OFF = N_GROUPS * QKV_COLS
QM_OFF = Z_OFF + POOL_WIDTH
GATE_OFF = QM_OFF + MEM_WIDTH
IN_COLS = GATE_OFF + N_BRANCH * D_MODEL


def _inproj_body(x_ref, g_ref, w_ref, *rest, q_scale, with_windows, under_matmul=None):
    qkv_refs = rest[0:N_GROUPS]
    kv_refs = rest[N_GROUPS:2 * N_GROUPS] if with_windows else (None,) * N_GROUPS
    z_ref, qm_ref, u_ref, p_scr = rest[-4:]
    u = _rms(x_ref[...], g_ref[...]).astype(BF16)
    u_ref[...] = u
    tm = u.shape[0]
    width = N_GROUPS * GROUP_COLS
    p = _dot(u, w_ref[...])
    if under_matmul is not None:
        under_matmul()
    for g, (qkv_ref, kv_ref) in enumerate(zip(qkv_refs, kv_refs)):
        dil, per_class = qkv_ref.shape[1], qkv_ref.shape[2]
        q, k, v = [p[:, t * width + g * GROUP_COLS:t * width + (g + 1) * GROUP_COLS]
                   for t in range(3)]
        qkv = jnp.concatenate([q * q_scale, k, v], axis=1)
        if dil == 1:
            qkv_ref[0, 0] = qkv.astype(qkv_ref.dtype)
        else:
            for c in range(QKV_COLS // LANES):
                p_scr[c] = qkv[:, c * LANES:(c + 1) * LANES]
            for r in range(dil):
                blk = jnp.concatenate([p_scr[c, pl.ds(r, per_class, stride=dil), :]
                                       for c in range(QKV_COLS // LANES)], axis=1)
                qkv_ref[0, r] = blk.astype(qkv_ref.dtype)
        if with_windows:
            rows = kv_ref.shape[2]
            kv_ref[0] = jnp.concatenate([k[tm - rows:], v[tm - rows:]], axis=1).T
    z_ref[...] = p[:, Z_OFF:QM_OFF]
    qm_ref[...] = p[:, QM_OFF:GATE_OFF].astype(qm_ref.dtype)


def _inproj(x, g, w, n, s, windows, dils, q_dtype, q_scale, rider=()):
    tm = min(TOKEN_TILE, s)
    tj = s // tm

    def tok(cols):
        return pl.BlockSpec((tm, cols), lambda b, j: (b * tj + j, 0))

    qkv_specs = [pl.BlockSpec((1, d, tm // d, QKV_COLS), lambda b, j: (b, 0, j, 0)) for d in dils]
    qkv_shapes = [jax.ShapeDtypeStruct((n, d, s // d, QKV_COLS), q_dtype) for d in dils]

    kv_specs, kv_shapes = [], []
    for win in windows or ():
        rows = min(win, tm)
        assert tm % rows == 0 and win % rows == 0
        first = (s - win) // tm
        if win >= tm:
            spec = pl.BlockSpec((1, 2 * GROUP_COLS, rows),
                                lambda b, j, first=first: (b, 0, jnp.maximum(j - first, 0)))
        else:
            spec = pl.BlockSpec((1, 2 * GROUP_COLS, rows), lambda b, j: (b, 0, 0))
        kv_specs.append(spec)
        kv_shapes.append(jax.ShapeDtypeStruct((n, 2 * GROUP_COLS, win), F32))
    m = n * s
    out_shape = (qkv_shapes + kv_shapes + [
        jax.ShapeDtypeStruct((m, POOL_WIDTH), F32),
        jax.ShapeDtypeStruct((m, MEM_WIDTH), q_dtype),
        jax.ShapeDtypeStruct((m, D_MODEL), BF16)])
    out_specs = qkv_specs + kv_specs + [tok(POOL_WIDTH), tok(MEM_WIDTH), tok(D_MODEL)]
    in_specs = [tok(D_MODEL), _resident((1, D_MODEL)),
                pl.BlockSpec((D_MODEL, GATE_OFF), lambda b, j: (0, 0),
                             pipeline_mode=pl.Buffered(1))]
    body = functools.partial(_inproj_body, q_scale=q_scale, with_windows=bool(windows))
    n_own_out = len(out_specs)
    if rider:
        assert rider[0].shape[0] == n * tj
        r_in, r_out, r_shapes = _sample_branch_specs(rider, lambda b, j: b * tj + j)
        in_specs, out_specs, out_shape = in_specs + r_in, out_specs + r_out, out_shape + r_shapes

        def body(*refs, own=body):
            ins, r_ins = refs[:3], refs[3:3 + N_SAMPLE_INPUTS]
            outs, scratch = refs[3 + N_SAMPLE_INPUTS:-1], refs[-1:]
            own(*ins, *outs[:n_own_out], *scratch, under_matmul=lambda: _sample_branch_math(
                pl.program_id(0) * tj + pl.program_id(1), *r_ins, *outs[n_own_out:]))

    return pl.pallas_call(
        body,
        grid=(n, tj),
        in_specs=in_specs,
        out_specs=out_specs,
        out_shape=out_shape,
        scratch_shapes=[pltpu.VMEM((QKV_COLS // LANES, tm, LANES), F32)],
        compiler_params=_params("arbitrary", "arbitrary"),
        name="inproj",
    )(x, g, w, *rider)


ATTN_CHUNKS = 32
HEAD_LANES = HEADS * STRIDES


def _band_body(row_ref, band_ref):
    for g in range(N_GROUPS):
        for h in range(HEADS):
            rows = jnp.broadcast_to(row_ref[g, h] * LOG2E, (STRIDES, 2 * STRIDES))
            band = pltpu.roll(rows, 0, 1, stride=1, stride_axis=0).T
            band_ref[g, :, h * STRIDES:(h + 1) * STRIDES] = band


def _bands(rows):
    return pl.pallas_call(
        _band_body,
        grid=(1,),
        in_specs=[_resident(rows.shape)],
        out_specs=pl.BlockSpec((N_GROUPS, 2 * STRIDES, HEAD_LANES), lambda i: (0, 0, 0)),
        out_shape=jax.ShapeDtypeStruct((N_GROUPS, 2 * STRIDES, HEAD_LANES), F32),
        compiler_params=_params("arbitrary"),
        name="bands",
    )(rows)


def _attn_body(qkv_ref, band_ref, o_ref, lse_ref, *, dil, chunks):
    lane_head = lax.broadcasted_iota(jnp.int32, (1, GROUP_COLS), 1) // HEAD_DIM

    def keys(i):
        return slice(max(i - 1, 0) * STRIDES, (i + 1) * STRIDES)

    def scores(r, i):
        q = qkv_ref[0, r, i * STRIDES:(i + 1) * STRIDES, 0:GROUP_COLS]
        qm = jnp.concatenate(
            [jnp.where(lane_head == h, q, jnp.zeros_like(q)) for h in range(HEADS)], axis=0)
        return _dot_t(qkv_ref[0, r, keys(i), GROUP_COLS:2 * GROUP_COLS], qm)

    def ones_row(n_keys):
        return jnp.where(lax.broadcasted_iota(jnp.int32, (BF16_SUBLANES, n_keys), 0) == 0,
                         1.0, 0.0).astype(BF16)

    order = [(r, i) for r in range(dil) for i in range(chunks)]
    st_next = scores(0, 0)
    for idx, (r, i) in enumerate(order):
        st = st_next
        if i == 0:
            vt_cls = qkv_ref[0, r, :, 2 * GROUP_COLS:].T
        if idx + 1 < len(order):
            st_next = scores(*order[idx + 1])
        vt2 = vt_cls[:, keys(i)]
        band_lo = STRIDES if i == 0 else 0
        o_parts, lse_parts = [], []
        for h in range(HEADS):
            hl = slice(h * STRIDES, (h + 1) * STRIDES)
            s_h = st[:, hl] + band_ref[band_lo:, hl]
            m = jnp.max(s_h, axis=0, keepdims=True)
            e = jnp.exp2(s_h - m).astype(BF16)
            ot = _dot(jnp.concatenate([vt2[h * HEAD_DIM:(h + 1) * HEAD_DIM],
                                       ones_row(vt2.shape[1])], axis=0), e)
            l = ot[HEAD_DIM:HEAD_DIM + 1]
            o_parts.append(ot[:HEAD_DIM] * (1.0 / l))
            lse_parts.append(jnp.broadcast_to(m * LN2 + jnp.log(l), (HEAD_DIM, STRIDES)))
        rows = pl.ds(i * STRIDES * dil + r, STRIDES, stride=dil)
        o_rows = jnp.concatenate(o_parts, axis=0).T
        lse_rows = jnp.concatenate(lse_parts, axis=0).T
        for c in range(GROUP_COLS // LANES):
            o_ref[c, rows, :] = o_rows[:, c * LANES:(c + 1) * LANES]
            lse_ref[c, rows, :] = lse_rows[:, c * LANES:(c + 1) * LANES]


def _attn(qkv, bands, g, n, s, dil):
    chunks = s // (STRIDES * dil)
    assert chunks * dil == ATTN_CHUNKS
    span = chunks * STRIDES
    slabs = GROUP_COLS // LANES
    out_spec = pl.BlockSpec((slabs, s, LANES), lambda b: (0, b, 0))
    out_sds = jax.ShapeDtypeStruct((slabs, n * s, LANES), F32)
    return pl.pallas_call(
        functools.partial(_attn_body, dil=dil, chunks=chunks),
        grid=(n,),
        in_specs=[pl.BlockSpec((1, dil, span, QKV_COLS), lambda b: (b, 0, 0, 0)),
                  pl.BlockSpec((None, 2 * STRIDES, HEAD_LANES), lambda b: (g, 0, 0))],
        out_specs=[out_spec, out_spec],
        out_shape=[out_sds, out_sds],
        compiler_params=_params("parallel"),
        name="attn_d%d" % dil,
    )(qkv, bands)


def _memkv_body(mem_ref, g_ref, w_ref, o_ref):
    o_ref[...] = _dot(_rms(mem_ref[...], g_ref[...]).astype(BF16), w_ref[...])


def _memkv(mem, g, w):
    m = mem.shape[0]
    row = pl.BlockSpec((MEM_LEN, D_MODEL), lambda i: (i, 0))
    return pl.pallas_call(
        _memkv_body,
        grid=(m // MEM_LEN,),
        in_specs=[row, _resident((1, D_MODEL)), _resident((D_MODEL, 2 * MEM_WIDTH))],
        out_specs=pl.BlockSpec((MEM_LEN, 2 * MEM_WIDTH), lambda i: (i, 0)),
        out_shape=jax.ShapeDtypeStruct((m, 2 * MEM_WIDTH), F32),
        compiler_params=_params("parallel"),
        name="memkv",
    )(mem, g, w)


def _merge_math(a, pooled, c, x, u, win_ref, wpool_ref, scale_ref, woa_ref, wob_ref,
                woc_ref, wout_ref):
    mixed = [_dot(pooled[gi].astype(BF16), wpool_ref[gi]) for gi in range(len(POOL_WINDOWS))]
    b = jnp.concatenate(mixed, axis=1) * scale_ref[...]
    m = None
    for k, (branch, wo_ref) in enumerate(((a, woa_ref), (b, wob_ref), (c, woc_ref))):
        lo = GATE_OFF + k * D_MODEL
        gate = jax.nn.sigmoid(_dot(u, win_ref[:, lo:lo + D_MODEL]))
        term = gate * _dot(branch.astype(BF16), wo_ref[...])
        m = term if m is None else m + term
    return x + _dot(m.astype(BF16), wout_ref[...])


def _mix_body(o0, o1, o2, l0, l1, l2, z_ref, halo_ref, qm_ref, mkv_ref, x_ref, u_ref, *rest,
              tile_in_seq, tail=None):
    merge_refs, out_ref = rest[:-1], rest[-1]
    j = tile_in_seq
    tm = x_ref.shape[0]
    unslab = lambda ref: jnp.concatenate([ref[c] for c in range(ref.shape[0])], axis=1)
    lses = [unslab(l0), unslab(l1), unslab(l2)]
    mx = jnp.maximum(jnp.maximum(lses[0], lses[1]), lses[2])
    es = [jnp.exp(l - mx) for l in lses]
    a = ((es[0] * unslab(o0) + es[1] * unslab(o1) + es[2] * unslab(o2))
         / (es[0] + es[1] + es[2]))
    z = z_ref[...]
    halo = jnp.where(j == 0, 0.0, halo_ref[...])
    zc = jnp.concatenate([halo, z], axis=0)
    pos = j * tm + lax.broadcasted_iota(jnp.int32, (tm, 1), 0)
    pooled = []
    for gi, kw in enumerate(POOL_WINDOWS):
        cs = slice(gi * POOL_GROUP, (gi + 1) * POOL_GROUP)
        run = zc[:, cs]
        width = 1
        while width < kw:
            run = run[width:] + run[:-width]
            width *= 2
        first = POOL_HALO - (kw - 1)
        cnt = jnp.minimum(kw, pos + 1).astype(F32)
        pooled.append(run[first:first + tm] / cnt - z[:, cs])
    mkv = mkv_ref[...].astype(BF16)
    qm = qm_ref[...]
    cs_out = []
    for h in range(MEM_HEADS):
        hs = slice(h * MEM_HEAD_DIM, (h + 1) * MEM_HEAD_DIM)
        vs = slice(MEM_WIDTH + h * MEM_HEAD_DIM, MEM_WIDTH + (h + 1) * MEM_HEAD_DIM)
        s = _dot_t(qm[:, hs], mkv[:, hs]) * (1.0 / math.sqrt(MEM_HEAD_DIM))
        mm = jnp.max(s, axis=-1, keepdims=True)
        p = jnp.exp(s - mm)
        l = jnp.sum(p, axis=-1, keepdims=True)
        cs_out.append(_dot(p.astype(BF16), mkv[:, vs]) / l)
    c = jnp.concatenate(cs_out, axis=1)
    x, u = x_ref[...], u_ref[...]
    if tail is not None:
        a_ref, pooled_ref, c_ref, xs_ref, us_ref, tail_out = tail
        under = lambda top, bottom: jnp.concatenate([top, bottom], axis=0)
        pooled_tail = pooled_ref[...]
        pooled = [under(p, pooled_tail[:, gi * POOL_GROUP:(gi + 1) * POOL_GROUP])
                  for gi, p in enumerate(pooled)]
        a, c = under(a, a_ref[...]), under(c, c_ref[...])
        x, u = under(x, xs_ref[...]), under(u, us_ref[...])
    y = _merge_math(a, pooled, c, x, u, *merge_refs)
    out_ref[...] = y[:tm]
    if tail is not None:
        tail_out[...] = y[tm:]


def _merge_weight_specs():
    return [_resident((D_MODEL, IN_COLS)),
            _resident((len(POOL_WINDOWS), POOL_GROUP, POOL_GROUP)), _resident((1, POOL_WIDTH)),
            _resident((GROUP_COLS, D_MODEL)), _resident((POOL_WIDTH, D_MODEL)),
            _resident((MEM_WIDTH, D_MODEL)), _resident((D_MODEL, D_MODEL))]


N_MERGE_WEIGHTS = 7


def _mix_with_tail_body(*refs, tiles_per_seq, steps):
    n_tile_in = 12
    tile_in, merge_refs = refs[:n_tile_in], refs[n_tile_in:n_tile_in + N_MERGE_WEIGHTS]
    *tail_in, out_ref, tail_out = refs[n_tile_in + N_MERGE_WEIGHTS:]
    t = pl.program_id(0)

    @pl.when(t < steps - 1)
    def _():
        _mix_body(*tile_in, *merge_refs, out_ref, tile_in_seq=lax.rem(t, tiles_per_seq))

    @pl.when(t == steps - 1)
    def _():
        _mix_body(*tile_in, *merge_refs, out_ref, tile_in_seq=(steps - 1) % tiles_per_seq,
                  tail=(*tail_in, tail_out))


def _mix(os, lses, z, qm, mkv, x, u, weights, n, s, tail):
    tm = TOKEN_TILE
    tj = s // tm
    steps = n * tj
    last = steps - 1
    halo_per_tile = tm // POOL_HALO
    tile = lambda t: jnp.minimum(t, last)

    def tok(cols):
        return pl.BlockSpec((tm, cols), lambda t: (tile(t), 0))

    halo = pl.BlockSpec((POOL_HALO, POOL_WIDTH),
                        lambda t: (jnp.maximum(tile(t) * halo_per_tile - 1, 0), 0))
    slab = pl.BlockSpec((GROUP_COLS // LANES, tm, LANES), lambda t: (0, tile(t), 0))
    weight_specs = _merge_weight_specs()
    assert len(weight_specs) == N_MERGE_WEIGHTS
    in_specs = ([slab] * 6 + [tok(POOL_WIDTH), halo, tok(MEM_WIDTH),
                pl.BlockSpec((MEM_LEN, 2 * MEM_WIDTH), lambda t: (tile(t) // tj, 0)),
                tok(D_MODEL), tok(D_MODEL)] + weight_specs + [_resident(t.shape) for t in tail])
    xs = tail[3]
    return pl.pallas_call(
        functools.partial(_mix_with_tail_body, tiles_per_seq=tj, steps=steps),
        grid=(steps,),
        in_specs=in_specs,
        out_specs=[tok(D_MODEL), pl.BlockSpec(xs.shape, lambda t: (0, 0))],
        out_shape=[jax.ShapeDtypeStruct((n * s, D_MODEL), F32),
                   jax.ShapeDtypeStruct(xs.shape, F32)],
        compiler_params=_params("arbitrary"),
        name="mix",
    )(*os, *lses, z, z, qm, mkv, x, u, *weights, *tail)


N_SAMPLE_INPUTS = 14


def _sample_branch_math(req, qkv0, qkv1, qkv2, c0, c1, c2, b0, b1, b2, bias0_ref, z_ref, st_ref,
                        qm_ref, cm_ref, a_ref, pooled_ref, c_ref):
    row = pl.ds(req, 1)
    eye = (lax.broadcasted_iota(jnp.int32, (HEAD_DIM, HEAD_DIM), 0)
           == lax.broadcasted_iota(jnp.int32, (HEAD_DIM, HEAD_DIM), 1))
    cube = (HEADS, HEAD_DIM, HEAD_DIM)

    def heads_of(vec, lo):
        return jnp.stack([vec[:, lo + h * HEAD_DIM:lo + (h + 1) * HEAD_DIM]
                          for h in range(HEADS)], axis=0)

    outs, lses = [], []
    for g, (qkv_ref, cache_ref, bias_ref) in enumerate(
            ((qkv0, c0, b0), (qkv1, c1, b1), (qkv2, c2, b2))):
        qkv = qkv_ref[row, :]
        q, kn, vn = heads_of(qkv, 0), heads_of(qkv, GROUP_COLS), heads_of(qkv, 2 * GROUP_COLS)
        q_col = jnp.sum(jnp.where(eye, jnp.broadcast_to(q, cube), 0.0), axis=2, keepdims=True)
        s = jnp.sum(cache_ref[0, 0] * q_col, axis=1, keepdims=True) + bias_ref[...]
        sn = jnp.sum(kn * q, axis=2, keepdims=True) + bias0_ref[g]
        m = jnp.maximum(jnp.max(s, axis=2, keepdims=True), sn)
        p = jnp.exp(s - m)
        pn = jnp.exp(sn - m)
        l = jnp.sum(p, axis=2, keepdims=True) + pn
        pv = jnp.sum(cache_ref[0, 1] * p, axis=2, keepdims=True)
        pv_row = jnp.sum(jnp.where(eye, jnp.broadcast_to(pv, cube), 0.0), axis=1, keepdims=True)
        outs.append((pv_row + pn * vn) / l)
        lses.append(m + jnp.log(l))
    mx = jnp.maximum(jnp.maximum(lses[0], lses[1]), lses[2])
    es = [jnp.exp(lse - mx) for lse in lses]
    a = (es[0] * outs[0] + es[1] * outs[1] + es[2] * outs[2]) / (es[0] + es[1] + es[2])
    a_ref[row, :] = jnp.concatenate([a[h] for h in range(HEADS)], axis=1)

    zn = z_ref[row, :]
    st = st_ref[:, row, :]
    pooled = []
    for gi, kw in enumerate(POOL_WINDOWS):
        cs = slice(gi * POOL_GROUP, (gi + 1) * POOL_GROUP)
        tot = jnp.sum(st[POOL_STATE - (kw - 1):, :, cs], axis=0) + zn[:, cs]
        pooled.append(tot / float(min(kw, PAST_LEN + 1)) - zn[:, cs])
    pooled_ref[row, :] = jnp.concatenate(pooled, axis=1)

    qm_row = qm_ref[row, :]
    qm = jnp.concatenate([qm_row[:, h * MEM_HEAD_DIM:(h + 1) * MEM_HEAD_DIM]
                          for h in range(MEM_HEADS)], axis=0)
    km = cm_ref[0, :, 0]
    vm = cm_ref[0, :, 1]
    s = jnp.sum(km * qm[None], axis=-1, keepdims=True) * (1.0 / math.sqrt(MEM_HEAD_DIM))
    m = jnp.max(s, axis=0)
    p = jnp.exp(s - m[None])
    c = jnp.sum(p * vm, axis=0) / jnp.sum(p, axis=0)
    c_ref[row, :] = jnp.concatenate([c[h:h + 1] for h in range(MEM_HEADS)], axis=1)


def _sample_branch_specs(operands, request_of):
    assert len(operands) == N_SAMPLE_INPUTS
    nb = operands[0].shape[0]

    def per_request(t):
        zeros = (0,) * (t.ndim - 1)
        return pl.BlockSpec((1,) + t.shape[1:], lambda *g: (request_of(*g),) + zeros)

    def whole(shape):
        zeros = (0,) * len(shape)
        return pl.BlockSpec(shape, lambda *g: zeros)

    streamed = (3, 4, 5, 13)
    in_specs = [per_request(t) if k in streamed else whole(t.shape)
                for k, t in enumerate(operands)]
    widths = (GROUP_COLS, POOL_WIDTH, MEM_WIDTH)
    return (in_specs, [whole((nb, w)) for w in widths],
            [jax.ShapeDtypeStruct((nb, w), F32) for w in widths])


def _rel_bucket(n):
    max_exact = N_BUCKETS // 2
    nf = jnp.maximum(n, 1).astype(F32)
    large = max_exact + (jnp.log(nf / max_exact) / math.log(MAX_DISTANCE / max_exact)
                         * (N_BUCKETS - max_exact)).astype(jnp.int32)
    large = jnp.minimum(large, N_BUCKETS - 1)
    return jnp.where(n < max_exact, n, large)


def _stride_bias(rel_bias, g, dil):
    j = jnp.arange(STRIDES + 1, dtype=jnp.int32)
    return rel_bias[_rel_bucket(j * dil)][:, g * HEADS:(g + 1) * HEADS].astype(F32)


def _band_row(bias_j):
    row = jnp.concatenate([bias_j[::-1], jnp.full((STRIDES - 1, HEADS), NEG_INF, F32)], axis=0)
    return row.T.reshape(HEADS, 1, 2 * STRIDES)


def _cache_bias(bias_j, dil):
    on_grid = bias_j[STRIDES:0:-1].T
    full = jnp.full((HEADS, STRIDES, dil), NEG_INF, F32).at[:, :, 0].set(on_grid)
    return full.reshape(HEADS, 1, STRIDES * dil)


def kernel(x_prompt, x_sample, cache_win0_kv, cache_win1_kv, cache_win2_kv, state_pool, cache_mem_kv, mem_prompt, rel_bias, g_ffn1, w1_gate, w1_up, w1_down, g_mix, w_in, w_pool, pool_scale, g_mem, w_mem_kv, w_oa, w_ob, w_oc, w_out, g_ffn2, w2_gate, w2_up, w2_down, g_final):
    n, s, _ = x_prompt.shape
    nb = x_sample.shape[0]
    depth = g_ffn1.shape[0]
    win_caches = (cache_win0_kv, cache_win1_kv, cache_win2_kv)
    bias_js = [_stride_bias(rel_bias, g, dil) for g, (_, dil) in enumerate(DIL_GROUPS)]
    bands = _bands(jnp.stack([_band_row(b) for b in bias_js]))
    bias_cache = [_cache_bias(b, dil) for b, (_, dil) in zip(bias_js, DIL_GROUPS)]
    bias_new = jnp.stack([b[0] for b in bias_js]).reshape(N_GROUPS, HEADS, 1, 1)
    gfin = g_final.reshape(1, D_MODEL)

    xp = x_prompt.reshape(n * s, D_MODEL)
    xs = x_sample.reshape(nb, D_MODEL)
    st_p = [[] for _ in range(5)]
    st_s = [[] for _ in range(4)]
    for l in range(depth):
        last = l == depth - 1
        vec = lambda v: v[l].reshape(1, -1)
        q_scale = 1.0 / math.sqrt(HEAD_DIM)

        later = [w2_gate[l], w2_up[l], w2_down[l], w_in[l], w_oa[l], w_ob[l], w_oc[l], w_out[l],
                 w_mem_kv[l], w_pool[l].reshape(len(POOL_WINDOWS) * POOL_GROUP, POOL_GROUP)]
        xp, later, xs = _ffn(xp, vec(g_ffn1), w1_gate[l], w1_up[l], w1_down[l], xs, casts=later)
        w2, (win_l, woa_l, wob_l, woc_l, wout_l, wmem_l, wpool_l) = later[0:3], later[3:]
        merge_w = (win_l, wpool_l.reshape(w_pool.shape[1:]), vec(pool_scale),
                   woa_l, wob_l, woc_l, wout_l)

        res = _inproj(xs, vec(g_mix), win_l, 1, nb, None, [1] * N_GROUPS, F32, q_scale)
        s_qkvs, s_z, s_qm, s_u = [t.reshape(nb, QKV_COLS) for t in res[0:3]], *res[3:6]
        rider = (s_qkvs + [jnp.transpose(cw[l], (0, 2, 3, 4, 1)) for cw in win_caches]
                 + bias_cache + [bias_new, s_z, jnp.transpose(state_pool[l], (1, 0, 2)), s_qm,
                                 cache_mem_kv[l]])
        res = _inproj(xp, vec(g_mix), win_l, n, s, [min(w, s) for w, _ in DIL_GROUPS],
                      [d for _, d in DIL_GROUPS], BF16, q_scale * LOG2E, rider)
        qkvs, kvwins, z, qm, u = res[0:3], res[3:6], res[6], res[7], res[8]
        s_branches = res[9:12]
        os, lses = [], []
        for g, (_, dil) in enumerate(DIL_GROUPS):
            o, lse = _attn(qkvs[g], bands, g, n, s, dil)
            os.append(o)
            lses.append(lse)
        mkv = _memkv(mem_prompt.reshape(n * MEM_LEN, D_MODEL), vec(g_mem), wmem_l)
        xp, xs = _mix(os, lses, z, qm, mkv, xp, u, merge_w, n, s, tail=(*s_branches, xs, s_u))
        xp, xs = _ffn(xp, vec(g_ffn2), *w2, xs, g_final=gfin if last else None)
        for g, (win, _) in enumerate(DIL_GROUPS):
            kv_t = kvwins[g].reshape(n, 2, HEADS, HEAD_DIM, min(win, s))
            st_p[g].append(jnp.transpose(kv_t, (0, 4, 1, 2, 3)))
        st_p[3].append(z.reshape(n, s, POOL_WIDTH)[:, s - POOL_STATE:])
        st_p[4].append(mkv.reshape(n, MEM_LEN, 2, MEM_HEADS, MEM_HEAD_DIM))

        for g in range(N_GROUPS):
            st_s[g].append(s_qkvs[g][:, GROUP_COLS:].reshape(nb, 1, 2, HEADS, HEAD_DIM))
        st_s[3].append(s_z.reshape(nb, 1, POOL_WIDTH))

    y_prompt = xp.reshape(n, s, D_MODEL)
    y_sample = xs.reshape(nb, 1, D_MODEL)
    stack = lambda ts: jnp.stack(ts, axis=0)
    return (y_prompt, y_sample, stack(st_p[0]), stack(st_p[1]), stack(st_p[2]), stack(st_p[3]),
            stack(st_p[4]), stack(st_s[0]), stack(st_s[1]), stack(st_s[2]), stack(st_s[3]))
```

```python
import functools
import math

import jax
import jax.numpy as jnp
from jax import lax
from jax.experimental import pallas as pl
from jax.experimental.pallas import tpu as pltpu

F32 = jnp.float32
BF16 = jnp.bfloat16

D_MODEL = 1024
D_FF = 2816
LANES = 128
HEAD_DIM = 64
HEADS = 4
DIL_GROUPS = ((128, 1), (512, 4), (2048, 16))
N_GROUPS = 3
GROUP_COLS = HEADS * HEAD_DIM
QKV_COLS = 3 * GROUP_COLS
STRIDES = 128
POOL_WINDOWS = (2, 4, 8, 16)
POOL_GROUP = 128
POOL_WIDTH = 512
POOL_STATE = 15
POOL_HALO = 16
MEM_LEN = 256
MEM_HEADS = 4
MEM_HEAD_DIM = 128
MEM_WIDTH = 512
N_BUCKETS = 32
MAX_DISTANCE = 2048
N_BRANCH = 3
EPS = 1e-6
NEG_INF = -1e30
LOG2E = math.log2(math.e)
LN2 = math.log(2.0)
PAST_LEN = 8192

V7X_VMEM_LIMIT_BYTES = 56 * 1024 * 1024
TOKEN_TILE = 512


def _params(*sem):
    return pltpu.CompilerParams(dimension_semantics=sem,
                                vmem_limit_bytes=V7X_VMEM_LIMIT_BYTES)


def _resident(shape):
    zeros = (0,) * len(shape)
    return pl.BlockSpec(shape, lambda *_: zeros, pipeline_mode=pl.Buffered(1))


def _rms(x, g):
    return x * lax.rsqrt(jnp.mean(x * x, axis=-1, keepdims=True) + EPS) * g


def _dot(a, b):
    return jnp.dot(a, b, preferred_element_type=F32)


def _dot_t(a, b):
    return lax.dot_general(a, b, (((1,), (1,)), ((), ())), preferred_element_type=F32)


BF16_SUBLANES = 16


FF_CHUNK = 256
FF_CHUNKS = D_FF // FF_CHUNK


def _ffn_math(x, g_ref, chunk, gf_ref):
    h = _rms(x, g_ref[...]).astype(BF16)
    acc = None
    for c in range(FF_CHUNKS):
        wg, wu, wd = chunk(c)
        a = _dot(h, wg)
        b = _dot(h, wu)
        part = _dot((a * jax.nn.sigmoid(a) * b).astype(BF16), wd)
        acc = part if acc is None else acc + part
    y = x + 0.5 * acc
    return y if gf_ref is None else _rms(y, gf_ref[...])


def _ffn_body(*refs, final, n_casts, steps, own_weights):
    refs = iter(refs)
    x_ref, g_ref, wg_ref, wu_ref, wd_ref = (next(refs) for _ in range(5))
    gf_ref = next(refs) if final else None
    cast_in = [next(refs) for _ in range(n_casts)]
    tail_in = next(refs)
    o_ref = next(refs)
    cast_out = [next(refs) for _ in range(n_casts)]
    tail_out = next(refs)
    t = pl.program_id(0)

    def convert_slices():
        for src, dst in zip(cast_in, cast_out):
            dst[...] = src[...].astype(BF16)

    if own_weights:
        wg_s, wu_s, wd_s, h_s, acc_s = (next(refs) for _ in range(5))
        base = FF_CHUNKS - 1
        full_from = base + 1

        @pl.when(t <= base)
        def _():
            wg_s[t] = wg_ref[...].astype(BF16)
            wu_s[t] = wu_ref[...].astype(BF16)
            wd_s[t] = wd_ref[...].astype(BF16)

            @pl.when(t == 0)
            def _():
                h_s[...] = _rms(x_ref[...], g_ref[...]).astype(BF16)
                acc_s[...] = jnp.zeros_like(acc_s)

            h = h_s[...]
            a = _dot(h, wg_s[t])
            b = _dot(h, wu_s[t])
            acc_s[...] += _dot((a * jax.nn.sigmoid(a) * b).astype(BF16), wd_s[t])

            @pl.when(t == base)
            def _():
                y = x_ref[...] + 0.5 * acc_s[...]
                o_ref[...] = y if gf_ref is None else _rms(y, gf_ref[...])
                convert_slices()

        chunk = lambda c: (wg_s[c], wu_s[c], wd_s[c])
    else:
        base = full_from = 0
        cols = lambda c: slice(c * FF_CHUNK, (c + 1) * FF_CHUNK)
        chunk = lambda c: (wg_ref[:, cols(c)], wu_ref[:, cols(c)], wd_ref[cols(c), :])

    last_step = base + steps - 1

    @pl.when(jnp.logical_and(t >= full_from, t < last_step))
    def _():
        o_ref[...] = _ffn_math(x_ref[...], g_ref, chunk, gf_ref)
        convert_slices()

    @pl.when(t == last_step)
    def _():
        rows = x_ref.shape[0]
        both = jnp.concatenate([x_ref[...], tail_in[...]], axis=0)
        y = _ffn_math(both, g_ref, chunk, gf_ref)
        o_ref[...] = y[:rows]
        tail_out[...] = y[rows:]
        convert_slices()


def _ffn(x, g, wg, wu, wd, tail, g_final=None, casts=()):
    m = x.shape[0]
    tm = min(TOKEN_TILE, m)
    steps = m // tm
    final = g_final is not None
    own_weights = wg.dtype == F32
    assert steps > 1 or not own_weights
    base = FF_CHUNKS - 1 if own_weights else 0
    last = steps - 1
    tile = lambda t: jnp.clip(t - base, 0, last)
    row = pl.BlockSpec((tm, D_MODEL), lambda t: (tile(t), 0))
    if own_weights:
        load = lambda t: jnp.minimum(t, FF_CHUNKS - 1)
        weight_specs = [pl.BlockSpec((D_MODEL, FF_CHUNK), lambda t: (0, load(t))),
                        pl.BlockSpec((D_MODEL, FF_CHUNK), lambda t: (0, load(t))),
                        pl.BlockSpec((FF_CHUNK, D_MODEL), lambda t: (load(t), 0))]
        scratch = [pltpu.VMEM((FF_CHUNKS, D_MODEL, FF_CHUNK), BF16),
                   pltpu.VMEM((FF_CHUNKS, D_MODEL, FF_CHUNK), BF16),
                   pltpu.VMEM((FF_CHUNKS, FF_CHUNK, D_MODEL), BF16),
                   pltpu.VMEM((tm, D_MODEL), BF16),
                   pltpu.VMEM((tm, D_MODEL), F32)]
    else:
        weight_specs = [_resident((D_MODEL, D_FF)), _resident((D_MODEL, D_FF)),
                        _resident((D_FF, D_MODEL))]
        scratch = []
    in_specs = [row, _resident((1, D_MODEL))] + weight_specs
    args = [x, g, wg, wu, wd]
    if final:
        in_specs.append(_resident((1, D_MODEL)))
        args.append(g_final)
    cast_specs = []
    for w in casts:
        blocks = math.gcd(steps, w.shape[0] // BF16_SUBLANES)
        per = steps // blocks
        cast_specs.append(pl.BlockSpec((w.shape[0] // blocks, w.shape[1]),
                                       lambda t, per=per: (tile(t) // per, 0)))
    n_casts = len(casts)
    out = pl.pallas_call(
        functools.partial(_ffn_body, final=final, n_casts=n_casts, steps=steps,
                          own_weights=own_weights),
        grid=(base + steps,),
        in_specs=in_specs + cast_specs + [_resident(tail.shape)],
        out_specs=[row] + cast_specs + [pl.BlockSpec(tail.shape, lambda t: (0, 0))],
        out_shape=[jax.ShapeDtypeStruct((m, D_MODEL), F32)]
        + [jax.ShapeDtypeStruct(w.shape, BF16) for w in casts]
        + [jax.ShapeDtypeStruct(tail.shape, F32)],
        scratch_shapes=scratch,
        compiler_params=_params("arbitrary"),
        name="ffn_final" if final else "ffn",
    )(*args, *casts, tail)
    return (out[0], out[1:1 + n_casts], out[1 + n_casts]) if casts else (out[0], out[1])


Z_OFF = N_GROUPS * QKV_COLS
QM_OFF = Z_OFF + POOL_WIDTH
GATE_OFF = QM_OFF + MEM_WIDTH
IN_COLS = GATE_OFF + N_BRANCH * D_MODEL


def _inproj_body(x_ref, g_ref, w_ref, *rest, q_scale, with_windows, under_matmul=None):
    qkv_refs = rest[0:N_GROUPS]
    kv_refs = rest[N_GROUPS:2 * N_GROUPS] if with_windows else (None,) * N_GROUPS
    z_ref, qm_ref, u_ref, p_scr = rest[-4:]
    u = _rms(x_ref[...], g_ref[...]).astype(BF16)
    u_ref[...] = u
    tm = u.shape[0]
    width = N_GROUPS * GROUP_COLS
    p = _dot(u, w_ref[...])
    if under_matmul is not None:
        under_matmul()
    for g, (qkv_ref, kv_ref) in enumerate(zip(qkv_refs, kv_refs)):
        dil, per_class = qkv_ref.shape[1], qkv_ref.shape[2]
        q, k, v = [p[:, t * width + g * GROUP_COLS:t * width + (g + 1) * GROUP_COLS]
                   for t in range(3)]
        qkv = jnp.concatenate([q * q_scale, k, v], axis=1)
        if dil == 1:
            qkv_ref[0, 0] = qkv.astype(qkv_ref.dtype)
        else:
            for c in range(QKV_COLS // LANES):
                p_scr[c] = qkv[:, c * LANES:(c + 1) * LANES]
            for r in range(dil):
                blk = jnp.concatenate([p_scr[c, pl.ds(r, per_class, stride=dil), :]
                                       for c in range(QKV_COLS // LANES)], axis=1)
                qkv_ref[0, r] = blk.astype(qkv_ref.dtype)
        if with_windows:
            rows = kv_ref.shape[2]
            kv_ref[0] = jnp.concatenate([k[tm - rows:], v[tm - rows:]], axis=1).T
    z_ref[...] = p[:, Z_OFF:QM_OFF]
    qm_ref[...] = p[:, QM_OFF:GATE_OFF].astype(qm_ref.dtype)


def _inproj(x, g, w, n, s, windows, dils, q_dtype, q_scale, rider=()):
    tm = min(TOKEN_TILE, s)
    tj = s // tm

    def tok(cols):
        return pl.BlockSpec((tm, cols), lambda b, j: (b * tj + j, 0))

    qkv_specs = [pl.BlockSpec((1, d, tm // d, QKV_COLS), lambda b, j: (b, 0, j, 0)) for d in dils]
    qkv_shapes = [jax.ShapeDtypeStruct((n, d, s // d, QKV_COLS), q_dtype) for d in dils]

    kv_specs, kv_shapes = [], []
    for win in windows or ():
        rows = min(win, tm)
        assert tm % rows == 0 and win % rows == 0
        first = (s - win) // tm
        if win >= tm:
            spec = pl.BlockSpec((1, 2 * GROUP_COLS, rows),
                                lambda b, j, first=first: (b, 0, jnp.maximum(j - first, 0)))
        else:
            spec = pl.BlockSpec((1, 2 * GROUP_COLS, rows), lambda b, j: (b, 0, 0))
        kv_specs.append(spec)
        kv_shapes.append(jax.ShapeDtypeStruct((n, 2 * GROUP_COLS, win), F32))
    m = n * s
    out_shape = (qkv_shapes + kv_shapes + [
        jax.ShapeDtypeStruct((m, POOL_WIDTH), F32),
        jax.ShapeDtypeStruct((m, MEM_WIDTH), q_dtype),
        jax.ShapeDtypeStruct((m, D_MODEL), BF16)])
    out_specs = qkv_specs + kv_specs + [tok(POOL_WIDTH), tok(MEM_WIDTH), tok(D_MODEL)]
    in_specs = [tok(D_MODEL), _resident((1, D_MODEL)),
                pl.BlockSpec((D_MODEL, GATE_OFF), lambda b, j: (0, 0),
                             pipeline_mode=pl.Buffered(1))]
    body = functools.partial(_inproj_body, q_scale=q_scale, with_windows=bool(windows))
    n_own_out = len(out_specs)
    if rider:
        assert rider[0].shape[0] == n * tj
        r_in, r_out, r_shapes = _sample_branch_specs(rider, lambda b, j: b * tj + j)
        in_specs, out_specs, out_shape = in_specs + r_in, out_specs + r_out, out_shape + r_shapes

        def body(*refs, own=body):
            ins, r_ins = refs[:3], refs[3:3 + N_SAMPLE_INPUTS]
            outs, scratch = refs[3 + N_SAMPLE_INPUTS:-1], refs[-1:]
            own(*ins, *outs[:n_own_out], *scratch, under_matmul=lambda: _sample_branch_math(
                pl.program_id(0) * tj + pl.program_id(1), *r_ins, *outs[n_own_out:]))

    return pl.pallas_call(
        body,
        grid=(n, tj),
        in_specs=in_specs,
        out_specs=out_specs,
        out_shape=out_shape,
        scratch_shapes=[pltpu.VMEM((QKV_COLS // LANES, tm, LANES), F32)],
        compiler_params=_params("arbitrary", "arbitrary"),
        name="inproj",
    )(x, g, w, *rider)


ATTN_CHUNKS = 32
HEAD_LANES = HEADS * STRIDES


def _band_body(row_ref, band_ref):
    for g in range(N_GROUPS):
        for h in range(HEADS):
            rows = jnp.broadcast_to(row_ref[g, h] * LOG2E, (STRIDES, 2 * STRIDES))
            band = pltpu.roll(rows, 0, 1, stride=1, stride_axis=0).T
            band_ref[g, :, h * STRIDES:(h + 1) * STRIDES] = band


def _bands(rows):
    return pl.pallas_call(
        _band_body,
        grid=(1,),
        in_specs=[_resident(rows.shape)],
        out_specs=pl.BlockSpec((N_GROUPS, 2 * STRIDES, HEAD_LANES), lambda i: (0, 0, 0)),
        out_shape=jax.ShapeDtypeStruct((N_GROUPS, 2 * STRIDES, HEAD_LANES), F32),
        compiler_params=_params("arbitrary"),
        name="bands",
    )(rows)


def _attn_body(qkv_ref, band_ref, o_ref, lse_ref, *, dil, chunks):
    lane_head = lax.broadcasted_iota(jnp.int32, (1, GROUP_COLS), 1) // HEAD_DIM

    def keys(i):
        return slice(max(i - 1, 0) * STRIDES, (i + 1) * STRIDES)

    def scores(r, i):
        q = qkv_ref[0, r, i * STRIDES:(i + 1) * STRIDES, 0:GROUP_COLS]
        qm = jnp.concatenate(
            [jnp.where(lane_head == h, q, jnp.zeros_like(q)) for h in range(HEADS)], axis=0)
        return _dot_t(qkv_ref[0, r, keys(i), GROUP_COLS:2 * GROUP_COLS], qm)

    def ones_row(n_keys):
        return jnp.where(lax.broadcasted_iota(jnp.int32, (BF16_SUBLANES, n_keys), 0) == 0,
                         1.0, 0.0).astype(BF16)

    order = [(r, i) for r in range(dil) for i in range(chunks)]
    st_next = scores(0, 0)
    for idx, (r, i) in enumerate(order):
        st = st_next
        if i == 0:
            vt_cls = qkv_ref[0, r, :, 2 * GROUP_COLS:].T
        if idx + 1 < len(order):
            st_next = scores(*order[idx + 1])
        vt2 = vt_cls[:, keys(i)]
        band_lo = STRIDES if i == 0 else 0
        o_parts, lse_parts = [], []
        for h in range(HEADS):
            hl = slice(h * STRIDES, (h + 1) * STRIDES)
            s_h = st[:, hl] + band_ref[band_lo:, hl]
            m = jnp.max(s_h, axis=0, keepdims=True)
            e = jnp.exp2(s_h - m).astype(BF16)
            ot = _dot(jnp.concatenate([vt2[h * HEAD_DIM:(h + 1) * HEAD_DIM],
                                       ones_row(vt2.shape[1])], axis=0), e)
            l = ot[HEAD_DIM:HEAD_DIM + 1]
            o_parts.append(ot[:HEAD_DIM] * (1.0 / l))
            lse_parts.append(jnp.broadcast_to(m * LN2 + jnp.log(l), (HEAD_DIM, STRIDES)))
        rows = pl.ds(i * STRIDES * dil + r, STRIDES, stride=dil)
        o_rows = jnp.concatenate(o_parts, axis=0).T
        lse_rows = jnp.concatenate(lse_parts, axis=0).T
        for c in range(GROUP_COLS // LANES):
            o_ref[c, rows, :] = o_rows[:, c * LANES:(c + 1) * LANES]
            lse_ref[c, rows, :] = lse_rows[:, c * LANES:(c + 1) * LANES]


def _attn(qkv, bands, g, n, s, dil):
    chunks = s // (STRIDES * dil)
    assert chunks * dil == ATTN_CHUNKS
    span = chunks * STRIDES
    slabs = GROUP_COLS // LANES
    out_spec = pl.BlockSpec((slabs, s, LANES), lambda b: (0, b, 0))
    out_sds = jax.ShapeDtypeStruct((slabs, n * s, LANES), F32)
    return pl.pallas_call(
        functools.partial(_attn_body, dil=dil, chunks=chunks),
        grid=(n,),
        in_specs=[pl.BlockSpec((1, dil, span, QKV_COLS), lambda b: (b, 0, 0, 0)),
                  pl.BlockSpec((None, 2 * STRIDES, HEAD_LANES), lambda b: (g, 0, 0))],
        out_specs=[out_spec, out_spec],
        out_shape=[out_sds, out_sds],
        compiler_params=_params("parallel"),
        name="attn_d%d" % dil,
    )(qkv, bands)


MEM_SLABS = 2 * MEM_WIDTH // LANES


def _mem_slab(c):
    return pl.ds(c, MEM_LEN, stride=MEM_SLABS)


def _memkv_body(mem_ref, g_ref, w_ref, o_ref):
    kv = _dot(_rms(mem_ref[...], g_ref[...]).astype(BF16), w_ref[...])
    for c in range(MEM_SLABS):
        o_ref[_mem_slab(c), :] = kv[:, c * LANES:(c + 1) * LANES]


def _memkv(mem, g, w):
    assert MEM_HEAD_DIM == LANES
    m = mem.shape[0]
    row = pl.BlockSpec((MEM_LEN, D_MODEL), lambda i: (i, 0))
    return pl.pallas_call(
        _memkv_body,
        grid=(m // MEM_LEN,),
        in_specs=[row, _resident((1, D_MODEL)), _resident((D_MODEL, 2 * MEM_WIDTH))],
        out_specs=pl.BlockSpec((MEM_LEN * MEM_SLABS, LANES), lambda i: (i, 0)),
        out_shape=jax.ShapeDtypeStruct((m * MEM_SLABS, LANES), F32),
        compiler_params=_params("parallel"),
        name="memkv",
    )(mem, g, w)


def _merge_math(a, pooled, c, x, u, win_ref, wpool_ref, scale_ref, woa_ref, wob_ref,
                woc_ref, wout_ref):
    mixed = [_dot(pooled[gi].astype(BF16), wpool_ref[gi]) for gi in range(len(POOL_WINDOWS))]
    b = jnp.concatenate(mixed, axis=1) * scale_ref[...]
    m = None
    for k, (branch, wo_ref) in enumerate(((a, woa_ref), (b, wob_ref), (c, woc_ref))):
        lo = GATE_OFF + k * D_MODEL
        gate = jax.nn.sigmoid(_dot(u, win_ref[:, lo:lo + D_MODEL]))
        term = gate * _dot(branch.astype(BF16), wo_ref[...])
        m = term if m is None else m + term
    return x + _dot(m.astype(BF16), wout_ref[...])


def _mix_body(o0, o1, o2, l0, l1, l2, z_ref, halo_ref, qm_ref, mkv_ref, x_ref, u_ref, *rest,
              tile_in_seq, tail=None):
    merge_refs, out_ref = rest[:-1], rest[-1]
    j = tile_in_seq
    tm = x_ref.shape[0]
    unslab = lambda ref: jnp.concatenate([ref[c] for c in range(ref.shape[0])], axis=1)
    lses = [unslab(l0), unslab(l1), unslab(l2)]
    mx = jnp.maximum(jnp.maximum(lses[0], lses[1]), lses[2])
    es = [jnp.exp(l - mx) for l in lses]
    a = ((es[0] * unslab(o0) + es[1] * unslab(o1) + es[2] * unslab(o2))
         / (es[0] + es[1] + es[2]))
    z = z_ref[...]
    halo = jnp.where(j == 0, 0.0, halo_ref[...])
    zc = jnp.concatenate([halo, z], axis=0)
    pos = j * tm + lax.broadcasted_iota(jnp.int32, (tm, 1), 0)
    pooled = []
    for gi, kw in enumerate(POOL_WINDOWS):
        cs = slice(gi * POOL_GROUP, (gi + 1) * POOL_GROUP)
        run = zc[:, cs]
        width = 1
        while width < kw:
            run = run[width:] + run[:-width]
            width *= 2
        first = POOL_HALO - (kw - 1)
        cnt = jnp.minimum(kw, pos + 1).astype(F32)
        pooled.append(run[first:first + tm] / cnt - z[:, cs])
    qm = qm_ref[...]
    cs_out = []
    for h in range(MEM_HEADS):
        hs = slice(h * MEM_HEAD_DIM, (h + 1) * MEM_HEAD_DIM)
        k_h = mkv_ref[_mem_slab(h), :].astype(BF16)
        v_h = mkv_ref[_mem_slab(MEM_HEADS + h), :].astype(BF16)
        s = _dot_t(qm[:, hs], k_h) * (1.0 / math.sqrt(MEM_HEAD_DIM))
        mm = jnp.max(s, axis=-1, keepdims=True)
        p = jnp.exp(s - mm)
        l = jnp.sum(p, axis=-1, keepdims=True)
        cs_out.append(_dot(p.astype(BF16), v_h) / l)
    c = jnp.concatenate(cs_out, axis=1)
    x, u = x_ref[...], u_ref[...]
    if tail is not None:
        a_ref, pooled_ref, c_ref, xs_ref, us_ref, tail_out = tail
        under = lambda top, bottom: jnp.concatenate([top, bottom], axis=0)
        pooled_tail = pooled_ref[...]
        pooled = [under(p, pooled_tail[:, gi * POOL_GROUP:(gi + 1) * POOL_GROUP])
                  for gi, p in enumerate(pooled)]
        a, c = under(a, a_ref[...]), under(c, c_ref[...])
        x, u = under(x, xs_ref[...]), under(u, us_ref[...])
    y = _merge_math(a, pooled, c, x, u, *merge_refs)
    out_ref[...] = y[:tm]
    if tail is not None:
        tail_out[...] = y[tm:]


def _merge_weight_specs():
    return [_resident((D_MODEL, IN_COLS)),
            _resident((len(POOL_WINDOWS), POOL_GROUP, POOL_GROUP)), _resident((1, POOL_WIDTH)),
            _resident((GROUP_COLS, D_MODEL)), _resident((POOL_WIDTH, D_MODEL)),
            _resident((MEM_WIDTH, D_MODEL)), _resident((D_MODEL, D_MODEL))]


N_MERGE_WEIGHTS = 7


def _mix_with_tail_body(*refs, tiles_per_seq, steps):
    n_tile_in = 12
    tile_in, merge_refs = refs[:n_tile_in], refs[n_tile_in:n_tile_in + N_MERGE_WEIGHTS]
    *tail_in, out_ref, tail_out = refs[n_tile_in + N_MERGE_WEIGHTS:]
    t = pl.program_id(0)

    @pl.when(t < steps - 1)
    def _():
        _mix_body(*tile_in, *merge_refs, out_ref, tile_in_seq=lax.rem(t, tiles_per_seq))

    @pl.when(t == steps - 1)
    def _():
        _mix_body(*tile_in, *merge_refs, out_ref, tile_in_seq=(steps - 1) % tiles_per_seq,
                  tail=(*tail_in, tail_out))


def _mix(os, lses, z, qm, mkv, x, u, weights, n, s, tail):
    tm = TOKEN_TILE
    tj = s // tm
    steps = n * tj
    last = steps - 1
    halo_per_tile = tm // POOL_HALO
    tile = lambda t: jnp.minimum(t, last)

    def tok(cols):
        return pl.BlockSpec((tm, cols), lambda t: (tile(t), 0))

    halo = pl.BlockSpec((POOL_HALO, POOL_WIDTH),
                        lambda t: (jnp.maximum(tile(t) * halo_per_tile - 1, 0), 0))
    slab = pl.BlockSpec((GROUP_COLS // LANES, tm, LANES), lambda t: (0, tile(t), 0))
    weight_specs = _merge_weight_specs()
    assert len(weight_specs) == N_MERGE_WEIGHTS
    in_specs = ([slab] * 6 + [tok(POOL_WIDTH), halo, tok(MEM_WIDTH),
                pl.BlockSpec((MEM_LEN * MEM_SLABS, LANES), lambda t: (tile(t) // tj, 0)),
                tok(D_MODEL), tok(D_MODEL)] + weight_specs + [_resident(t.shape) for t in tail])
    xs = tail[3]
    return pl.pallas_call(
        functools.partial(_mix_with_tail_body, tiles_per_seq=tj, steps=steps),
        grid=(steps,),
        in_specs=in_specs,
        out_specs=[tok(D_MODEL), pl.BlockSpec(xs.shape, lambda t: (0, 0))],
        out_shape=[jax.ShapeDtypeStruct((n * s, D_MODEL), F32),
                   jax.ShapeDtypeStruct(xs.shape, F32)],
        compiler_params=_params("arbitrary"),
        name="mix",
    )(*os, *lses, z, z, qm, mkv, x, u, *weights, *tail)


N_SAMPLE_INPUTS = 14


def _sample_branch_math(req, qkv0, qkv1, qkv2, c0, c1, c2, b0, b1, b2, bias0_ref, z_ref, st_ref,
                        qm_ref, cm_ref, a_ref, pooled_ref, c_ref):
    row = pl.ds(req, 1)
    eye = (lax.broadcasted_iota(jnp.int32, (HEAD_DIM, HEAD_DIM), 0)
           == lax.broadcasted_iota(jnp.int32, (HEAD_DIM, HEAD_DIM), 1))
    cube = (HEADS, HEAD_DIM, HEAD_DIM)

    def heads_of(vec, lo):
        return jnp.stack([vec[:, lo + h * HEAD_DIM:lo + (h + 1) * HEAD_DIM]
                          for h in range(HEADS)], axis=0)

    outs, lses = [], []
    for g, (qkv_ref, cache_ref, bias_ref) in enumerate(
            ((qkv0, c0, b0), (qkv1, c1, b1), (qkv2, c2, b2))):
        qkv = qkv_ref[row, :]
        q, kn, vn = heads_of(qkv, 0), heads_of(qkv, GROUP_COLS), heads_of(qkv, 2 * GROUP_COLS)
        q_col = jnp.sum(jnp.where(eye, jnp.broadcast_to(q, cube), 0.0), axis=2, keepdims=True)
        s = jnp.sum(cache_ref[0, 0] * q_col, axis=1, keepdims=True) + bias_ref[...]
        sn = jnp.sum(kn * q, axis=2, keepdims=True) + bias0_ref[g]
        m = jnp.maximum(jnp.max(s, axis=2, keepdims=True), sn)
        p = jnp.exp(s - m)
        pn = jnp.exp(sn - m)
        l = jnp.sum(p, axis=2, keepdims=True) + pn
        pv = jnp.sum(cache_ref[0, 1] * p, axis=2, keepdims=True)
        pv_row = jnp.sum(jnp.where(eye, jnp.broadcast_to(pv, cube), 0.0), axis=1, keepdims=True)
        outs.append((pv_row + pn * vn) / l)
        lses.append(m + jnp.log(l))
    mx = jnp.maximum(jnp.maximum(lses[0], lses[1]), lses[2])
    es = [jnp.exp(lse - mx) for lse in lses]
    a = (es[0] * outs[0] + es[1] * outs[1] + es[2] * outs[2]) / (es[0] + es[1] + es[2])
    a_ref[row, :] = jnp.concatenate([a[h] for h in range(HEADS)], axis=1)

    zn = z_ref[row, :]
    st = st_ref[:, row, :]
    pooled = []
    for gi, kw in enumerate(POOL_WINDOWS):
        cs = slice(gi * POOL_GROUP, (gi + 1) * POOL_GROUP)
        tot = jnp.sum(st[POOL_STATE - (kw - 1):, :, cs], axis=0) + zn[:, cs]
        pooled.append(tot / float(min(kw, PAST_LEN + 1)) - zn[:, cs])
    pooled_ref[row, :] = jnp.concatenate(pooled, axis=1)

    qm_row = qm_ref[row, :]
    qm = jnp.concatenate([qm_row[:, h * MEM_HEAD_DIM:(h + 1) * MEM_HEAD_DIM]
                          for h in range(MEM_HEADS)], axis=0)
    km = cm_ref[0, :, 0]
    vm = cm_ref[0, :, 1]
    s = jnp.sum(km * qm[None], axis=-1, keepdims=True) * (1.0 / math.sqrt(MEM_HEAD_DIM))
    m = jnp.max(s, axis=0)
    p = jnp.exp(s - m[None])
    c = jnp.sum(p * vm, axis=0) / jnp.sum(p, axis=0)
    c_ref[row, :] = jnp.concatenate([c[h:h + 1] for h in range(MEM_HEADS)], axis=1)


def _sample_branch_specs(operands, request_of):
    assert len(operands) == N_SAMPLE_INPUTS
    nb = operands[0].shape[0]

    def per_request(t):
        zeros = (0,) * (t.ndim - 1)
        return pl.BlockSpec((1,) + t.shape[1:], lambda *g: (request_of(*g),) + zeros)

    def whole(shape):
        zeros = (0,) * len(shape)
        return pl.BlockSpec(shape, lambda *g: zeros)

    streamed = (3, 4, 5, 13)
    in_specs = [per_request(t) if k in streamed else whole(t.shape)
                for k, t in enumerate(operands)]
    widths = (GROUP_COLS, POOL_WIDTH, MEM_WIDTH)
    return (in_specs, [whole((nb, w)) for w in widths],
            [jax.ShapeDtypeStruct((nb, w), F32) for w in widths])


def _rel_bucket(n):
    max_exact = N_BUCKETS // 2
    nf = jnp.maximum(n, 1).astype(F32)
    large = max_exact + (jnp.log(nf / max_exact) / math.log(MAX_DISTANCE / max_exact)
                         * (N_BUCKETS - max_exact)).astype(jnp.int32)
    large = jnp.minimum(large, N_BUCKETS - 1)
    return jnp.where(n < max_exact, n, large)


def _stride_bias(rel_bias, g, dil):
    j = jnp.arange(STRIDES + 1, dtype=jnp.int32)
    return rel_bias[_rel_bucket(j * dil)][:, g * HEADS:(g + 1) * HEADS].astype(F32)


def _band_row(bias_j):
    row = jnp.concatenate([bias_j[::-1], jnp.full((STRIDES - 1, HEADS), NEG_INF, F32)], axis=0)
    return row.T.reshape(HEADS, 1, 2 * STRIDES)


def _cache_bias(bias_j, dil):
    on_grid = bias_j[STRIDES:0:-1].T
    full = jnp.full((HEADS, STRIDES, dil), NEG_INF, F32).at[:, :, 0].set(on_grid)
    return full.reshape(HEADS, 1, STRIDES * dil)


def kernel(x_prompt, x_sample, cache_win0_kv, cache_win1_kv, cache_win2_kv, state_pool, cache_mem_kv, mem_prompt, rel_bias, g_ffn1, w1_gate, w1_up, w1_down, g_mix, w_in, w_pool, pool_scale, g_mem, w_mem_kv, w_oa, w_ob, w_oc, w_out, g_ffn2, w2_gate, w2_up, w2_down, g_final):
    n, s, _ = x_prompt.shape
    nb = x_sample.shape[0]
    depth = g_ffn1.shape[0]
    win_caches = (cache_win0_kv, cache_win1_kv, cache_win2_kv)
    bias_js = [_stride_bias(rel_bias, g, dil) for g, (_, dil) in enumerate(DIL_GROUPS)]
    bands = _bands(jnp.stack([_band_row(b) for b in bias_js]))
    bias_cache = [_cache_bias(b, dil) for b, (_, dil) in zip(bias_js, DIL_GROUPS)]
    bias_new = jnp.stack([b[0] for b in bias_js]).reshape(N_GROUPS, HEADS, 1, 1)
    gfin = g_final.reshape(1, D_MODEL)

    xp = x_prompt.reshape(n * s, D_MODEL)
    xs = x_sample.reshape(nb, D_MODEL)
    st_p = [[] for _ in range(5)]
    st_s = [[] for _ in range(4)]
    for l in range(depth):
        last = l == depth - 1
        vec = lambda v: v[l].reshape(1, -1)
        q_scale = 1.0 / math.sqrt(HEAD_DIM)

        later = [w2_gate[l], w2_up[l], w2_down[l], w_in[l], w_oa[l], w_ob[l], w_oc[l], w_out[l],
                 w_mem_kv[l], w_pool[l].reshape(len(POOL_WINDOWS) * POOL_GROUP, POOL_GROUP)]
        xp, later, xs = _ffn(xp, vec(g_ffn1), w1_gate[l], w1_up[l], w1_down[l], xs, casts=later)
        w2, (win_l, woa_l, wob_l, woc_l, wout_l, wmem_l, wpool_l) = later[0:3], later[3:]
        merge_w = (win_l, wpool_l.reshape(w_pool.shape[1:]), vec(pool_scale),
                   woa_l, wob_l, woc_l, wout_l)

        res = _inproj(xs, vec(g_mix), win_l, 1, nb, None, [1] * N_GROUPS, F32, q_scale)
        s_qkvs, s_z, s_qm, s_u = [t.reshape(nb, QKV_COLS) for t in res[0:3]], *res[3:6]
        rider = (s_qkvs + [jnp.transpose(cw[l], (0, 2, 3, 4, 1)) for cw in win_caches]
                 + bias_cache + [bias_new, s_z, jnp.transpose(state_pool[l], (1, 0, 2)), s_qm,
                                 cache_mem_kv[l]])
        res = _inproj(xp, vec(g_mix), win_l, n, s, [min(w, s) for w, _ in DIL_GROUPS],
                      [d for _, d in DIL_GROUPS], BF16, q_scale * LOG2E, rider)
        qkvs, kvwins, z, qm, u = res[0:3], res[3:6], res[6], res[7], res[8]
        s_branches = res[9:12]
        os, lses = [], []
        for g, (_, dil) in enumerate(DIL_GROUPS):
            o, lse = _attn(qkvs[g], bands, g, n, s, dil)
            os.append(o)
            lses.append(lse)
        mkv = _memkv(mem_prompt.reshape(n * MEM_LEN, D_MODEL), vec(g_mem), wmem_l)
        xp, xs = _mix(os, lses, z, qm, mkv, xp, u, merge_w, n, s, tail=(*s_branches, xs, s_u))
        xp, xs = _ffn(xp, vec(g_ffn2), *w2, xs, g_final=gfin if last else None)
        for g, (win, _) in enumerate(DIL_GROUPS):
            kv_t = kvwins[g].reshape(n, 2, HEADS, HEAD_DIM, min(win, s))
            st_p[g].append(jnp.transpose(kv_t, (0, 4, 1, 2, 3)))
        st_p[3].append(z.reshape(n, s, POOL_WIDTH)[:, s - POOL_STATE:])
        st_p[4].append(mkv.reshape(n, MEM_LEN, 2, MEM_HEADS, MEM_HEAD_DIM))

        for g in range(N_GROUPS):
            st_s[g].append(s_qkvs[g][:, GROUP_COLS:].reshape(nb, 1, 2, HEADS, HEAD_DIM))
        st_s[3].append(s_z.reshape(nb, 1, POOL_WIDTH))

    y_prompt = xp.reshape(n, s, D_MODEL)
    y_sample = xs.reshape(nb, 1, D_MODEL)
    stack = lambda ts: jnp.stack(ts, axis=0)
    return (y_prompt, y_sample, stack(st_p[0]), stack(st_p[1]), stack(st_p[2]), stack(st_p[3]),
            stack(st_p[4]), stack(st_s[0]), stack(st_s[1]), stack(st_s[2]), stack(st_s[3]))
```

```python
import functools
import math

import jax
import jax.numpy as jnp
from jax import lax
from jax.experimental import pallas as pl
from jax.experimental.pallas import tpu as pltpu

F32 = jnp.float32
BF16 = jnp.bfloat16

D_MODEL = 1024
D_FF = 2816
LANES = 128
HEAD_DIM = 64
HEADS = 4
DIL_GROUPS = ((128, 1), (512, 4), (2048, 16))
N_GROUPS = 3
GROUP_COLS = HEADS * HEAD_DIM
QKV_COLS = 3 * GROUP_COLS
STRIDES = 128
POOL_WINDOWS = (2, 4, 8, 16)
POOL_GROUP = 128
POOL_WIDTH = 512
POOL_STATE = 15
POOL_HALO = 16
MEM_LEN = 256
MEM_HEADS = 4
MEM_HEAD_DIM = 128
MEM_WIDTH = 512
N_BUCKETS = 32
MAX_DISTANCE = 2048
N_BRANCH = 3
EPS = 1e-6
NEG_INF = -1e30
LOG2E = math.log2(math.e)
LN2 = math.log(2.0)
PAST_LEN = 8192

V7X_VMEM_LIMIT_BYTES = 56 * 1024 * 1024
TOKEN_TILE = 512


def _params(*sem):
    return pltpu.CompilerParams(dimension_semantics=sem,
                                vmem_limit_bytes=V7X_VMEM_LIMIT_BYTES)


def _resident(shape):
    zeros = (0,) * len(shape)
    return pl.BlockSpec(shape, lambda *_: zeros, pipeline_mode=pl.Buffered(1))


def _rms(x, g):
    return x * lax.rsqrt(jnp.mean(x * x, axis=-1, keepdims=True) + EPS) * g


def _dot(a, b):
    return jnp.dot(a, b, preferred_element_type=F32)


def _dot_t(a, b):
    return lax.dot_general(a, b, (((1,), (1,)), ((), ())), preferred_element_type=F32)


BF16_SUBLANES = 16


FF_CHUNK = 256
FF_CHUNKS = D_FF // FF_CHUNK


def _ffn_math(x, g_ref, chunk, gf_ref):
    h = _rms(x, g_ref[...]).astype(BF16)
    acc = None
    for c in range(FF_CHUNKS):
        wg, wu, wd = chunk(c)
        a = _dot(h, wg)
        b = _dot(h, wu)
        part = _dot((a * jax.nn.sigmoid(a) * b).astype(BF16), wd)
        acc = part if acc is None else acc + part
    y = x + 0.5 * acc
    return y if gf_ref is None else _rms(y, gf_ref[...])


MODEL_SLABS = D_MODEL // LANES


def _as_slabs(rows):
    return (rows * MODEL_SLABS, LANES)


def _read_rows(ref):
    if ref.shape[1] == D_MODEL:
        return ref[...]
    n = ref.shape[0] // MODEL_SLABS
    return jnp.concatenate([ref[pl.ds(c, n, stride=MODEL_SLABS), :] for c in range(MODEL_SLABS)],
                           axis=1)


def _write_rows(ref, y):
    if ref.shape[1] == D_MODEL:
        ref[...] = y
        return
    for c in range(MODEL_SLABS):
        ref[pl.ds(c, y.shape[0], stride=MODEL_SLABS), :] = y[:, c * LANES:(c + 1) * LANES]


def _ffn_body(*refs, final, n_casts, steps, own_weights):
    refs = iter(refs)
    x_ref, g_ref, wg_ref, wu_ref, wd_ref = (next(refs) for _ in range(5))
    gf_ref = next(refs) if final else None
    cast_in = [next(refs) for _ in range(n_casts)]
    tail_in = next(refs)
    o_ref = next(refs)
    cast_out = [next(refs) for _ in range(n_casts)]
    tail_out = next(refs)
    t = pl.program_id(0)

    def convert_slices():
        for src, dst in zip(cast_in, cast_out):
            dst[...] = src[...].astype(BF16)

    if own_weights:
        wg_s, wu_s, wd_s, h_s, acc_s = (next(refs) for _ in range(5))
        base = FF_CHUNKS - 1
        full_from = base + 1

        @pl.when(t <= base)
        def _():
            wg_s[t] = wg_ref[...].astype(BF16)
            wu_s[t] = wu_ref[...].astype(BF16)
            wd_s[t] = wd_ref[...].astype(BF16)

            @pl.when(t == 0)
            def _():
                h_s[...] = _rms(x_ref[...], g_ref[...]).astype(BF16)
                acc_s[...] = jnp.zeros_like(acc_s)

            h = h_s[...]
            a = _dot(h, wg_s[t])
            b = _dot(h, wu_s[t])
            acc_s[...] += _dot((a * jax.nn.sigmoid(a) * b).astype(BF16), wd_s[t])

            @pl.when(t == base)
            def _():
                y = x_ref[...] + 0.5 * acc_s[...]
                o_ref[...] = y if gf_ref is None else _rms(y, gf_ref[...])
                convert_slices()

        chunk = lambda c: (wg_s[c], wu_s[c], wd_s[c])
    else:
        base = full_from = 0
        cols = lambda c: slice(c * FF_CHUNK, (c + 1) * FF_CHUNK)
        chunk = lambda c: (wg_ref[:, cols(c)], wu_ref[:, cols(c)], wd_ref[cols(c), :])

    last_step = base + steps - 1

    @pl.when(jnp.logical_and(t >= full_from, t < last_step))
    def _():
        o_ref[...] = _ffn_math(x_ref[...], g_ref, chunk, gf_ref)
        convert_slices()

    @pl.when(t == last_step)
    def _():
        rows = x_ref.shape[0]
        both = jnp.concatenate([x_ref[...], _read_rows(tail_in)], axis=0)
        y = _ffn_math(both, g_ref, chunk, gf_ref)
        o_ref[...] = y[:rows]
        _write_rows(tail_out, y[rows:])
        convert_slices()


def _ffn(x, g, wg, wu, wd, tail, g_final=None, casts=(), tail_out_slabs=False):
    m = x.shape[0]
    tm = min(TOKEN_TILE, m)
    steps = m // tm
    final = g_final is not None
    own_weights = wg.dtype == F32
    tail_rows = tail.size // D_MODEL
    tail_out_shape = _as_slabs(tail_rows) if tail_out_slabs else (tail_rows, D_MODEL)
    assert steps > 1 or not own_weights
    base = FF_CHUNKS - 1 if own_weights else 0
    last = steps - 1
    tile = lambda t: jnp.clip(t - base, 0, last)
    row = pl.BlockSpec((tm, D_MODEL), lambda t: (tile(t), 0))
    if own_weights:
        load = lambda t: jnp.minimum(t, FF_CHUNKS - 1)
        weight_specs = [pl.BlockSpec((D_MODEL, FF_CHUNK), lambda t: (0, load(t))),
                        pl.BlockSpec((D_MODEL, FF_CHUNK), lambda t: (0, load(t))),
                        pl.BlockSpec((FF_CHUNK, D_MODEL), lambda t: (load(t), 0))]
        scratch = [pltpu.VMEM((FF_CHUNKS, D_MODEL, FF_CHUNK), BF16),
                   pltpu.VMEM((FF_CHUNKS, D_MODEL, FF_CHUNK), BF16),
                   pltpu.VMEM((FF_CHUNKS, FF_CHUNK, D_MODEL), BF16),
                   pltpu.VMEM((tm, D_MODEL), BF16),
                   pltpu.VMEM((tm, D_MODEL), F32)]
    else:
        weight_specs = [_resident((D_MODEL, D_FF)), _resident((D_MODEL, D_FF)),
                        _resident((D_FF, D_MODEL))]
        scratch = []
    in_specs = [row, _resident((1, D_MODEL))] + weight_specs
    args = [x, g, wg, wu, wd]
    if final:
        in_specs.append(_resident((1, D_MODEL)))
        args.append(g_final)
    cast_specs = []
    for w in casts:
        blocks = math.gcd(steps, w.shape[0] // BF16_SUBLANES)
        per = steps // blocks
        cast_specs.append(pl.BlockSpec((w.shape[0] // blocks, w.shape[1]),
                                       lambda t, per=per: (tile(t) // per, 0)))
    n_casts = len(casts)
    out = pl.pallas_call(
        functools.partial(_ffn_body, final=final, n_casts=n_casts, steps=steps,
                          own_weights=own_weights),
        grid=(base + steps,),
        in_specs=in_specs + cast_specs + [_resident(tail.shape)],
        out_specs=[row] + cast_specs + [pl.BlockSpec(tail_out_shape, lambda t: (0, 0))],
        out_shape=[jax.ShapeDtypeStruct((m, D_MODEL), F32)]
        + [jax.ShapeDtypeStruct(w.shape, BF16) for w in casts]
        + [jax.ShapeDtypeStruct(tail_out_shape, F32)],
        scratch_shapes=scratch,
        compiler_params=_params("arbitrary"),
        name="ffn_final" if final else "ffn",
    )(*args, *casts, tail)
    return (out[0], out[1:1 + n_casts], out[1 + n_casts]) if casts else (out[0], out[1])


Z_OFF = N_GROUPS * QKV_COLS
QM_OFF = Z_OFF + POOL_WIDTH
GATE_OFF = QM_OFF + MEM_WIDTH
IN_COLS = GATE_OFF + N_BRANCH * D_MODEL


def _inproj_body(x_ref, g_ref, w_ref, *rest, q_scale, with_windows, under_matmul=None):
    qkv_refs = rest[0:N_GROUPS]
    kv_refs = rest[N_GROUPS:2 * N_GROUPS] if with_windows else (None,) * N_GROUPS
    z_ref, qm_ref, u_ref, p_scr = rest[-4:]
    u = _rms(x_ref[...], g_ref[...]).astype(BF16)
    u_ref[...] = u
    tm = u.shape[0]
    width = N_GROUPS * GROUP_COLS
    p = _dot(u, w_ref[...])
    if under_matmul is not None:
        under_matmul()
    for g, (qkv_ref, kv_ref) in enumerate(zip(qkv_refs, kv_refs)):
        dil, per_class = qkv_ref.shape[1], qkv_ref.shape[2]
        q, k, v = [p[:, t * width + g * GROUP_COLS:t * width + (g + 1) * GROUP_COLS]
                   for t in range(3)]
        qkv = jnp.concatenate([q * q_scale, k, v], axis=1)
        if dil == 1:
            qkv_ref[0, 0] = qkv.astype(qkv_ref.dtype)
        else:
            for c in range(QKV_COLS // LANES):
                p_scr[c] = qkv[:, c * LANES:(c + 1) * LANES]
            for r in range(dil):
                blk = jnp.concatenate([p_scr[c, pl.ds(r, per_class, stride=dil), :]
                                       for c in range(QKV_COLS // LANES)], axis=1)
                qkv_ref[0, r] = blk.astype(qkv_ref.dtype)
        if with_windows:
            rows = kv_ref.shape[2]
            kv_ref[0] = jnp.concatenate([k[tm - rows:], v[tm - rows:]], axis=1).T
    z_ref[...] = p[:, Z_OFF:QM_OFF]
    qm_ref[...] = p[:, QM_OFF:GATE_OFF].astype(qm_ref.dtype)


def _inproj(x, g, w, n, s, windows, dils, q_dtype, q_scale, rider=()):
    tm = min(TOKEN_TILE, s)
    tj = s // tm

    def tok(cols):
        return pl.BlockSpec((tm, cols), lambda b, j: (b * tj + j, 0))

    qkv_specs = [pl.BlockSpec((1, d, tm // d, QKV_COLS), lambda b, j: (b, 0, j, 0)) for d in dils]
    qkv_shapes = [jax.ShapeDtypeStruct((n, d, s // d, QKV_COLS), q_dtype) for d in dils]

    kv_specs, kv_shapes = [], []
    for win in windows or ():
        rows = min(win, tm)
        assert tm % rows == 0 and win % rows == 0
        first = (s - win) // tm
        if win >= tm:
            spec = pl.BlockSpec((1, 2 * GROUP_COLS, rows),
                                lambda b, j, first=first: (b, 0, jnp.maximum(j - first, 0)))
        else:
            spec = pl.BlockSpec((1, 2 * GROUP_COLS, rows), lambda b, j: (b, 0, 0))
        kv_specs.append(spec)
        kv_shapes.append(jax.ShapeDtypeStruct((n, 2 * GROUP_COLS, win), F32))
    m = n * s
    out_shape = (qkv_shapes + kv_shapes + [
        jax.ShapeDtypeStruct((m, POOL_WIDTH), F32),
        jax.ShapeDtypeStruct((m, MEM_WIDTH), q_dtype),
        jax.ShapeDtypeStruct((m, D_MODEL), BF16)])
    out_specs = qkv_specs + kv_specs + [tok(POOL_WIDTH), tok(MEM_WIDTH), tok(D_MODEL)]
    in_specs = [tok(D_MODEL), _resident((1, D_MODEL)),
                pl.BlockSpec((D_MODEL, GATE_OFF), lambda b, j: (0, 0),
                             pipeline_mode=pl.Buffered(1))]
    body = functools.partial(_inproj_body, q_scale=q_scale, with_windows=bool(windows))
    n_own_out = len(out_specs)
    if rider:
        assert rider[0].shape[0] == n * tj
        r_in, r_out, r_shapes = _sample_branch_specs(rider, lambda b, j: b * tj + j)
        in_specs, out_specs, out_shape = in_specs + r_in, out_specs + r_out, out_shape + r_shapes

        def body(*refs, own=body):
            ins, r_ins = refs[:3], refs[3:3 + N_SAMPLE_INPUTS]
            outs, scratch = refs[3 + N_SAMPLE_INPUTS:-1], refs[-1:]
            own(*ins, *outs[:n_own_out], *scratch, under_matmul=lambda: _sample_branch_math(
                pl.program_id(0) * tj + pl.program_id(1), *r_ins, *outs[n_own_out:]))

    return pl.pallas_call(
        body,
        grid=(n, tj),
        in_specs=in_specs,
        out_specs=out_specs,
        out_shape=out_shape,
        scratch_shapes=[pltpu.VMEM((QKV_COLS // LANES, tm, LANES), F32)],
        compiler_params=_params("arbitrary", "arbitrary"),
        name="inproj",
    )(x, g, w, *rider)


ATTN_CHUNKS = 32
HEAD_LANES = HEADS * STRIDES


def _band_body(row_ref, band_ref):
    for g in range(N_GROUPS):
        for h in range(HEADS):
            rows = jnp.broadcast_to(row_ref[g, h] * LOG2E, (STRIDES, 2 * STRIDES))
            band = pltpu.roll(rows, 0, 1, stride=1, stride_axis=0).T
            band_ref[g, :, h * STRIDES:(h + 1) * STRIDES] = band


def _bands(rows):
    return pl.pallas_call(
        _band_body,
        grid=(1,),
        in_specs=[_resident(rows.shape)],
        out_specs=pl.BlockSpec((N_GROUPS, 2 * STRIDES, HEAD_LANES), lambda i: (0, 0, 0)),
        out_shape=jax.ShapeDtypeStruct((N_GROUPS, 2 * STRIDES, HEAD_LANES), F32),
        compiler_params=_params("arbitrary"),
        name="bands",
    )(rows)


def _attn_body(qkv_ref, band_ref, o_ref, lse_ref, *, dil, chunks):
    lane_head = lax.broadcasted_iota(jnp.int32, (1, GROUP_COLS), 1) // HEAD_DIM

    def keys(i):
        return slice(max(i - 1, 0) * STRIDES, (i + 1) * STRIDES)

    def scores(r, i):
        q = qkv_ref[0, r, i * STRIDES:(i + 1) * STRIDES, 0:GROUP_COLS]
        qm = jnp.concatenate(
            [jnp.where(lane_head == h, q, jnp.zeros_like(q)) for h in range(HEADS)], axis=0)
        return _dot_t(qkv_ref[0, r, keys(i), GROUP_COLS:2 * GROUP_COLS], qm)

    def ones_row(n_keys):
        return jnp.where(lax.broadcasted_iota(jnp.int32, (BF16_SUBLANES, n_keys), 0) == 0,
                         1.0, 0.0).astype(BF16)

    order = [(r, i) for r in range(dil) for i in range(chunks)]
    st_next = scores(0, 0)
    for idx, (r, i) in enumerate(order):
        st = st_next
        if i == 0:
            vt_cls = qkv_ref[0, r, :, 2 * GROUP_COLS:].T
        if idx + 1 < len(order):
            st_next = scores(*order[idx + 1])
        vt2 = vt_cls[:, keys(i)]
        band_lo = STRIDES if i == 0 else 0
        o_parts, lse_parts = [], []
        for h in range(HEADS):
            hl = slice(h * STRIDES, (h + 1) * STRIDES)
            s_h = st[:, hl] + band_ref[band_lo:, hl]
            m = jnp.max(s_h, axis=0, keepdims=True)
            e = jnp.exp2(s_h - m).astype(BF16)
            ot = _dot(jnp.concatenate([vt2[h * HEAD_DIM:(h + 1) * HEAD_DIM],
                                       ones_row(vt2.shape[1])], axis=0), e)
            l = ot[HEAD_DIM:HEAD_DIM + 1]
            o_parts.append(ot[:HEAD_DIM] * (1.0 / l))
            lse_parts.append(jnp.broadcast_to(m * LN2 + jnp.log(l), (HEAD_DIM, STRIDES)))
        rows = pl.ds(i * STRIDES * dil + r, STRIDES, stride=dil)
        o_rows = jnp.concatenate(o_parts, axis=0).T
        lse_rows = jnp.concatenate(lse_parts, axis=0).T
        for c in range(GROUP_COLS // LANES):
            o_ref[c, rows, :] = o_rows[:, c * LANES:(c + 1) * LANES]
            lse_ref[c, rows, :] = lse_rows[:, c * LANES:(c + 1) * LANES]


def _attn(qkv, bands, g, n, s, dil):
    chunks = s // (STRIDES * dil)
    assert chunks * dil == ATTN_CHUNKS
    span = chunks * STRIDES
    slabs = GROUP_COLS // LANES
    out_spec = pl.BlockSpec((slabs, s, LANES), lambda b: (0, b, 0))
    out_sds = jax.ShapeDtypeStruct((slabs, n * s, LANES), F32)
    return pl.pallas_call(
        functools.partial(_attn_body, dil=dil, chunks=chunks),
        grid=(n,),
        in_specs=[pl.BlockSpec((1, dil, span, QKV_COLS), lambda b: (b, 0, 0, 0)),
                  pl.BlockSpec((None, 2 * STRIDES, HEAD_LANES), lambda b: (g, 0, 0))],
        out_specs=[out_spec, out_spec],
        out_shape=[out_sds, out_sds],
        compiler_params=_params("parallel"),
        name="attn_d%d" % dil,
    )(qkv, bands)


MEM_SLABS = 2 * MEM_WIDTH // LANES


def _mem_slab(c):
    return pl.ds(c, MEM_LEN, stride=MEM_SLABS)


def _memkv_body(mem_ref, g_ref, w_ref, o_ref):
    kv = _dot(_rms(mem_ref[...], g_ref[...]).astype(BF16), w_ref[...])
    for c in range(MEM_SLABS):
        o_ref[_mem_slab(c), :] = kv[:, c * LANES:(c + 1) * LANES]


def _memkv(mem, g, w):
    assert MEM_HEAD_DIM == LANES
    m = mem.shape[0]
    row = pl.BlockSpec((MEM_LEN, D_MODEL), lambda i: (i, 0))
    return pl.pallas_call(
        _memkv_body,
        grid=(m // MEM_LEN,),
        in_specs=[row, _resident((1, D_MODEL)), _resident((D_MODEL, 2 * MEM_WIDTH))],
        out_specs=pl.BlockSpec((MEM_LEN * MEM_SLABS, LANES), lambda i: (i, 0)),
        out_shape=jax.ShapeDtypeStruct((m * MEM_SLABS, LANES), F32),
        compiler_params=_params("parallel"),
        name="memkv",
    )(mem, g, w)


def _merge_math(a, pooled, c, x, u, win_ref, wpool_ref, scale_ref, woa_ref, wob_ref,
                woc_ref, wout_ref):
    mixed = [_dot(pooled[gi].astype(BF16), wpool_ref[gi]) for gi in range(len(POOL_WINDOWS))]
    b = jnp.concatenate(mixed, axis=1) * scale_ref[...]
    m = None
    for k, (branch, wo_ref) in enumerate(((a, woa_ref), (b, wob_ref), (c, woc_ref))):
        lo = GATE_OFF + k * D_MODEL
        gate = jax.nn.sigmoid(_dot(u, win_ref[:, lo:lo + D_MODEL]))
        term = gate * _dot(branch.astype(BF16), wo_ref[...])
        m = term if m is None else m + term
    return x + _dot(m.astype(BF16), wout_ref[...])


def _mix_body(o0, o1, o2, l0, l1, l2, z_ref, halo_ref, qm_ref, mkv_ref, x_ref, u_ref, *rest,
              tile_in_seq, tail=None):
    merge_refs, out_ref = rest[:-1], rest[-1]
    j = tile_in_seq
    tm = x_ref.shape[0]
    unslab = lambda ref: jnp.concatenate([ref[c] for c in range(ref.shape[0])], axis=1)
    lses = [unslab(l0), unslab(l1), unslab(l2)]
    mx = jnp.maximum(jnp.maximum(lses[0], lses[1]), lses[2])
    es = [jnp.exp(l - mx) for l in lses]
    a = ((es[0] * unslab(o0) + es[1] * unslab(o1) + es[2] * unslab(o2))
         / (es[0] + es[1] + es[2]))
    z = z_ref[...]
    halo = jnp.where(j == 0, 0.0, halo_ref[...])
    zc = jnp.concatenate([halo, z], axis=0)
    pos = j * tm + lax.broadcasted_iota(jnp.int32, (tm, 1), 0)
    pooled = []
    for gi, kw in enumerate(POOL_WINDOWS):
        cs = slice(gi * POOL_GROUP, (gi + 1) * POOL_GROUP)
        run = zc[:, cs]
        width = 1
        while width < kw:
            run = run[width:] + run[:-width]
            width *= 2
        first = POOL_HALO - (kw - 1)
        cnt = jnp.minimum(kw, pos + 1).astype(F32)
        pooled.append(run[first:first + tm] / cnt - z[:, cs])
    qm = qm_ref[...]
    cs_out = []
    for h in range(MEM_HEADS):
        hs = slice(h * MEM_HEAD_DIM, (h + 1) * MEM_HEAD_DIM)
        k_h = mkv_ref[_mem_slab(h), :].astype(BF16)
        v_h = mkv_ref[_mem_slab(MEM_HEADS + h), :].astype(BF16)
        s = _dot_t(qm[:, hs], k_h) * (1.0 / math.sqrt(MEM_HEAD_DIM))
        mm = jnp.max(s, axis=-1, keepdims=True)
        p = jnp.exp(s - mm)
        l = jnp.sum(p, axis=-1, keepdims=True)
        cs_out.append(_dot(p.astype(BF16), v_h) / l)
    c = jnp.concatenate(cs_out, axis=1)
    x, u = x_ref[...], u_ref[...]
    if tail is not None:
        a_ref, pooled_ref, c_ref, xs_ref, us_ref, tail_out = tail
        under = lambda top, bottom: jnp.concatenate([top, bottom], axis=0)
        pooled_tail = pooled_ref[...]
        pooled = [under(p, pooled_tail[:, gi * POOL_GROUP:(gi + 1) * POOL_GROUP])
                  for gi, p in enumerate(pooled)]
        a, c = under(a, a_ref[...]), under(c, c_ref[...])
        x, u = under(x, xs_ref[...]), under(u, us_ref[...])
    y = _merge_math(a, pooled, c, x, u, *merge_refs)
    out_ref[...] = y[:tm]
    if tail is not None:
        tail_out[...] = y[tm:]


def _merge_weight_specs():
    return [_resident((D_MODEL, IN_COLS)),
            _resident((len(POOL_WINDOWS), POOL_GROUP, POOL_GROUP)), _resident((1, POOL_WIDTH)),
            _resident((GROUP_COLS, D_MODEL)), _resident((POOL_WIDTH, D_MODEL)),
            _resident((MEM_WIDTH, D_MODEL)), _resident((D_MODEL, D_MODEL))]


N_MERGE_WEIGHTS = 7


def _mix_with_tail_body(*refs, tiles_per_seq, steps):
    n_tile_in = 12
    tile_in, merge_refs = refs[:n_tile_in], refs[n_tile_in:n_tile_in + N_MERGE_WEIGHTS]
    *tail_in, out_ref, tail_out = refs[n_tile_in + N_MERGE_WEIGHTS:]
    t = pl.program_id(0)

    @pl.when(t < steps - 1)
    def _():
        _mix_body(*tile_in, *merge_refs, out_ref, tile_in_seq=lax.rem(t, tiles_per_seq))

    @pl.when(t == steps - 1)
    def _():
        _mix_body(*tile_in, *merge_refs, out_ref, tile_in_seq=(steps - 1) % tiles_per_seq,
                  tail=(*tail_in, tail_out))


def _mix(os, lses, z, qm, mkv, x, u, weights, n, s, tail):
    tm = TOKEN_TILE
    tj = s // tm
    steps = n * tj
    last = steps - 1
    halo_per_tile = tm // POOL_HALO
    tile = lambda t: jnp.minimum(t, last)

    def tok(cols):
        return pl.BlockSpec((tm, cols), lambda t: (tile(t), 0))

    halo = pl.BlockSpec((POOL_HALO, POOL_WIDTH),
                        lambda t: (jnp.maximum(tile(t) * halo_per_tile - 1, 0), 0))
    slab = pl.BlockSpec((GROUP_COLS // LANES, tm, LANES), lambda t: (0, tile(t), 0))
    weight_specs = _merge_weight_specs()
    assert len(weight_specs) == N_MERGE_WEIGHTS
    in_specs = ([slab] * 6 + [tok(POOL_WIDTH), halo, tok(MEM_WIDTH),
                pl.BlockSpec((MEM_LEN * MEM_SLABS, LANES), lambda t: (tile(t) // tj, 0)),
                tok(D_MODEL), tok(D_MODEL)] + weight_specs + [_resident(t.shape) for t in tail])
    xs = tail[3]
    return pl.pallas_call(
        functools.partial(_mix_with_tail_body, tiles_per_seq=tj, steps=steps),
        grid=(steps,),
        in_specs=in_specs,
        out_specs=[tok(D_MODEL), pl.BlockSpec(xs.shape, lambda t: (0, 0))],
        out_shape=[jax.ShapeDtypeStruct((n * s, D_MODEL), F32),
                   jax.ShapeDtypeStruct(xs.shape, F32)],
        compiler_params=_params("arbitrary"),
        name="mix",
    )(*os, *lses, z, z, qm, mkv, x, u, *weights, *tail)


N_SAMPLE_INPUTS = 14


def _sample_branch_math(req, qkv0, qkv1, qkv2, c0, c1, c2, b0, b1, b2, bias0_ref, z_ref, st_ref,
                        qm_ref, cm_ref, a_ref, pooled_ref, c_ref):
    row = pl.ds(req, 1)
    eye = (lax.broadcasted_iota(jnp.int32, (HEAD_DIM, HEAD_DIM), 0)
           == lax.broadcasted_iota(jnp.int32, (HEAD_DIM, HEAD_DIM), 1))
    cube = (HEADS, HEAD_DIM, HEAD_DIM)

    def heads_of(vec, lo):
        return jnp.stack([vec[:, lo + h * HEAD_DIM:lo + (h + 1) * HEAD_DIM]
                          for h in range(HEADS)], axis=0)

    outs, lses = [], []
    for g, (qkv_ref, cache_ref, bias_ref) in enumerate(
            ((qkv0, c0, b0), (qkv1, c1, b1), (qkv2, c2, b2))):
        qkv = qkv_ref[row, :]
        q, kn, vn = heads_of(qkv, 0), heads_of(qkv, GROUP_COLS), heads_of(qkv, 2 * GROUP_COLS)
        q_col = jnp.sum(jnp.where(eye, jnp.broadcast_to(q, cube), 0.0), axis=2, keepdims=True)
        s = jnp.sum(cache_ref[0, 0] * q_col, axis=1, keepdims=True) + bias_ref[...]
        sn = jnp.sum(kn * q, axis=2, keepdims=True) + bias0_ref[g]
        m = jnp.maximum(jnp.max(s, axis=2, keepdims=True), sn)
        p = jnp.exp(s - m)
        pn = jnp.exp(sn - m)
        l = jnp.sum(p, axis=2, keepdims=True) + pn
        pv = jnp.sum(cache_ref[0, 1] * p, axis=2, keepdims=True)
        pv_row = jnp.sum(jnp.where(eye, jnp.broadcast_to(pv, cube), 0.0), axis=1, keepdims=True)
        outs.append((pv_row + pn * vn) / l)
        lses.append(m + jnp.log(l))
    mx = jnp.maximum(jnp.maximum(lses[0], lses[1]), lses[2])
    es = [jnp.exp(lse - mx) for lse in lses]
    a = (es[0] * outs[0] + es[1] * outs[1] + es[2] * outs[2]) / (es[0] + es[1] + es[2])
    a_ref[row, :] = jnp.concatenate([a[h] for h in range(HEADS)], axis=1)

    zn = z_ref[row, :]
    st = st_ref[:, row, :]
    pooled = []
    for gi, kw in enumerate(POOL_WINDOWS):
        cs = slice(gi * POOL_GROUP, (gi + 1) * POOL_GROUP)
        tot = jnp.sum(st[POOL_STATE - (kw - 1):, :, cs], axis=0) + zn[:, cs]
        pooled.append(tot / float(min(kw, PAST_LEN + 1)) - zn[:, cs])
    pooled_ref[row, :] = jnp.concatenate(pooled, axis=1)

    qm_row = qm_ref[row, :]
    qm = jnp.concatenate([qm_row[:, h * MEM_HEAD_DIM:(h + 1) * MEM_HEAD_DIM]
                          for h in range(MEM_HEADS)], axis=0)
    km = cm_ref[0, :, 0]
    vm = cm_ref[0, :, 1]
    s = jnp.sum(km * qm[None], axis=-1, keepdims=True) * (1.0 / math.sqrt(MEM_HEAD_DIM))
    m = jnp.max(s, axis=0)
    p = jnp.exp(s - m[None])
    c = jnp.sum(p * vm, axis=0) / jnp.sum(p, axis=0)
    c_ref[row, :] = jnp.concatenate([c[h:h + 1] for h in range(MEM_HEADS)], axis=1)


def _sample_branch_specs(operands, request_of):
    assert len(operands) == N_SAMPLE_INPUTS
    nb = operands[0].shape[0]

    def per_request(t):
        zeros = (0,) * (t.ndim - 1)
        return pl.BlockSpec((1,) + t.shape[1:], lambda *g: (request_of(*g),) + zeros)

    def whole(shape):
        zeros = (0,) * len(shape)
        return pl.BlockSpec(shape, lambda *g: zeros)

    streamed = (3, 4, 5, 13)
    in_specs = [per_request(t) if k in streamed else whole(t.shape)
                for k, t in enumerate(operands)]
    widths = (GROUP_COLS, POOL_WIDTH, MEM_WIDTH)
    return (in_specs, [whole((nb, w)) for w in widths],
            [jax.ShapeDtypeStruct((nb, w), F32) for w in widths])


def _rel_bucket(n):
    max_exact = N_BUCKETS // 2
    nf = jnp.maximum(n, 1).astype(F32)
    large = max_exact + (jnp.log(nf / max_exact) / math.log(MAX_DISTANCE / max_exact)
                         * (N_BUCKETS - max_exact)).astype(jnp.int32)
    large = jnp.minimum(large, N_BUCKETS - 1)
    return jnp.where(n < max_exact, n, large)


def _stride_bias(rel_bias, g, dil):
    j = jnp.arange(STRIDES + 1, dtype=jnp.int32)
    return rel_bias[_rel_bucket(j * dil)][:, g * HEADS:(g + 1) * HEADS].astype(F32)


def _band_row(bias_j):
    row = jnp.concatenate([bias_j[::-1], jnp.full((STRIDES - 1, HEADS), NEG_INF, F32)], axis=0)
    return row.T.reshape(HEADS, 1, 2 * STRIDES)


def _cache_bias(bias_j, dil):
    on_grid = bias_j[STRIDES:0:-1].T
    full = jnp.full((HEADS, STRIDES, dil), NEG_INF, F32).at[:, :, 0].set(on_grid)
    return full.reshape(HEADS, 1, STRIDES * dil)


def kernel(x_prompt, x_sample, cache_win0_kv, cache_win1_kv, cache_win2_kv, state_pool, cache_mem_kv, mem_prompt, rel_bias, g_ffn1, w1_gate, w1_up, w1_down, g_mix, w_in, w_pool, pool_scale, g_mem, w_mem_kv, w_oa, w_ob, w_oc, w_out, g_ffn2, w2_gate, w2_up, w2_down, g_final):
    n, s, _ = x_prompt.shape
    nb = x_sample.shape[0]
    depth = g_ffn1.shape[0]
    win_caches = (cache_win0_kv, cache_win1_kv, cache_win2_kv)
    bias_js = [_stride_bias(rel_bias, g, dil) for g, (_, dil) in enumerate(DIL_GROUPS)]
    bands = _bands(jnp.stack([_band_row(b) for b in bias_js]))
    bias_cache = [_cache_bias(b, dil) for b, (_, dil) in zip(bias_js, DIL_GROUPS)]
    bias_new = jnp.stack([b[0] for b in bias_js]).reshape(N_GROUPS, HEADS, 1, 1)
    gfin = g_final.reshape(1, D_MODEL)

    xp = x_prompt.reshape(n * s, D_MODEL)
    xs = x_sample.reshape(_as_slabs(nb))
    st_p = [[] for _ in range(5)]
    st_s = [[] for _ in range(4)]
    for l in range(depth):
        last = l == depth - 1
        vec = lambda v: v[l].reshape(1, -1)
        q_scale = 1.0 / math.sqrt(HEAD_DIM)

        later = [w2_gate[l], w2_up[l], w2_down[l], w_in[l], w_oa[l], w_ob[l], w_oc[l], w_out[l],
                 w_mem_kv[l], w_pool[l].reshape(len(POOL_WINDOWS) * POOL_GROUP, POOL_GROUP)]
        xp, later, xs = _ffn(xp, vec(g_ffn1), w1_gate[l], w1_up[l], w1_down[l], xs, casts=later)
        w2, (win_l, woa_l, wob_l, woc_l, wout_l, wmem_l, wpool_l) = later[0:3], later[3:]
        merge_w = (win_l, wpool_l.reshape(w_pool.shape[1:]), vec(pool_scale),
                   woa_l, wob_l, woc_l, wout_l)

        res = _inproj(xs, vec(g_mix), win_l, 1, nb, None, [1] * N_GROUPS, F32, q_scale)
        s_qkvs, s_z, s_qm, s_u = [t.reshape(nb, QKV_COLS) for t in res[0:3]], *res[3:6]
        rider = (s_qkvs + [jnp.transpose(cw[l], (0, 2, 3, 4, 1)) for cw in win_caches]
                 + bias_cache + [bias_new, s_z, jnp.transpose(state_pool[l], (1, 0, 2)), s_qm,
                                 cache_mem_kv[l]])
        res = _inproj(xp, vec(g_mix), win_l, n, s, [min(w, s) for w, _ in DIL_GROUPS],
                      [d for _, d in DIL_GROUPS], BF16, q_scale * LOG2E, rider)
        qkvs, kvwins, z, qm, u = res[0:3], res[3:6], res[6], res[7], res[8]
        s_branches = res[9:12]
        os, lses = [], []
        for g, (_, dil) in enumerate(DIL_GROUPS):
            o, lse = _attn(qkvs[g], bands, g, n, s, dil)
            os.append(o)
            lses.append(lse)
        mkv = _memkv(mem_prompt.reshape(n * MEM_LEN, D_MODEL), vec(g_mem), wmem_l)
        xp, xs = _mix(os, lses, z, qm, mkv, xp, u, merge_w, n, s, tail=(*s_branches, xs, s_u))
        xp, xs = _ffn(xp, vec(g_ffn2), *w2, xs, g_final=gfin if last else None,
                      tail_out_slabs=True)
        for g, (win, _) in enumerate(DIL_GROUPS):
            kv_t = kvwins[g].reshape(n, 2, HEADS, HEAD_DIM, min(win, s))
            st_p[g].append(jnp.transpose(kv_t, (0, 4, 1, 2, 3)))
        st_p[3].append(z.reshape(n, s, POOL_WIDTH)[:, s - POOL_STATE:])
        st_p[4].append(mkv.reshape(n, MEM_LEN, 2, MEM_HEADS, MEM_HEAD_DIM))

        for g in range(N_GROUPS):
            st_s[g].append(s_qkvs[g][:, GROUP_COLS:].reshape(nb, 1, 2, HEADS, HEAD_DIM))
        st_s[3].append(s_z.reshape(nb, 1, POOL_WIDTH))

    y_prompt = xp.reshape(n, s, D_MODEL)
    y_sample = xs.reshape(nb, 1, D_MODEL)
    stack = lambda ts: jnp.stack(ts, axis=0)
    return (y_prompt, y_sample, stack(st_p[0]), stack(st_p[1]), stack(st_p[2]), stack(st_p[3]),
            stack(st_p[4]), stack(st_s[0]), stack(st_s[1]), stack(st_s[2]), stack(st_s[3]))
```

```python
import functools
import math

import jax
import jax.numpy as jnp
from jax import lax
from jax.experimental import pallas as pl
from jax.experimental.pallas import tpu as pltpu

F32 = jnp.float32
BF16 = jnp.bfloat16

D_MODEL = 1024
D_FF = 2816
LANES = 128
HEAD_DIM = 64
HEADS = 4
DIL_GROUPS = ((128, 1), (512, 4), (2048, 16))
N_GROUPS = 3
GROUP_COLS = HEADS * HEAD_DIM
QKV_COLS = 3 * GROUP_COLS
STRIDES = 128
POOL_WINDOWS = (2, 4, 8, 16)
POOL_GROUP = 128
POOL_WIDTH = 512
POOL_STATE = 15
POOL_HALO = 16
MEM_LEN = 256
MEM_HEADS = 4
MEM_HEAD_DIM = 128
MEM_WIDTH = 512
N_BUCKETS = 32
MAX_DISTANCE = 2048
N_BRANCH = 3
EPS = 1e-6
NEG_INF = -1e30
LOG2E = math.log2(math.e)
PAST_LEN = 8192

V7X_VMEM_LIMIT_BYTES = 56 * 1024 * 1024
TOKEN_TILE = 512


def _params(*sem):
    return pltpu.CompilerParams(dimension_semantics=sem,
                                vmem_limit_bytes=V7X_VMEM_LIMIT_BYTES)


def _resident(shape):
    zeros = (0,) * len(shape)
    return pl.BlockSpec(shape, lambda *_: zeros, pipeline_mode=pl.Buffered(1))


def _rms(x, g):
    return x * lax.rsqrt(jnp.mean(x * x, axis=-1, keepdims=True) + EPS) * g


def _dot(a, b):
    return jnp.dot(a, b, preferred_element_type=F32)


def _dot_t(a, b):
    return lax.dot_general(a, b, (((1,), (1,)), ((), ())), preferred_element_type=F32)


BF16_SUBLANES = 16


FF_CHUNK = 256
FF_CHUNKS = D_FF // FF_CHUNK


def _ffn_math(x, g_ref, chunk, gf_ref):
    h = _rms(x, g_ref[...]).astype(BF16)
    acc = None
    for c in range(FF_CHUNKS):
        wg, wu, wd = chunk(c)
        a = _dot(h, wg)
        b = _dot(h, wu)
        part = _dot((a * jax.nn.sigmoid(a) * b).astype(BF16), wd)
        acc = part if acc is None else acc + part
    y = x + 0.5 * acc
    return y if gf_ref is None else _rms(y, gf_ref[...])


MODEL_SLABS = D_MODEL // LANES


def _as_slabs(rows):
    return (rows * MODEL_SLABS, LANES)


def _read_rows(ref):
    if ref.shape[1] == D_MODEL:
        return ref[...]
    n = ref.shape[0] // MODEL_SLABS
    return jnp.concatenate([ref[pl.ds(c, n, stride=MODEL_SLABS), :] for c in range(MODEL_SLABS)],
                           axis=1)


def _write_rows(ref, y):
    if ref.shape[1] == D_MODEL:
        ref[...] = y
        return
    for c in range(MODEL_SLABS):
        ref[pl.ds(c, y.shape[0], stride=MODEL_SLABS), :] = y[:, c * LANES:(c + 1) * LANES]


def _ffn_body(*refs, final, n_casts, steps, own_weights):
    refs = iter(refs)
    x_ref, g_ref, wg_ref, wu_ref, wd_ref = (next(refs) for _ in range(5))
    gf_ref = next(refs) if final else None
    cast_in = [next(refs) for _ in range(n_casts)]
    tail_in = next(refs)
    o_ref = next(refs)
    cast_out = [next(refs) for _ in range(n_casts)]
    tail_out = next(refs)
    t = pl.program_id(0)

    def convert_slices():
        for src, dst in zip(cast_in, cast_out):
            dst[...] = src[...].astype(BF16)

    if own_weights:
        wg_s, wu_s, wd_s, h_s, acc_s = (next(refs) for _ in range(5))
        base = FF_CHUNKS - 1
        full_from = base + 1

        @pl.when(t <= base)
        def _():
            wg_s[t] = wg_ref[...].astype(BF16)
            wu_s[t] = wu_ref[...].astype(BF16)
            wd_s[t] = wd_ref[...].astype(BF16)

            @pl.when(t == 0)
            def _():
                h_s[...] = _rms(x_ref[...], g_ref[...]).astype(BF16)
                acc_s[...] = jnp.zeros_like(acc_s)

            h = h_s[...]
            a = _dot(h, wg_s[t])
            b = _dot(h, wu_s[t])
            acc_s[...] += _dot((a * jax.nn.sigmoid(a) * b).astype(BF16), wd_s[t])

            @pl.when(t == base)
            def _():
                y = x_ref[...] + 0.5 * acc_s[...]
                o_ref[...] = y if gf_ref is None else _rms(y, gf_ref[...])
                convert_slices()

        chunk = lambda c: (wg_s[c], wu_s[c], wd_s[c])
    else:
        base = full_from = 0
        cols = lambda c: slice(c * FF_CHUNK, (c + 1) * FF_CHUNK)
        chunk = lambda c: (wg_ref[:, cols(c)], wu_ref[:, cols(c)], wd_ref[cols(c), :])

    last_step = base + steps - 1

    @pl.when(jnp.logical_and(t >= full_from, t < last_step))
    def _():
        o_ref[...] = _ffn_math(x_ref[...], g_ref, chunk, gf_ref)
        convert_slices()

    @pl.when(t == last_step)
    def _():
        rows = x_ref.shape[0]
        both = jnp.concatenate([x_ref[...], _read_rows(tail_in)], axis=0)
        y = _ffn_math(both, g_ref, chunk, gf_ref)
        o_ref[...] = y[:rows]
        _write_rows(tail_out, y[rows:])
        convert_slices()


def _ffn(x, g, wg, wu, wd, tail, g_final=None, casts=(), tail_out_slabs=False):
    m = x.shape[0]
    tm = min(TOKEN_TILE, m)
    steps = m // tm
    final = g_final is not None
    own_weights = wg.dtype == F32
    tail_rows = tail.size // D_MODEL
    tail_out_shape = _as_slabs(tail_rows) if tail_out_slabs else (tail_rows, D_MODEL)
    assert steps > 1 or not own_weights
    base = FF_CHUNKS - 1 if own_weights else 0
    last = steps - 1
    tile = lambda t: jnp.clip(t - base, 0, last)
    row = pl.BlockSpec((tm, D_MODEL), lambda t: (tile(t), 0))
    if own_weights:
        load = lambda t: jnp.minimum(t, FF_CHUNKS - 1)
        weight_specs = [pl.BlockSpec((D_MODEL, FF_CHUNK), lambda t: (0, load(t))),
                        pl.BlockSpec((D_MODEL, FF_CHUNK), lambda t: (0, load(t))),
                        pl.BlockSpec((FF_CHUNK, D_MODEL), lambda t: (load(t), 0))]
        scratch = [pltpu.VMEM((FF_CHUNKS, D_MODEL, FF_CHUNK), BF16),
                   pltpu.VMEM((FF_CHUNKS, D_MODEL, FF_CHUNK), BF16),
                   pltpu.VMEM((FF_CHUNKS, FF_CHUNK, D_MODEL), BF16),
                   pltpu.VMEM((tm, D_MODEL), BF16),
                   pltpu.VMEM((tm, D_MODEL), F32)]
    else:
        weight_specs = [_resident((D_MODEL, D_FF)), _resident((D_MODEL, D_FF)),
                        _resident((D_FF, D_MODEL))]
        scratch = []
    in_specs = [row, _resident((1, D_MODEL))] + weight_specs
    args = [x, g, wg, wu, wd]
    if final:
        in_specs.append(_resident((1, D_MODEL)))
        args.append(g_final)
    cast_specs = []
    for w in casts:
        blocks = math.gcd(steps, w.shape[0] // BF16_SUBLANES)
        per = steps // blocks
        cast_specs.append(pl.BlockSpec((w.shape[0] // blocks, w.shape[1]),
                                       lambda t, per=per: (tile(t) // per, 0)))
    n_casts = len(casts)
    out = pl.pallas_call(
        functools.partial(_ffn_body, final=final, n_casts=n_casts, steps=steps,
                          own_weights=own_weights),
        grid=(base + steps,),
        in_specs=in_specs + cast_specs + [_resident(tail.shape)],
        out_specs=[row] + cast_specs + [pl.BlockSpec(tail_out_shape, lambda t: (0, 0))],
        out_shape=[jax.ShapeDtypeStruct((m, D_MODEL), F32)]
        + [jax.ShapeDtypeStruct(w.shape, BF16) for w in casts]
        + [jax.ShapeDtypeStruct(tail_out_shape, F32)],
        scratch_shapes=scratch,
        compiler_params=_params("arbitrary"),
        name="ffn_final" if final else "ffn",
    )(*args, *casts, tail)
    return (out[0], out[1:1 + n_casts], out[1 + n_casts]) if casts else (out[0], out[1])


Z_OFF = N_GROUPS * QKV_COLS
QM_OFF = Z_OFF + POOL_WIDTH
GATE_OFF = QM_OFF + MEM_WIDTH
IN_COLS = GATE_OFF + N_BRANCH * D_MODEL


def _inproj_body(x_ref, g_ref, w_ref, *rest, q_scale, with_windows, under_matmul=None):
    qkv_refs = rest[0:N_GROUPS]
    kv_refs = rest[N_GROUPS:2 * N_GROUPS] if with_windows else (None,) * N_GROUPS
    z_ref, qm_ref, u_ref, p_scr = rest[-4:]
    u = _rms(x_ref[...], g_ref[...]).astype(BF16)
    u_ref[...] = u
    tm = u.shape[0]
    width = N_GROUPS * GROUP_COLS
    p = _dot(u, w_ref[...])
    if under_matmul is not None:
        under_matmul()
    for g, (qkv_ref, kv_ref) in enumerate(zip(qkv_refs, kv_refs)):
        dil, per_class = qkv_ref.shape[1], qkv_ref.shape[2]
        q, k, v = [p[:, t * width + g * GROUP_COLS:t * width + (g + 1) * GROUP_COLS]
                   for t in range(3)]
        qkv = jnp.concatenate([q * q_scale, k, v], axis=1)
        if dil == 1:
            qkv_ref[0, 0] = qkv.astype(qkv_ref.dtype)
        else:
            for c in range(QKV_COLS // LANES):
                p_scr[c] = qkv[:, c * LANES:(c + 1) * LANES]
            for r in range(dil):
                blk = jnp.concatenate([p_scr[c, pl.ds(r, per_class, stride=dil), :]
                                       for c in range(QKV_COLS // LANES)], axis=1)
                qkv_ref[0, r] = blk.astype(qkv_ref.dtype)
        if with_windows:
            rows = kv_ref.shape[2]
            kv_ref[0] = jnp.concatenate([k[tm - rows:], v[tm - rows:]], axis=1).T
    z_ref[...] = p[:, Z_OFF:QM_OFF]
    qm_ref[...] = p[:, QM_OFF:GATE_OFF].astype(qm_ref.dtype)


def _inproj(x, g, w, n, s, windows, dils, q_dtype, q_scale, rider=()):
    tm = min(TOKEN_TILE, s)
    tj = s // tm

    def tok(cols):
        return pl.BlockSpec((tm, cols), lambda b, j: (b * tj + j, 0))

    qkv_specs = [pl.BlockSpec((1, d, tm // d, QKV_COLS), lambda b, j: (b, 0, j, 0)) for d in dils]
    qkv_shapes = [jax.ShapeDtypeStruct((n, d, s // d, QKV_COLS), q_dtype) for d in dils]

    kv_specs, kv_shapes = [], []
    for win in windows or ():
        rows = min(win, tm)
        assert tm % rows == 0 and win % rows == 0
        first = (s - win) // tm
        if win >= tm:
            spec = pl.BlockSpec((1, 2 * GROUP_COLS, rows),
                                lambda b, j, first=first: (b, 0, jnp.maximum(j - first, 0)))
        else:
            spec = pl.BlockSpec((1, 2 * GROUP_COLS, rows), lambda b, j: (b, 0, 0))
        kv_specs.append(spec)
        kv_shapes.append(jax.ShapeDtypeStruct((n, 2 * GROUP_COLS, win), F32))
    m = n * s
    out_shape = (qkv_shapes + kv_shapes + [
        jax.ShapeDtypeStruct((m, POOL_WIDTH), F32),
        jax.ShapeDtypeStruct((m, MEM_WIDTH), q_dtype),
        jax.ShapeDtypeStruct((m, D_MODEL), BF16)])
    out_specs = qkv_specs + kv_specs + [tok(POOL_WIDTH), tok(MEM_WIDTH), tok(D_MODEL)]
    in_specs = [tok(D_MODEL), _resident((1, D_MODEL)),
                pl.BlockSpec((D_MODEL, GATE_OFF), lambda b, j: (0, 0),
                             pipeline_mode=pl.Buffered(1))]
    body = functools.partial(_inproj_body, q_scale=q_scale, with_windows=bool(windows))
    n_own_out = len(out_specs)
    if rider:
        assert rider[0].shape[0] == n * tj
        r_in, r_out, r_shapes = _sample_branch_specs(rider, lambda b, j: b * tj + j)
        in_specs, out_specs, out_shape = in_specs + r_in, out_specs + r_out, out_shape + r_shapes

        def body(*refs, own=body):
            ins, r_ins = refs[:3], refs[3:3 + N_SAMPLE_INPUTS]
            outs, scratch = refs[3 + N_SAMPLE_INPUTS:-1], refs[-1:]
            own(*ins, *outs[:n_own_out], *scratch, under_matmul=lambda: _sample_branch_math(
                pl.program_id(0) * tj + pl.program_id(1), *r_ins, *outs[n_own_out:]))

    return pl.pallas_call(
        body,
        grid=(n, tj),
        in_specs=in_specs,
        out_specs=out_specs,
        out_shape=out_shape,
        scratch_shapes=[pltpu.VMEM((QKV_COLS // LANES, tm, LANES), F32)],
        compiler_params=_params("arbitrary", "arbitrary"),
        name="inproj",
    )(x, g, w, *rider)


ATTN_CHUNKS = 32
HEAD_LANES = HEADS * STRIDES


def _band_body(row_ref, band_ref):
    for g in range(N_GROUPS):
        for h in range(HEADS):
            rows = jnp.broadcast_to(row_ref[g, h] * LOG2E, (STRIDES, 2 * STRIDES))
            band = pltpu.roll(rows, 0, 1, stride=1, stride_axis=0).T
            band_ref[g, :, h * STRIDES:(h + 1) * STRIDES] = band


def _bands(rows):
    return pl.pallas_call(
        _band_body,
        grid=(1,),
        in_specs=[_resident(rows.shape)],
        out_specs=pl.BlockSpec((N_GROUPS, 2 * STRIDES, HEAD_LANES), lambda i: (0, 0, 0)),
        out_shape=jax.ShapeDtypeStruct((N_GROUPS, 2 * STRIDES, HEAD_LANES), F32),
        compiler_params=_params("arbitrary"),
        name="bands",
    )(rows)


def _attn_body(qkv_ref, band_ref, o_ref, lse_ref, *, dil, chunks):
    lane_head = lax.broadcasted_iota(jnp.int32, (1, GROUP_COLS), 1) // HEAD_DIM

    def keys(i):
        return slice(max(i - 1, 0) * STRIDES, (i + 1) * STRIDES)

    def scores(r, i):
        q = qkv_ref[0, r, i * STRIDES:(i + 1) * STRIDES, 0:GROUP_COLS]
        qm = jnp.concatenate(
            [jnp.where(lane_head == h, q, jnp.zeros_like(q)) for h in range(HEADS)], axis=0)
        return _dot_t(qkv_ref[0, r, keys(i), GROUP_COLS:2 * GROUP_COLS], qm)

    def ones_row(n_keys):
        return jnp.where(lax.broadcasted_iota(jnp.int32, (BF16_SUBLANES, n_keys), 0) == 0,
                         1.0, 0.0).astype(BF16)

    order = [(r, i) for r in range(dil) for i in range(chunks)]
    st_next = scores(0, 0)
    for idx, (r, i) in enumerate(order):
        st = st_next
        if i == 0:
            vt_cls = qkv_ref[0, r, :, 2 * GROUP_COLS:].T
        if idx + 1 < len(order):
            st_next = scores(*order[idx + 1])
        vt2 = vt_cls[:, keys(i)]
        band_lo = STRIDES if i == 0 else 0
        o_parts, lse_parts = [], []
        for h in range(HEADS):
            hl = slice(h * STRIDES, (h + 1) * STRIDES)
            s_h = st[:, hl] + band_ref[band_lo:, hl]
            m = jnp.max(s_h, axis=0, keepdims=True)
            e = jnp.exp2(s_h - m).astype(BF16)
            ot = _dot(jnp.concatenate([vt2[h * HEAD_DIM:(h + 1) * HEAD_DIM],
                                       ones_row(vt2.shape[1])], axis=0), e)
            l = ot[HEAD_DIM:HEAD_DIM + 1]
            o_parts.append(ot[:HEAD_DIM] * (1.0 / l))
            lse_parts.append(jnp.broadcast_to(m + jnp.log2(l), (HEAD_DIM, STRIDES)))
        rows = pl.ds(i * STRIDES * dil + r, STRIDES, stride=dil)
        o_rows = jnp.concatenate(o_parts, axis=0).T
        lse_rows = jnp.concatenate(lse_parts, axis=0).T
        for c in range(GROUP_COLS // LANES):
            o_ref[c, rows, :] = o_rows[:, c * LANES:(c + 1) * LANES]
            lse_ref[c, rows, :] = lse_rows[:, c * LANES:(c + 1) * LANES]


def _attn(qkv, bands, g, n, s, dil):
    chunks = s // (STRIDES * dil)
    assert chunks * dil == ATTN_CHUNKS
    span = chunks * STRIDES
    slabs = GROUP_COLS // LANES
    out_spec = pl.BlockSpec((slabs, s, LANES), lambda b: (0, b, 0))
    out_sds = jax.ShapeDtypeStruct((slabs, n * s, LANES), F32)
    return pl.pallas_call(
        functools.partial(_attn_body, dil=dil, chunks=chunks),
        grid=(n,),
        in_specs=[pl.BlockSpec((1, dil, span, QKV_COLS), lambda b: (b, 0, 0, 0)),
                  pl.BlockSpec((None, 2 * STRIDES, HEAD_LANES), lambda b: (g, 0, 0))],
        out_specs=[out_spec, out_spec],
        out_shape=[out_sds, out_sds],
        compiler_params=_params("parallel"),
        name="attn_d%d" % dil,
    )(qkv, bands)


MEM_SLABS = 2 * MEM_WIDTH // LANES


def _mem_slab(c):
    return pl.ds(c, MEM_LEN, stride=MEM_SLABS)


def _memkv_body(mem_ref, g_ref, w_ref, o_ref):
    kv = _dot(_rms(mem_ref[...], g_ref[...]).astype(BF16), w_ref[...])
    for c in range(MEM_SLABS):
        o_ref[_mem_slab(c), :] = kv[:, c * LANES:(c + 1) * LANES]


def _memkv(mem, g, w):
    assert MEM_HEAD_DIM == LANES
    m = mem.shape[0]
    row = pl.BlockSpec((MEM_LEN, D_MODEL), lambda i: (i, 0))
    return pl.pallas_call(
        _memkv_body,
        grid=(m // MEM_LEN,),
        in_specs=[row, _resident((1, D_MODEL)), _resident((D_MODEL, 2 * MEM_WIDTH))],
        out_specs=pl.BlockSpec((MEM_LEN * MEM_SLABS, LANES), lambda i: (i, 0)),
        out_shape=jax.ShapeDtypeStruct((m * MEM_SLABS, LANES), F32),
        compiler_params=_params("parallel"),
        name="memkv",
    )(mem, g, w)


def _merge_math(a, pooled, c, x, u, win_ref, wpool_ref, scale_ref, woa_ref, wob_ref,
                woc_ref, wout_ref):
    mixed = [_dot(pooled[gi].astype(BF16), wpool_ref[gi]) for gi in range(len(POOL_WINDOWS))]
    b = jnp.concatenate(mixed, axis=1) * scale_ref[...]
    m = None
    for k, (branch, wo_ref) in enumerate(((a, woa_ref), (b, wob_ref), (c, woc_ref))):
        lo = GATE_OFF + k * D_MODEL
        gate = jax.nn.sigmoid(_dot(u, win_ref[:, lo:lo + D_MODEL]))
        term = gate * _dot(branch.astype(BF16), wo_ref[...])
        m = term if m is None else m + term
    return x + _dot(m.astype(BF16), wout_ref[...])


def _mix_body(o0, o1, o2, l0, l1, l2, z_ref, halo_ref, qm_ref, mkv_ref, x_ref, u_ref, *rest,
              tile_in_seq, tail=None):
    merge_refs, out_ref = rest[:-1], rest[-1]
    j = tile_in_seq
    tm = x_ref.shape[0]
    unslab = lambda ref: jnp.concatenate([ref[c] for c in range(ref.shape[0])], axis=1)
    lses = [unslab(l0), unslab(l1), unslab(l2)]
    mx = jnp.maximum(jnp.maximum(lses[0], lses[1]), lses[2])
    es = [jnp.exp2(l - mx) for l in lses]
    a = ((es[0] * unslab(o0) + es[1] * unslab(o1) + es[2] * unslab(o2))
         / (es[0] + es[1] + es[2]))
    z = z_ref[...]
    halo = jnp.where(j == 0, 0.0, halo_ref[...])
    zc = jnp.concatenate([halo, z], axis=0)
    pos = j * tm + lax.broadcasted_iota(jnp.int32, (tm, 1), 0)
    pooled = []
    for gi, kw in enumerate(POOL_WINDOWS):
        cs = slice(gi * POOL_GROUP, (gi + 1) * POOL_GROUP)
        run = zc[:, cs]
        width = 1
        while width < kw:
            run = run[width:] + run[:-width]
            width *= 2
        first = POOL_HALO - (kw - 1)
        cnt = jnp.minimum(kw, pos + 1).astype(F32)
        pooled.append(run[first:first + tm] / cnt - z[:, cs])
    qm = qm_ref[...]
    cs_out = []
    for h in range(MEM_HEADS):
        hs = slice(h * MEM_HEAD_DIM, (h + 1) * MEM_HEAD_DIM)
        k_h = mkv_ref[_mem_slab(h), :].astype(BF16)
        v_h = mkv_ref[_mem_slab(MEM_HEADS + h), :].astype(BF16)
        s = _dot_t(qm[:, hs], k_h) * (1.0 / math.sqrt(MEM_HEAD_DIM))
        mm = jnp.max(s, axis=-1, keepdims=True)
        p = jnp.exp(s - mm)
        l = jnp.sum(p, axis=-1, keepdims=True)
        cs_out.append(_dot(p.astype(BF16), v_h) / l)
    c = jnp.concatenate(cs_out, axis=1)
    x, u = x_ref[...], u_ref[...]
    if tail is not None:
        a_ref, pooled_ref, c_ref, xs_ref, us_ref, tail_out = tail
        under = lambda top, bottom: jnp.concatenate([top, bottom], axis=0)
        pooled_tail = pooled_ref[...]
        pooled = [under(p, pooled_tail[:, gi * POOL_GROUP:(gi + 1) * POOL_GROUP])
                  for gi, p in enumerate(pooled)]
        a, c = under(a, a_ref[...]), under(c, c_ref[...])
        x, u = under(x, xs_ref[...]), under(u, us_ref[...])
    y = _merge_math(a, pooled, c, x, u, *merge_refs)
    out_ref[...] = y[:tm]
    if tail is not None:
        tail_out[...] = y[tm:]


def _merge_weight_specs():
    return [_resident((D_MODEL, IN_COLS)),
            _resident((len(POOL_WINDOWS), POOL_GROUP, POOL_GROUP)), _resident((1, POOL_WIDTH)),
            _resident((GROUP_COLS, D_MODEL)), _resident((POOL_WIDTH, D_MODEL)),
            _resident((MEM_WIDTH, D_MODEL)), _resident((D_MODEL, D_MODEL))]


N_MERGE_WEIGHTS = 7


def _mix_with_tail_body(*refs, tiles_per_seq, steps):
    n_tile_in = 12
    tile_in, merge_refs = refs[:n_tile_in], refs[n_tile_in:n_tile_in + N_MERGE_WEIGHTS]
    *tail_in, out_ref, tail_out = refs[n_tile_in + N_MERGE_WEIGHTS:]
    t = pl.program_id(0)

    @pl.when(t < steps - 1)
    def _():
        _mix_body(*tile_in, *merge_refs, out_ref, tile_in_seq=lax.rem(t, tiles_per_seq))

    @pl.when(t == steps - 1)
    def _():
        _mix_body(*tile_in, *merge_refs, out_ref, tile_in_seq=(steps - 1) % tiles_per_seq,
                  tail=(*tail_in, tail_out))


def _mix(os, lses, z, qm, mkv, x, u, weights, n, s, tail):
    tm = TOKEN_TILE
    tj = s // tm
    steps = n * tj
    last = steps - 1
    halo_per_tile = tm // POOL_HALO
    tile = lambda t: jnp.minimum(t, last)

    def tok(cols):
        return pl.BlockSpec((tm, cols), lambda t: (tile(t), 0))

    halo = pl.BlockSpec((POOL_HALO, POOL_WIDTH),
                        lambda t: (jnp.maximum(tile(t) * halo_per_tile - 1, 0), 0))
    slab = pl.BlockSpec((GROUP_COLS // LANES, tm, LANES), lambda t: (0, tile(t), 0))
    weight_specs = _merge_weight_specs()
    assert len(weight_specs) == N_MERGE_WEIGHTS
    in_specs = ([slab] * 6 + [tok(POOL_WIDTH), halo, tok(MEM_WIDTH),
                pl.BlockSpec((MEM_LEN * MEM_SLABS, LANES), lambda t: (tile(t) // tj, 0)),
                tok(D_MODEL), tok(D_MODEL)] + weight_specs + [_resident(t.shape) for t in tail])
    xs = tail[3]
    return pl.pallas_call(
        functools.partial(_mix_with_tail_body, tiles_per_seq=tj, steps=steps),
        grid=(steps,),
        in_specs=in_specs,
        out_specs=[tok(D_MODEL), pl.BlockSpec(xs.shape, lambda t: (0, 0))],
        out_shape=[jax.ShapeDtypeStruct((n * s, D_MODEL), F32),
                   jax.ShapeDtypeStruct(xs.shape, F32)],
        compiler_params=_params("arbitrary"),
        name="mix",
    )(*os, *lses, z, z, qm, mkv, x, u, *weights, *tail)


N_SAMPLE_INPUTS = 14


def _sample_branch_math(req, qkv0, qkv1, qkv2, c0, c1, c2, b0, b1, b2, bias0_ref, z_ref, st_ref,
                        qm_ref, cm_ref, a_ref, pooled_ref, c_ref):
    row = pl.ds(req, 1)
    eye = (lax.broadcasted_iota(jnp.int32, (HEAD_DIM, HEAD_DIM), 0)
           == lax.broadcasted_iota(jnp.int32, (HEAD_DIM, HEAD_DIM), 1))
    cube = (HEADS, HEAD_DIM, HEAD_DIM)

    def heads_of(vec, lo):
        return jnp.stack([vec[:, lo + h * HEAD_DIM:lo + (h + 1) * HEAD_DIM]
                          for h in range(HEADS)], axis=0)

    outs, lses = [], []
    for g, (qkv_ref, cache_ref, bias_ref) in enumerate(
            ((qkv0, c0, b0), (qkv1, c1, b1), (qkv2, c2, b2))):
        qkv = qkv_ref[row, :]
        q, kn, vn = heads_of(qkv, 0), heads_of(qkv, GROUP_COLS), heads_of(qkv, 2 * GROUP_COLS)
        q_col = jnp.sum(jnp.where(eye, jnp.broadcast_to(q, cube), 0.0), axis=2, keepdims=True)
        s = jnp.sum(cache_ref[0, 0] * q_col, axis=1, keepdims=True) + bias_ref[...]
        sn = jnp.sum(kn * q, axis=2, keepdims=True) + bias0_ref[g]
        m = jnp.maximum(jnp.max(s, axis=2, keepdims=True), sn)
        p = jnp.exp(s - m)
        pn = jnp.exp(sn - m)
        l = jnp.sum(p, axis=2, keepdims=True) + pn
        pv = jnp.sum(cache_ref[0, 1] * p, axis=2, keepdims=True)
        pv_row = jnp.sum(jnp.where(eye, jnp.broadcast_to(pv, cube), 0.0), axis=1, keepdims=True)
        outs.append((pv_row + pn * vn) / l)
        lses.append(m + jnp.log(l))
    mx = jnp.maximum(jnp.maximum(lses[0], lses[1]), lses[2])
    es = [jnp.exp(lse - mx) for lse in lses]
    a = (es[0] * outs[0] + es[1] * outs[1] + es[2] * outs[2]) / (es[0] + es[1] + es[2])
    a_ref[row, :] = jnp.concatenate([a[h] for h in range(HEADS)], axis=1)

    zn = z_ref[row, :]
    st = st_ref[:, row, :]
    pooled = []
    for gi, kw in enumerate(POOL_WINDOWS):
        cs = slice(gi * POOL_GROUP, (gi + 1) * POOL_GROUP)
        tot = jnp.sum(st[POOL_STATE - (kw - 1):, :, cs], axis=0) + zn[:, cs]
        pooled.append(tot / float(min(kw, PAST_LEN + 1)) - zn[:, cs])
    pooled_ref[row, :] = jnp.concatenate(pooled, axis=1)

    qm_row = qm_ref[row, :]
    qm = jnp.concatenate([qm_row[:, h * MEM_HEAD_DIM:(h + 1) * MEM_HEAD_DIM]
                          for h in range(MEM_HEADS)], axis=0)
    km = cm_ref[0, :, 0]
    vm = cm_ref[0, :, 1]
    s = jnp.sum(km * qm[None], axis=-1, keepdims=True) * (1.0 / math.sqrt(MEM_HEAD_DIM))
    m = jnp.max(s, axis=0)
    p = jnp.exp(s - m[None])
    c = jnp.sum(p * vm, axis=0) / jnp.sum(p, axis=0)
    c_ref[row, :] = jnp.concatenate([c[h:h + 1] for h in range(MEM_HEADS)], axis=1)


def _sample_branch_specs(operands, request_of):
    assert len(operands) == N_SAMPLE_INPUTS
    nb = operands[0].shape[0]

    def per_request(t):
        zeros = (0,) * (t.ndim - 1)
        return pl.BlockSpec((1,) + t.shape[1:], lambda *g: (request_of(*g),) + zeros)

    def whole(shape):
        zeros = (0,) * len(shape)
        return pl.BlockSpec(shape, lambda *g: zeros)

    streamed = (3, 4, 5, 13)
    in_specs = [per_request(t) if k in streamed else whole(t.shape)
                for k, t in enumerate(operands)]
    widths = (GROUP_COLS, POOL_WIDTH, MEM_WIDTH)
    return (in_specs, [whole((nb, w)) for w in widths],
            [jax.ShapeDtypeStruct((nb, w), F32) for w in widths])


def _rel_bucket(n):
    max_exact = N_BUCKETS // 2
    nf = jnp.maximum(n, 1).astype(F32)
    large = max_exact + (jnp.log(nf / max_exact) / math.log(MAX_DISTANCE / max_exact)
                         * (N_BUCKETS - max_exact)).astype(jnp.int32)
    large = jnp.minimum(large, N_BUCKETS - 1)
    return jnp.where(n < max_exact, n, large)


def _stride_bias(rel_bias, g, dil):
    j = jnp.arange(STRIDES + 1, dtype=jnp.int32)
    return rel_bias[_rel_bucket(j * dil)][:, g * HEADS:(g + 1) * HEADS].astype(F32)


def _band_row(bias_j):
    row = jnp.concatenate([bias_j[::-1], jnp.full((STRIDES - 1, HEADS), NEG_INF, F32)], axis=0)
    return row.T.reshape(HEADS, 1, 2 * STRIDES)


def _cache_bias(bias_j, dil):
    on_grid = bias_j[STRIDES:0:-1].T
    full = jnp.full((HEADS, STRIDES, dil), NEG_INF, F32).at[:, :, 0].set(on_grid)
    return full.reshape(HEADS, 1, STRIDES * dil)


def kernel(x_prompt, x_sample, cache_win0_kv, cache_win1_kv, cache_win2_kv, state_pool, cache_mem_kv, mem_prompt, rel_bias, g_ffn1, w1_gate, w1_up, w1_down, g_mix, w_in, w_pool, pool_scale, g_mem, w_mem_kv, w_oa, w_ob, w_oc, w_out, g_ffn2, w2_gate, w2_up, w2_down, g_final):
    n, s, _ = x_prompt.shape
    nb = x_sample.shape[0]
    depth = g_ffn1.shape[0]
    win_caches = (cache_win0_kv, cache_win1_kv, cache_win2_kv)
    bias_js = [_stride_bias(rel_bias, g, dil) for g, (_, dil) in enumerate(DIL_GROUPS)]
    bands = _bands(jnp.stack([_band_row(b) for b in bias_js]))
    bias_cache = [_cache_bias(b, dil) for b, (_, dil) in zip(bias_js, DIL_GROUPS)]
    bias_new = jnp.stack([b[0] for b in bias_js]).reshape(N_GROUPS, HEADS, 1, 1)
    gfin = g_final.reshape(1, D_MODEL)

    xp = x_prompt.reshape(n * s, D_MODEL)
    xs = x_sample.reshape(_as_slabs(nb))
    st_p = [[] for _ in range(5)]
    st_s = [[] for _ in range(4)]
    for l in range(depth):
        last = l == depth - 1
        vec = lambda v: v[l].reshape(1, -1)
        q_scale = 1.0 / math.sqrt(HEAD_DIM)

        later = [w2_gate[l], w2_up[l], w2_down[l], w_in[l], w_oa[l], w_ob[l], w_oc[l], w_out[l],
                 w_mem_kv[l], w_pool[l].reshape(len(POOL_WINDOWS) * POOL_GROUP, POOL_GROUP)]
        xp, later, xs = _ffn(xp, vec(g_ffn1), w1_gate[l], w1_up[l], w1_down[l], xs, casts=later)
        w2, (win_l, woa_l, wob_l, woc_l, wout_l, wmem_l, wpool_l) = later[0:3], later[3:]
        merge_w = (win_l, wpool_l.reshape(w_pool.shape[1:]), vec(pool_scale),
                   woa_l, wob_l, woc_l, wout_l)

        res = _inproj(xs, vec(g_mix), win_l, 1, nb, None, [1] * N_GROUPS, F32, q_scale)
        s_qkvs, s_z, s_qm, s_u = [t.reshape(nb, QKV_COLS) for t in res[0:3]], *res[3:6]
        rider = (s_qkvs + [jnp.transpose(cw[l], (0, 2, 3, 4, 1)) for cw in win_caches]
                 + bias_cache + [bias_new, s_z, jnp.transpose(state_pool[l], (1, 0, 2)), s_qm,
                                 cache_mem_kv[l]])
        res = _inproj(xp, vec(g_mix), win_l, n, s, [min(w, s) for w, _ in DIL_GROUPS],
                      [d for _, d in DIL_GROUPS], BF16, q_scale * LOG2E, rider)
        qkvs, kvwins, z, qm, u = res[0:3], res[3:6], res[6], res[7], res[8]
        s_branches = res[9:12]
        os, lses = [], []
        for g, (_, dil) in enumerate(DIL_GROUPS):
            o, lse = _attn(qkvs[g], bands, g, n, s, dil)
            os.append(o)
            lses.append(lse)
        mkv = _memkv(mem_prompt.reshape(n * MEM_LEN, D_MODEL), vec(g_mem), wmem_l)
        xp, xs = _mix(os, lses, z, qm, mkv, xp, u, merge_w, n, s, tail=(*s_branches, xs, s_u))
        xp, xs = _ffn(xp, vec(g_ffn2), *w2, xs, g_final=gfin if last else None,
                      tail_out_slabs=True)
        for g, (win, _) in enumerate(DIL_GROUPS):
            kv_t = kvwins[g].reshape(n, 2, HEADS, HEAD_DIM, min(win, s))
            st_p[g].append(jnp.transpose(kv_t, (0, 4, 1, 2, 3)))
        st_p[3].append(z.reshape(n, s, POOL_WIDTH)[:, s - POOL_STATE:])
        st_p[4].append(mkv.reshape(n, MEM_LEN, 2, MEM_HEADS, MEM_HEAD_DIM))

        for g in range(N_GROUPS):
            st_s[g].append(s_qkvs[g][:, GROUP_COLS:].reshape(nb, 1, 2, HEADS, HEAD_DIM))
        st_s[3].append(s_z.reshape(nb, 1, POOL_WIDTH))

    y_prompt = xp.reshape(n, s, D_MODEL)
    y_sample = xs.reshape(nb, 1, D_MODEL)
    stack = lambda ts: jnp.stack(ts, axis=0)
    return (y_prompt, y_sample, stack(st_p[0]), stack(st_p[1]), stack(st_p[2]), stack(st_p[3]),
            stack(st_p[4]), stack(st_s[0]), stack(st_s[1]), stack(st_s[2]), stack(st_s[3]))
```
